```python
import math
import jax, jax.numpy as jnp
from jax import lax
import numpy as np

D_MODEL = 2048
BATCH = 4
SEQ = 2048
DEPTH = 2
DEC_BATCH = 128
DEC_SEQ = 4
PAST_LEN = 16384
PAGE_SIZE = 128

MIX_WIDTH = 2 * D_MODEL
M_WIDTH = MIX_WIDTH // 4
M_HEADS = 4
M_DH = M_WIDTH // M_HEADS
M_CHUNK = 64
S_WIDTH = MIX_WIDTH // 2
S_DH = 64
S_HEADS = S_WIDTH // S_DH
S_GROUPS = 4
S_HPG = S_HEADS // S_GROUPS
S_STATE = 128
S_CHUNK = 128
S_CONV_DIM = S_WIDTH + 2 * S_GROUPS * S_STATE
C_WIDTH = MIX_WIDTH // 4
C_GROUPS = 4
C_DG = C_WIDTH // C_GROUPS
C_CHUNK = 128
CONV_K = 4
EPS = 1e-6
PROJ_SIZES = (M_WIDTH, M_WIDTH, S_WIDTH, S_CONV_DIM, S_HEADS, C_WIDTH, C_WIDTH, C_WIDTH)
PROJ_WIDTH = sum(PROJ_SIZES)

kernel_name = "hymba_mlstm_ssd_chunkmlp_step"


def rmsnorm(x, g):
    xf = x.astype(jnp.float32)
    xf = xf * lax.rsqrt(jnp.mean(xf * xf, axis=-1, keepdims=True) + EPS)
    return xf.astype(x.dtype) * g


def group_rmsnorm(x, g, groups):
    shp = x.shape
    xg = x.reshape(shp[:-1] + (groups, shp[-1] // groups))
    return rmsnorm(xg, g.reshape(groups, -1)).reshape(shp)


def causal_dwconv(x, buf, w, b):
    xf = jnp.concatenate([buf.astype(x.dtype), x], axis=1)
    y = lax.conv_general_dilated(xf, w[:, None, :].astype(x.dtype), window_strides=(1,), padding='VALID',
                                 dimension_numbers=('NWC', 'WIO', 'NWC'), feature_group_count=x.shape[-1])
    return jax.nn.silu(y + b), xf[:, -(CONV_K - 1):]


def mlstm_chunked(q, k, v, i_pre, logf, C0, n0, m0):
    Bsz, T, H, D = q.shape
    L = math.gcd(T, M_CHUNK)
    NC = T // L
    k = k * (D ** -0.5)
    def to_chunks(a):
        a = a.reshape((Bsz, NC, L) + a.shape[2:])
        return jnp.moveaxis(jnp.moveaxis(a, 1, 0), 2, 3)
    qc, kc, vc, ic, fc = to_chunks(q), to_chunks(k), to_chunks(v), to_chunks(i_pre), to_chunks(logf)
    causal = jnp.tril(jnp.ones((L, L), bool))

    def step(carry, inp):
        C, n, m = carry
        qk_, kk, vk, ik, fk = inp
        b = jnp.cumsum(fk, axis=-1)
        a = b + m[..., None]
        dmat = jnp.where(causal, b[..., :, None] - b[..., None, :] + ik[..., None, :], -jnp.inf)
        m_new = jnp.maximum(a, jnp.max(dmat, axis=-1))
        w_inter = jnp.exp(a - m_new)
        w_intra = jnp.exp(dmat - m_new[..., None])
        s = jnp.einsum('bhtd,bhsd->bhts', qk_, kk) * w_intra
        num = jnp.einsum('bhts,bhsd->bhtd', s, vk) + w_inter[..., None] * jnp.einsum('bhvk,bhtk->bhtv', C, qk_)
        den = jnp.sum(s, axis=-1) + w_inter * jnp.einsum('bhk,bhtk->bht', n, qk_)
        h = num / jnp.maximum(jnp.abs(den), jnp.exp(-m_new))[..., None]
        wl_inter = w_inter[..., -1]
        wl = w_intra[..., -1, :]
        C_new = wl_inter[..., None, None] * C + jnp.einsum('bhs,bhsv,bhsk->bhvk', wl, vk, kk)
        n_new = wl_inter[..., None] * n + jnp.einsum('bhs,bhsk->bhk', wl, kk)
        return (C_new, n_new, m_new[..., -1]), h

    (C1, n1, m1), hs = lax.scan(step, (C0, n0, m0), (qc, kc, vc, ic, fc))
    h = jnp.moveaxis(jnp.moveaxis(hs, 0, 1), 3, 2).reshape(Bsz, T, H, D)
    return h, C1, n1, m1


def ssd_chunked(x, dt, A, Bm, Cm, h0):
    Bsz, T = x.shape[:2]
    L = math.gcd(T, S_CHUNK)
    NC = T // L
    def to_chunks(a):
        return jnp.moveaxis(a.reshape((Bsz, NC, L) + a.shape[2:]), 1, 0)
    xc = to_chunks(x.reshape(Bsz, T, S_GROUPS, S_HPG, S_DH))
    dtc = to_chunks(dt.reshape(Bsz, T, S_GROUPS, S_HPG))
    bc, cc = to_chunks(Bm), to_chunks(Cm)
    Ag = A.reshape(S_GROUPS, S_HPG)
    causal = jnp.tril(jnp.ones((L, L), bool))[None, :, :, None, None]

    def step(h, inp):
        xk, dtk, bk, ck = inp
        cs = jnp.cumsum(dtk * Ag, axis=1)
        decay = jnp.exp(jnp.where(causal, cs[:, :, None] - cs[:, None, :], -jnp.inf))
        cb = jnp.einsum('btgn,bsgn->btsg', ck, bk)
        mix = cb[..., None] * decay * dtk[:, None]
        y = jnp.einsum('btsgh,bsghp->btghp', mix, xk)
        y = y + jnp.exp(cs)[..., None] * jnp.einsum('btgn,bghpn->btghp', ck, h)
        w_end = jnp.exp(cs[:, -1:] - cs) * dtk
        h = jnp.exp(cs[:, -1])[..., None, None] * h + jnp.einsum('bsgh,bsgn,bsghp->bghpn', w_end, bk, xk)
        return h, y

    hT, ys = lax.scan(step, h0.reshape(Bsz, S_GROUPS, S_HPG, S_DH, S_STATE), (xc, dtc, bc, cc))
    y = jnp.moveaxis(ys, 0, 1).reshape(Bsz, T, S_HEADS, S_DH)
    return y, hT.reshape(Bsz, S_HEADS, S_DH, S_STATE)


def chunk_mlp(u, v, w_s, b_s):
    Bsz, T = u.shape[:2]
    blk = T if T < C_CHUNK else C_CHUNK
    NC = -(-T // blk)
    pad = NC * blk - T
    vp = jnp.pad(v, ((0, 0), (0, pad), (0, 0))).reshape(Bsz, NC, blk, C_GROUPS, C_DG)
    w = w_s[:, :blk, :blk] * jnp.tril(jnp.ones((blk, blk), w_s.dtype))
    mixed = jnp.einsum('gts,bcsgd->bctgd', w, vp) + b_s[:, :blk].T[:, :, None]
    mixed = mixed.reshape(Bsz, NC * blk, C_WIDTH)[:, :T]
    return u * mixed


def mixer_layer(x, mconv_buf, C0, n0, m0, sconv_buf, h0,
                norm_g, w_in, m_conv_w, m_conv_b, m_w_qk, m_w_vo, m_b_o, m_w_gate, m_b_gate, m_norm_g,
                s_conv_w, s_conv_b, s_dt_bias, s_A_log, s_D, s_norm_g,
                c_v_norm_g, c_w_s, c_b_s, w_out):
    f32 = jnp.float32
    Bsz, T, _ = x.shape
    h = rmsnorm(x, norm_g)
    proj = jnp.einsum('btd,de->bte', h, w_in)
    splits = np.cumsum(PROJ_SIZES)[:-1].tolist()
    xm, zm, zs, xbc, dt_raw, u, v, zc = jnp.split(proj, splits, axis=-1)

    xmc, mconv_new = causal_dwconv(xm, mconv_buf, m_conv_w, m_conv_b)
    q, k = jnp.split(jnp.einsum('bthd,hde->bthe', xmc.reshape(Bsz, T, M_HEADS, M_DH), m_w_qk), 2, axis=-1)
    vm, o_pre = jnp.split(jnp.einsum('bthd,hde->bthe', xm.reshape(Bsz, T, M_HEADS, M_DH), m_w_vo), 2, axis=-1)
    gates = jnp.einsum('bte,eg->btg', jnp.concatenate([q, k, vm], axis=-1).reshape(Bsz, T, 3 * M_WIDTH), m_w_gate) + m_b_gate
    i_pre, f_pre = jnp.split(gates.astype(f32), 2, axis=-1)
    hm, C1, n1, m1 = mlstm_chunked(q.astype(f32), k.astype(f32), vm.astype(f32), i_pre, jax.nn.log_sigmoid(f_pre),
                                   C0.astype(f32), n0.astype(f32), m0.astype(f32))
    hm = jax.nn.sigmoid(o_pre.astype(f32) + m_b_o.reshape(M_HEADS, M_DH)) * hm
    hm = rmsnorm(hm, m_norm_g.reshape(M_HEADS, M_DH)).reshape(Bsz, T, M_WIDTH).astype(x.dtype)
    out_m = hm * jax.nn.silu(zm)

    xbc_c, sconv_new = causal_dwconv(xbc, sconv_buf, s_conv_w, s_conv_b)
    xs, Bm, Cm = jnp.split(xbc_c, [S_WIDTH, S_WIDTH + S_GROUPS * S_STATE], axis=-1)
    dt = jax.nn.softplus(dt_raw.astype(f32) + s_dt_bias)
    A = -jnp.exp(s_A_log.astype(f32))
    xs_h = xs.reshape(Bsz, T, S_HEADS, S_DH).astype(f32)
    ys, h1 = ssd_chunked(xs_h, dt, A, Bm.reshape(Bsz, T, S_GROUPS, S_STATE).astype(f32),
                         Cm.reshape(Bsz, T, S_GROUPS, S_STATE).astype(f32), h0.astype(f32))
    ys = (ys + s_D[:, None] * xs_h).reshape(Bsz, T, S_WIDTH) * jax.nn.silu(zs.astype(f32))
    out_s = group_rmsnorm(ys, s_norm_g, S_GROUPS).astype(x.dtype)

    vn = group_rmsnorm(v, c_v_norm_g, C_GROUPS)
    out_c = chunk_mlp(u, vn, c_w_s, c_b_s) * jax.nn.silu(zc)

    mix = jnp.concatenate([out_m, out_s, out_c], axis=-1)
    y = x + jnp.einsum('bte,ed->btd', mix, w_out)
    return y, (C1.astype(C0.dtype), n1.astype(n0.dtype), m1.astype(m0.dtype), mconv_new,
               h1.astype(h0.dtype), sconv_new, vn)


def setup_inputs(seed: int = 0) -> dict:
    key = jax.random.key(seed)
    ks = iter(jax.random.split(key, 40))
    f32 = jnp.float32

    def nrm(shape, scale):
        return jax.random.normal(next(ks), shape, f32) * scale

    x_prompt = nrm((BATCH, SEQ, D_MODEL), 1.0)
    x_sample = nrm((DEC_BATCH, DEC_SEQ, D_MODEL), 1.0)
    state_mlstm_C = nrm((DEPTH, DEC_BATCH, M_HEADS, M_DH, M_DH), 0.05)
    state_mlstm_n = nrm((DEPTH, DEC_BATCH, M_HEADS, M_DH), 0.05)
    state_mlstm_m = jax.random.uniform(next(ks), (DEPTH, DEC_BATCH, M_HEADS), f32, -1.0, 2.0)
    state_mlstm_conv = nrm((DEPTH, DEC_BATCH, CONV_K - 1, M_WIDTH), 1.0)
    state_ssm = nrm((DEPTH, DEC_BATCH, S_HEADS, S_DH, S_STATE), 0.1)
    state_ssm_conv = nrm((DEPTH, DEC_BATCH, CONV_K - 1, S_CONV_DIM), 1.0)

    norm_g = 1.0 + nrm((DEPTH, D_MODEL), 0.02)
    w_in = nrm((DEPTH, D_MODEL, PROJ_WIDTH), D_MODEL ** -0.5)
    m_conv_w = nrm((DEPTH, CONV_K, M_WIDTH), CONV_K ** -0.5)
    m_conv_b = nrm((DEPTH, M_WIDTH), 0.02)
    m_w_qk = nrm((DEPTH, M_HEADS, M_DH, 2 * M_DH), M_DH ** -0.5)
    m_w_vo = nrm((DEPTH, M_HEADS, M_DH, 2 * M_DH), M_DH ** -0.5)
    m_b_o = nrm((DEPTH, M_WIDTH), 0.02)
    m_w_gate = nrm((DEPTH, 3 * M_WIDTH, 2 * M_HEADS), (3 * M_WIDTH) ** -0.5)
    f_bias = jnp.broadcast_to(jnp.linspace(3.0, 6.0, M_HEADS, dtype=f32), (DEPTH, M_HEADS))
    m_b_gate = jnp.concatenate([nrm((DEPTH, M_HEADS), 0.1), f_bias + nrm((DEPTH, M_HEADS), 0.1)], axis=-1)
    m_norm_g = 1.0 + nrm((DEPTH, M_WIDTH), 0.02)
    s_conv_w = nrm((DEPTH, CONV_K, S_CONV_DIM), CONV_K ** -0.5)
    s_conv_b = nrm((DEPTH, S_CONV_DIM), 0.02)
    dt0 = jnp.exp(jax.random.uniform(next(ks), (DEPTH, S_HEADS), f32, math.log(1e-3), math.log(1e-1)))
    s_dt_bias = dt0 + jnp.log(-jnp.expm1(-dt0))
    s_A_log = jnp.log(jax.random.uniform(next(ks), (DEPTH, S_HEADS), f32, 1.0, 16.0))
    s_D = 1.0 + nrm((DEPTH, S_HEADS), 0.1)
    s_norm_g = 1.0 + nrm((DEPTH, S_WIDTH), 0.02)
    c_v_norm_g = 1.0 + nrm((DEPTH, C_WIDTH), 0.02)
    c_w_s = nrm((DEPTH, C_GROUPS, C_CHUNK, C_CHUNK), C_CHUNK ** -0.5)
    c_b_s = 1.0 + nrm((DEPTH, C_GROUPS, C_CHUNK), 0.02)
    w_out = nrm((DEPTH, MIX_WIDTH, D_MODEL), (2 * DEPTH * MIX_WIDTH) ** -0.5)
    final_norm_g = 1.0 + nrm((D_MODEL,), 0.02)
    return {"x_prompt": x_prompt, "x_sample": x_sample,
            "state_mlstm_C": state_mlstm_C, "state_mlstm_n": state_mlstm_n, "state_mlstm_m": state_mlstm_m,
            "state_mlstm_conv": state_mlstm_conv, "state_ssm": state_ssm, "state_ssm_conv": state_ssm_conv,
            "norm_g": norm_g, "w_in": w_in, "m_conv_w": m_conv_w, "m_conv_b": m_conv_b,
            "m_w_qk": m_w_qk, "m_w_vo": m_w_vo, "m_b_o": m_b_o, "m_w_gate": m_w_gate, "m_b_gate": m_b_gate,
            "m_norm_g": m_norm_g, "s_conv_w": s_conv_w, "s_conv_b": s_conv_b, "s_dt_bias": s_dt_bias,
            "s_A_log": s_A_log, "s_D": s_D, "s_norm_g": s_norm_g, "c_v_norm_g": c_v_norm_g,
            "c_w_s": c_w_s, "c_b_s": c_b_s, "w_out": w_out, "final_norm_g": final_norm_g}


def reference(x_prompt, x_sample, state_mlstm_C, state_mlstm_n, state_mlstm_m, state_mlstm_conv,
              state_ssm, state_ssm_conv, norm_g, w_in, m_conv_w, m_conv_b, m_w_qk, m_w_vo, m_b_o,
              m_w_gate, m_b_gate, m_norm_g, s_conv_w, s_conv_b, s_dt_bias, s_A_log, s_D, s_norm_g,
              c_v_norm_g, c_w_s, c_b_s, w_out, final_norm_g):
    dtype = x_prompt.dtype
    bp = x_prompt.shape[0]
    yp, ysm = x_prompt, x_sample
    outs_p, outs_s = [], []
    for l in range(DEPTH):
        weights = (norm_g[l], w_in[l], m_conv_w[l], m_conv_b[l], m_w_qk[l], m_w_vo[l], m_b_o[l],
                   m_w_gate[l], m_b_gate[l], m_norm_g[l], s_conv_w[l], s_conv_b[l], s_dt_bias[l],
                   s_A_log[l], s_D[l], s_norm_g[l], c_v_norm_g[l], c_w_s[l], c_b_s[l], w_out[l])
        yp, sp = mixer_layer(yp,
                             jnp.zeros((bp, CONV_K - 1, M_WIDTH), dtype),
                             jnp.zeros((bp, M_HEADS, M_DH, M_DH), dtype),
                             jnp.zeros((bp, M_HEADS, M_DH), dtype),
                             jnp.zeros((bp, M_HEADS), dtype),
                             jnp.zeros((bp, CONV_K - 1, S_CONV_DIM), dtype),
                             jnp.zeros((bp, S_HEADS, S_DH, S_STATE), dtype),
                             *weights)
        ysm, ss = mixer_layer(ysm, state_mlstm_conv[l], state_mlstm_C[l], state_mlstm_n[l], state_mlstm_m[l],
                              state_ssm_conv[l], state_ssm[l], *weights)
        outs_p.append(sp)
        outs_s.append(ss)
    p_C, p_n, p_m, p_mconv, p_ssm, p_sconv = [jnp.stack([s[i] for s in outs_p]) for i in range(6)]
    s_C, s_n, s_m, s_mconv, s_ssm, s_sconv, s_cv = [jnp.stack([s[i] for s in outs_s]) for i in range(7)]
    y_prompt = rmsnorm(yp, final_norm_g)
    y_sample = rmsnorm(ysm, final_norm_g)
    return (y_prompt, y_sample, p_C, p_n, p_m, p_mconv, p_ssm, p_sconv,
            s_C, s_n, s_m, s_mconv, s_ssm, s_sconv, s_cv)
```

```python
import functools

import jax
import jax.numpy as jnp
from jax import lax
from jax.experimental import pallas as pl
from jax.experimental.pallas import tpu as pltpu

F32 = jnp.float32
BF16 = jnp.bfloat16

D_MODEL = 2048
MIX_WIDTH = 2 * D_MODEL
M_WIDTH = MIX_WIDTH // 4
M_HEADS = 4
M_DH = M_WIDTH // M_HEADS
S_WIDTH = MIX_WIDTH // 2
S_DH = 64
S_HEADS = S_WIDTH // S_DH
S_GROUPS = 4
S_HPG = S_HEADS // S_GROUPS
S_STATE = 128
S_GW = S_HPG * S_DH
C_WIDTH = MIX_WIDTH // 4
C_GROUPS = 4
C_DG = C_WIDTH // C_GROUPS
C_CHUNK = 128
CONV_K = 4
EPS = 1e-6

LANES = 128
SUBLANES = 8
SAMPLE_ROWS = 8
CHUNK = 128
VMEM_LIMIT = 56 * 1024 * 1024

COL_XM, COL_ZM, COL_ZS, COL_XBC, COL_U, COL_V, COL_ZC = 0, 1024, 2048, 4096, 7168, 8192, 9216
PROJ_MAIN = 10240
COL_B = COL_XBC + S_WIDTH
COL_C = COL_B + S_GROUPS * S_STATE


def _dot(a, b):
    return jnp.dot(a, b, preferred_element_type=F32)


def _dot_nt(a, b):
    return lax.dot_general(a, b, (((1,), (1,)), ((), ())), preferred_element_type=F32)


def _dot_exact(a, b):
    return jnp.dot(a, b, preferred_element_type=F32, precision=lax.Precision.HIGHEST)


def _silu(x):
    return x * jax.nn.sigmoid(x)


def _softplus(x):
    return jnp.maximum(x, 0.0) + jnp.log1p(jnp.exp(-jnp.abs(x)))


def _log_sigmoid(x):
    return jnp.minimum(x, 0.0) - jnp.log1p(jnp.exp(-jnp.abs(x)))


def _params(sem):
    return pltpu.CompilerParams(dimension_semantics=sem, vmem_limit_bytes=VMEM_LIMIT)


def _inproj_kernel(x_ref, g_ref, w_ref, wdt_ref, proj_ref, dt_ref, h_sc, *, sub):
    j = pl.program_id(1)

    @pl.when(j == 0)
    def _():
        def body(r, carry):
            rows = pl.ds(pl.multiple_of(r * sub, sub), sub)
            x = x_ref[rows, :]
            ms = jnp.mean(x * x, axis=-1, keepdims=True)
            h_sc[rows, :] = ((x * lax.rsqrt(ms + EPS)) * g_ref[...]).astype(BF16)
            return carry
        lax.fori_loop(0, x_ref.shape[0] // sub, body, 0)
        dt_ref[...] = _dot(h_sc[...], wdt_ref[...])

    proj_ref[...] = _dot(h_sc[...], w_ref[...])


def _inproj(x, g, w_main, w_dt):
    rows = x.shape[0]
    tm = min(1024, rows)
    tn = 1024
    return pl.pallas_call(
        functools.partial(_inproj_kernel, sub=min(128, tm)),
        grid=(rows // tm, PROJ_MAIN // tn),
        in_specs=[
            pl.BlockSpec((tm, D_MODEL), lambda i, j: (i, 0)),
            pl.BlockSpec((1, D_MODEL), lambda i, j: (0, 0)),
            pl.BlockSpec((D_MODEL, tn), lambda i, j: (0, j)),
            pl.BlockSpec((D_MODEL, S_GROUPS * LANES), lambda i, j: (0, 0)),
        ],
        out_specs=[
            pl.BlockSpec((tm, tn), lambda i, j: (i, j)),
            pl.BlockSpec((tm, S_GROUPS * LANES), lambda i, j: (i, 0)),
        ],
        out_shape=[
            jax.ShapeDtypeStruct((rows, PROJ_MAIN), F32),
            jax.ShapeDtypeStruct((rows, S_GROUPS * LANES), F32),
        ],
        scratch_shapes=[pltpu.VMEM((tm, D_MODEL), BF16)],
        compiler_params=_params(("arbitrary", "arbitrary")),
        name="inproj",
    )(x, g, w_main, w_dt)


def _conv_rows(x, first, xe_sc, cw_ref, cb_ref):
    tb = x.shape[0]

    @pl.when(first)
    def _():
        xe_sc[0:SUBLANES, :] = jnp.zeros((SUBLANES, x.shape[1]), F32)

    @pl.when(jnp.logical_not(first))
    def _():
        xe_sc[0:SUBLANES, :] = xe_sc[tb:tb + SUBLANES, :]

    xe_sc[SUBLANES:SUBLANES + tb, :] = x
    acc = cb_ref[...] + cw_ref[CONV_K - 1:CONV_K, :] * x
    for d in range(1, CONV_K):
        acc = acc + cw_ref[CONV_K - 1 - d:CONV_K - d, :] * xe_sc[SUBLANES - d:SUBLANES - d + tb, :]
    return _silu(acc)


def _mlstm_proj_kernel(*refs, has_hist, blocks_per_seq):
    if has_hist:
        (xm_ref, he_ref, cw_ref, cb_ref, wqk_ref, wvo_ref, wg_ref, bg_ref,
         q_ref, k_ref, v_ref, o_ref, gates_ref, xe_sc) = refs
    else:
        (xm_ref, cw_ref, cb_ref, wqk_ref, wvo_ref, wg_ref, bg_ref,
         q_ref, k_ref, v_ref, o_ref, gates_ref, xe_sc) = refs
    i = pl.program_id(0)
    x = xm_ref[...]
    xin = x + he_ref[...] if has_hist else x
    xmc = _conv_rows(xin, i % blocks_per_seq == 0, xe_sc, cw_ref, cb_ref)
    tb = x.shape[0]
    gates = jnp.broadcast_to(bg_ref[...], (tb, LANES))
    for h in range(M_HEADS):
        cols = slice(h * M_DH, (h + 1) * M_DH)
        qk = _dot(xmc[:, cols].astype(BF16), wqk_ref[h])
        vo = _dot(x[:, cols].astype(BF16), wvo_ref[h])
        qb = qk[:, :M_DH].astype(BF16)
        kb = qk[:, M_DH:].astype(BF16)
        vb = vo[:, :M_DH].astype(BF16)
        gates = gates + _dot(qb, wg_ref[h, 0]) + _dot(kb, wg_ref[h, 1]) + _dot(vb, wg_ref[h, 2])
        q_ref[:, cols] = qb
        k_ref[:, cols] = (qk[:, M_DH:] * (M_DH ** -0.5)).astype(BF16)
        v_ref[:, cols] = vb
        o_ref[:, cols] = vo[:, M_DH:]
    lane = lax.broadcasted_iota(jnp.int32, (tb, LANES), 1)
    gsel = jnp.where(lane < M_HEADS, gates, _log_sigmoid(gates))
    gates_ref[0] = gsel
    for h in range(1, M_HEADS):
        gates_ref[h] = pltpu.roll(gsel, LANES - h, 1)


def _mlstm_proj(proj, row0, rows, hist, cw, cb, wqk, wvo, wg, bg, blocks_per_seq, tb):
    has_hist = hist is not None
    nblk = rows // tb
    off = row0 // tb
    in_specs = [pl.BlockSpec((tb, M_WIDTH), lambda i: (i + off, COL_XM // M_WIDTH))]
    args = [proj]
    if has_hist:
        in_specs.append(pl.BlockSpec((tb, M_WIDTH), lambda i: (i, 0)))
        args.append(hist)
    in_specs += [
        pl.BlockSpec((CONV_K, M_WIDTH), lambda i: (0, 0)),
        pl.BlockSpec((1, M_WIDTH), lambda i: (0, 0)),
        pl.BlockSpec((M_HEADS, M_DH, 2 * M_DH), lambda i: (0, 0, 0)),
        pl.BlockSpec((M_HEADS, M_DH, 2 * M_DH), lambda i: (0, 0, 0)),
        pl.BlockSpec((M_HEADS, 3, M_DH, LANES), lambda i: (0, 0, 0, 0)),
        pl.BlockSpec((1, LANES), lambda i: (0, 0)),
    ]
    args += [cw, cb, wqk, wvo, wg, bg]
    row_spec = pl.BlockSpec((tb, M_WIDTH), lambda i: (i, 0))
    return pl.pallas_call(
        functools.partial(_mlstm_proj_kernel, has_hist=has_hist, blocks_per_seq=blocks_per_seq),
        grid=(nblk,),
        in_specs=in_specs,
        out_specs=[row_spec, row_spec, row_spec, row_spec,
                   pl.BlockSpec((M_HEADS, tb, LANES), lambda i: (0, i, 0))],
        out_shape=[
            jax.ShapeDtypeStruct((rows, M_WIDTH), BF16),
            jax.ShapeDtypeStruct((rows, M_WIDTH), BF16),
            jax.ShapeDtypeStruct((rows, M_WIDTH), BF16),
            jax.ShapeDtypeStruct((rows, M_WIDTH), F32),
            jax.ShapeDtypeStruct((M_HEADS, rows, LANES), F32),
        ],
        scratch_shapes=[pltpu.VMEM((tb + SUBLANES, M_WIDTH), F32)],
        compiler_params=_params(("arbitrary",)),
        name="mlstm_proj",
    )(*args)


def _block_masks(L, rs, t_lo):
    shift = rs.bit_length() - 1
    r2 = lax.broadcasted_iota(jnp.int32, (L, L), 0)
    c2 = lax.broadcasted_iota(jnp.int32, (L, L), 1)
    same = (r2 >> shift) == (c2 >> shift)
    valid_c = (c2 & (rs - 1)) >= t_lo
    ridx = lax.broadcasted_iota(jnp.int32, (L, 1), 0)
    valid_r = (ridx & (rs - 1)) >= t_lo
    return r2, c2, same, valid_c, ridx, valid_r


def _mlstm_rec_kernel(*refs, has_state, nseq, t_lo):
    if has_state:
        (q_ref, k_ref, v_ref, o_ref, zm_ref, gates_ref, bo_ref, ng_ref, c0_ref, n0_ref, m0_ref,
         out_ref, c1_ref, n1_ref, mrow_ref, mprev_sc, col_sc, numi_sc, nrow_sc, wk_sc) = refs
    else:
        (q_ref, k_ref, v_ref, o_ref, zm_ref, gates_ref, bo_ref, ng_ref,
         out_ref, c1_ref, n1_ref, mrow_ref, mprev_sc, col_sc, numi_sc, nrow_sc, wk_sc) = refs
    c = pl.program_id(2)
    L = q_ref.shape[0]
    rs = L // nseq

    @pl.when(c == 0)
    def _():
        if has_state:
            c1_ref[...] = c0_ref[...]
            n1_ref[...] = n0_ref[...]
            mprev_sc[...] = m0_ref[0]
        else:
            c1_ref[...] = jnp.zeros(c1_ref.shape, F32)
            n1_ref[...] = jnp.zeros(n1_ref.shape, F32)
            mprev_sc[...] = jnp.zeros(mprev_sc.shape, F32)

    r2, c2, same, valid_c, ridx, valid_r = _block_masks(L, rs, t_lo)
    eye = r2 == c2
    neg_inf = -jnp.inf
    g = gates_ref[0]
    i_col = g[:, 0:1]
    f_col = jnp.where(valid_r, g[:, M_HEADS:M_HEADS + 1], 0.0)
    i_row = jnp.sum(jnp.where(eye, i_col, 0.0), axis=0, keepdims=True)
    b_row = jnp.sum(jnp.where((r2 <= c2) & same, f_col, 0.0), axis=0, keepdims=True)
    b_col = jnp.sum(jnp.where(eye, b_row, 0.0), axis=1, keepdims=True)
    g_row = i_row - b_row
    g_col = i_col - b_col
    mask = same & (c2 <= r2) & valid_c
    mprev = mprev_sc[...]
    m_col = jnp.maximum(mprev, jnp.max(jnp.where(mask, g_row, neg_inf), axis=1, keepdims=True))
    m_row = jnp.sum(jnp.where(eye, m_col, 0.0), axis=0, keepdims=True)
    mlast_col = jnp.max(jnp.where(same, m_row, neg_inf), axis=1, keepdims=True)
    w_intra = jnp.exp(jnp.where(mask, g_row - m_col, neg_inf))
    w_inter = jnp.exp(mprev - m_col)
    m_new = b_col + m_col
    wl_col = jnp.where(valid_r, jnp.exp(g_col - mlast_col), 0.0)
    col_sc[:, 0:1] = jnp.exp(mprev - mlast_col)

    q = q_ref[...]
    k = k_ref[...]
    v = v_ref[...]
    s = _dot_nt(q, k) * w_intra
    num = _dot(s.astype(BF16), v)
    den = jnp.sum(s, axis=1, keepdims=True)

    wvt = (v.astype(F32) * wl_col).T
    wk_sc[...] = k.astype(F32) * wl_col
    lane = lax.broadcasted_iota(jnp.int32, (1, L), 1)

    def seq_step(j, carry):
        if nseq == 1:
            qj, wvt_j = q, wvt
            rows = slice(0, L)
            last = L - 1
        else:
            in_seq_r = (ridx >= j * rs) & (ridx < (j + 1) * rs)
            in_seq_c = (lane >= j * rs) & (lane < (j + 1) * rs)
            qj = jnp.where(in_seq_r, q, jnp.zeros_like(q))
            wvt_j = jnp.where(in_seq_c, wvt, 0.0)
            rows = pl.ds(pl.multiple_of(j * rs, rs), rs)
            last = j * rs + rs - 1
        cj = c1_ref[j, 0]
        nj = n1_ref[j, 0]
        contrib = _dot_nt(qj, cj.astype(BF16))
        if nseq == 1:
            numi_sc[...] = contrib
        else:
            numi_sc[...] += contrib
        nrow_sc[rows, :] = jnp.broadcast_to(nj, (rs, M_DH))
        wli = col_sc[pl.ds(last, 1), 0:1]
        c1_ref[j, 0] = wli * cj + _dot(wvt_j.astype(BF16), k)
        n1_ref[j, 0] = wli * nj + jnp.sum(wk_sc[rows, :], axis=0, keepdims=True)
        return carry

    if nseq == 1:
        seq_step(0, 0)
    else:
        numi_sc[...] = jnp.zeros(numi_sc.shape, F32)
        lax.fori_loop(0, nseq, seq_step, 0)

    num = num + w_inter * numi_sc[...]
    den = den + w_inter * jnp.sum(q.astype(F32) * nrow_sc[...], axis=1, keepdims=True)
    hh = num / jnp.maximum(jnp.abs(den), jnp.exp(-m_new))
    hm = jax.nn.sigmoid(o_ref[...] + bo_ref[...]) * hh
    hm = hm * lax.rsqrt(jnp.mean(hm * hm, axis=-1, keepdims=True) + EPS) * ng_ref[...]
    out_ref[...] = hm * _silu(zm_ref[...])
    mrow_ref[0] = m_new
    mprev_sc[...] = jnp.broadcast_to(m_new[L - 1:L, :], (L, 1))


def _mlstm_rec(q, k, v, o, gates, proj, row0, bo, ng, state, nb, nseq, nc, t_lo):
    L = CHUNK
    rows = nb * nc * L
    off = row0 // L
    has_state = state is not None
    rowblk = lambda b, h, c: (b * nc + c, h)
    in_specs = [
        pl.BlockSpec((L, M_DH), rowblk),
        pl.BlockSpec((L, M_DH), rowblk),
        pl.BlockSpec((L, M_DH), rowblk),
        pl.BlockSpec((L, M_DH), rowblk),
        pl.BlockSpec((L, M_DH), lambda b, h, c: (b * nc + c + off, COL_ZM // M_DH + h)),
        pl.BlockSpec((1, L, LANES), lambda b, h, c: (h, b * nc + c, 0)),
        pl.BlockSpec((1, M_DH), lambda b, h, c: (0, h)),
        pl.BlockSpec((1, M_DH), lambda b, h, c: (0, h)),
    ]
    args = [q, k, v, o, proj, gates, bo, ng]
    c_spec = pl.BlockSpec((nseq, 1, M_DH, M_DH), lambda b, h, c: (b, h, 0, 0))
    n_spec = pl.BlockSpec((nseq, 1, 1, M_DH), lambda b, h, c: (b, h, 0, 0))
    m_spec = pl.BlockSpec((1, L, 1), lambda b, h, c: (h, b * nc + c, 0))
    if has_state:
        in_specs += [c_spec, n_spec, m_spec]
        args += list(state)
    nbatch = nb * nseq
    return pl.pallas_call(
        functools.partial(_mlstm_rec_kernel, has_state=has_state, nseq=nseq, t_lo=t_lo),
        grid=(nb, M_HEADS, nc),
        in_specs=in_specs,
        out_specs=[pl.BlockSpec((L, M_DH), rowblk), c_spec, n_spec, m_spec],
        out_shape=[
            jax.ShapeDtypeStruct((rows, M_WIDTH), F32),
            jax.ShapeDtypeStruct((nbatch, M_HEADS, M_DH, M_DH), F32),
            jax.ShapeDtypeStruct((nbatch, M_HEADS, 1, M_DH), F32),
            jax.ShapeDtypeStruct((M_HEADS, rows, 1), F32),
        ],
        scratch_shapes=[
            pltpu.VMEM((L, 1), F32),
            pltpu.VMEM((L, LANES), F32),
            pltpu.VMEM((L, M_DH), F32),
            pltpu.VMEM((L, M_DH), F32),
            pltpu.VMEM((L, M_DH), F32),
        ],
        compiler_params=_params(("arbitrary", "arbitrary", "arbitrary")),
        name="mlstm_rec",
    )(*args)


def _ssd_kernel(*refs, has_state, nseq, t_lo):
    if has_state:
        (xs_ref, b_ref, c_ref, zs_ref, dt_ref, hx_ref, hb_ref, hc_ref,
         cwx_ref, cwb_ref, cwc_ref, cbx_ref, cbb_ref, cbc_ref,
         dtb_ref, alog_ref, dskip_ref, ng_ref, h0_ref,
         out_ref, h1_ref, xex_sc, xeb_sc, xec_sc, row_sc, yi_sc, wx_sc, y_sc) = refs
    else:
        (xs_ref, b_ref, c_ref, zs_ref, dt_ref,
         cwx_ref, cwb_ref, cwc_ref, cbx_ref, cbb_ref, cbc_ref,
         dtb_ref, alog_ref, dskip_ref, ng_ref,
         out_ref, h1_ref, xex_sc, xeb_sc, xec_sc, row_sc, yi_sc, wx_sc, y_sc) = refs
    c = pl.program_id(2)
    L = xs_ref.shape[0]
    rs = L // nseq
    first = c == 0

    @pl.when(first)
    def _():
        if has_state:
            h1_ref[...] = h0_ref[...]
        else:
            h1_ref[...] = jnp.zeros(h1_ref.shape, F32)

    x_in = xs_ref[...] + hx_ref[...] if has_state else xs_ref[...]
    b_in = b_ref[...] + hb_ref[...] if has_state else b_ref[...]
    c_in = c_ref[...] + hc_ref[...] if has_state else c_ref[...]
    xs = _conv_rows(x_in, first, xex_sc, cwx_ref, cbx_ref)
    bm = _conv_rows(b_in, first, xeb_sc, cwb_ref, cbb_ref).astype(BF16)
    cm = _conv_rows(c_in, first, xec_sc, cwc_ref, cbc_ref).astype(BF16)

    r2, c2, same, valid_c, ridx, valid_r = _block_masks(L, rs, t_lo)
    eye = r2 == c2
    mask = same & (c2 <= r2) & valid_c
    neg_inf = -jnp.inf
    dt = jnp.where(valid_r, _softplus(dt_ref[...] + dtb_ref[...]), 0.0)
    da = dt * (-jnp.exp(alog_ref[...]))
    cs = _dot_exact(jnp.where(mask, 1.0, 0.0), da)
    sfx = _dot_exact(jnp.where(same & (c2 > r2), 1.0, 0.0), da)
    row_sc[...] = cs + sfx
    ecs = jnp.exp(cs)
    wend = jnp.exp(sfx) * dt

    cb = _dot_nt(cm, bm)
    for hh in range(S_HPG):
        cols = slice(hh * S_DH, (hh + 1) * S_DH)
        cs_c = cs[:, hh:hh + 1]
        cs_r = jnp.sum(jnp.where(eye, cs_c, 0.0), axis=0, keepdims=True)
        dec = jnp.exp(jnp.where(mask, cs_c - cs_r, neg_inf))
        xh = xs[:, cols]
        y_sc[:, cols] = _dot((cb * dec).astype(BF16), (xh * dt[:, hh:hh + 1]).astype(BF16))
        wx_sc[:, cols] = xh * wend[:, hh:hh + 1]
    wxt = wx_sc[...].T
    lane = lax.broadcasted_iota(jnp.int32, (1, L), 1)

    def seq_step(j, carry):
        if nseq == 1:
            cm_j, wxt_j = cm, wxt
            last = L - 1
        else:
            in_seq_r = (ridx >= j * rs) & (ridx < (j + 1) * rs)
            in_seq_c = (lane >= j * rs) & (lane < (j + 1) * rs)
            cm_j = jnp.where(in_seq_r, cm, jnp.zeros_like(cm))
            wxt_j = jnp.where(in_seq_c, wxt, 0.0)
            last = j * rs + rs - 1
        hj = h1_ref[j]
        contrib = _dot_nt(cm_j, hj.reshape(S_GW, S_STATE).astype(BF16))
        if nseq == 1:
            yi_sc[...] = contrib
        else:
            yi_sc[...] += contrib
        upd = _dot(wxt_j.astype(BF16), bm)
        dec_j = jnp.exp(row_sc[pl.ds(last, 1), :])
        for hh in range(S_HPG):
            h1_ref[j, hh] = dec_j[:, hh:hh + 1] * hj[hh] + upd[hh * S_DH:(hh + 1) * S_DH, :]
        return carry

    if nseq == 1:
        seq_step(0, 0)
    else:
        yi_sc[...] = jnp.zeros(yi_sc.shape, F32)
        lax.fori_loop(0, nseq, seq_step, 0)

    for hh in range(S_HPG):
        cols = slice(hh * S_DH, (hh + 1) * S_DH)
        y_sc[:, cols] = (y_sc[:, cols] + ecs[:, hh:hh + 1] * yi_sc[:, cols]
                         + dskip_ref[:, hh:hh + 1] * xs[:, cols])
    y = y_sc[...] * _silu(zs_ref[...])
    out_ref[...] = y * lax.rsqrt(jnp.mean(y * y, axis=-1, keepdims=True) + EPS) * ng_ref[...]


def _ssd(proj, dt_raw, row0, hist, cw, cb, dtb, alog, dskip, ng, h0, nb, nseq, nc, t_lo):
    L = CHUNK
    rows = nb * nc * L
    off = row0 // L
    has_state = h0 is not None
    xblk = COL_XBC // S_GW
    bblk = COL_B // S_STATE
    cblk = COL_C // S_STATE
    hb_blk = S_WIDTH // S_STATE
    hc_blk = hb_blk + S_GROUPS

    def row(b, c):
        return b * nc + c

    in_specs = [
        pl.BlockSpec((L, S_GW), lambda b, g, c: (row(b, c) + off, xblk + g)),
        pl.BlockSpec((L, S_STATE), lambda b, g, c: (row(b, c) + off, bblk + g)),
        pl.BlockSpec((L, S_STATE), lambda b, g, c: (row(b, c) + off, cblk + g)),
        pl.BlockSpec((L, S_GW), lambda b, g, c: (row(b, c) + off, COL_ZS // S_GW + g)),
        pl.BlockSpec((L, LANES), lambda b, g, c: (row(b, c) + off, g)),
    ]
    args = [proj, proj, proj, proj, dt_raw]
    if has_state:
        in_specs += [
            pl.BlockSpec((L, S_GW), lambda b, g, c: (row(b, c), g)),
            pl.BlockSpec((L, S_STATE), lambda b, g, c: (row(b, c), hb_blk + g)),
            pl.BlockSpec((L, S_STATE), lambda b, g, c: (row(b, c), hc_blk + g)),
        ]
        args += [hist, hist, hist]
    in_specs += [
        pl.BlockSpec((CONV_K, S_GW), lambda b, g, c: (0, g)),
        pl.BlockSpec((CONV_K, S_STATE), lambda b, g, c: (0, hb_blk + g)),
        pl.BlockSpec((CONV_K, S_STATE), lambda b, g, c: (0, hc_blk + g)),
        pl.BlockSpec((1, S_GW), lambda b, g, c: (0, g)),
        pl.BlockSpec((1, S_STATE), lambda b, g, c: (0, hb_blk + g)),
        pl.BlockSpec((1, S_STATE), lambda b, g, c: (0, hc_blk + g)),
        pl.BlockSpec((1, LANES), lambda b, g, c: (0, g)),
        pl.BlockSpec((1, LANES), lambda b, g, c: (0, g)),
        pl.BlockSpec((1, LANES), lambda b, g, c: (0, g)),
        pl.BlockSpec((1, S_GW), lambda b, g, c: (0, g)),
    ]
    args += [cw, cw, cw, cb, cb, cb, dtb, alog, dskip, ng]
    h_spec = pl.BlockSpec((nseq, S_HPG, S_DH, S_STATE), lambda b, g, c: (b, g, 0, 0))
    if has_state:
        in_specs.append(h_spec)
        args.append(h0)
    return pl.pallas_call(
        functools.partial(_ssd_kernel, has_state=has_state, nseq=nseq, t_lo=t_lo),
        grid=(nb, S_GROUPS, nc),
        in_specs=in_specs,
        out_specs=[pl.BlockSpec((L, S_GW), lambda b, g, c: (row(b, c), g)), h_spec],
        out_shape=[
            jax.ShapeDtypeStruct((rows, S_WIDTH), F32),
            jax.ShapeDtypeStruct((nb * nseq, S_HEADS, S_DH, S_STATE), F32),
        ],
        scratch_shapes=[
            pltpu.VMEM((L + SUBLANES, S_GW), F32),
            pltpu.VMEM((L + SUBLANES, S_STATE), F32),
            pltpu.VMEM((L + SUBLANES, S_STATE), F32),
            pltpu.VMEM((L, LANES), F32),
            pltpu.VMEM((L, S_GW), F32),
            pltpu.VMEM((L, S_GW), F32),
            pltpu.VMEM((L, S_GW), F32),
        ],
        compiler_params=_params(("arbitrary", "arbitrary", "arbitrary")),
        name="ssd",
    )(*args)


def _cmlp_kernel(u_ref, v_ref, zc_ref, w_ref, bias_ref, gv_ref, out_ref, vn_ref, *, rs, t_lo):
    L = C_CHUNK
    nchunk = u_ref.shape[0] // L
    r2, c2, same, valid_c, _, _ = _block_masks(L, rs, t_lo)
    mask = same & (c2 <= r2) & valid_c
    for g in range(C_GROUPS):
        cols = slice(g * C_DG, (g + 1) * C_DG)
        w = jnp.where(mask, w_ref[g], 0.0).astype(BF16)
        for ch in range(nchunk):
            rows = slice(ch * L, (ch + 1) * L)
            vg = v_ref[rows, cols]
            vn = vg * lax.rsqrt(jnp.mean(vg * vg, axis=-1, keepdims=True) + EPS) * gv_ref[:, cols]
            vn_ref[rows, cols] = vn
            mixed = _dot(w, vn.astype(BF16)) + bias_ref[g]
            out_ref[rows, cols] = u_ref[rows, cols] * mixed * _silu(zc_ref[rows, cols])


def _cmlp(proj, wmix, bias, gv, rs, t_lo):
    rows = proj.shape[0]
    tb = min(512, rows)
    row_spec = pl.BlockSpec((tb, C_WIDTH), lambda i: (i, 0))
    return pl.pallas_call(
        functools.partial(_cmlp_kernel, rs=rs, t_lo=t_lo),
        grid=(rows // tb,),
        in_specs=[
            pl.BlockSpec((tb, C_WIDTH), lambda i: (i, COL_U // C_WIDTH)),
            pl.BlockSpec((tb, C_WIDTH), lambda i: (i, COL_V // C_WIDTH)),
            pl.BlockSpec((tb, C_WIDTH), lambda i: (i, COL_ZC // C_WIDTH)),
            pl.BlockSpec((C_GROUPS, C_CHUNK, C_CHUNK), lambda i: (0, 0, 0)),
            pl.BlockSpec((C_GROUPS, C_CHUNK, 1), lambda i: (0, 0, 0)),
            pl.BlockSpec((1, C_WIDTH), lambda i: (0, 0)),
        ],
        out_specs=[row_spec, row_spec],
        out_shape=[jax.ShapeDtypeStruct((rows, C_WIDTH), F32), jax.ShapeDtypeStruct((rows, C_WIDTH), F32)],
        compiler_params=_params(("arbitrary",)),
        name="cmlp",
    )(proj, proj, proj, wmix, bias, gv)


def _outproj_kernel(x_ref, om_ref, os_ref, oc_ref, wm_ref, ws_ref, wc_ref, fg_ref, y_ref, *, final, rs, t_lo):
    acc = _dot(om_ref[...].astype(BF16), wm_ref[...])
    acc = acc + _dot(os_ref[...].astype(BF16), ws_ref[...])
    acc = acc + _dot(oc_ref[...].astype(BF16), wc_ref[...])
    y = x_ref[...] + acc
    if final:
        y = y * lax.rsqrt(jnp.mean(y * y, axis=-1, keepdims=True) + EPS) * fg_ref[...]
    if t_lo:
        ridx = lax.broadcasted_iota(jnp.int32, (y.shape[0], 1), 0)
        y = jnp.where((ridx & (rs - 1)) >= t_lo, y, 0.0)
    y_ref[...] = y


def _outproj(x, om, os_, oc, wm, ws, wc, fg, final, rs, t_lo):
    rows = x.shape[0]
    tm = min(256, rows)
    const = lambda i: (0, 0)
    return pl.pallas_call(
        functools.partial(_outproj_kernel, final=final, rs=rs, t_lo=t_lo),
        grid=(rows // tm,),
        in_specs=[
            pl.BlockSpec((tm, D_MODEL), lambda i: (i, 0)),
            pl.BlockSpec((tm, M_WIDTH), lambda i: (i, 0)),
            pl.BlockSpec((tm, S_WIDTH), lambda i: (i, 0)),
            pl.BlockSpec((tm, C_WIDTH), lambda i: (i, 0)),
            pl.BlockSpec((M_WIDTH, D_MODEL), const, pipeline_mode=pl.Buffered(1)),
            pl.BlockSpec((S_WIDTH, D_MODEL), const, pipeline_mode=pl.Buffered(1)),
            pl.BlockSpec((C_WIDTH, D_MODEL), const, pipeline_mode=pl.Buffered(1)),
            pl.BlockSpec((1, D_MODEL), const),
        ],
        out_specs=pl.BlockSpec((tm, D_MODEL), lambda i: (i, 0)),
        out_shape=jax.ShapeDtypeStruct((rows, D_MODEL), F32),
        compiler_params=_params(("arbitrary",)),
        name="outproj",
    )(x, om, os_, oc, wm, ws, wc, fg)


def _pad_heads(a):
    lead = a.shape[:-1]
    a = a.reshape(lead + (S_GROUPS, S_HPG))
    a = jnp.pad(a, [(0, 0)] * len(lead) + [(0, 0), (0, LANES - S_HPG)])
    return a.reshape(lead + (S_GROUPS * LANES,))


def _layer_weights(l, norm_g, w_in, m_conv_w, m_conv_b, m_w_qk, m_w_vo, m_b_o, m_w_gate, m_b_gate,
                   m_norm_g, s_conv_w, s_conv_b, s_dt_bias, s_A_log, s_D, s_norm_g, c_v_norm_g,
                   c_w_s, c_b_s, w_out):
    w = w_in[l]
    dt0 = COL_XBC + S_WIDTH + 2 * S_GROUPS * S_STATE
    w_main = jnp.concatenate([w[:, :dt0], w[:, dt0 + S_HEADS:]], axis=1).astype(BF16)
    w_dt = _pad_heads(w[:, dt0:dt0 + S_HEADS]).astype(BF16)
    wg = m_w_gate[l].reshape(M_HEADS, 3, M_DH, 2 * M_HEADS)
    wg = jnp.pad(wg, ((0, 0), (0, 0), (0, 0), (0, LANES - 2 * M_HEADS))).astype(BF16)
    bg = jnp.pad(m_b_gate[l], (0, LANES - 2 * M_HEADS))[None, :]
    wo = w_out[l].astype(BF16)
    return dict(
        norm_g=norm_g[l][None, :], w_main=w_main, w_dt=w_dt,
        m_cw=m_conv_w[l], m_cb=m_conv_b[l][None, :],
        wqk=m_w_qk[l].astype(BF16), wvo=m_w_vo[l].astype(BF16), wg=wg, bg=bg,
        bo=m_b_o[l][None, :], m_ng=m_norm_g[l][None, :],
        s_cw=s_conv_w[l], s_cb=s_conv_b[l][None, :],
        dtb=_pad_heads(s_dt_bias[l])[None, :], alog=_pad_heads(s_A_log[l])[None, :],
        dskip=_pad_heads(s_D[l])[None, :], s_ng=s_norm_g[l][None, :],
        gv=c_v_norm_g[l][None, :], c_w=c_w_s[l], c_b=c_b_s[l],
        wo_m=wo[:M_WIDTH], wo_s=wo[M_WIDTH:M_WIDTH + S_WIDTH], wo_c=wo[M_WIDTH + S_WIDTH:],
    )


def _mixer(x, W, final_g, final, seq_rows, t_lo, hist_m, hist_s, mstate, sstate, cw_mix, cb_mix):
    rows = x.shape[0]
    nseq = max(1, CHUNK // seq_rows)
    nc = max(1, seq_rows // CHUNK)
    nb = rows // (nc * CHUNK)
    proj, dt_raw = _inproj(x, W["norm_g"], W["w_main"], W["w_dt"])
    tb = min(512, rows)
    q, k, v, o, gates = _mlstm_proj(proj, 0, rows, hist_m, W["m_cw"], W["m_cb"], W["wqk"], W["wvo"],
                                    W["wg"], W["bg"], max(1, seq_rows // tb), tb)
    out_m, c1, n1, mrow = _mlstm_rec(q, k, v, o, gates, proj, 0, W["bo"], W["m_ng"], mstate,
                                     nb, nseq, nc, t_lo)
    out_s, h1 = _ssd(proj, dt_raw, 0, hist_s, W["s_cw"], W["s_cb"], W["dtb"], W["alog"], W["dskip"],
                     W["s_ng"], sstate, nb, nseq, nc, t_lo)
    out_c, vn = _cmlp(proj, cw_mix, cb_mix, W["gv"], min(seq_rows, C_CHUNK), t_lo)
    y = _outproj(x, out_m, out_s, out_c, W["wo_m"], W["wo_s"], W["wo_c"], final_g, final,
                 min(seq_rows, CHUNK), t_lo)
    return y, proj, c1, n1, mrow, h1, vn


def kernel(x_prompt, x_sample, state_mlstm_C, state_mlstm_n, state_mlstm_m, state_mlstm_conv, state_ssm, state_ssm_conv, norm_g, w_in, m_conv_w, m_conv_b, m_w_qk, m_w_vo, m_b_o, m_w_gate, m_b_gate, m_norm_g, s_conv_w, s_conv_b, s_dt_bias, s_A_log, s_D, s_norm_g, c_v_norm_g, c_w_s, c_b_s, w_out, final_norm_g):
    bp, seq, _ = x_prompt.shape
    bs, dec_seq, _ = x_sample.shape
    depth = w_in.shape[0]
    t_lo = SAMPLE_ROWS - dec_seq
    hist_lo = t_lo - (CONV_K - 1)
    yp = x_prompt.reshape(bp * seq, D_MODEL)
    ys = jnp.pad(x_sample, ((0, 0), (t_lo, 0), (0, 0))).reshape(bs * SAMPLE_ROWS, D_MODEL)
    fg = final_norm_g[None, :]
    reps = C_CHUNK // SAMPLE_ROWS
    outs_p, outs_s = [], []
    for l in range(depth):
        W = _layer_weights(l, norm_g, w_in, m_conv_w, m_conv_b, m_w_qk, m_w_vo, m_b_o, m_w_gate,
                           m_b_gate, m_norm_g, s_conv_w, s_conv_b, s_dt_bias, s_A_log, s_D,
                           s_norm_g, c_v_norm_g, c_w_s, c_b_s, w_out)
        final = l == depth - 1
        yp, proj_p, c1, n1, mrow, h1, _ = _mixer(
            yp, W, fg, final, seq, 0, None, None, None, None, W["c_w"], W["c_b"][:, :, None])
        pj = proj_p.reshape(bp, seq, PROJ_MAIN)
        outs_p.append((
            c1, n1.reshape(bp, M_HEADS, M_DH), mrow[:, seq - 1::seq, 0].T,
            pj[:, seq - (CONV_K - 1):, COL_XM:COL_XM + M_WIDTH], h1,
            pj[:, seq - (CONV_K - 1):, COL_XBC:COL_XBC + S_WIDTH + 2 * S_GROUPS * S_STATE]))
        pad_hist = ((0, 0), (hist_lo, dec_seq), (0, 0))
        hist_m = jnp.pad(state_mlstm_conv[l], pad_hist).reshape(bs * SAMPLE_ROWS, M_WIDTH)
        hist_s = jnp.pad(state_ssm_conv[l], pad_hist).reshape(bs * SAMPLE_ROWS, -1)
        m0rows = jnp.repeat(state_mlstm_m[l].T[:, :, None], SAMPLE_ROWS, axis=1)
        mstate = (state_mlstm_C[l], state_mlstm_n[l][:, :, None, :], m0rows)
        w4 = jnp.pad(c_w_s[l][:, :dec_seq, :dec_seq], ((0, 0), (t_lo, 0), (t_lo, 0)))
        b4 = jnp.pad(c_b_s[l][:, :dec_seq], ((0, 0), (t_lo, 0)))
        ys, proj_s, c1, n1, mrow, h1, vn = _mixer(
            ys, W, fg, final, SAMPLE_ROWS, t_lo, hist_m, hist_s, mstate, state_ssm[l],
            jnp.tile(w4, (1, reps, reps)), jnp.tile(b4, (1, reps))[:, :, None])
        pj = proj_s.reshape(bs, SAMPLE_ROWS, PROJ_MAIN)
        outs_s.append((
            c1, n1.reshape(bs, M_HEADS, M_DH), mrow[:, SAMPLE_ROWS - 1::SAMPLE_ROWS, 0].T,
            pj[:, SAMPLE_ROWS - (CONV_K - 1):, COL_XM:COL_XM + M_WIDTH], h1,
            pj[:, SAMPLE_ROWS - (CONV_K - 1):, COL_XBC:COL_XBC + S_WIDTH + 2 * S_GROUPS * S_STATE],
            vn.reshape(bs, SAMPLE_ROWS, C_WIDTH)[:, t_lo:]))
    p_out = [jnp.stack([s[i] for s in outs_p]) for i in range(6)]
    s_out = [jnp.stack([s[i] for s in outs_s]) for i in range(7)]
    y_prompt = yp.reshape(bp, seq, D_MODEL)
    y_sample = ys.reshape(bs, SAMPLE_ROWS, D_MODEL)[:, t_lo:]
    return (y_prompt, y_sample, *p_out, *s_out)
```

```python
import functools

import jax
import jax.numpy as jnp
from jax import lax
from jax.experimental import pallas as pl
from jax.experimental.pallas import tpu as pltpu

F32 = jnp.float32
BF16 = jnp.bfloat16

D_MODEL = 2048
MIX_WIDTH = 2 * D_MODEL
M_WIDTH = MIX_WIDTH // 4
M_HEADS = 4
M_DH = M_WIDTH // M_HEADS
S_WIDTH = MIX_WIDTH // 2
S_DH = 64
S_HEADS = S_WIDTH // S_DH
S_GROUPS = 4
S_HPG = S_HEADS // S_GROUPS
S_STATE = 128
S_GW = S_HPG * S_DH
S_BC = S_GROUPS * S_STATE
C_WIDTH = MIX_WIDTH // 4
C_GROUPS = 4
C_DG = C_WIDTH // C_GROUPS
C_CHUNK = 128
CONV_K = 4
EPS = 1e-6

LANES = 128
SUBLANES = 8
SAMPLE_ROWS = 8
CHUNK = 128
M_CHUNK = 256
VMEM_LIMIT = 56 * 1024 * 1024

COL_XM, COL_ZM, COL_ZS, COL_XBC, COL_U, COL_V, COL_ZC = 0, 1024, 2048, 4096, 7168, 8192, 9216
PROJ_MAIN = 10240
PROJ_HEAD = COL_U
COL_B = COL_XBC + S_WIDTH
COL_C = COL_B + S_BC
DT_COLS = LANES + S_GROUPS * LANES


def _dot(a, b):
    return jnp.dot(a, b, preferred_element_type=F32)


def _dot_nt(a, b):
    return lax.dot_general(a, b, (((1,), (1,)), ((), ())), preferred_element_type=F32)


def _dot_mask(mask, x):
    m = jnp.where(mask, 1.0, 0.0).astype(BF16)
    hi = x.astype(BF16)
    r1 = x - hi.astype(F32)
    mid = r1.astype(BF16)
    lo = (r1 - mid.astype(F32)).astype(BF16)
    return _dot(m, hi) + _dot(m, mid) + _dot(m, lo)


def _sigmoid(x):
    return 0.5 * jnp.tanh(0.5 * x) + 0.5


def _silu(x):
    h = 0.5 * x
    return h * jnp.tanh(h) + h


def _softplus(x):
    return jnp.maximum(x, 0.0) + jnp.log1p(jnp.exp(-jnp.abs(x)))


def _log_sigmoid(x):
    return jnp.minimum(x, 0.0) - jnp.log1p(jnp.exp(-jnp.abs(x)))


def _rms(x, g):
    return x * lax.rsqrt(jnp.mean(x * x, axis=-1, keepdims=True) + EPS) * g


def _params(sem):
    return pltpu.CompilerParams(dimension_semantics=sem, vmem_limit_bytes=VMEM_LIMIT)


def _vec_spec(l, n, col=None):
    if col is None:
        return pl.BlockSpec((None, 1, n), lambda *ids: (l, 0, 0))
    return pl.BlockSpec((None, 1, n), lambda *ids: (l, 0, col(*ids)))


def _inproj_kernel(x_ref, g_ref, wa_ref, wb_ref, wdt_ref, proj_ref, dt_ref, h_sc, *, sub, n_head):
    j = pl.program_id(1)

    @pl.when(j == 0)
    def _():
        def body(r, carry):
            rows = pl.ds(pl.multiple_of(r * sub, sub), sub)
            h_sc[rows, :] = _rms(x_ref[rows, :], g_ref[...]).astype(BF16)
            return carry
        lax.fori_loop(0, x_ref.shape[0] // sub, body, 0)
        dt_ref[...] = _dot(h_sc[...], wdt_ref[...])

    @pl.when(j < n_head)
    def _():
        proj_ref[...] = _dot(h_sc[...], wa_ref[...])

    @pl.when(j >= n_head)
    def _():
        proj_ref[...] = _dot(h_sc[...], wb_ref[...])


def _inproj(l, x, g, w_all, w_tail, w_dt):
    rows = x.shape[0]
    tm = min(1024, rows)
    tn = 1024
    n_head = PROJ_HEAD // tn
    return pl.pallas_call(
        functools.partial(_inproj_kernel, sub=min(128, tm), n_head=n_head),
        grid=(rows // tm, PROJ_MAIN // tn),
        in_specs=[
            pl.BlockSpec((tm, D_MODEL), lambda i, j: (i, 0)),
            _vec_spec(l, D_MODEL),
            pl.BlockSpec((None, D_MODEL, tn), lambda i, j: (l, 0, jnp.minimum(j, n_head - 1))),
            pl.BlockSpec((None, D_MODEL, tn), lambda i, j: (l, 0, jnp.maximum(j - n_head, 0))),
            pl.BlockSpec((None, D_MODEL, DT_COLS), lambda i, j: (l, 0, 0)),
        ],
        out_specs=[
            pl.BlockSpec((tm, tn), lambda i, j: (i, j)),
            pl.BlockSpec((tm, DT_COLS), lambda i, j: (i, 0)),
        ],
        out_shape=[
            jax.ShapeDtypeStruct((rows, PROJ_MAIN), F32),
            jax.ShapeDtypeStruct((rows, DT_COLS), F32),
        ],
        scratch_shapes=[pltpu.VMEM((tm, D_MODEL), BF16)],
        compiler_params=_params(("arbitrary", "arbitrary")),
        name="inproj",
    )(x, g, w_all, w_tail, w_dt)


def _conv_rows(x, first, prev_sc, cw_ref, cb_ref):
    tb = x.shape[0]

    @pl.when(first)
    def _():
        prev_sc[...] = jnp.zeros(prev_sc.shape, F32)

    xe = jnp.concatenate([prev_sc[...], x], axis=0)
    acc = cb_ref[...] + cw_ref[CONV_K - 1:CONV_K, :] * x
    for d in range(1, CONV_K):
        acc = acc + cw_ref[CONV_K - 1 - d:CONV_K - d, :] * pltpu.roll(xe, d, 0)[SUBLANES:, :]
    prev_sc[...] = x[tb - SUBLANES:, :]
    return _silu(acc)


def _mlstm_proj_kernel(*refs, has_hist, blocks_per_seq):
    if has_hist:
        (xm_ref, he_ref, cw_ref, cb_ref, wqk_ref, wvo_ref, wg_ref, bg_ref,
         q_ref, k_ref, v_ref, o_ref, gates_ref, xe_sc) = refs
    else:
        (xm_ref, cw_ref, cb_ref, wqk_ref, wvo_ref, wg_ref, bg_ref,
         q_ref, k_ref, v_ref, o_ref, gates_ref, xe_sc) = refs
    i = pl.program_id(0)
    x = xm_ref[...]
    xin = x + he_ref[...] if has_hist else x
    xmc = _conv_rows(xin, i % blocks_per_seq == 0, xe_sc, cw_ref, cb_ref)
    tb = x.shape[0]
    gates = jnp.broadcast_to(bg_ref[...], (tb, LANES))
    for h in range(M_HEADS):
        cols = slice(h * M_DH, (h + 1) * M_DH)
        qk = _dot(xmc[:, cols].astype(BF16), wqk_ref[h])
        vo = _dot(x[:, cols].astype(BF16), wvo_ref[h])
        qb = qk[:, :M_DH].astype(BF16)
        kb = qk[:, M_DH:].astype(BF16)
        vb = vo[:, :M_DH].astype(BF16)
        gates = gates + _dot(qb, wg_ref[h, 0]) + _dot(kb, wg_ref[h, 1]) + _dot(vb, wg_ref[h, 2])
        q_ref[:, cols] = qb
        k_ref[:, cols] = (qk[:, M_DH:] * (M_DH ** -0.5)).astype(BF16)
        v_ref[:, cols] = vb
        o_ref[:, cols] = vo[:, M_DH:]
    lane = lax.broadcasted_iota(jnp.int32, (tb, LANES), 1)
    gates_ref[...] = jnp.where(lane < M_HEADS, gates, _log_sigmoid(gates))


def _mlstm_proj(l, proj, hist, W, blocks_per_seq, tb):
    rows = proj.shape[0]
    has_hist = hist is not None
    in_specs = [pl.BlockSpec((tb, M_WIDTH), lambda i: (i, COL_XM // M_WIDTH))]
    args = [proj]
    if has_hist:
        in_specs.append(pl.BlockSpec((None, tb, M_WIDTH), lambda i: (l, i, 0)))
        args.append(hist)
    in_specs += [
        pl.BlockSpec((None, CONV_K, M_WIDTH), lambda i: (l, 0, 0)),
        _vec_spec(l, M_WIDTH),
        pl.BlockSpec((None, M_HEADS, M_DH, 2 * M_DH), lambda i: (l, 0, 0, 0)),
        pl.BlockSpec((None, M_HEADS, M_DH, 2 * M_DH), lambda i: (l, 0, 0, 0)),
        pl.BlockSpec((None, M_HEADS, 3, M_DH, LANES), lambda i: (l, 0, 0, 0, 0)),
        _vec_spec(l, LANES),
    ]
    args += [W["m_cw"], W["m_cb"], W["wqk"], W["wvo"], W["wg"], W["bg"]]
    row_spec = pl.BlockSpec((tb, M_WIDTH), lambda i: (i, 0))
    return pl.pallas_call(
        functools.partial(_mlstm_proj_kernel, has_hist=has_hist, blocks_per_seq=blocks_per_seq),
        grid=(rows // tb,),
        in_specs=in_specs,
        out_specs=[row_spec, row_spec, row_spec, row_spec, pl.BlockSpec((tb, LANES), lambda i: (i, 0))],
        out_shape=[
            jax.ShapeDtypeStruct((rows, M_WIDTH), BF16),
            jax.ShapeDtypeStruct((rows, M_WIDTH), BF16),
            jax.ShapeDtypeStruct((rows, M_WIDTH), BF16),
            jax.ShapeDtypeStruct((rows, M_WIDTH), F32),
            jax.ShapeDtypeStruct((rows, LANES), F32),
        ],
        scratch_shapes=[pltpu.VMEM((SUBLANES, M_WIDTH), F32)],
        compiler_params=_params(("arbitrary",)),
        name="mlstm_proj",
    )(*args)


def _mlstm_finish(hh, o, bo, ng, zm):
    hm = _sigmoid(o + bo) * hh
    return _rms(hm, ng) * _silu(zm)


def _mlstm_prompt_kernel(q_ref, k_ref, v_ref, o_ref, zm_ref, gates_ref, bo_ref, ng_ref,
                         out_ref, c1_ref, n1_ref, m1_ref):
    c = pl.program_id(1)
    L = q_ref.shape[0]

    @pl.when(c == 0)
    def _():
        c1_ref[...] = jnp.zeros(c1_ref.shape, F32)
        n1_ref[...] = jnp.zeros(n1_ref.shape, F32)
        m1_ref[...] = jnp.zeros(m1_ref.shape, F32)

    r2 = lax.broadcasted_iota(jnp.int32, (L, L), 0)
    c2 = lax.broadcasted_iota(jnp.int32, (L, L), 1)
    causal = c2 <= r2
    neg_inf = -jnp.inf
    g = gates_ref[...]
    b_all = _dot_mask(causal, g)
    g_t = g.T
    b_t = b_all.T
    for h in range(M_HEADS):
        cols = slice(h * M_DH, (h + 1) * M_DH)
        f = M_HEADS + h
        g_row = g_t[h:h + 1, :] - b_t[f:f + 1, :]
        b_col = b_all[:, f:f + 1]
        g_col = g[:, h:h + 1] - b_col
        mprev = m1_ref[0, h:h + 1, 0:1]
        gm = jnp.where(causal, g_row, neg_inf)
        m_col = jnp.maximum(mprev, jnp.max(gm, axis=1, keepdims=True))
        w_intra = jnp.exp(gm - m_col)
        w_inter = jnp.exp(mprev - m_col)
        m_new = b_col + m_col
        m_last = m_col[L - 1:L, :]
        wl_col = jnp.exp(g_col - m_last)
        wli = jnp.exp(mprev - m_last)
        q = q_ref[:, cols]
        k = k_ref[:, cols]
        v = v_ref[:, cols]
        cst = c1_ref[0, h]
        nst = n1_ref[0, h]
        s = _dot_nt(q, k) * w_intra
        num = _dot(s.astype(BF16), v) + w_inter * _dot_nt(q, cst.astype(BF16))
        den = (jnp.sum(s, axis=1, keepdims=True)
               + w_inter * jnp.sum(q.astype(F32) * nst, axis=1, keepdims=True))
        hh = num / jnp.maximum(jnp.abs(den), jnp.exp(-m_new))
        out_ref[:, cols] = _mlstm_finish(hh, o_ref[:, cols], bo_ref[:, cols], ng_ref[:, cols], zm_ref[:, cols])
        c1_ref[0, h] = wli * cst + _dot((v.astype(F32) * wl_col).T.astype(BF16), k)
        n1_ref[0, h] = wli * nst + jnp.sum(k.astype(F32) * wl_col, axis=0, keepdims=True)
        m1_ref[0, h:h + 1, :] = jnp.broadcast_to(m_new[L - 1:L, :], (1, LANES))


def _mlstm_prompt(l, q, k, v, o, gates, proj, W, nb, nc):
    L = M_CHUNK
    rows = nb * nc * L
    rowblk = lambda b, c: (b * nc + c, 0)
    return pl.pallas_call(
        _mlstm_prompt_kernel,
        grid=(nb, nc),
        in_specs=[
            pl.BlockSpec((L, M_WIDTH), rowblk),
            pl.BlockSpec((L, M_WIDTH), rowblk),
            pl.BlockSpec((L, M_WIDTH), rowblk),
            pl.BlockSpec((L, M_WIDTH), rowblk),
            pl.BlockSpec((L, M_WIDTH), lambda b, c: (b * nc + c, COL_ZM // M_WIDTH)),
            pl.BlockSpec((L, LANES), rowblk),
            _vec_spec(l, M_WIDTH),
            _vec_spec(l, M_WIDTH),
        ],
        out_specs=[
            pl.BlockSpec((L, M_WIDTH), rowblk),
            pl.BlockSpec((1, M_HEADS, M_DH, M_DH), lambda b, c: (b, 0, 0, 0)),
            pl.BlockSpec((1, M_HEADS, 1, M_DH), lambda b, c: (b, 0, 0, 0)),
            pl.BlockSpec((1, SUBLANES, LANES), lambda b, c: (b, 0, 0)),
        ],
        out_shape=[
            jax.ShapeDtypeStruct((rows, M_WIDTH), F32),
            jax.ShapeDtypeStruct((nb, M_HEADS, M_DH, M_DH), F32),
            jax.ShapeDtypeStruct((nb, M_HEADS, 1, M_DH), F32),
            jax.ShapeDtypeStruct((nb, SUBLANES, LANES), F32),
        ],
        compiler_params=_params(("arbitrary", "arbitrary")),
        name="mlstm_prompt",
    )(q, k, v, o, proj, gates, W["bo"], W["m_ng"])


def _block_masks(L, rs, t_lo):
    shift = rs.bit_length() - 1
    r2 = lax.broadcasted_iota(jnp.int32, (L, L), 0)
    c2 = lax.broadcasted_iota(jnp.int32, (L, L), 1)
    same = (r2 >> shift) == (c2 >> shift)
    valid_c = (c2 & (rs - 1)) >= t_lo
    ridx = lax.broadcasted_iota(jnp.int32, (L, 1), 0)
    valid_r = (ridx & (rs - 1)) >= t_lo
    return r2, c2, same, valid_c, ridx, valid_r


def _mlstm_sample_kernel(q_ref, k_ref, v_ref, o_ref, zm_ref, gates_ref, bo_ref, ng_ref,
                         c0_ref, n0_ref, m0_ref, alias_ref,
                         out_ref, c1_ref, n1_ref, mrow_ref, col_sc, numi_sc, nrow_sc, wk_sc, *, nseq, t_lo):
    del alias_ref
    h = pl.program_id(1)
    L = q_ref.shape[0]
    rs = L // nseq
    r2, c2, same, valid_c, ridx, valid_r = _block_masks(L, rs, t_lo)
    eye = r2 == c2
    neg_inf = -jnp.inf
    g = gates_ref[...]
    lane_g = lax.broadcasted_iota(jnp.int32, (1, LANES), 1)
    i_col = jnp.sum(jnp.where(lane_g == h, g, 0.0), axis=1, keepdims=True)
    f_col = jnp.sum(jnp.where(lane_g == h + M_HEADS, g, 0.0), axis=1, keepdims=True)
    f_col = jnp.where(valid_r, f_col, 0.0)
    i_row = jnp.sum(jnp.where(eye, i_col, 0.0), axis=0, keepdims=True)
    b_row = jnp.sum(jnp.where((r2 <= c2) & same, f_col, 0.0), axis=0, keepdims=True)
    b_col = jnp.sum(jnp.where(eye, b_row, 0.0), axis=1, keepdims=True)
    g_row = i_row - b_row
    g_col = i_col - b_col
    mask = same & (c2 <= r2) & valid_c
    mprev = m0_ref[0]
    m_col = jnp.maximum(mprev, jnp.max(jnp.where(mask, g_row, neg_inf), axis=1, keepdims=True))
    m_row = jnp.sum(jnp.where(eye, m_col, 0.0), axis=0, keepdims=True)
    mlast_col = jnp.max(jnp.where(same, m_row, neg_inf), axis=1, keepdims=True)
    w_intra = jnp.exp(jnp.where(mask, g_row - m_col, neg_inf))
    w_inter = jnp.exp(mprev - m_col)
    m_new = b_col + m_col
    wl_col = jnp.where(valid_r, jnp.exp(g_col - mlast_col), 0.0)
    col_sc[:, 0:1] = jnp.exp(mprev - mlast_col)

    q = q_ref[...]
    k = k_ref[...]
    v = v_ref[...]
    s = _dot_nt(q, k) * w_intra
    num = _dot(s.astype(BF16), v)
    den = jnp.sum(s, axis=1, keepdims=True)
    wvt = (v.astype(F32) * wl_col).T
    wk_sc[...] = k.astype(F32) * wl_col
    lane = lax.broadcasted_iota(jnp.int32, (1, L), 1)
    numi_sc[...] = jnp.zeros(numi_sc.shape, F32)

    def seq_step(j, carry):
        in_seq_r = (ridx >= j * rs) & (ridx < (j + 1) * rs)
        in_seq_c = (lane >= j * rs) & (lane < (j + 1) * rs)
        rows = pl.ds(pl.multiple_of(j * rs, rs), rs)
        cj = c0_ref[j, 0]
        nj = n0_ref[j, 0]
        numi_sc[...] += _dot_nt(jnp.where(in_seq_r, q, jnp.zeros_like(q)), cj.astype(BF16))
        nrow_sc[rows, :] = jnp.broadcast_to(nj, (rs, M_DH))
        wli = col_sc[pl.ds(j * rs + rs - 1, 1), 0:1]
        c1_ref[j, 0] = wli * cj + _dot(jnp.where(in_seq_c, wvt, 0.0).astype(BF16), k)
        n1_ref[j, 0] = wli * nj + jnp.sum(wk_sc[rows, :], axis=0, keepdims=True)
        return carry

    lax.fori_loop(0, nseq, seq_step, 0)
    num = num + w_inter * numi_sc[...]
    den = den + w_inter * jnp.sum(q.astype(F32) * nrow_sc[...], axis=1, keepdims=True)
    hh = num / jnp.maximum(jnp.abs(den), jnp.exp(-m_new))
    out_ref[...] = _mlstm_finish(hh, o_ref[...], bo_ref[...], ng_ref[...], zm_ref[...])
    mrow_ref[0] = m_new


def _mlstm_sample(l, depth, q, k, v, o, gates, proj, W, c0, n0, m0rows, c_prev, nseq, t_lo):
    L = CHUNK
    rows = q.shape[0]
    nb = rows // L
    nbatch = nb * nseq
    rowblk = lambda b, h: (b, h)
    c_spec = pl.BlockSpec((None, nseq, 1, M_DH, M_DH), lambda b, h: (l, b, h, 0, 0))
    n_spec_in = pl.BlockSpec((None, nseq, 1, 1, M_DH), lambda b, h: (l, b, h, 0, 0))
    n_spec_out = pl.BlockSpec((nseq, 1, 1, M_DH), lambda b, h: (b, h, 0, 0))
    has_prev = c_prev is not None
    in_specs = [
        pl.BlockSpec((L, M_DH), rowblk),
        pl.BlockSpec((L, M_DH), rowblk),
        pl.BlockSpec((L, M_DH), rowblk),
        pl.BlockSpec((L, M_DH), rowblk),
        pl.BlockSpec((L, M_DH), lambda b, h: (b, COL_ZM // M_DH + h)),
        pl.BlockSpec((L, LANES), lambda b, h: (b, 0)),
        _vec_spec(l, M_DH, lambda b, h: h),
        _vec_spec(l, M_DH, lambda b, h: h),
        c_spec,
        n_spec_in,
        pl.BlockSpec((None, 1, L, 1), lambda b, h: (l, h, b, 0)),
        pl.BlockSpec(memory_space=pl.ANY),
    ]
    args = [q, k, v, o, proj, gates, W["bo"], W["m_ng"], c0, n0, m0rows,
            c_prev if has_prev else jnp.zeros((SUBLANES, LANES), F32)]
    return pl.pallas_call(
        functools.partial(_mlstm_sample_kernel, nseq=nseq, t_lo=t_lo),
        grid=(nb, M_HEADS),
        in_specs=in_specs,
        out_specs=[pl.BlockSpec((L, M_DH), rowblk), c_spec, n_spec_out,
                   pl.BlockSpec((1, L, 1), lambda b, h: (h, b, 0))],
        out_shape=[
            jax.ShapeDtypeStruct((rows, M_WIDTH), F32),
            jax.ShapeDtypeStruct((depth, nbatch, M_HEADS, M_DH, M_DH), F32),
            jax.ShapeDtypeStruct((nbatch, M_HEADS, 1, M_DH), F32),
            jax.ShapeDtypeStruct((M_HEADS, rows, 1), F32),
        ],
        scratch_shapes=[
            pltpu.VMEM((L, LANES), F32),
            pltpu.VMEM((L, M_DH), F32),
            pltpu.VMEM((L, M_DH), F32),
            pltpu.VMEM((L, M_DH), F32),
        ],
        input_output_aliases={11: 1} if has_prev else {},
        compiler_params=_params(("arbitrary", "arbitrary")),
        name="mlstm_sample",
    )(*args)


def _ssd_prompt_kernel(xs_ref, b_ref, c_ref, zs_ref, dt_ref,
                       cwx_ref, cwb_ref, cwc_ref, cbx_ref, cbb_ref, cbc_ref,
                       dtb_ref, alog_ref, dskip_ref, ng_ref, expand_ref,
                       out_ref, h1_ref, xex_sc, xeb_sc, xec_sc, ht_sc):
    c = pl.program_id(1)
    nc = pl.num_programs(1)
    L = xs_ref.shape[0]
    first = c == 0

    @pl.when(first)
    def _():
        ht_sc[...] = jnp.zeros(ht_sc.shape, F32)

    xs = _conv_rows(xs_ref[...], first, xex_sc, cwx_ref, cbx_ref)
    bm = _conv_rows(b_ref[...], first, xeb_sc, cwb_ref, cbb_ref)
    cm = _conv_rows(c_ref[...], first, xec_sc, cwc_ref, cbc_ref).astype(BF16)

    r2 = lax.broadcasted_iota(jnp.int32, (L, L), 0)
    c2 = lax.broadcasted_iota(jnp.int32, (L, L), 1)
    causal = c2 <= r2
    neg_inf = -jnp.inf
    dt = _softplus(dt_ref[...] + dtb_ref[...])
    da = dt * (-jnp.exp(alog_ref[...]))
    cs = _dot_mask(causal, da)
    cs_t = cs.T
    ecs = jnp.exp(cs)
    wend = jnp.exp(cs[L - 1:L, :] - cs) * dt

    expand = expand_ref[...]

    def widen(a):
        hi = a.astype(BF16)
        lo = (a - hi.astype(F32)).astype(BF16)
        return _dot(hi, expand) + _dot(lo, expand)

    dte = widen(dt)
    wende = widen(wend)
    ecse = widen(ecs)
    xdt = (xs * dte).astype(BF16)
    wx = (xs * wende).astype(BF16)
    lane = lax.broadcasted_iota(jnp.int32, (1, LANES), 1)
    low_half = lane < S_DH
    zero_slab = jnp.zeros((L, LANES), BF16)
    for g in range(S_GROUPS):
        gcols = slice(g * S_GW, (g + 1) * S_GW)
        scols = slice(g * S_STATE, (g + 1) * S_STATE)
        bg = bm[:, scols]
        cg = cm[:, scols]
        cb = _dot_nt(cg, bg.astype(BF16))
        pairs = []
        for pr in range(S_HPG // 2):
            h0 = g * S_HPG + 2 * pr
            mixes = []
            for hh in (h0, h0 + 1):
                dec = jnp.exp(jnp.where(causal, cs[:, hh:hh + 1] - cs_t[hh:hh + 1, :], neg_inf))
                mixes.append((cb * dec).astype(BF16))
            slab = xdt[:, h0 * S_DH:(h0 + 2) * S_DH]
            rhs = jnp.concatenate([jnp.where(low_half, slab, zero_slab),
                                   jnp.where(low_half, zero_slab, slab)], axis=0)
            pairs.append(_dot(jnp.concatenate(mixes, axis=1), rhs))
        y_intra = jnp.concatenate(pairs, axis=1)
        ht = ht_sc[:, gcols]
        y = y_intra + ecse[:, gcols] * _dot(cg, ht.astype(BF16)) + dskip_ref[:, gcols] * xs[:, gcols]
        y = y * _silu(zs_ref[:, gcols])
        out_ref[:, gcols] = _rms(y, ng_ref[:, gcols])
        ht_sc[:, gcols] = ecse[L - 1:L, gcols] * ht + _dot(bg.T.astype(BF16), wx[:, gcols])

    @pl.when(c == nc - 1)
    def _():
        for pr in range(S_HEADS // 2):
            blk = ht_sc[:, pr * LANES:(pr + 1) * LANES].T
            h1_ref[0, 2 * pr] = blk[:S_DH]
            h1_ref[0, 2 * pr + 1] = blk[S_DH:]


def _ssd_prompt(l, proj, dt_raw, W, nb, nc):
    L = CHUNK
    rows = nb * nc * L
    row = lambda b, c: b * nc + c
    conv_w = lambda width, blk: pl.BlockSpec((None, CONV_K, width), lambda b, c: (l, 0, blk))
    return pl.pallas_call(
        _ssd_prompt_kernel,
        grid=(nb, nc),
        in_specs=[
            pl.BlockSpec((L, S_WIDTH), lambda b, c: (row(b, c), COL_XBC // S_WIDTH)),
            pl.BlockSpec((L, S_BC), lambda b, c: (row(b, c), COL_B // S_BC)),
            pl.BlockSpec((L, S_BC), lambda b, c: (row(b, c), COL_C // S_BC)),
            pl.BlockSpec((L, S_WIDTH), lambda b, c: (row(b, c), COL_ZS // S_WIDTH)),
            pl.BlockSpec((L, LANES), lambda b, c: (row(b, c), 0)),
            conv_w(S_WIDTH, 0), conv_w(S_BC, S_WIDTH // S_BC), conv_w(S_BC, S_WIDTH // S_BC + 1),
            _vec_spec(l, S_WIDTH, lambda b, c: 0),
            _vec_spec(l, S_BC, lambda b, c: S_WIDTH // S_BC),
            _vec_spec(l, S_BC, lambda b, c: S_WIDTH // S_BC + 1),
            _vec_spec(l, LANES, lambda b, c: 0),
            _vec_spec(l, LANES, lambda b, c: 0),
            _vec_spec(l, S_WIDTH),
            _vec_spec(l, S_WIDTH),
            pl.BlockSpec((LANES, S_WIDTH), lambda b, c: (0, 0)),
        ],
        out_specs=[
            pl.BlockSpec((L, S_WIDTH), lambda b, c: (row(b, c), 0)),
            pl.BlockSpec((1, S_HEADS, S_DH, S_STATE), lambda b, c: (b, 0, 0, 0)),
        ],
        out_shape=[
            jax.ShapeDtypeStruct((rows, S_WIDTH), F32),
            jax.ShapeDtypeStruct((nb, S_HEADS, S_DH, S_STATE), F32),
        ],
        scratch_shapes=[
            pltpu.VMEM((SUBLANES, S_WIDTH), F32),
            pltpu.VMEM((SUBLANES, S_BC), F32),
            pltpu.VMEM((SUBLANES, S_BC), F32),
            pltpu.VMEM((S_STATE, S_WIDTH), F32),
        ],
        compiler_params=_params(("arbitrary", "arbitrary")),
        name="ssd_prompt",
    )(proj, proj, proj, proj, dt_raw, W["s_cw"], W["s_cw"], W["s_cw"], W["s_cb"], W["s_cb"], W["s_cb"],
      W["dtb"], W["alog"], W["dskip_wide"], W["s_ng"], W["expand"])


def _ssd_sample_kernel(xs_ref, b_ref, c_ref, zs_ref, dt_ref, hx_ref, hb_ref, hc_ref,
                       cwx_ref, cwb_ref, cwc_ref, cbx_ref, cbb_ref, cbc_ref,
                       dtb_ref, alog_ref, dskip_ref, ng_ref, h0_ref, alias_ref,
                       out_ref, h1_ref, xex_sc, xeb_sc, xec_sc, row_sc, yi_sc, wx_sc, y_sc, *, nseq, t_lo):
    del alias_ref
    L = xs_ref.shape[0]
    rs = L // nseq
    first = True
    xs = _conv_rows(xs_ref[...] + hx_ref[...], first, xex_sc, cwx_ref, cbx_ref)
    bm = _conv_rows(b_ref[...] + hb_ref[...], first, xeb_sc, cwb_ref, cbb_ref).astype(BF16)
    cm = _conv_rows(c_ref[...] + hc_ref[...], first, xec_sc, cwc_ref, cbc_ref).astype(BF16)

    r2, c2, same, valid_c, ridx, valid_r = _block_masks(L, rs, t_lo)
    eye = r2 == c2
    mask = same & (c2 <= r2) & valid_c
    neg_inf = -jnp.inf
    dt = jnp.where(valid_r, _softplus(dt_ref[...] + dtb_ref[...]), 0.0)
    da = dt * (-jnp.exp(alog_ref[...]))
    cs = _dot_mask(mask, da)
    sfx = _dot_mask(same & (c2 > r2), da)
    row_sc[...] = cs + sfx
    ecs = jnp.exp(cs)
    wend = jnp.exp(sfx) * dt

    cb = _dot_nt(cm, bm)
    for hh in range(S_HPG):
        cols = slice(hh * S_DH, (hh + 1) * S_DH)
        cs_c = cs[:, hh:hh + 1]
        cs_r = jnp.sum(jnp.where(eye, cs_c, 0.0), axis=0, keepdims=True)
        dec = jnp.exp(jnp.where(mask, cs_c - cs_r, neg_inf))
        xh = xs[:, cols]
        y_sc[:, cols] = _dot((cb * dec).astype(BF16), (xh * dt[:, hh:hh + 1]).astype(BF16))
        wx_sc[:, cols] = xh * wend[:, hh:hh + 1]
    wxt = wx_sc[...].T
    lane = lax.broadcasted_iota(jnp.int32, (1, L), 1)
    yi_sc[...] = jnp.zeros(yi_sc.shape, F32)

    def seq_step(j, carry):
        in_seq_r = (ridx >= j * rs) & (ridx < (j + 1) * rs)
        in_seq_c = (lane >= j * rs) & (lane < (j + 1) * rs)
        hj = h0_ref[j]
        yi_sc[...] += _dot_nt(jnp.where(in_seq_r, cm, jnp.zeros_like(cm)),
                              hj.reshape(S_GW, S_STATE).astype(BF16))
        upd = _dot(jnp.where(in_seq_c, wxt, 0.0).astype(BF16), bm)
        dec_j = jnp.exp(row_sc[pl.ds(j * rs + rs - 1, 1), :])
        for hh in range(S_HPG):
            h1_ref[j, hh] = dec_j[:, hh:hh + 1] * hj[hh] + upd[hh * S_DH:(hh + 1) * S_DH, :]
        return carry

    lax.fori_loop(0, nseq, seq_step, 0)
    for hh in range(S_HPG):
        cols = slice(hh * S_DH, (hh + 1) * S_DH)
        y_sc[:, cols] = (y_sc[:, cols] + ecs[:, hh:hh + 1] * yi_sc[:, cols]
                         + dskip_ref[:, hh:hh + 1] * xs[:, cols])
    out_ref[...] = _rms(y_sc[...] * _silu(zs_ref[...]), ng_ref[...])


def _ssd_sample(l, depth, proj, dt_raw, hist, W, h0, h_prev, nseq, t_lo):
    L = CHUNK
    rows = proj.shape[0]
    nb = rows // L
    xblk = COL_XBC // S_GW
    bblk = COL_B // S_STATE
    cblk = COL_C // S_STATE
    hb_blk = S_WIDTH // S_STATE
    hc_blk = hb_blk + S_GROUPS
    has_prev = h_prev is not None
    conv_w = lambda width, blk: pl.BlockSpec((None, CONV_K, width), lambda b, g: (l, 0, blk(g)))
    h_spec = pl.BlockSpec((None, nseq, S_HPG, S_DH, S_STATE), lambda b, g: (l, b, g, 0, 0))
    in_specs = [
        pl.BlockSpec((L, S_GW), lambda b, g: (b, xblk + g)),
        pl.BlockSpec((L, S_STATE), lambda b, g: (b, bblk + g)),
        pl.BlockSpec((L, S_STATE), lambda b, g: (b, cblk + g)),
        pl.BlockSpec((L, S_GW), lambda b, g: (b, COL_ZS // S_GW + g)),
        pl.BlockSpec((L, LANES), lambda b, g: (b, 1 + g)),
        pl.BlockSpec((None, L, S_GW), lambda b, g: (l, b, g)),
        pl.BlockSpec((None, L, S_STATE), lambda b, g: (l, b, hb_blk + g)),
        pl.BlockSpec((None, L, S_STATE), lambda b, g: (l, b, hc_blk + g)),
        conv_w(S_GW, lambda g: g), conv_w(S_STATE, lambda g: hb_blk + g), conv_w(S_STATE, lambda g: hc_blk + g),
        _vec_spec(l, S_GW, lambda b, g: g),
        _vec_spec(l, S_STATE, lambda b, g: hb_blk + g),
        _vec_spec(l, S_STATE, lambda b, g: hc_blk + g),
        _vec_spec(l, LANES, lambda b, g: 1 + g),
        _vec_spec(l, LANES, lambda b, g: 1 + g),
        _vec_spec(l, LANES, lambda b, g: 1 + g),
        _vec_spec(l, S_GW, lambda b, g: g),
        h_spec,
        pl.BlockSpec(memory_space=pl.ANY),
    ]
    args = [proj, proj, proj, proj, dt_raw, hist, hist, hist,
            W["s_cw"], W["s_cw"], W["s_cw"], W["s_cb"], W["s_cb"], W["s_cb"],
            W["dtb"], W["alog"], W["dskip"], W["s_ng"], h0,
            h_prev if has_prev else jnp.zeros((SUBLANES, LANES), F32)]
    return pl.pallas_call(
        functools.partial(_ssd_sample_kernel, nseq=nseq, t_lo=t_lo),
        grid=(nb, S_GROUPS),
        in_specs=in_specs,
        out_specs=[pl.BlockSpec((L, S_GW), lambda b, g: (b, g)), h_spec],
        out_shape=[
            jax.ShapeDtypeStruct((rows, S_WIDTH), F32),
            jax.ShapeDtypeStruct((depth, nb * nseq, S_HEADS, S_DH, S_STATE), F32),
        ],
        scratch_shapes=[
            pltpu.VMEM((SUBLANES, S_GW), F32),
            pltpu.VMEM((SUBLANES, S_STATE), F32),
            pltpu.VMEM((SUBLANES, S_STATE), F32),
            pltpu.VMEM((L, LANES), F32),
            pltpu.VMEM((L, S_GW), F32),
            pltpu.VMEM((L, S_GW), F32),
            pltpu.VMEM((L, S_GW), F32),
        ],
        input_output_aliases={19: 1} if has_prev else {},
        compiler_params=_params(("arbitrary", "arbitrary")),
        name="ssd_sample",
    )(*args)


def _cmlp_kernel(*refs, rs, t_lo, emit_vn):
    if emit_vn:
        u_ref, v_ref, zc_ref, w_ref, bias_ref, gv_ref, out_ref, vn_ref = refs
    else:
        u_ref, v_ref, zc_ref, w_ref, bias_ref, gv_ref, out_ref = refs
    L = C_CHUNK
    nchunk = u_ref.shape[0] // L
    r2, c2, same, valid_c, _, _ = _block_masks(L, rs, t_lo)
    mask = same & (c2 <= r2) & valid_c
    for g in range(C_GROUPS):
        cols = slice(g * C_DG, (g + 1) * C_DG)
        w = jnp.where(mask, w_ref[g], 0.0).astype(BF16)
        for ch in range(nchunk):
            rows = slice(ch * L, (ch + 1) * L)
            vn = _rms(v_ref[rows, cols], gv_ref[:, cols])
            if emit_vn:
                vn_ref[rows, cols] = vn
            mixed = _dot(w, vn.astype(BF16)) + bias_ref[g]
            out_ref[rows, cols] = u_ref[rows, cols] * mixed * _silu(zc_ref[rows, cols])


def _cmlp(l, proj, wmix, bias, gv, rs, t_lo, emit_vn):
    rows = proj.shape[0]
    tb = min(512, rows)
    row_spec = pl.BlockSpec((tb, C_WIDTH), lambda i: (i, 0))
    out_shape = [jax.ShapeDtypeStruct((rows, C_WIDTH), F32)]
    if emit_vn:
        out_shape.append(jax.ShapeDtypeStruct((rows, C_WIDTH), F32))
    return pl.pallas_call(
        functools.partial(_cmlp_kernel, rs=rs, t_lo=t_lo, emit_vn=emit_vn),
        grid=(rows // tb,),
        in_specs=[
            pl.BlockSpec((tb, C_WIDTH), lambda i: (i, COL_U // C_WIDTH)),
            pl.BlockSpec((tb, C_WIDTH), lambda i: (i, COL_V // C_WIDTH)),
            pl.BlockSpec((tb, C_WIDTH), lambda i: (i, COL_ZC // C_WIDTH)),
            pl.BlockSpec((None, C_GROUPS, C_CHUNK, C_CHUNK), lambda i: (l, 0, 0, 0)),
            pl.BlockSpec((None, C_GROUPS, C_CHUNK, 1), lambda i: (l, 0, 0, 0)),
            _vec_spec(l, C_WIDTH),
        ],
        out_specs=[row_spec] * len(out_shape),
        out_shape=out_shape,
        compiler_params=_params(("arbitrary",)),
        name="cmlp",
    )(proj, proj, proj, wmix, bias, gv)


def _outproj_kernel(x_ref, om_ref, os_ref, oc_ref, w0_ref, w1_ref, w2_ref, w3_ref, fg_ref, y_ref,
                    *, final, rs, t_lo):
    half = S_WIDTH // 2
    acc = _dot(om_ref[...].astype(BF16), w0_ref[...])
    acc = acc + _dot(os_ref[:, :half].astype(BF16), w1_ref[...])
    acc = acc + _dot(os_ref[:, half:].astype(BF16), w2_ref[...])
    acc = acc + _dot(oc_ref[...].astype(BF16), w3_ref[...])
    y = x_ref[...] + acc
    if final:
        y = _rms(y, fg_ref[...])
    if t_lo:
        ridx = lax.broadcasted_iota(jnp.int32, (y.shape[0], 1), 0)
        y = jnp.where((ridx & (rs - 1)) >= t_lo, y, 0.0)
    y_ref[...] = y


def _outproj(l, x, om, os_, oc, wo, fg, final, rs, t_lo):
    rows = x.shape[0]
    tm = min(256, rows)
    kb = M_WIDTH
    wspec = lambda r: pl.BlockSpec((None, kb, D_MODEL), lambda i: (l, r, 0), pipeline_mode=pl.Buffered(1))
    return pl.pallas_call(
        functools.partial(_outproj_kernel, final=final, rs=rs, t_lo=t_lo),
        grid=(rows // tm,),
        in_specs=[
            pl.BlockSpec((tm, D_MODEL), lambda i: (i, 0)),
            pl.BlockSpec((tm, M_WIDTH), lambda i: (i, 0)),
            pl.BlockSpec((tm, S_WIDTH), lambda i: (i, 0)),
            pl.BlockSpec((tm, C_WIDTH), lambda i: (i, 0)),
            wspec(0), wspec(1), wspec(2), wspec(3),
            pl.BlockSpec((1, D_MODEL), lambda i: (0, 0)),
        ],
        out_specs=pl.BlockSpec((tm, D_MODEL), lambda i: (i, 0)),
        out_shape=jax.ShapeDtypeStruct((rows, D_MODEL), F32),
        compiler_params=_params(("arbitrary",)),
        name="outproj",
    )(x, om, os_, oc, wo, wo, wo, wo, fg)


def _heads_compact(a):
    return jnp.pad(a, ((0, 0), (0, LANES - S_HEADS)))


def _heads_grouped(a):
    a = a.reshape(a.shape[0], S_GROUPS, S_HPG)
    return jnp.pad(a, ((0, 0), (0, 0), (0, LANES - S_HPG))).reshape(a.shape[0], S_GROUPS * LANES)


def _heads_both(a):
    return jnp.concatenate([_heads_compact(a), _heads_grouped(a)], axis=1)[:, None, :]


def _prepare_weights(norm_g, w_in, m_conv_w, m_conv_b, m_w_qk, m_w_vo, m_b_o, m_w_gate, m_b_gate,
                     m_norm_g, s_conv_w, s_conv_b, s_dt_bias, s_A_log, s_D, s_norm_g, c_v_norm_g,
                     c_w_s, c_b_s, w_out):
    depth = w_in.shape[0]
    vec = lambda a: a.reshape(depth, 1, -1)
    w_all = w_in.astype(BF16)
    dt0 = PROJ_HEAD
    w_dt = w_in[:, :, dt0:dt0 + S_HEADS]
    w_dt_grouped = jnp.pad(w_dt.reshape(depth, D_MODEL, S_GROUPS, S_HPG),
                           ((0, 0), (0, 0), (0, 0), (0, LANES - S_HPG))).reshape(depth, D_MODEL, -1)
    w_dt = jnp.concatenate([jnp.pad(w_dt, ((0, 0), (0, 0), (0, LANES - S_HEADS))), w_dt_grouped],
                           axis=2).astype(BF16)
    wg = m_w_gate.reshape(depth, M_HEADS, 3, M_DH, 2 * M_HEADS)
    wg = jnp.pad(wg, ((0, 0),) * 4 + ((0, LANES - 2 * M_HEADS),)).astype(BF16)
    head_of_lane = jnp.arange(S_WIDTH) // S_DH
    expand = (jnp.arange(LANES)[:, None] == head_of_lane[None, :]).astype(BF16)
    return dict(
        norm_g=vec(norm_g), w_all=w_all, w_tail=w_all[:, :, dt0 + S_HEADS:], w_dt=w_dt,
        m_cw=m_conv_w, m_cb=vec(m_conv_b),
        wqk=m_w_qk.astype(BF16), wvo=m_w_vo.astype(BF16), wg=wg,
        bg=vec(jnp.pad(m_b_gate, ((0, 0), (0, LANES - 2 * M_HEADS)))),
        bo=vec(m_b_o), m_ng=vec(m_norm_g),
        s_cw=s_conv_w, s_cb=vec(s_conv_b),
        dtb=_heads_both(s_dt_bias), alog=_heads_both(s_A_log), dskip=_heads_both(s_D),
        dskip_wide=vec(jnp.repeat(s_D, S_DH, axis=1)), s_ng=vec(s_norm_g), expand=expand,
        gv=vec(c_v_norm_g), wo=w_out.astype(BF16),
    )


def kernel(x_prompt, x_sample, state_mlstm_C, state_mlstm_n, state_mlstm_m, state_mlstm_conv, state_ssm, state_ssm_conv, norm_g, w_in, m_conv_w, m_conv_b, m_w_qk, m_w_vo, m_b_o, m_w_gate, m_b_gate, m_norm_g, s_conv_w, s_conv_b, s_dt_bias, s_A_log, s_D, s_norm_g, c_v_norm_g, c_w_s, c_b_s, w_out, final_norm_g):
    bp, seq, _ = x_prompt.shape
    bs, dec_seq, _ = x_sample.shape
    depth = w_in.shape[0]
    t_lo = SAMPLE_ROWS - dec_seq
    hist_lo = t_lo - (CONV_K - 1)
    nseq = CHUNK // SAMPLE_ROWS
    W = _prepare_weights(norm_g, w_in, m_conv_w, m_conv_b, m_w_qk, m_w_vo, m_b_o, m_w_gate, m_b_gate,
                         m_norm_g, s_conv_w, s_conv_b, s_dt_bias, s_A_log, s_D, s_norm_g, c_v_norm_g,
                         c_w_s, c_b_s, w_out)
    fg = final_norm_g[None, :]
    yp = x_prompt.reshape(bp * seq, D_MODEL)
    ys = jnp.pad(x_sample, ((0, 0), (t_lo, 0), (0, 0))).reshape(bs * SAMPLE_ROWS, D_MODEL)

    pad_hist = ((0, 0), (0, 0), (hist_lo, dec_seq), (0, 0))
    hist_m = jnp.pad(state_mlstm_conv, pad_hist).reshape(depth, bs * SAMPLE_ROWS, M_WIDTH)
    hist_s = jnp.pad(state_ssm_conv, pad_hist).reshape(depth, bs * SAMPLE_ROWS, -1)
    m0rows = jnp.repeat(jnp.swapaxes(state_mlstm_m, 1, 2)[..., None], SAMPLE_ROWS, axis=2)
    n0 = state_mlstm_n[:, :, :, None, :]
    reps = C_CHUNK // SAMPLE_ROWS
    w4 = jnp.pad(c_w_s[:, :, :dec_seq, :dec_seq], ((0, 0), (0, 0), (t_lo, 0), (t_lo, 0)))
    b4 = jnp.pad(c_b_s[:, :, :dec_seq], ((0, 0), (0, 0), (t_lo, 0)))
    wmix_s = jnp.tile(w4, (1, 1, reps, reps))
    bias_s = jnp.tile(b4, (1, 1, reps))[..., None]
    bias_p = c_b_s[..., None]

    outs_p, outs_s = [], []
    c_all = h_all = None
    mc = min(M_CHUNK, seq)
    sc = min(CHUNK, seq)
    tbp = min(512, bp * seq)
    tbs = min(512, bs * SAMPLE_ROWS)
    for l in range(depth):
        final = l == depth - 1
        proj, dt_raw = _inproj(l, yp, W["norm_g"], W["w_all"], W["w_tail"], W["w_dt"])
        q, k, v, o, gates = _mlstm_proj(l, proj, None, W, max(1, seq // tbp), tbp)
        out_m, c1, n1, m1 = _mlstm_prompt(l, q, k, v, o, gates, proj, W, bp, seq // mc)
        out_s, h1 = _ssd_prompt(l, proj, dt_raw, W, bp, seq // sc)
        (out_c,) = _cmlp(l, proj, c_w_s, bias_p, W["gv"], C_CHUNK, 0, False)
        yp = _outproj(l, yp, out_m, out_s, out_c, W["wo"], fg, final, CHUNK, 0)
        pj = proj.reshape(bp, seq, PROJ_MAIN)
        outs_p.append((
            c1, n1.reshape(bp, M_HEADS, M_DH), m1[:, :M_HEADS, 0],
            pj[:, seq - (CONV_K - 1):, COL_XM:COL_XM + M_WIDTH], h1,
            pj[:, seq - (CONV_K - 1):, COL_XBC:COL_XBC + S_WIDTH + 2 * S_BC]))
        proj, dt_raw = _inproj(l, ys, W["norm_g"], W["w_all"], W["w_tail"], W["w_dt"])
        q, k, v, o, gates = _mlstm_proj(l, proj, hist_m, W, 1, tbs)
        out_m, c_all, n1, mrow = _mlstm_sample(l, depth, q, k, v, o, gates, proj, W, state_mlstm_C, n0,
                                               m0rows, c_all, nseq, t_lo)
        out_s, h_all = _ssd_sample(l, depth, proj, dt_raw, hist_s, W, state_ssm, h_all, nseq, t_lo)
        out_c, vn = _cmlp(l, proj, wmix_s, bias_s, W["gv"], SAMPLE_ROWS, t_lo, True)
        ys = _outproj(l, ys, out_m, out_s, out_c, W["wo"], fg, final, SAMPLE_ROWS, t_lo)
        pj = proj.reshape(bs, SAMPLE_ROWS, PROJ_MAIN)
        outs_s.append((
            n1.reshape(bs, M_HEADS, M_DH), mrow[:, SAMPLE_ROWS - 1::SAMPLE_ROWS, 0].T,
            pj[:, SAMPLE_ROWS - (CONV_K - 1):, COL_XM:COL_XM + M_WIDTH],
            pj[:, SAMPLE_ROWS - (CONV_K - 1):, COL_XBC:COL_XBC + S_WIDTH + 2 * S_BC],
            vn.reshape(bs, SAMPLE_ROWS, C_WIDTH)[:, t_lo:]))
    p_out = [jnp.stack([s[i] for s in outs_p]) for i in range(6)]
    s_n, s_m, s_mconv, s_sconv, s_cv = [jnp.stack([s[i] for s in outs_s]) for i in range(5)]
    y_prompt = yp.reshape(bp, seq, D_MODEL)
    y_sample = ys.reshape(bs, SAMPLE_ROWS, D_MODEL)[:, t_lo:]
    return (y_prompt, y_sample, *p_out, c_all, s_n, s_m, s_mconv, h_all, s_sconv, s_cv)
```

```python
import functools

import jax
import jax.numpy as jnp
from jax import lax
from jax.experimental import pallas as pl
from jax.experimental.pallas import tpu as pltpu

F32 = jnp.float32
BF16 = jnp.bfloat16

D_MODEL = 2048
MIX_WIDTH = 2 * D_MODEL
M_WIDTH = MIX_WIDTH // 4
M_HEADS = 4
M_DH = M_WIDTH // M_HEADS
S_WIDTH = MIX_WIDTH // 2
S_DH = 64
S_HEADS = S_WIDTH // S_DH
S_GROUPS = 4
S_HPG = S_HEADS // S_GROUPS
S_STATE = 128
S_GW = S_HPG * S_DH
S_BC = S_GROUPS * S_STATE
C_WIDTH = MIX_WIDTH // 4
C_GROUPS = 4
C_DG = C_WIDTH // C_GROUPS
C_CHUNK = 128
CONV_K = 4
EPS = 1e-6

LANES = 128
SUBLANES = 8
SAMPLE_ROWS = 8
CHUNK = 128
M_CHUNK = 256
SEQ_UNROLL = 8
VMEM_LIMIT = 56 * 1024 * 1024

COL_XM, COL_ZM, COL_ZS, COL_XBC, COL_U, COL_V, COL_ZC = 0, 1024, 2048, 4096, 7168, 8192, 9216
PROJ_MAIN = 10240
PROJ_HEAD = COL_U
COL_B = COL_XBC + S_WIDTH
COL_C = COL_B + S_BC


def _dot(a, b):
    return jnp.dot(a, b, preferred_element_type=F32)


def _dot_nt(a, b):
    return lax.dot_general(a, b, (((1,), (1,)), ((), ())), preferred_element_type=F32)


def _dot_mask(mask, x):
    m = jnp.where(mask, 1.0, 0.0).astype(BF16)
    hi = x.astype(BF16)
    r1 = x - hi.astype(F32)
    mid = r1.astype(BF16)
    lo = (r1 - mid.astype(F32)).astype(BF16)
    return _dot(m, hi) + _dot(m, mid) + _dot(m, lo)


def _sigmoid(x):
    return 0.5 * jnp.tanh(0.5 * x) + 0.5


def _silu(x):
    h = 0.5 * x
    return h * jnp.tanh(h) + h


def _softplus(x):
    return jnp.maximum(x, 0.0) + jnp.log1p(jnp.exp(-jnp.abs(x)))


def _log_sigmoid(x):
    return jnp.minimum(x, 0.0) - jnp.log1p(jnp.exp(-jnp.abs(x)))


def _rms(x, g):
    return x * lax.rsqrt(jnp.mean(x * x, axis=-1, keepdims=True) + EPS) * g


def _params(sem):
    return pltpu.CompilerParams(dimension_semantics=sem, vmem_limit_bytes=VMEM_LIMIT)


def _vec_spec(l, n, col=None):
    if col is None:
        return pl.BlockSpec((None, 1, n), lambda *ids: (l, 0, 0))
    return pl.BlockSpec((None, 1, n), lambda *ids: (l, 0, col(*ids)))


def _norm_kernel(x_ref, g_ref, h_ref):
    h_ref[...] = _rms(x_ref[...], g_ref[...]).astype(BF16)


def _norm(l, x, g):
    rows = x.shape[0]
    tb = min(256, rows)
    return pl.pallas_call(
        _norm_kernel,
        grid=(rows // tb,),
        in_specs=[pl.BlockSpec((tb, D_MODEL), lambda i: (i, 0)), _vec_spec(l, D_MODEL)],
        out_specs=pl.BlockSpec((tb, D_MODEL), lambda i: (i, 0)),
        out_shape=jax.ShapeDtypeStruct((rows, D_MODEL), BF16),
        compiler_params=_params(("arbitrary",)),
        name="norm",
    )(x, g)


def _inproj_kernel(h_ref, wa_ref, wb_ref, proj_ref, w_sc, *, n_head):
    j = pl.program_id(0)
    i = pl.program_id(1)

    @pl.when((i == 0) & (j < n_head))
    def _():
        w_sc[...] = wa_ref[...].astype(BF16)

    @pl.when((i == 0) & (j >= n_head))
    def _():
        w_sc[...] = wb_ref[...]

    proj_ref[...] = _dot(h_ref[...], w_sc[...])


def _inproj(l, h, w_in, w_tail):
    rows = h.shape[0]
    tm = min(1024, rows)
    tn = 1024
    n_head = PROJ_HEAD // tn
    return pl.pallas_call(
        functools.partial(_inproj_kernel, n_head=n_head),
        grid=(PROJ_MAIN // tn, rows // tm),
        in_specs=[
            pl.BlockSpec((tm, D_MODEL), lambda j, i: (i, 0)),
            pl.BlockSpec((None, D_MODEL, tn), lambda j, i: (l, 0, jnp.minimum(j, n_head - 1))),
            pl.BlockSpec((None, D_MODEL, tn), lambda j, i: (l, 0, jnp.maximum(j - n_head, 0))),
        ],
        out_specs=pl.BlockSpec((tm, tn), lambda j, i: (i, j)),
        out_shape=jax.ShapeDtypeStruct((rows, PROJ_MAIN), F32),
        scratch_shapes=[pltpu.VMEM((D_MODEL, tn), BF16)],
        compiler_params=_params(("arbitrary", "arbitrary")),
        name="inproj",
    )(h, w_in, w_tail)


def _conv_rows(x, first, prev_sc, cw_ref, cb_ref):
    tb = x.shape[0]

    @pl.when(first)
    def _():
        prev_sc[...] = jnp.zeros(prev_sc.shape, F32)

    xe = jnp.concatenate([prev_sc[...], x], axis=0)
    acc = cb_ref[...] + cw_ref[CONV_K - 1:CONV_K, :] * x
    for d in range(1, CONV_K):
        acc = acc + cw_ref[CONV_K - 1 - d:CONV_K - d, :] * pltpu.roll(xe, d, 0)[SUBLANES:, :]
    prev_sc[...] = x[tb - SUBLANES:, :]
    return _silu(acc)


def _mlstm_proj_kernel(*refs, has_hist, blocks_per_seq):
    if has_hist:
        (xm_ref, he_ref, cw_ref, cb_ref, wqk_ref, wvo_ref, wg_ref, bg_ref,
         q_ref, k_ref, v_ref, o_ref, gates_ref, xe_sc) = refs
    else:
        (xm_ref, cw_ref, cb_ref, wqk_ref, wvo_ref, wg_ref, bg_ref,
         q_ref, k_ref, v_ref, o_ref, gates_ref, xe_sc) = refs
    i = pl.program_id(0)
    x = xm_ref[...]
    xin = x + he_ref[...] if has_hist else x
    xmc = _conv_rows(xin, i % blocks_per_seq == 0, xe_sc, cw_ref, cb_ref)
    tb = x.shape[0]
    gates = jnp.broadcast_to(bg_ref[...], (tb, LANES))
    for h in range(M_HEADS):
        cols = slice(h * M_DH, (h + 1) * M_DH)
        qk = _dot(xmc[:, cols].astype(BF16), wqk_ref[h])
        vo = _dot(x[:, cols].astype(BF16), wvo_ref[h])
        qb = qk[:, :M_DH].astype(BF16)
        kb = qk[:, M_DH:].astype(BF16)
        vb = vo[:, :M_DH].astype(BF16)
        gates = gates + _dot(qb, wg_ref[h, 0]) + _dot(kb, wg_ref[h, 1]) + _dot(vb, wg_ref[h, 2])
        q_ref[:, cols] = qb
        k_ref[:, cols] = (qk[:, M_DH:] * (M_DH ** -0.5)).astype(BF16)
        v_ref[:, cols] = vb
        o_ref[:, cols] = vo[:, M_DH:]
    lane = lax.broadcasted_iota(jnp.int32, (tb, LANES), 1)
    gates_ref[...] = jnp.where(lane < M_HEADS, gates, _log_sigmoid(gates))


def _mlstm_proj(l, proj, hist, W, blocks_per_seq, tb):
    rows = proj.shape[0]
    has_hist = hist is not None
    in_specs = [pl.BlockSpec((tb, M_WIDTH), lambda i: (i, COL_XM // M_WIDTH))]
    args = [proj]
    if has_hist:
        in_specs.append(pl.BlockSpec((None, tb, M_WIDTH), lambda i: (l, i, 0)))
        args.append(hist)
    in_specs += [
        pl.BlockSpec((None, CONV_K, M_WIDTH), lambda i: (l, 0, 0)),
        _vec_spec(l, M_WIDTH),
        pl.BlockSpec((None, M_HEADS, M_DH, 2 * M_DH), lambda i: (l, 0, 0, 0)),
        pl.BlockSpec((None, M_HEADS, M_DH, 2 * M_DH), lambda i: (l, 0, 0, 0)),
        pl.BlockSpec((None, M_HEADS, 3, M_DH, LANES), lambda i: (l, 0, 0, 0, 0)),
        _vec_spec(l, LANES),
    ]
    args += [W["m_cw"], W["m_cb"], W["wqk"], W["wvo"], W["wg"], W["bg"]]
    row_spec = pl.BlockSpec((tb, M_WIDTH), lambda i: (i, 0))
    return pl.pallas_call(
        functools.partial(_mlstm_proj_kernel, has_hist=has_hist, blocks_per_seq=blocks_per_seq),
        grid=(rows // tb,),
        in_specs=in_specs,
        out_specs=[row_spec, row_spec, row_spec, row_spec, pl.BlockSpec((tb, LANES), lambda i: (i, 0))],
        out_shape=[
            jax.ShapeDtypeStruct((rows, M_WIDTH), BF16),
            jax.ShapeDtypeStruct((rows, M_WIDTH), BF16),
            jax.ShapeDtypeStruct((rows, M_WIDTH), BF16),
            jax.ShapeDtypeStruct((rows, M_WIDTH), F32),
            jax.ShapeDtypeStruct((rows, LANES), F32),
        ],
        scratch_shapes=[pltpu.VMEM((SUBLANES, M_WIDTH), F32)],
        compiler_params=_params(("arbitrary",)),
        name="mlstm_proj",
    )(*args)


def _mlstm_finish(hh, o, bo, ng, zm):
    hm = _sigmoid(o + bo) * hh
    return _rms(hm, ng) * _silu(zm)


def _mlstm_prompt_kernel(q_ref, k_ref, v_ref, o_ref, zm_ref, gates_ref, bo_ref, ng_ref,
                         out_ref, c1_ref, n1_ref, m1_ref):
    c = pl.program_id(1)
    L = q_ref.shape[0]

    @pl.when(c == 0)
    def _():
        c1_ref[...] = jnp.zeros(c1_ref.shape, F32)
        n1_ref[...] = jnp.zeros(n1_ref.shape, F32)
        m1_ref[...] = jnp.zeros(m1_ref.shape, F32)

    r2 = lax.broadcasted_iota(jnp.int32, (L, L), 0)
    c2 = lax.broadcasted_iota(jnp.int32, (L, L), 1)
    causal = c2 <= r2
    neg_inf = -jnp.inf
    g = gates_ref[...]
    b_all = _dot_mask(causal, g)
    g_t = g.T
    b_t = b_all.T
    for h in range(M_HEADS):
        cols = slice(h * M_DH, (h + 1) * M_DH)
        f = M_HEADS + h
        g_row = g_t[h:h + 1, :] - b_t[f:f + 1, :]
        b_col = b_all[:, f:f + 1]
        g_col = g[:, h:h + 1] - b_col
        mprev = m1_ref[0, h:h + 1, 0:1]
        gm = jnp.where(causal, g_row, neg_inf)
        m_col = jnp.maximum(mprev, jnp.max(gm, axis=1, keepdims=True))
        w_intra = jnp.exp(gm - m_col)
        w_inter = jnp.exp(mprev - m_col)
        m_new = b_col + m_col
        m_last = m_col[L - 1:L, :]
        wl_col = jnp.exp(g_col - m_last)
        wli = jnp.exp(mprev - m_last)
        q = q_ref[:, cols]
        k = k_ref[:, cols]
        v = v_ref[:, cols]
        cst = c1_ref[0, h]
        nst = n1_ref[0, h]
        s = _dot_nt(q, k) * w_intra
        num = _dot(s.astype(BF16), v) + w_inter * _dot_nt(q, cst.astype(BF16))
        den = (jnp.sum(s, axis=1, keepdims=True)
               + w_inter * jnp.sum(q.astype(F32) * nst, axis=1, keepdims=True))
        hh = num / jnp.maximum(jnp.abs(den), jnp.exp(-m_new))
        out_ref[:, cols] = _mlstm_finish(hh, o_ref[:, cols], bo_ref[:, cols], ng_ref[:, cols], zm_ref[:, cols])
        c1_ref[0, h] = wli * cst + _dot((v.astype(F32) * wl_col).T.astype(BF16), k)
        n1_ref[0, h] = wli * nst + jnp.sum(k.astype(F32) * wl_col, axis=0, keepdims=True)
        m1_ref[0, h:h + 1, :] = jnp.broadcast_to(m_new[L - 1:L, :], (1, LANES))


def _mlstm_prompt(l, q, k, v, o, gates, proj, W, nb, nc):
    L = M_CHUNK
    rows = nb * nc * L
    rowblk = lambda b, c: (b * nc + c, 0)
    return pl.pallas_call(
        _mlstm_prompt_kernel,
        grid=(nb, nc),
        in_specs=[
            pl.BlockSpec((L, M_WIDTH), rowblk),
            pl.BlockSpec((L, M_WIDTH), rowblk),
            pl.BlockSpec((L, M_WIDTH), rowblk),
            pl.BlockSpec((L, M_WIDTH), rowblk),
            pl.BlockSpec((L, M_WIDTH), lambda b, c: (b * nc + c, COL_ZM // M_WIDTH)),
            pl.BlockSpec((L, LANES), rowblk),
            _vec_spec(l, M_WIDTH),
            _vec_spec(l, M_WIDTH),
        ],
        out_specs=[
            pl.BlockSpec((L, M_WIDTH), rowblk),
            pl.BlockSpec((1, M_HEADS, M_DH, M_DH), lambda b, c: (b, 0, 0, 0)),
            pl.BlockSpec((1, M_HEADS, 1, M_DH), lambda b, c: (b, 0, 0, 0)),
            pl.BlockSpec((1, SUBLANES, LANES), lambda b, c: (b, 0, 0)),
        ],
        out_shape=[
            jax.ShapeDtypeStruct((rows, M_WIDTH), F32),
            jax.ShapeDtypeStruct((nb, M_HEADS, M_DH, M_DH), F32),
            jax.ShapeDtypeStruct((nb, M_HEADS, 1, M_DH), F32),
            jax.ShapeDtypeStruct((nb, SUBLANES, LANES), F32),
        ],
        compiler_params=_params(("arbitrary", "arbitrary")),
        name="mlstm_prompt",
    )(q, k, v, o, proj, gates, W["bo"], W["m_ng"])


def _block_masks(L, rs, t_lo):
    shift = rs.bit_length() - 1
    r2 = lax.broadcasted_iota(jnp.int32, (L, L), 0)
    c2 = lax.broadcasted_iota(jnp.int32, (L, L), 1)
    same = (r2 >> shift) == (c2 >> shift)
    valid_c = (c2 & (rs - 1)) >= t_lo
    ridx = lax.broadcasted_iota(jnp.int32, (L, 1), 0)
    valid_r = (ridx & (rs - 1)) >= t_lo
    return r2, c2, same, valid_c, ridx, valid_r


def _mlstm_sample_kernel(q_ref, k_ref, v_ref, o_ref, zm_ref, gates_ref, bo_ref, ng_ref,
                         c0_ref, n0_ref, m0_ref, alias_ref,
                         out_ref, c1_ref, n1_ref, mrow_ref,
                         gt_sc, bt_sc, col_sc, q_sc, numi_sc, nrow_sc, wk_sc, *, nseq, t_lo):
    del alias_ref
    h = pl.program_id(1)
    L = q_ref.shape[0]
    rs = L // nseq
    r2, c2, same, valid_c, ridx, valid_r = _block_masks(L, rs, t_lo)
    eye = r2 == c2
    neg_inf = -jnp.inf
    g = gates_ref[...]
    b_all = _dot_mask(same & (c2 <= r2), jnp.where(valid_r, g, 0.0))
    gt_sc[...] = g.T
    bt_sc[...] = b_all.T
    lane_g = lax.broadcasted_iota(jnp.int32, (1, LANES), 1)
    i_col = jnp.sum(jnp.where(lane_g == h, g, 0.0), axis=1, keepdims=True)
    b_col = jnp.sum(jnp.where(lane_g == h + M_HEADS, b_all, 0.0), axis=1, keepdims=True)
    g_row = gt_sc[pl.ds(h, 1), :] - bt_sc[pl.ds(h + M_HEADS, 1), :]
    g_col = i_col - b_col
    mask = same & (c2 <= r2) & valid_c
    mprev = m0_ref[0]
    gm = jnp.where(mask, g_row, neg_inf)
    m_col = jnp.maximum(mprev, jnp.max(gm, axis=1, keepdims=True))
    m_row = jnp.sum(jnp.where(eye, m_col, 0.0), axis=0, keepdims=True)
    mlast_col = jnp.max(jnp.where(same, m_row, neg_inf), axis=1, keepdims=True)
    w_intra = jnp.exp(gm - m_col)
    w_inter = jnp.exp(mprev - m_col)
    m_new = b_col + m_col
    wl_col = jnp.where(valid_r, jnp.exp(g_col - mlast_col), 0.0)
    col_sc[:, 0:1] = jnp.exp(mprev - mlast_col)

    q = q_ref[...]
    k = k_ref[...]
    v = v_ref[...]
    s = _dot_nt(q, k) * w_intra
    num = _dot(s.astype(BF16), v)
    den = jnp.sum(s, axis=1, keepdims=True)
    wvt = (v.astype(F32) * wl_col).T.astype(BF16)
    wk_sc[...] = k.astype(F32) * wl_col
    qf = q.astype(F32)
    q_sc[0:L, :] = qf
    q_sc[L:L + SUBLANES, :] = jnp.zeros((SUBLANES, M_DH), F32)

    def seq_step(j, carry):
        rows = pl.ds(pl.multiple_of(j * rs, rs), rs)
        in_seq_r = (ridx >= j * rs) & (ridx < (j + 1) * rs)
        cj = c0_ref[j, 0]
        nj = n0_ref[j, 0]
        q2 = q_sc[pl.ds(pl.multiple_of(j * rs, rs), 2 * rs), :].astype(BF16)
        numi_sc[rows, :] = _dot_nt(q2, cj.astype(BF16))[:rs]
        nrow_sc[rows, :] = jnp.broadcast_to(nj, (rs, M_DH))
        wli = col_sc[pl.ds(j * rs + rs - 1, 1), 0:1]
        c1_ref[j, 0] = wli * cj + _dot(wvt, jnp.where(in_seq_r, k, jnp.zeros_like(k)))
        n1_ref[j, 0] = wli * nj + jnp.sum(wk_sc[rows, :], axis=0, keepdims=True)
        return carry

    lax.fori_loop(0, nseq, seq_step, 0, unroll=SEQ_UNROLL)
    num = num + w_inter * numi_sc[...]
    den = den + w_inter * jnp.sum(qf * nrow_sc[...], axis=1, keepdims=True)
    hh = num / jnp.maximum(jnp.abs(den), jnp.exp(-m_new))
    out_ref[...] = _mlstm_finish(hh, o_ref[...], bo_ref[...], ng_ref[...], zm_ref[...])
    mrow_ref[0] = m_new


def _mlstm_sample(l, depth, q, k, v, o, gates, proj, W, c0, n0, m0rows, c_prev, nseq, t_lo):
    L = CHUNK
    rows = q.shape[0]
    nb = rows // L
    nbatch = nb * nseq
    rowblk = lambda b, h: (b, h)
    c_spec = pl.BlockSpec((None, nseq, 1, M_DH, M_DH), lambda b, h: (l, b, h, 0, 0))
    n_spec_in = pl.BlockSpec((None, nseq, 1, 1, M_DH), lambda b, h: (l, b, h, 0, 0))
    n_spec_out = pl.BlockSpec((nseq, 1, 1, M_DH), lambda b, h: (b, h, 0, 0))
    has_prev = c_prev is not None
    in_specs = [
        pl.BlockSpec((L, M_DH), rowblk),
        pl.BlockSpec((L, M_DH), rowblk),
        pl.BlockSpec((L, M_DH), rowblk),
        pl.BlockSpec((L, M_DH), rowblk),
        pl.BlockSpec((L, M_DH), lambda b, h: (b, COL_ZM // M_DH + h)),
        pl.BlockSpec((L, LANES), lambda b, h: (b, 0)),
        _vec_spec(l, M_DH, lambda b, h: h),
        _vec_spec(l, M_DH, lambda b, h: h),
        c_spec,
        n_spec_in,
        pl.BlockSpec((None, 1, L, 1), lambda b, h: (l, h, b, 0)),
        pl.BlockSpec(memory_space=pl.ANY),
    ]
    args = [q, k, v, o, proj, gates, W["bo"], W["m_ng"], c0, n0, m0rows,
            c_prev if has_prev else jnp.zeros((SUBLANES, LANES), F32)]
    return pl.pallas_call(
        functools.partial(_mlstm_sample_kernel, nseq=nseq, t_lo=t_lo),
        grid=(nb, M_HEADS),
        in_specs=in_specs,
        out_specs=[pl.BlockSpec((L, M_DH), rowblk), c_spec, n_spec_out,
                   pl.BlockSpec((1, L, 1), lambda b, h: (h, b, 0))],
        out_shape=[
            jax.ShapeDtypeStruct((rows, M_WIDTH), F32),
            jax.ShapeDtypeStruct((depth, nbatch, M_HEADS, M_DH, M_DH), F32),
            jax.ShapeDtypeStruct((nbatch, M_HEADS, 1, M_DH), F32),
            jax.ShapeDtypeStruct((M_HEADS, rows, 1), F32),
        ],
        scratch_shapes=[
            pltpu.VMEM((LANES, L), F32),
            pltpu.VMEM((LANES, L), F32),
            pltpu.VMEM((L, LANES), F32),
            pltpu.VMEM((L + SUBLANES, M_DH), F32),
            pltpu.VMEM((L, M_DH), F32),
            pltpu.VMEM((L, M_DH), F32),
            pltpu.VMEM((L, M_DH), F32),
        ],
        input_output_aliases={11: 1} if has_prev else {},
        compiler_params=_params(("arbitrary", "arbitrary")),
        name="mlstm_sample",
    )(*args)


def _ssd_prompt_kernel(xs_ref, b_ref, c_ref, zs_ref, h_ref, wdt_ref,
                       cwx_ref, cwb_ref, cwc_ref, cbx_ref, cbb_ref, cbc_ref,
                       dtb_ref, alog_ref, dskip_ref, ng_ref, expand_ref,
                       out_ref, h1_ref, xex_sc, xeb_sc, xec_sc, ht_sc):
    c = pl.program_id(1)
    nc = pl.num_programs(1)
    L = xs_ref.shape[0]
    first = c == 0

    @pl.when(first)
    def _():
        ht_sc[...] = jnp.zeros(ht_sc.shape, F32)

    xs = _conv_rows(xs_ref[...], first, xex_sc, cwx_ref, cbx_ref)
    bm = _conv_rows(b_ref[...], first, xeb_sc, cwb_ref, cbb_ref)
    cm = _conv_rows(c_ref[...], first, xec_sc, cwc_ref, cbc_ref).astype(BF16)

    r2 = lax.broadcasted_iota(jnp.int32, (L, L), 0)
    c2 = lax.broadcasted_iota(jnp.int32, (L, L), 1)
    causal = c2 <= r2
    neg_inf = -jnp.inf
    dt = _softplus(_dot(h_ref[...], wdt_ref[...]) + dtb_ref[...])
    da = dt * (-jnp.exp(alog_ref[...]))
    cs = _dot_mask(causal, da)
    cs_t = cs.T
    ecs = jnp.exp(cs)
    wend = jnp.exp(cs[L - 1:L, :] - cs) * dt

    expand = expand_ref[...]

    def widen(a):
        hi = a.astype(BF16)
        lo = (a - hi.astype(F32)).astype(BF16)
        return _dot(hi, expand) + _dot(lo, expand)

    dte = widen(dt)
    wende = widen(wend)
    ecse = widen(ecs)
    xdt = (xs * dte).astype(BF16)
    wx = (xs * wende).astype(BF16)
    lane = lax.broadcasted_iota(jnp.int32, (1, LANES), 1)
    low_half = lane < S_DH
    zero_slab = jnp.zeros((L, LANES), BF16)
    for g in range(S_GROUPS):
        gcols = slice(g * S_GW, (g + 1) * S_GW)
        scols = slice(g * S_STATE, (g + 1) * S_STATE)
        bg = bm[:, scols]
        cg = cm[:, scols]
        cb = _dot_nt(cg, bg.astype(BF16))
        pairs = []
        for pr in range(S_HPG // 2):
            h0 = g * S_HPG + 2 * pr
            mixes = []
            for hh in (h0, h0 + 1):
                dec = jnp.exp(jnp.where(causal, cs[:, hh:hh + 1] - cs_t[hh:hh + 1, :], neg_inf))
                mixes.append((cb * dec).astype(BF16))
            slab = xdt[:, h0 * S_DH:(h0 + 2) * S_DH]
            rhs = jnp.concatenate([jnp.where(low_half, slab, zero_slab),
                                   jnp.where(low_half, zero_slab, slab)], axis=0)
            pairs.append(_dot(jnp.concatenate(mixes, axis=1), rhs))
        y_intra = jnp.concatenate(pairs, axis=1)
        ht = ht_sc[:, gcols]
        y = y_intra + ecse[:, gcols] * _dot(cg, ht.astype(BF16)) + dskip_ref[:, gcols] * xs[:, gcols]
        y = y * _silu(zs_ref[:, gcols])
        out_ref[:, gcols] = _rms(y, ng_ref[:, gcols])
        ht_sc[:, gcols] = ecse[L - 1:L, gcols] * ht + _dot(bg.T.astype(BF16), wx[:, gcols])

    @pl.when(c == nc - 1)
    def _():
        for pr in range(S_HEADS // 2):
            blk = ht_sc[:, pr * LANES:(pr + 1) * LANES].T
            h1_ref[0, 2 * pr] = blk[:S_DH]
            h1_ref[0, 2 * pr + 1] = blk[S_DH:]


def _ssd_prompt(l, proj, h, W, nb, nc):
    L = CHUNK
    rows = nb * nc * L
    row = lambda b, c: b * nc + c
    conv_w = lambda width, blk: pl.BlockSpec((None, CONV_K, width), lambda b, c: (l, 0, blk))
    return pl.pallas_call(
        _ssd_prompt_kernel,
        grid=(nb, nc),
        in_specs=[
            pl.BlockSpec((L, S_WIDTH), lambda b, c: (row(b, c), COL_XBC // S_WIDTH)),
            pl.BlockSpec((L, S_BC), lambda b, c: (row(b, c), COL_B // S_BC)),
            pl.BlockSpec((L, S_BC), lambda b, c: (row(b, c), COL_C // S_BC)),
            pl.BlockSpec((L, S_WIDTH), lambda b, c: (row(b, c), COL_ZS // S_WIDTH)),
            pl.BlockSpec((L, D_MODEL), lambda b, c: (row(b, c), 0)),
            pl.BlockSpec((None, D_MODEL, LANES), lambda b, c: (l, 0, 0)),
            conv_w(S_WIDTH, 0), conv_w(S_BC, S_WIDTH // S_BC), conv_w(S_BC, S_WIDTH // S_BC + 1),
            _vec_spec(l, S_WIDTH, lambda b, c: 0),
            _vec_spec(l, S_BC, lambda b, c: S_WIDTH // S_BC),
            _vec_spec(l, S_BC, lambda b, c: S_WIDTH // S_BC + 1),
            _vec_spec(l, LANES),
            _vec_spec(l, LANES),
            _vec_spec(l, S_WIDTH),
            _vec_spec(l, S_WIDTH),
            pl.BlockSpec((LANES, S_WIDTH), lambda b, c: (0, 0)),
        ],
        out_specs=[
            pl.BlockSpec((L, S_WIDTH), lambda b, c: (row(b, c), 0)),
            pl.BlockSpec((1, S_HEADS, S_DH, S_STATE), lambda b, c: (b, 0, 0, 0)),
        ],
        out_shape=[
            jax.ShapeDtypeStruct((rows, S_WIDTH), F32),
            jax.ShapeDtypeStruct((nb, S_HEADS, S_DH, S_STATE), F32),
        ],
        scratch_shapes=[
            pltpu.VMEM((SUBLANES, S_WIDTH), F32),
            pltpu.VMEM((SUBLANES, S_BC), F32),
            pltpu.VMEM((SUBLANES, S_BC), F32),
            pltpu.VMEM((S_STATE, S_WIDTH), F32),
        ],
        compiler_params=_params(("arbitrary", "arbitrary")),
        name="ssd_prompt",
    )(proj, proj, proj, proj, h, W["w_dt_c"], W["s_cw"], W["s_cw"], W["s_cw"], W["s_cb"], W["s_cb"], W["s_cb"],
      W["dtb_c"], W["alog_c"], W["dskip_wide"], W["s_ng"], W["expand"])


def _ssd_sample_kernel(xs_ref, b_ref, c_ref, zs_ref, h_ref, wdt_ref, hx_ref, hb_ref, hc_ref,
                       cwx_ref, cwb_ref, cwc_ref, cbx_ref, cbb_ref, cbc_ref,
                       dtb_ref, alog_ref, dskip_ref, ng_ref, expand_ref, h0_ref, alias_ref,
                       out_ref, h1_ref, xex_sc, xeb_sc, xec_sc, tot_sc, cm_sc, yi_sc, *, nseq, t_lo):
    del alias_ref
    L = xs_ref.shape[0]
    rs = L // nseq
    first = True
    xs = _conv_rows(xs_ref[...] + hx_ref[...], first, xex_sc, cwx_ref, cbx_ref)
    bm = _conv_rows(b_ref[...] + hb_ref[...], first, xeb_sc, cwb_ref, cbb_ref)
    cm = _conv_rows(c_ref[...] + hc_ref[...], first, xec_sc, cwc_ref, cbc_ref)
    bmb = bm.astype(BF16)
    cmb = cm.astype(BF16)
    cm_sc[0:L, :] = cm
    cm_sc[L:L + SUBLANES, :] = jnp.zeros((SUBLANES, S_STATE), F32)

    r2, c2, same, valid_c, ridx, valid_r = _block_masks(L, rs, t_lo)
    mask = same & (c2 <= r2) & valid_c
    neg_inf = -jnp.inf
    dt = jnp.where(valid_r, _softplus(_dot(h_ref[...], wdt_ref[...]) + dtb_ref[...]), 0.0)
    da = dt * (-jnp.exp(alog_ref[...]))
    cs = _dot_mask(mask, da)
    sfx = _dot_mask(same & (c2 > r2), da)
    tot_sc[...] = cs + sfx
    cs_t = cs.T
    expand = expand_ref[...]

    def widen(a):
        hi = a.astype(BF16)
        lo = (a - hi.astype(F32)).astype(BF16)
        return _dot(hi, expand) + _dot(lo, expand)

    dte = widen(dt)
    wende = widen(jnp.exp(sfx) * dt)
    ecse = widen(jnp.exp(cs))
    xdt = (xs * dte).astype(BF16)
    wxt = (xs * wende).T.astype(BF16)
    cb = _dot_nt(cmb, bmb)
    lane = lax.broadcasted_iota(jnp.int32, (1, LANES), 1)
    low_half = lane < S_DH
    zero_slab = jnp.zeros((L, LANES), BF16)
    pairs = []
    for pr in range(S_HPG // 2):
        mixes = []
        for hh in (2 * pr, 2 * pr + 1):
            dec = jnp.exp(jnp.where(mask, cs[:, hh:hh + 1] - cs_t[hh:hh + 1, :], neg_inf))
            mixes.append((cb * dec).astype(BF16))
        slab = xdt[:, pr * LANES:(pr + 1) * LANES]
        rhs = jnp.concatenate([jnp.where(low_half, slab, zero_slab),
                               jnp.where(low_half, zero_slab, slab)], axis=0)
        pairs.append(_dot(jnp.concatenate(mixes, axis=1), rhs))
    y_intra = jnp.concatenate(pairs, axis=1)

    def seq_step(j, carry):
        rows = pl.ds(pl.multiple_of(j * rs, rs), rs)
        in_seq_r = (ridx >= j * rs) & (ridx < (j + 1) * rs)
        hj = h0_ref[j]
        c2rows = cm_sc[pl.ds(pl.multiple_of(j * rs, rs), 2 * rs), :].astype(BF16)
        yi_sc[rows, :] = _dot_nt(c2rows, hj.reshape(S_GW, S_STATE).astype(BF16))[:rs]
        upd = _dot(wxt, jnp.where(in_seq_r, bmb, jnp.zeros_like(bmb)))
        dec_j = jnp.exp(tot_sc[pl.ds(j * rs + rs - 1, 1), :])
        for hh in range(S_HPG):
            h1_ref[j, hh] = dec_j[:, hh:hh + 1] * hj[hh] + upd[hh * S_DH:(hh + 1) * S_DH, :]
        return carry

    lax.fori_loop(0, nseq, seq_step, 0, unroll=SEQ_UNROLL)
    y = y_intra + ecse * yi_sc[...] + dskip_ref[...] * xs
    out_ref[...] = _rms(y * _silu(zs_ref[...]), ng_ref[...])


def _ssd_sample(l, depth, proj, h, hist, W, h0, h_prev, nseq, t_lo):
    L = CHUNK
    rows = proj.shape[0]
    nb = rows // L
    xblk = COL_XBC // S_GW
    bblk = COL_B // S_STATE
    cblk = COL_C // S_STATE
    hb_blk = S_WIDTH // S_STATE
    hc_blk = hb_blk + S_GROUPS
    has_prev = h_prev is not None
    conv_w = lambda width, blk: pl.BlockSpec((None, CONV_K, width), lambda b, g: (l, 0, blk(g)))
    h_spec = pl.BlockSpec((None, nseq, S_HPG, S_DH, S_STATE), lambda b, g: (l, b, g, 0, 0))
    in_specs = [
        pl.BlockSpec((L, S_GW), lambda b, g: (b, xblk + g)),
        pl.BlockSpec((L, S_STATE), lambda b, g: (b, bblk + g)),
        pl.BlockSpec((L, S_STATE), lambda b, g: (b, cblk + g)),
        pl.BlockSpec((L, S_GW), lambda b, g: (b, COL_ZS // S_GW + g)),
        pl.BlockSpec((L, D_MODEL), lambda b, g: (b, 0)),
        pl.BlockSpec((None, D_MODEL, LANES), lambda b, g: (l, 0, g)),
        pl.BlockSpec((None, L, S_GW), lambda b, g: (l, b, g)),
        pl.BlockSpec((None, L, S_STATE), lambda b, g: (l, b, hb_blk + g)),
        pl.BlockSpec((None, L, S_STATE), lambda b, g: (l, b, hc_blk + g)),
        conv_w(S_GW, lambda g: g), conv_w(S_STATE, lambda g: hb_blk + g), conv_w(S_STATE, lambda g: hc_blk + g),
        _vec_spec(l, S_GW, lambda b, g: g),
        _vec_spec(l, S_STATE, lambda b, g: hb_blk + g),
        _vec_spec(l, S_STATE, lambda b, g: hc_blk + g),
        _vec_spec(l, LANES, lambda b, g: g),
        _vec_spec(l, LANES, lambda b, g: g),
        _vec_spec(l, S_GW, lambda b, g: g),
        _vec_spec(l, S_GW, lambda b, g: g),
        pl.BlockSpec((LANES, S_GW), lambda b, g: (0, 0)),
        h_spec,
        pl.BlockSpec(memory_space=pl.ANY),
    ]
    args = [proj, proj, proj, proj, h, W["w_dt_g"], hist, hist, hist,
            W["s_cw"], W["s_cw"], W["s_cw"], W["s_cb"], W["s_cb"], W["s_cb"],
            W["dtb_g"], W["alog_g"], W["dskip_wide"], W["s_ng"], W["expand"], h0,
            h_prev if has_prev else jnp.zeros((SUBLANES, LANES), F32)]
    return pl.pallas_call(
        functools.partial(_ssd_sample_kernel, nseq=nseq, t_lo=t_lo),
        grid=(nb, S_GROUPS),
        in_specs=in_specs,
        out_specs=[pl.BlockSpec((L, S_GW), lambda b, g: (b, g)), h_spec],
        out_shape=[
            jax.ShapeDtypeStruct((rows, S_WIDTH), F32),
            jax.ShapeDtypeStruct((depth, nb * nseq, S_HEADS, S_DH, S_STATE), F32),
        ],
        scratch_shapes=[
            pltpu.VMEM((SUBLANES, S_GW), F32),
            pltpu.VMEM((SUBLANES, S_STATE), F32),
            pltpu.VMEM((SUBLANES, S_STATE), F32),
            pltpu.VMEM((L, LANES), F32),
            pltpu.VMEM((L + SUBLANES, S_STATE), F32),
            pltpu.VMEM((L, S_GW), F32),
        ],
        input_output_aliases={21: 1} if has_prev else {},
        compiler_params=_params(("arbitrary", "arbitrary")),
        name="ssd_sample",
    )(*args)


def _cmlp_kernel(*refs, rs, t_lo, emit_vn):
    if emit_vn:
        u_ref, v_ref, zc_ref, w_ref, bias_ref, gv_ref, out_ref, vn_ref = refs
    else:
        u_ref, v_ref, zc_ref, w_ref, bias_ref, gv_ref, out_ref = refs
    L = C_CHUNK
    nchunk = u_ref.shape[0] // L
    r2, c2, same, valid_c, _, _ = _block_masks(L, rs, t_lo)
    mask = same & (c2 <= r2) & valid_c
    for g in range(C_GROUPS):
        cols = slice(g * C_DG, (g + 1) * C_DG)
        w = jnp.where(mask, w_ref[g], 0.0).astype(BF16)
        for ch in range(nchunk):
            rows = slice(ch * L, (ch + 1) * L)
            vn = _rms(v_ref[rows, cols], gv_ref[:, cols])
            if emit_vn:
                vn_ref[rows, cols] = vn
            mixed = _dot(w, vn.astype(BF16)) + bias_ref[g]
            out_ref[rows, cols] = u_ref[rows, cols] * mixed * _silu(zc_ref[rows, cols])


def _cmlp(l, proj, wmix, bias, gv, rs, t_lo, emit_vn):
    rows = proj.shape[0]
    tb = min(512, rows)
    row_spec = pl.BlockSpec((tb, C_WIDTH), lambda i: (i, 0))
    out_shape = [jax.ShapeDtypeStruct((rows, C_WIDTH), F32)]
    if emit_vn:
        out_shape.append(jax.ShapeDtypeStruct((rows, C_WIDTH), F32))
    return pl.pallas_call(
        functools.partial(_cmlp_kernel, rs=rs, t_lo=t_lo, emit_vn=emit_vn),
        grid=(rows // tb,),
        in_specs=[
            pl.BlockSpec((tb, C_WIDTH), lambda i: (i, COL_U // C_WIDTH)),
            pl.BlockSpec((tb, C_WIDTH), lambda i: (i, COL_V // C_WIDTH)),
            pl.BlockSpec((tb, C_WIDTH), lambda i: (i, COL_ZC // C_WIDTH)),
            pl.BlockSpec((None, C_GROUPS, C_CHUNK, C_CHUNK), lambda i: (l, 0, 0, 0)),
            pl.BlockSpec((None, C_GROUPS, C_CHUNK, 1), lambda i: (l, 0, 0, 0)),
            _vec_spec(l, C_WIDTH),
        ],
        out_specs=[row_spec] * len(out_shape),
        out_shape=out_shape,
        compiler_params=_params(("arbitrary",)),
        name="cmlp",
    )(proj, proj, proj, wmix, bias, gv)


def _outproj_kernel(*refs, final, rs, t_lo):
    if final:
        x_ref, om_ref, os_ref, oc_ref, w0_ref, w1_ref, w2_ref, w3_ref, g_ref, y_ref = refs
    else:
        x_ref, om_ref, os_ref, oc_ref, w0_ref, w1_ref, w2_ref, w3_ref, g_ref, y_ref, h_ref = refs
    half = S_WIDTH // 2
    acc = _dot(om_ref[...].astype(BF16), w0_ref[...])
    acc = acc + _dot(os_ref[:, :half].astype(BF16), w1_ref[...])
    acc = acc + _dot(os_ref[:, half:].astype(BF16), w2_ref[...])
    acc = acc + _dot(oc_ref[...].astype(BF16), w3_ref[...])
    y = x_ref[...] + acc
    if t_lo:
        ridx = lax.broadcasted_iota(jnp.int32, (y.shape[0], 1), 0)
        y = jnp.where((ridx & (rs - 1)) >= t_lo, y, 0.0)
    if final:
        y_ref[...] = _rms(y, g_ref[...])
    else:
        y_ref[...] = y
        h_ref[...] = _rms(y, g_ref[...]).astype(BF16)


def _outproj(l, x, om, os_, oc, wo, g, final, rs, t_lo):
    rows = x.shape[0]
    tm = min(256, rows)
    kb = M_WIDTH
    wspec = lambda r: pl.BlockSpec((None, kb, D_MODEL), lambda i: (l, r, 0), pipeline_mode=pl.Buffered(1))
    row_spec = pl.BlockSpec((tm, D_MODEL), lambda i: (i, 0))
    out_shape = [jax.ShapeDtypeStruct((rows, D_MODEL), F32)]
    if not final:
        out_shape.append(jax.ShapeDtypeStruct((rows, D_MODEL), BF16))
    return pl.pallas_call(
        functools.partial(_outproj_kernel, final=final, rs=rs, t_lo=t_lo),
        grid=(rows // tm,),
        in_specs=[
            row_spec,
            pl.BlockSpec((tm, M_WIDTH), lambda i: (i, 0)),
            pl.BlockSpec((tm, S_WIDTH), lambda i: (i, 0)),
            pl.BlockSpec((tm, C_WIDTH), lambda i: (i, 0)),
            wspec(0), wspec(1), wspec(2), wspec(3),
            pl.BlockSpec((1, D_MODEL), lambda i: (0, 0)) if final else _vec_spec(l + 1, D_MODEL),
        ],
        out_specs=[row_spec] * len(out_shape),
        out_shape=out_shape,
        compiler_params=_params(("arbitrary",)),
        name="outproj",
    )(x, om, os_, oc, wo, wo, wo, wo, g)


def _heads_compact(a):
    return jnp.pad(a, ((0, 0), (0, LANES - S_HEADS)))[:, None, :]


def _heads_grouped(a):
    a = a.reshape(a.shape[0], S_GROUPS, S_HPG)
    return jnp.pad(a, ((0, 0), (0, 0), (0, LANES - S_HPG))).reshape(a.shape[0], 1, S_GROUPS * LANES)


def _prepare_weights(norm_g, w_in, m_conv_w, m_conv_b, m_w_qk, m_w_vo, m_b_o, m_w_gate, m_b_gate,
                     m_norm_g, s_conv_w, s_conv_b, s_dt_bias, s_A_log, s_D, s_norm_g, c_v_norm_g,
                     c_w_s, c_b_s, w_out):
    depth = w_in.shape[0]
    vec = lambda a: a.reshape(depth, 1, -1)
    dt0 = PROJ_HEAD
    w_dt = w_in[:, :, dt0:dt0 + S_HEADS]
    w_dt_g = jnp.pad(w_dt.reshape(depth, D_MODEL, S_GROUPS, S_HPG),
                     ((0, 0), (0, 0), (0, 0), (0, LANES - S_HPG))).reshape(depth, D_MODEL, -1).astype(BF16)
    w_dt_c = jnp.pad(w_dt, ((0, 0), (0, 0), (0, LANES - S_HEADS))).astype(BF16)
    wg = m_w_gate.reshape(depth, M_HEADS, 3, M_DH, 2 * M_HEADS)
    wg = jnp.pad(wg, ((0, 0),) * 4 + ((0, LANES - 2 * M_HEADS),)).astype(BF16)
    head_of_lane = jnp.arange(S_WIDTH) // S_DH
    expand = (jnp.arange(LANES)[:, None] == head_of_lane[None, :]).astype(BF16)
    return dict(
        norm_g=vec(norm_g), w_tail=w_in[:, :, dt0 + S_HEADS:].astype(BF16), w_dt_c=w_dt_c, w_dt_g=w_dt_g,
        m_cw=m_conv_w, m_cb=vec(m_conv_b),
        wqk=m_w_qk.astype(BF16), wvo=m_w_vo.astype(BF16), wg=wg,
        bg=vec(jnp.pad(m_b_gate, ((0, 0), (0, LANES - 2 * M_HEADS)))),
        bo=vec(m_b_o), m_ng=vec(m_norm_g),
        s_cw=s_conv_w, s_cb=vec(s_conv_b),
        dtb_c=_heads_compact(s_dt_bias), alog_c=_heads_compact(s_A_log),
        dtb_g=_heads_grouped(s_dt_bias), alog_g=_heads_grouped(s_A_log),
        dskip_wide=vec(jnp.repeat(s_D, S_DH, axis=1)), s_ng=vec(s_norm_g), expand=expand,
        gv=vec(c_v_norm_g), wo=w_out.astype(BF16),
    )


def kernel(x_prompt, x_sample, state_mlstm_C, state_mlstm_n, state_mlstm_m, state_mlstm_conv, state_ssm, state_ssm_conv, norm_g, w_in, m_conv_w, m_conv_b, m_w_qk, m_w_vo, m_b_o, m_w_gate, m_b_gate, m_norm_g, s_conv_w, s_conv_b, s_dt_bias, s_A_log, s_D, s_norm_g, c_v_norm_g, c_w_s, c_b_s, w_out, final_norm_g):
    bp, seq, _ = x_prompt.shape
    bs, dec_seq, _ = x_sample.shape
    depth = w_in.shape[0]
    t_lo = SAMPLE_ROWS - dec_seq
    hist_lo = t_lo - (CONV_K - 1)
    nseq = CHUNK // SAMPLE_ROWS
    W = _prepare_weights(norm_g, w_in, m_conv_w, m_conv_b, m_w_qk, m_w_vo, m_b_o, m_w_gate, m_b_gate,
                         m_norm_g, s_conv_w, s_conv_b, s_dt_bias, s_A_log, s_D, s_norm_g, c_v_norm_g,
                         c_w_s, c_b_s, w_out)
    fg = final_norm_g[None, :]
    yp = x_prompt.reshape(bp * seq, D_MODEL)
    ys = jnp.pad(x_sample, ((0, 0), (t_lo, 0), (0, 0))).reshape(bs * SAMPLE_ROWS, D_MODEL)

    pad_hist = ((0, 0), (0, 0), (hist_lo, dec_seq), (0, 0))
    hist_m = jnp.pad(state_mlstm_conv, pad_hist).reshape(depth, bs * SAMPLE_ROWS, M_WIDTH)
    hist_s = jnp.pad(state_ssm_conv, pad_hist).reshape(depth, bs * SAMPLE_ROWS, -1)
    m0rows = jnp.repeat(jnp.swapaxes(state_mlstm_m, 1, 2)[..., None], SAMPLE_ROWS, axis=2)
    n0 = state_mlstm_n[:, :, :, None, :]
    reps = C_CHUNK // SAMPLE_ROWS
    w4 = jnp.pad(c_w_s[:, :, :dec_seq, :dec_seq], ((0, 0), (0, 0), (t_lo, 0), (t_lo, 0)))
    b4 = jnp.pad(c_b_s[:, :, :dec_seq], ((0, 0), (0, 0), (t_lo, 0)))
    wmix_s = jnp.tile(w4, (1, 1, reps, reps))
    bias_s = jnp.tile(b4, (1, 1, reps))[..., None]
    bias_p = c_b_s[..., None]

    hp = _norm(0, yp, W["norm_g"])
    hs = _norm(0, ys, W["norm_g"])
    outs_p, outs_s = [], []
    c_all = h_all = None
    mc = min(M_CHUNK, seq)
    sc = min(CHUNK, seq)
    tbp = min(512, bp * seq)
    tbs = min(512, bs * SAMPLE_ROWS)
    for l in range(depth):
        final = l == depth - 1
        proj = _inproj(l, hp, w_in, W["w_tail"])
        q, k, v, o, gates = _mlstm_proj(l, proj, None, W, max(1, seq // tbp), tbp)
        out_m, c1, n1, m1 = _mlstm_prompt(l, q, k, v, o, gates, proj, W, bp, seq // mc)
        out_s, h1 = _ssd_prompt(l, proj, hp, W, bp, seq // sc)
        (out_c,) = _cmlp(l, proj, c_w_s, bias_p, W["gv"], C_CHUNK, 0, False)
        if final:
            (yp,) = _outproj(l, yp, out_m, out_s, out_c, W["wo"], fg, True, CHUNK, 0)
        else:
            yp, hp = _outproj(l, yp, out_m, out_s, out_c, W["wo"], W["norm_g"], False, CHUNK, 0)
        pj = proj.reshape(bp, seq, PROJ_MAIN)
        outs_p.append((
            c1, n1.reshape(bp, M_HEADS, M_DH), m1[:, :M_HEADS, 0],
            pj[:, seq - (CONV_K - 1):, COL_XM:COL_XM + M_WIDTH], h1,
            pj[:, seq - (CONV_K - 1):, COL_XBC:COL_XBC + S_WIDTH + 2 * S_BC]))
        proj = _inproj(l, hs, w_in, W["w_tail"])
        q, k, v, o, gates = _mlstm_proj(l, proj, hist_m, W, 1, tbs)
        out_m, c_all, n1, mrow = _mlstm_sample(l, depth, q, k, v, o, gates, proj, W, state_mlstm_C, n0,
                                               m0rows, c_all, nseq, t_lo)
        out_s, h_all = _ssd_sample(l, depth, proj, hs, hist_s, W, state_ssm, h_all, nseq, t_lo)
        out_c, vn = _cmlp(l, proj, wmix_s, bias_s, W["gv"], SAMPLE_ROWS, t_lo, True)
        if final:
            (ys,) = _outproj(l, ys, out_m, out_s, out_c, W["wo"], fg, True, SAMPLE_ROWS, t_lo)
        else:
            ys, hs = _outproj(l, ys, out_m, out_s, out_c, W["wo"], W["norm_g"], False, SAMPLE_ROWS, t_lo)
        pj = proj.reshape(bs, SAMPLE_ROWS, PROJ_MAIN)
        outs_s.append((
            n1.reshape(bs, M_HEADS, M_DH), mrow[:, SAMPLE_ROWS - 1::SAMPLE_ROWS, 0].T,
            pj[:, SAMPLE_ROWS - (CONV_K - 1):, COL_XM:COL_XM + M_WIDTH],
            pj[:, SAMPLE_ROWS - (CONV_K - 1):, COL_XBC:COL_XBC + S_WIDTH + 2 * S_BC],
            vn.reshape(bs, SAMPLE_ROWS, C_WIDTH)[:, t_lo:]))
    p_out = [jnp.stack([s[i] for s in outs_p]) for i in range(6)]
    s_n, s_m, s_mconv, s_sconv, s_cv = [jnp.stack([s[i] for s in outs_s]) for i in range(5)]
    y_prompt = yp.reshape(bp, seq, D_MODEL)
    y_sample = ys.reshape(bs, SAMPLE_ROWS, D_MODEL)[:, t_lo:]
    return (y_prompt, y_sample, *p_out, c_all, s_n, s_m, s_mconv, h_all, s_sconv, s_cv)
```

```python
import functools

import jax
import jax.numpy as jnp
from jax import lax
from jax.experimental import pallas as pl
from jax.experimental.pallas import tpu as pltpu

F32 = jnp.float32
BF16 = jnp.bfloat16

D_MODEL = 2048
MIX_WIDTH = 2 * D_MODEL
M_WIDTH = MIX_WIDTH // 4
M_HEADS = 4
M_DH = M_WIDTH // M_HEADS
S_WIDTH = MIX_WIDTH // 2
S_DH = 64
S_HEADS = S_WIDTH // S_DH
S_GROUPS = 4
S_HPG = S_HEADS // S_GROUPS
S_STATE = 128
S_GW = S_HPG * S_DH
S_BC = S_GROUPS * S_STATE
C_WIDTH = MIX_WIDTH // 4
C_GROUPS = 4
C_DG = C_WIDTH // C_GROUPS
C_CHUNK = 128
CONV_K = 4
EPS = 1e-6

LANES = 128
SUBLANES = 8
SAMPLE_ROWS = 8
CHUNK = 128
M_CHUNK = 256
SEQ_UNROLL = 8
VMEM_LIMIT = 56 * 1024 * 1024

COL_XM, COL_ZM, COL_ZS, COL_XBC, COL_U, COL_V, COL_ZC = 0, 1024, 2048, 4096, 7168, 8192, 9216
PROJ_MAIN = 10240
PROJ_HEAD = COL_U
COL_B = COL_XBC + S_WIDTH
COL_C = COL_B + S_BC


def _dot(a, b):
    return jnp.dot(a, b, preferred_element_type=F32)


def _dot_nt(a, b):
    return lax.dot_general(a, b, (((1,), (1,)), ((), ())), preferred_element_type=F32)


def _dot_mask(mask, x):
    m = jnp.where(mask, 1.0, 0.0).astype(BF16)
    hi = x.astype(BF16)
    r1 = x - hi.astype(F32)
    mid = r1.astype(BF16)
    lo = (r1 - mid.astype(F32)).astype(BF16)
    return _dot(m, hi) + _dot(m, mid) + _dot(m, lo)


def _sigmoid(x):
    return 0.5 * jnp.tanh(0.5 * x) + 0.5


def _silu(x):
    h = 0.5 * x
    return h * jnp.tanh(h) + h


def _softplus(x):
    return jnp.maximum(x, 0.0) + jnp.log1p(jnp.exp(-jnp.abs(x)))


def _log_sigmoid(x):
    return jnp.minimum(x, 0.0) - jnp.log1p(jnp.exp(-jnp.abs(x)))


def _rms(x, g):
    return x * lax.rsqrt(jnp.mean(x * x, axis=-1, keepdims=True) + EPS) * g


def _params(sem):
    return pltpu.CompilerParams(dimension_semantics=sem, vmem_limit_bytes=VMEM_LIMIT)


def _vec_spec(l, n, col=None):
    if col is None:
        return pl.BlockSpec((None, 1, n), lambda *ids: (l, 0, 0))
    return pl.BlockSpec((None, 1, n), lambda *ids: (l, 0, col(*ids)))


def _norm_kernel(x_ref, g_ref, h_ref):
    h_ref[...] = _rms(x_ref[...], g_ref[...]).astype(BF16)


def _norm(l, x, g):
    rows = x.shape[0]
    tb = min(256, rows)
    return pl.pallas_call(
        _norm_kernel,
        grid=(rows // tb,),
        in_specs=[pl.BlockSpec((tb, D_MODEL), lambda i: (i, 0)), _vec_spec(l, D_MODEL)],
        out_specs=pl.BlockSpec((tb, D_MODEL), lambda i: (i, 0)),
        out_shape=jax.ShapeDtypeStruct((rows, D_MODEL), BF16),
        compiler_params=_params(("arbitrary",)),
        name="norm",
    )(x, g)


def _inproj_kernel(h_ref, wa_ref, wb_ref, proj_ref, *, n_head):
    j = pl.program_id(0)

    @pl.when(j < n_head)
    def _():
        proj_ref[...] = _dot(h_ref[...], wa_ref[...])

    @pl.when(j >= n_head)
    def _():
        proj_ref[...] = _dot(h_ref[...], wb_ref[...])


def _inproj(l, h, w_all, w_tail):
    rows = h.shape[0]
    tm = min(1024, rows)
    tn = 1024
    n_head = PROJ_HEAD // tn
    return pl.pallas_call(
        functools.partial(_inproj_kernel, n_head=n_head),
        grid=(PROJ_MAIN // tn, rows // tm),
        in_specs=[
            pl.BlockSpec((tm, D_MODEL), lambda j, i: (i, 0)),
            pl.BlockSpec((None, D_MODEL, tn), lambda j, i: (l, 0, jnp.minimum(j, n_head - 1))),
            pl.BlockSpec((None, D_MODEL, tn), lambda j, i: (l, 0, jnp.maximum(j - n_head, 0))),
        ],
        out_specs=pl.BlockSpec((tm, tn), lambda j, i: (i, j)),
        out_shape=jax.ShapeDtypeStruct((rows, PROJ_MAIN), F32),
        compiler_params=_params(("arbitrary", "arbitrary")),
        name="inproj",
    )(h, w_all, w_tail)


def _conv_rows(x, first, prev_sc, cw_ref, cb_ref):
    tb = x.shape[0]

    @pl.when(first)
    def _():
        prev_sc[...] = jnp.zeros(prev_sc.shape, F32)

    xe = jnp.concatenate([prev_sc[...], x], axis=0)
    acc = cb_ref[...] + cw_ref[CONV_K - 1:CONV_K, :] * x
    for d in range(1, CONV_K):
        acc = acc + cw_ref[CONV_K - 1 - d:CONV_K - d, :] * pltpu.roll(xe, d, 0)[SUBLANES:, :]
    prev_sc[...] = x[tb - SUBLANES:, :]
    return _silu(acc)


def _mlstm_proj_kernel(*refs, has_hist, blocks_per_seq):
    if has_hist:
        (xm_ref, he_ref, cw_ref, cb_ref, wqk_ref, wvo_ref, wg_ref, bg_ref,
         q_ref, k_ref, v_ref, o_ref, gates_ref, xe_sc) = refs
    else:
        (xm_ref, cw_ref, cb_ref, wqk_ref, wvo_ref, wg_ref, bg_ref,
         q_ref, k_ref, v_ref, o_ref, gates_ref, xe_sc) = refs
    i = pl.program_id(0)
    x = xm_ref[...]
    xin = x + he_ref[...] if has_hist else x
    xmc = _conv_rows(xin, i % blocks_per_seq == 0, xe_sc, cw_ref, cb_ref)
    tb = x.shape[0]
    gates = jnp.broadcast_to(bg_ref[...], (tb, LANES))
    for h in range(M_HEADS):
        cols = slice(h * M_DH, (h + 1) * M_DH)
        qk = _dot(xmc[:, cols].astype(BF16), wqk_ref[h])
        vo = _dot(x[:, cols].astype(BF16), wvo_ref[h])
        qb = qk[:, :M_DH].astype(BF16)
        kb = qk[:, M_DH:].astype(BF16)
        vb = vo[:, :M_DH].astype(BF16)
        gates = gates + _dot(qb, wg_ref[h, 0]) + _dot(kb, wg_ref[h, 1]) + _dot(vb, wg_ref[h, 2])
        q_ref[:, cols] = qb
        k_ref[:, cols] = (qk[:, M_DH:] * (M_DH ** -0.5)).astype(BF16)
        v_ref[:, cols] = vb
        o_ref[:, cols] = vo[:, M_DH:]
    lane = lax.broadcasted_iota(jnp.int32, (tb, LANES), 1)
    gates_ref[...] = jnp.where(lane < M_HEADS, gates, _log_sigmoid(gates))


def _mlstm_proj(l, proj, hist, W, blocks_per_seq, tb):
    rows = proj.shape[0]
    has_hist = hist is not None
    in_specs = [pl.BlockSpec((tb, M_WIDTH), lambda i: (i, COL_XM // M_WIDTH))]
    args = [proj]
    if has_hist:
        in_specs.append(pl.BlockSpec((None, tb, M_WIDTH), lambda i: (l, i, 0)))
        args.append(hist)
    in_specs += [
        pl.BlockSpec((None, CONV_K, M_WIDTH), lambda i: (l, 0, 0)),
        _vec_spec(l, M_WIDTH),
        pl.BlockSpec((None, M_HEADS, M_DH, 2 * M_DH), lambda i: (l, 0, 0, 0)),
        pl.BlockSpec((None, M_HEADS, M_DH, 2 * M_DH), lambda i: (l, 0, 0, 0)),
        pl.BlockSpec((None, M_HEADS, 3, M_DH, LANES), lambda i: (l, 0, 0, 0, 0)),
        _vec_spec(l, LANES),
    ]
    args += [W["m_cw"], W["m_cb"], W["wqk"], W["wvo"], W["wg"], W["bg"]]
    row_spec = pl.BlockSpec((tb, M_WIDTH), lambda i: (i, 0))
    return pl.pallas_call(
        functools.partial(_mlstm_proj_kernel, has_hist=has_hist, blocks_per_seq=blocks_per_seq),
        grid=(rows // tb,),
        in_specs=in_specs,
        out_specs=[row_spec, row_spec, row_spec, row_spec, pl.BlockSpec((tb, LANES), lambda i: (i, 0))],
        out_shape=[
            jax.ShapeDtypeStruct((rows, M_WIDTH), BF16),
            jax.ShapeDtypeStruct((rows, M_WIDTH), BF16),
            jax.ShapeDtypeStruct((rows, M_WIDTH), BF16),
            jax.ShapeDtypeStruct((rows, M_WIDTH), F32),
            jax.ShapeDtypeStruct((rows, LANES), F32),
        ],
        scratch_shapes=[pltpu.VMEM((SUBLANES, M_WIDTH), F32)],
        compiler_params=_params(("arbitrary",)),
        name="mlstm_proj",
    )(*args)


def _mlstm_finish(hh, o, bo, ng, zm):
    hm = _sigmoid(o + bo) * hh
    return _rms(hm, ng) * _silu(zm)


def _mlstm_prompt_kernel(q_ref, k_ref, v_ref, o_ref, zm_ref, gates_ref, bo_ref, ng_ref,
                         out_ref, c1_ref, n1_ref, m1_ref):
    c = pl.program_id(1)
    L = q_ref.shape[0]

    @pl.when(c == 0)
    def _():
        c1_ref[...] = jnp.zeros(c1_ref.shape, F32)
        n1_ref[...] = jnp.zeros(n1_ref.shape, F32)
        m1_ref[...] = jnp.zeros(m1_ref.shape, F32)

    r2 = lax.broadcasted_iota(jnp.int32, (L, L), 0)
    c2 = lax.broadcasted_iota(jnp.int32, (L, L), 1)
    causal = c2 <= r2
    neg_inf = -jnp.inf
    g = gates_ref[...]
    b_all = _dot_mask(causal, g)
    g_t = g.T
    b_t = b_all.T
    for h in range(M_HEADS):
        cols = slice(h * M_DH, (h + 1) * M_DH)
        f = M_HEADS + h
        g_row = g_t[h:h + 1, :] - b_t[f:f + 1, :]
        b_col = b_all[:, f:f + 1]
        g_col = g[:, h:h + 1] - b_col
        mprev = m1_ref[0, h:h + 1, 0:1]
        gm = jnp.where(causal, g_row, neg_inf)
        m_col = jnp.maximum(mprev, jnp.max(gm, axis=1, keepdims=True))
        w_intra = jnp.exp(gm - m_col)
        w_inter = jnp.exp(mprev - m_col)
        m_new = b_col + m_col
        m_last = m_col[L - 1:L, :]
        wl_col = jnp.exp(g_col - m_last)
        wli = jnp.exp(mprev - m_last)
        q = q_ref[:, cols]
        k = k_ref[:, cols]
        v = v_ref[:, cols]
        cst = c1_ref[0, h]
        nst = n1_ref[0, h]
        s = _dot_nt(q, k) * w_intra
        num = _dot(s.astype(BF16), v) + w_inter * _dot_nt(q, cst.astype(BF16))
        den = (jnp.sum(s, axis=1, keepdims=True)
               + w_inter * jnp.sum(q.astype(F32) * nst, axis=1, keepdims=True))
        hh = num / jnp.maximum(jnp.abs(den), jnp.exp(-m_new))
        out_ref[:, cols] = _mlstm_finish(hh, o_ref[:, cols], bo_ref[:, cols], ng_ref[:, cols], zm_ref[:, cols])
        c1_ref[0, h] = wli * cst + _dot((v.astype(F32) * wl_col).T.astype(BF16), k)
        n1_ref[0, h] = wli * nst + jnp.sum(k.astype(F32) * wl_col, axis=0, keepdims=True)
        m1_ref[0, h:h + 1, :] = jnp.broadcast_to(m_new[L - 1:L, :], (1, LANES))


def _mlstm_prompt(l, q, k, v, o, gates, proj, W, nb, nc):
    L = M_CHUNK
    rows = nb * nc * L
    rowblk = lambda b, c: (b * nc + c, 0)
    return pl.pallas_call(
        _mlstm_prompt_kernel,
        grid=(nb, nc),
        in_specs=[
            pl.BlockSpec((L, M_WIDTH), rowblk),
            pl.BlockSpec((L, M_WIDTH), rowblk),
            pl.BlockSpec((L, M_WIDTH), rowblk),
            pl.BlockSpec((L, M_WIDTH), rowblk),
            pl.BlockSpec((L, M_WIDTH), lambda b, c: (b * nc + c, COL_ZM // M_WIDTH)),
            pl.BlockSpec((L, LANES), rowblk),
            _vec_spec(l, M_WIDTH),
            _vec_spec(l, M_WIDTH),
        ],
        out_specs=[
            pl.BlockSpec((L, M_WIDTH), rowblk),
            pl.BlockSpec((1, M_HEADS, M_DH, M_DH), lambda b, c: (b, 0, 0, 0)),
            pl.BlockSpec((1, M_HEADS, 1, M_DH), lambda b, c: (b, 0, 0, 0)),
            pl.BlockSpec((1, SUBLANES, LANES), lambda b, c: (b, 0, 0)),
        ],
        out_shape=[
            jax.ShapeDtypeStruct((rows, M_WIDTH), F32),
            jax.ShapeDtypeStruct((nb, M_HEADS, M_DH, M_DH), F32),
            jax.ShapeDtypeStruct((nb, M_HEADS, 1, M_DH), F32),
            jax.ShapeDtypeStruct((nb, SUBLANES, LANES), F32),
        ],
        compiler_params=_params(("arbitrary", "arbitrary")),
        name="mlstm_prompt",
    )(q, k, v, o, proj, gates, W["bo"], W["m_ng"])


def _block_masks(L, rs, t_lo):
    shift = rs.bit_length() - 1
    r2 = lax.broadcasted_iota(jnp.int32, (L, L), 0)
    c2 = lax.broadcasted_iota(jnp.int32, (L, L), 1)
    same = (r2 >> shift) == (c2 >> shift)
    valid_c = (c2 & (rs - 1)) >= t_lo
    ridx = lax.broadcasted_iota(jnp.int32, (L, 1), 0)
    valid_r = (ridx & (rs - 1)) >= t_lo
    return r2, c2, same, valid_c, ridx, valid_r


def _mlstm_sample_kernel(q_ref, k_ref, v_ref, o_ref, zm_ref, gates_ref, bo_ref, ng_ref,
                         c0_ref, n0_ref, m0_ref, alias_ref,
                         out_ref, c1_ref, n1_ref, mrow_ref,
                         gt_sc, bt_sc, col_sc, q_sc, numi_sc, nrow_sc, wk_sc, *, nseq, t_lo):
    del alias_ref
    h = pl.program_id(1)
    L = q_ref.shape[0]
    rs = L // nseq
    r2, c2, same, valid_c, ridx, valid_r = _block_masks(L, rs, t_lo)
    eye = r2 == c2
    neg_inf = -jnp.inf
    g = gates_ref[...]
    b_all = _dot_mask(same & (c2 <= r2), jnp.where(valid_r, g, 0.0))
    gt_sc[...] = g.T
    bt_sc[...] = b_all.T
    lane_g = lax.broadcasted_iota(jnp.int32, (1, LANES), 1)
    i_col = jnp.sum(jnp.where(lane_g == h, g, 0.0), axis=1, keepdims=True)
    b_col = jnp.sum(jnp.where(lane_g == h + M_HEADS, b_all, 0.0), axis=1, keepdims=True)
    g_row = gt_sc[pl.ds(h, 1), :] - bt_sc[pl.ds(h + M_HEADS, 1), :]
    g_col = i_col - b_col
    mask = same & (c2 <= r2) & valid_c
    mprev = m0_ref[0]
    gm = jnp.where(mask, g_row, neg_inf)
    m_col = jnp.maximum(mprev, jnp.max(gm, axis=1, keepdims=True))
    m_row = jnp.sum(jnp.where(eye, m_col, 0.0), axis=0, keepdims=True)
    mlast_col = jnp.max(jnp.where(same, m_row, neg_inf), axis=1, keepdims=True)
    w_intra = jnp.exp(gm - m_col)
    w_inter = jnp.exp(mprev - m_col)
    m_new = b_col + m_col
    wl_col = jnp.where(valid_r, jnp.exp(g_col - mlast_col), 0.0)
    col_sc[:, 0:1] = jnp.exp(mprev - mlast_col)

    q = q_ref[...]
    k = k_ref[...]
    v = v_ref[...]
    s = _dot_nt(q, k) * w_intra
    num = _dot(s.astype(BF16), v)
    den = jnp.sum(s, axis=1, keepdims=True)
    wvt = (v.astype(F32) * wl_col).T.astype(BF16)
    wk_sc[...] = k.astype(F32) * wl_col
    qf = q.astype(F32)
    q_sc[0:L, :] = qf
    q_sc[L:L + SUBLANES, :] = jnp.zeros((SUBLANES, M_DH), F32)

    def seq_step(j, carry):
        rows = pl.ds(pl.multiple_of(j * rs, rs), rs)
        in_seq_r = (ridx >= j * rs) & (ridx < (j + 1) * rs)
        cj = c0_ref[j, 0]
        nj = n0_ref[j, 0]
        q2 = q_sc[pl.ds(pl.multiple_of(j * rs, rs), 2 * rs), :].astype(BF16)
        numi_sc[rows, :] = _dot_nt(q2, cj.astype(BF16))[:rs]
        nrow_sc[rows, :] = jnp.broadcast_to(nj, (rs, M_DH))
        wli = col_sc[pl.ds(j * rs + rs - 1, 1), 0:1]
        c1_ref[j, 0] = wli * cj + _dot(wvt, jnp.where(in_seq_r, k, jnp.zeros_like(k)))
        n1_ref[j, 0] = wli * nj + jnp.sum(wk_sc[rows, :], axis=0, keepdims=True)
        return carry

    lax.fori_loop(0, nseq, seq_step, 0, unroll=SEQ_UNROLL)
    num = num + w_inter * numi_sc[...]
    den = den + w_inter * jnp.sum(qf * nrow_sc[...], axis=1, keepdims=True)
    hh = num / jnp.maximum(jnp.abs(den), jnp.exp(-m_new))
    out_ref[...] = _mlstm_finish(hh, o_ref[...], bo_ref[...], ng_ref[...], zm_ref[...])
    mrow_ref[0] = m_new


def _mlstm_sample(l, depth, q, k, v, o, gates, proj, W, c0, n0, m0rows, c_prev, nseq, t_lo):
    L = CHUNK
    rows = q.shape[0]
    nb = rows // L
    nbatch = nb * nseq
    rowblk = lambda b, h: (b, h)
    c_spec = pl.BlockSpec((None, nseq, 1, M_DH, M_DH), lambda b, h: (l, b, h, 0, 0))
    n_spec_in = pl.BlockSpec((None, nseq, 1, 1, M_DH), lambda b, h: (l, b, h, 0, 0))
    n_spec_out = pl.BlockSpec((nseq, 1, 1, M_DH), lambda b, h: (b, h, 0, 0))
    has_prev = c_prev is not None
    in_specs = [
        pl.BlockSpec((L, M_DH), rowblk),
        pl.BlockSpec((L, M_DH), rowblk),
        pl.BlockSpec((L, M_DH), rowblk),
        pl.BlockSpec((L, M_DH), rowblk),
        pl.BlockSpec((L, M_DH), lambda b, h: (b, COL_ZM // M_DH + h)),
        pl.BlockSpec((L, LANES), lambda b, h: (b, 0)),
        _vec_spec(l, M_DH, lambda b, h: h),
        _vec_spec(l, M_DH, lambda b, h: h),
        c_spec,
        n_spec_in,
        pl.BlockSpec((None, 1, L, 1), lambda b, h: (l, h, b, 0)),
        pl.BlockSpec(memory_space=pl.ANY),
    ]
    args = [q, k, v, o, proj, gates, W["bo"], W["m_ng"], c0, n0, m0rows,
            c_prev if has_prev else jnp.zeros((SUBLANES, LANES), F32)]
    return pl.pallas_call(
        functools.partial(_mlstm_sample_kernel, nseq=nseq, t_lo=t_lo),
        grid=(nb, M_HEADS),
        in_specs=in_specs,
        out_specs=[pl.BlockSpec((L, M_DH), rowblk), c_spec, n_spec_out,
                   pl.BlockSpec((1, L, 1), lambda b, h: (h, b, 0))],
        out_shape=[
            jax.ShapeDtypeStruct((rows, M_WIDTH), F32),
            jax.ShapeDtypeStruct((depth, nbatch, M_HEADS, M_DH, M_DH), F32),
            jax.ShapeDtypeStruct((nbatch, M_HEADS, 1, M_DH), F32),
            jax.ShapeDtypeStruct((M_HEADS, rows, 1), F32),
        ],
        scratch_shapes=[
            pltpu.VMEM((LANES, L), F32),
            pltpu.VMEM((LANES, L), F32),
            pltpu.VMEM((L, LANES), F32),
            pltpu.VMEM((L + SUBLANES, M_DH), F32),
            pltpu.VMEM((L, M_DH), F32),
            pltpu.VMEM((L, M_DH), F32),
            pltpu.VMEM((L, M_DH), F32),
        ],
        input_output_aliases={11: 1} if has_prev else {},
        compiler_params=_params(("arbitrary", "arbitrary")),
        name="mlstm_sample",
    )(*args)


def _ssd_prompt_kernel(xs_ref, b_ref, c_ref, zs_ref, h_ref, wdt_ref,
                       cwx_ref, cwb_ref, cwc_ref, cbx_ref, cbb_ref, cbc_ref,
                       dtb_ref, alog_ref, dskip_ref, ng_ref, expand_ref,
                       out_ref, h1_ref, xex_sc, xeb_sc, xec_sc, ht_sc):
    c = pl.program_id(1)
    nc = pl.num_programs(1)
    L = xs_ref.shape[0]
    first = c == 0

    @pl.when(first)
    def _():
        ht_sc[...] = jnp.zeros(ht_sc.shape, F32)

    xs = _conv_rows(xs_ref[...], first, xex_sc, cwx_ref, cbx_ref)
    bm = _conv_rows(b_ref[...], first, xeb_sc, cwb_ref, cbb_ref)
    cm = _conv_rows(c_ref[...], first, xec_sc, cwc_ref, cbc_ref).astype(BF16)

    r2 = lax.broadcasted_iota(jnp.int32, (L, L), 0)
    c2 = lax.broadcasted_iota(jnp.int32, (L, L), 1)
    causal = c2 <= r2
    neg_inf = -jnp.inf
    dt = _softplus(_dot(h_ref[...], wdt_ref[...]) + dtb_ref[...])
    da = dt * (-jnp.exp(alog_ref[...]))
    cs = _dot_mask(causal, da)
    cs_t = cs.T
    ecs = jnp.exp(cs)
    wend = jnp.exp(cs[L - 1:L, :] - cs) * dt

    expand = expand_ref[...]

    def widen(a):
        hi = a.astype(BF16)
        lo = (a - hi.astype(F32)).astype(BF16)
        return _dot(hi, expand) + _dot(lo, expand)

    dte = widen(dt)
    wende = widen(wend)
    ecse = widen(ecs)
    xdt = (xs * dte).astype(BF16)
    wx = (xs * wende).astype(BF16)
    lane = lax.broadcasted_iota(jnp.int32, (1, LANES), 1)
    low_half = lane < S_DH
    zero_slab = jnp.zeros((L, LANES), BF16)
    for g in range(S_GROUPS):
        gcols = slice(g * S_GW, (g + 1) * S_GW)
        scols = slice(g * S_STATE, (g + 1) * S_STATE)
        bg = bm[:, scols]
        cg = cm[:, scols]
        cb = _dot_nt(cg, bg.astype(BF16))
        pairs = []
        for pr in range(S_HPG // 2):
            h0 = g * S_HPG + 2 * pr
            mixes = []
            for hh in (h0, h0 + 1):
                dec = jnp.exp(jnp.where(causal, cs[:, hh:hh + 1] - cs_t[hh:hh + 1, :], neg_inf))
                mixes.append((cb * dec).astype(BF16))
            slab = xdt[:, h0 * S_DH:(h0 + 2) * S_DH]
            rhs = jnp.concatenate([jnp.where(low_half, slab, zero_slab),
                                   jnp.where(low_half, zero_slab, slab)], axis=0)
            pairs.append(_dot(jnp.concatenate(mixes, axis=1), rhs))
        y_intra = jnp.concatenate(pairs, axis=1)
        ht = ht_sc[:, gcols]
        y = y_intra + ecse[:, gcols] * _dot(cg, ht.astype(BF16)) + dskip_ref[:, gcols] * xs[:, gcols]
        y = y * _silu(zs_ref[:, gcols])
        out_ref[:, gcols] = _rms(y, ng_ref[:, gcols])
        ht_sc[:, gcols] = ecse[L - 1:L, gcols] * ht + _dot(bg.T.astype(BF16), wx[:, gcols])

    @pl.when(c == nc - 1)
    def _():
        for pr in range(S_HEADS // 2):
            blk = ht_sc[:, pr * LANES:(pr + 1) * LANES].T
            h1_ref[0, 2 * pr] = blk[:S_DH]
            h1_ref[0, 2 * pr + 1] = blk[S_DH:]


def _ssd_prompt(l, proj, h, W, nb, nc):
    L = CHUNK
    rows = nb * nc * L
    row = lambda b, c: b * nc + c
    conv_w = lambda width, blk: pl.BlockSpec((None, CONV_K, width), lambda b, c: (l, 0, blk))
    return pl.pallas_call(
        _ssd_prompt_kernel,
        grid=(nb, nc),
        in_specs=[
            pl.BlockSpec((L, S_WIDTH), lambda b, c: (row(b, c), COL_XBC // S_WIDTH)),
            pl.BlockSpec((L, S_BC), lambda b, c: (row(b, c), COL_B // S_BC)),
            pl.BlockSpec((L, S_BC), lambda b, c: (row(b, c), COL_C // S_BC)),
            pl.BlockSpec((L, S_WIDTH), lambda b, c: (row(b, c), COL_ZS // S_WIDTH)),
            pl.BlockSpec((L, D_MODEL), lambda b, c: (row(b, c), 0)),
            pl.BlockSpec((None, D_MODEL, LANES), lambda b, c: (l, 0, 0)),
            conv_w(S_WIDTH, 0), conv_w(S_BC, S_WIDTH // S_BC), conv_w(S_BC, S_WIDTH // S_BC + 1),
            _vec_spec(l, S_WIDTH, lambda b, c: 0),
            _vec_spec(l, S_BC, lambda b, c: S_WIDTH // S_BC),
            _vec_spec(l, S_BC, lambda b, c: S_WIDTH // S_BC + 1),
            _vec_spec(l, LANES),
            _vec_spec(l, LANES),
            _vec_spec(l, S_WIDTH),
            _vec_spec(l, S_WIDTH),
            pl.BlockSpec((LANES, S_WIDTH), lambda b, c: (0, 0)),
        ],
        out_specs=[
            pl.BlockSpec((L, S_WIDTH), lambda b, c: (row(b, c), 0)),
            pl.BlockSpec((1, S_HEADS, S_DH, S_STATE), lambda b, c: (b, 0, 0, 0)),
        ],
        out_shape=[
            jax.ShapeDtypeStruct((rows, S_WIDTH), F32),
            jax.ShapeDtypeStruct((nb, S_HEADS, S_DH, S_STATE), F32),
        ],
        scratch_shapes=[
            pltpu.VMEM((SUBLANES, S_WIDTH), F32),
            pltpu.VMEM((SUBLANES, S_BC), F32),
            pltpu.VMEM((SUBLANES, S_BC), F32),
            pltpu.VMEM((S_STATE, S_WIDTH), F32),
        ],
        compiler_params=_params(("arbitrary", "arbitrary")),
        name="ssd_prompt",
    )(proj, proj, proj, proj, h, W["w_dt_c"], W["s_cw"], W["s_cw"], W["s_cw"], W["s_cb"], W["s_cb"], W["s_cb"],
      W["dtb_c"], W["alog_c"], W["dskip_wide"], W["s_ng"], W["expand"])


def _ssd_sample_kernel(xs_ref, b_ref, c_ref, zs_ref, h_ref, wdt_ref, hx_ref, hb_ref, hc_ref,
                       cwx_ref, cwb_ref, cwc_ref, cbx_ref, cbb_ref, cbc_ref,
                       dtb_ref, alog_ref, dskip_ref, ng_ref, expand_ref, h0_ref, alias_ref,
                       out_ref, h1_ref, xex_sc, xeb_sc, xec_sc, tot_sc, cm_sc, yi_sc, *, nseq, t_lo):
    del alias_ref
    L = xs_ref.shape[0]
    rs = L // nseq
    first = True
    xs = _conv_rows(xs_ref[...] + hx_ref[...], first, xex_sc, cwx_ref, cbx_ref)
    bm = _conv_rows(b_ref[...] + hb_ref[...], first, xeb_sc, cwb_ref, cbb_ref)
    cm = _conv_rows(c_ref[...] + hc_ref[...], first, xec_sc, cwc_ref, cbc_ref)
    bmb = bm.astype(BF16)
    cmb = cm.astype(BF16)
    cm_sc[0:L, :] = cm
    cm_sc[L:L + SUBLANES, :] = jnp.zeros((SUBLANES, S_STATE), F32)

    r2, c2, same, valid_c, ridx, valid_r = _block_masks(L, rs, t_lo)
    mask = same & (c2 <= r2) & valid_c
    neg_inf = -jnp.inf
    dt = jnp.where(valid_r, _softplus(_dot(h_ref[...], wdt_ref[...]) + dtb_ref[...]), 0.0)
    da = dt * (-jnp.exp(alog_ref[...]))
    cs = _dot_mask(mask, da)
    sfx = _dot_mask(same & (c2 > r2), da)
    tot_sc[...] = cs + sfx
    cs_t = cs.T
    expand = expand_ref[...]

    def widen(a):
        hi = a.astype(BF16)
        lo = (a - hi.astype(F32)).astype(BF16)
        return _dot(hi, expand) + _dot(lo, expand)

    dte = widen(dt)
    wende = widen(jnp.exp(sfx) * dt)
    ecse = widen(jnp.exp(cs))
    xdt = (xs * dte).astype(BF16)
    wxt = (xs * wende).T.astype(BF16)
    cb = _dot_nt(cmb, bmb)
    lane = lax.broadcasted_iota(jnp.int32, (1, LANES), 1)
    low_half = lane < S_DH
    zero_slab = jnp.zeros((L, LANES), BF16)
    pairs = []
    for pr in range(S_HPG // 2):
        mixes = []
        for hh in (2 * pr, 2 * pr + 1):
            dec = jnp.exp(jnp.where(mask, cs[:, hh:hh + 1] - cs_t[hh:hh + 1, :], neg_inf))
            mixes.append((cb * dec).astype(BF16))
        slab = xdt[:, pr * LANES:(pr + 1) * LANES]
        rhs = jnp.concatenate([jnp.where(low_half, slab, zero_slab),
                               jnp.where(low_half, zero_slab, slab)], axis=0)
        pairs.append(_dot(jnp.concatenate(mixes, axis=1), rhs))
    y_intra = jnp.concatenate(pairs, axis=1)

    def seq_step(j, carry):
        rows = pl.ds(pl.multiple_of(j * rs, rs), rs)
        in_seq_r = (ridx >= j * rs) & (ridx < (j + 1) * rs)
        hj = h0_ref[j]
        c2rows = cm_sc[pl.ds(pl.multiple_of(j * rs, rs), 2 * rs), :].astype(BF16)
        yi_sc[rows, :] = _dot_nt(c2rows, hj.reshape(S_GW, S_STATE).astype(BF16))[:rs]
        upd = _dot(wxt, jnp.where(in_seq_r, bmb, jnp.zeros_like(bmb)))
        dec_j = jnp.exp(tot_sc[pl.ds(j * rs + rs - 1, 1), :])
        for hh in range(S_HPG):
            h1_ref[j, hh] = dec_j[:, hh:hh + 1] * hj[hh] + upd[hh * S_DH:(hh + 1) * S_DH, :]
        return carry

    lax.fori_loop(0, nseq, seq_step, 0, unroll=SEQ_UNROLL)
    y = y_intra + ecse * yi_sc[...] + dskip_ref[...] * xs
    out_ref[...] = _rms(y * _silu(zs_ref[...]), ng_ref[...])


def _ssd_sample(l, depth, proj, h, hist, W, h0, h_prev, nseq, t_lo):
    L = CHUNK
    rows = proj.shape[0]
    nb = rows // L
    xblk = COL_XBC // S_GW
    bblk = COL_B // S_STATE
    cblk = COL_C // S_STATE
    hb_blk = S_WIDTH // S_STATE
    hc_blk = hb_blk + S_GROUPS
    has_prev = h_prev is not None
    conv_w = lambda width, blk: pl.BlockSpec((None, CONV_K, width), lambda b, g: (l, 0, blk(g)))
    h_spec = pl.BlockSpec((None, nseq, S_HPG, S_DH, S_STATE), lambda b, g: (l, b, g, 0, 0))
    in_specs = [
        pl.BlockSpec((L, S_GW), lambda b, g: (b, xblk + g)),
        pl.BlockSpec((L, S_STATE), lambda b, g: (b, bblk + g)),
        pl.BlockSpec((L, S_STATE), lambda b, g: (b, cblk + g)),
        pl.BlockSpec((L, S_GW), lambda b, g: (b, COL_ZS // S_GW + g)),
        pl.BlockSpec((L, D_MODEL), lambda b, g: (b, 0)),
        pl.BlockSpec((None, D_MODEL, LANES), lambda b, g: (l, 0, g)),
        pl.BlockSpec((None, L, S_GW), lambda b, g: (l, b, g)),
        pl.BlockSpec((None, L, S_STATE), lambda b, g: (l, b, hb_blk + g)),
        pl.BlockSpec((None, L, S_STATE), lambda b, g: (l, b, hc_blk + g)),
        conv_w(S_GW, lambda g: g), conv_w(S_STATE, lambda g: hb_blk + g), conv_w(S_STATE, lambda g: hc_blk + g),
        _vec_spec(l, S_GW, lambda b, g: g),
        _vec_spec(l, S_STATE, lambda b, g: hb_blk + g),
        _vec_spec(l, S_STATE, lambda b, g: hc_blk + g),
        _vec_spec(l, LANES, lambda b, g: g),
        _vec_spec(l, LANES, lambda b, g: g),
        _vec_spec(l, S_GW, lambda b, g: g),
        _vec_spec(l, S_GW, lambda b, g: g),
        pl.BlockSpec((LANES, S_GW), lambda b, g: (0, 0)),
        h_spec,
        pl.BlockSpec(memory_space=pl.ANY),
    ]
    args = [proj, proj, proj, proj, h, W["w_dt_g"], hist, hist, hist,
            W["s_cw"], W["s_cw"], W["s_cw"], W["s_cb"], W["s_cb"], W["s_cb"],
            W["dtb_g"], W["alog_g"], W["dskip_wide"], W["s_ng"], W["expand"], h0,
            h_prev if has_prev else jnp.zeros((SUBLANES, LANES), F32)]
    return pl.pallas_call(
        functools.partial(_ssd_sample_kernel, nseq=nseq, t_lo=t_lo),
        grid=(nb, S_GROUPS),
        in_specs=in_specs,
        out_specs=[pl.BlockSpec((L, S_GW), lambda b, g: (b, g)), h_spec],
        out_shape=[
            jax.ShapeDtypeStruct((rows, S_WIDTH), F32),
            jax.ShapeDtypeStruct((depth, nb * nseq, S_HEADS, S_DH, S_STATE), F32),
        ],
        scratch_shapes=[
            pltpu.VMEM((SUBLANES, S_GW), F32),
            pltpu.VMEM((SUBLANES, S_STATE), F32),
            pltpu.VMEM((SUBLANES, S_STATE), F32),
            pltpu.VMEM((L, LANES), F32),
            pltpu.VMEM((L + SUBLANES, S_STATE), F32),
            pltpu.VMEM((L, S_GW), F32),
        ],
        input_output_aliases={21: 1} if has_prev else {},
        compiler_params=_params(("arbitrary", "arbitrary")),
        name="ssd_sample",
    )(*args)


def _outproj_kernel(*refs, final, emit_vn, rs, t_lo):
    (x_ref, om_ref, os_ref, u_ref, v_ref, zc_ref, wmix_ref, bias_ref, gv_ref,
     w0_ref, w1_ref, w2_ref, w3_ref, g_ref, y_ref) = refs[:15]
    rest = list(refs[15:])
    h_ref = None if final else rest.pop(0)
    vn_ref = rest.pop(0) if emit_vn else None
    (oc_sc,) = rest
    L = C_CHUNK
    r2, c2, same, valid_c, _, _ = _block_masks(L, min(rs, L), t_lo)
    mask = same & (c2 <= r2) & valid_c
    for g in range(C_GROUPS):
        cols = slice(g * C_DG, (g + 1) * C_DG)
        w = jnp.where(mask, wmix_ref[g], 0.0).astype(BF16)
        for ch in range(x_ref.shape[0] // L):
            rows = slice(ch * L, (ch + 1) * L)
            vn = _rms(v_ref[rows, cols], gv_ref[:, cols])
            if emit_vn:
                vn_ref[rows, cols] = vn
            mixed = _dot(w, vn.astype(BF16)) + bias_ref[g]
            oc_sc[rows, cols] = (u_ref[rows, cols] * mixed * _silu(zc_ref[rows, cols])).astype(BF16)
    half = S_WIDTH // 2
    acc = _dot(om_ref[...].astype(BF16), w0_ref[...])
    acc = acc + _dot(os_ref[:, :half].astype(BF16), w1_ref[...])
    acc = acc + _dot(os_ref[:, half:].astype(BF16), w2_ref[...])
    acc = acc + _dot(oc_sc[...], w3_ref[...])
    y = x_ref[...] + acc
    if t_lo:
        ridx = lax.broadcasted_iota(jnp.int32, (y.shape[0], 1), 0)
        y = jnp.where((ridx & (rs - 1)) >= t_lo, y, 0.0)
    if final:
        y_ref[...] = _rms(y, g_ref[...])
    else:
        y_ref[...] = y
        h_ref[...] = _rms(y, g_ref[...]).astype(BF16)


def _outproj(l, x, om, os_, proj, wmix, bias, gv, wo, g, final, emit_vn, rs, t_lo):
    rows = x.shape[0]
    tm = min(256, rows)
    kb = M_WIDTH
    wspec = lambda r: pl.BlockSpec((None, kb, D_MODEL), lambda i: (l, r, 0), pipeline_mode=pl.Buffered(1))
    row_spec = pl.BlockSpec((tm, D_MODEL), lambda i: (i, 0))
    out_shape = [jax.ShapeDtypeStruct((rows, D_MODEL), F32)]
    out_specs = [row_spec]
    if not final:
        out_shape.append(jax.ShapeDtypeStruct((rows, D_MODEL), BF16))
        out_specs.append(row_spec)
    if emit_vn:
        out_shape.append(jax.ShapeDtypeStruct((rows, C_WIDTH), F32))
        out_specs.append(pl.BlockSpec((tm, C_WIDTH), lambda i: (i, 0)))
    return pl.pallas_call(
        functools.partial(_outproj_kernel, final=final, emit_vn=emit_vn, rs=rs, t_lo=t_lo),
        grid=(rows // tm,),
        in_specs=[
            row_spec,
            pl.BlockSpec((tm, M_WIDTH), lambda i: (i, 0)),
            pl.BlockSpec((tm, S_WIDTH), lambda i: (i, 0)),
            pl.BlockSpec((tm, C_WIDTH), lambda i: (i, COL_U // C_WIDTH)),
            pl.BlockSpec((tm, C_WIDTH), lambda i: (i, COL_V // C_WIDTH)),
            pl.BlockSpec((tm, C_WIDTH), lambda i: (i, COL_ZC // C_WIDTH)),
            pl.BlockSpec((None, C_GROUPS, C_CHUNK, C_CHUNK), lambda i: (l, 0, 0, 0)),
            pl.BlockSpec((None, C_GROUPS, C_CHUNK, 1), lambda i: (l, 0, 0, 0)),
            _vec_spec(l, C_WIDTH),
            wspec(0), wspec(1), wspec(2), wspec(3),
            pl.BlockSpec((1, D_MODEL), lambda i: (0, 0)) if final else _vec_spec(l + 1, D_MODEL),
        ],
        out_specs=out_specs,
        out_shape=out_shape,
        scratch_shapes=[pltpu.VMEM((tm, C_WIDTH), BF16)],
        compiler_params=_params(("arbitrary",)),
        name="outproj",
    )(x, om, os_, proj, proj, proj, wmix, bias, gv, wo, wo, wo, wo, g)


def _heads_compact(a):
    return jnp.pad(a, ((0, 0), (0, LANES - S_HEADS)))[:, None, :]


def _heads_grouped(a):
    a = a.reshape(a.shape[0], S_GROUPS, S_HPG)
    return jnp.pad(a, ((0, 0), (0, 0), (0, LANES - S_HPG))).reshape(a.shape[0], 1, S_GROUPS * LANES)


def _prepare_weights(norm_g, w_in, m_conv_w, m_conv_b, m_w_qk, m_w_vo, m_b_o, m_w_gate, m_b_gate,
                     m_norm_g, s_conv_w, s_conv_b, s_dt_bias, s_A_log, s_D, s_norm_g, c_v_norm_g,
                     c_w_s, c_b_s, w_out):
    depth = w_in.shape[0]
    vec = lambda a: a.reshape(depth, 1, -1)
    dt0 = PROJ_HEAD
    w_all = w_in.astype(BF16)
    w_dt = w_in[:, :, dt0:dt0 + S_HEADS]
    w_dt_g = jnp.pad(w_dt.reshape(depth, D_MODEL, S_GROUPS, S_HPG),
                     ((0, 0), (0, 0), (0, 0), (0, LANES - S_HPG))).reshape(depth, D_MODEL, -1).astype(BF16)
    w_dt_c = jnp.pad(w_dt, ((0, 0), (0, 0), (0, LANES - S_HEADS))).astype(BF16)
    wg = m_w_gate.reshape(depth, M_HEADS, 3, M_DH, 2 * M_HEADS)
    wg = jnp.pad(wg, ((0, 0),) * 4 + ((0, LANES - 2 * M_HEADS),)).astype(BF16)
    head_of_lane = jnp.arange(S_WIDTH) // S_DH
    expand = (jnp.arange(LANES)[:, None] == head_of_lane[None, :]).astype(BF16)
    return dict(
        norm_g=vec(norm_g), w_all=w_all, w_tail=w_all[:, :, dt0 + S_HEADS:], w_dt_c=w_dt_c, w_dt_g=w_dt_g,
        m_cw=m_conv_w, m_cb=vec(m_conv_b),
        wqk=m_w_qk.astype(BF16), wvo=m_w_vo.astype(BF16), wg=wg,
        bg=vec(jnp.pad(m_b_gate, ((0, 0), (0, LANES - 2 * M_HEADS)))),
        bo=vec(m_b_o), m_ng=vec(m_norm_g),
        s_cw=s_conv_w, s_cb=vec(s_conv_b),
        dtb_c=_heads_compact(s_dt_bias), alog_c=_heads_compact(s_A_log),
        dtb_g=_heads_grouped(s_dt_bias), alog_g=_heads_grouped(s_A_log),
        dskip_wide=vec(jnp.repeat(s_D, S_DH, axis=1)), s_ng=vec(s_norm_g), expand=expand,
        gv=vec(c_v_norm_g), wo=w_out.astype(BF16),
    )


def kernel(x_prompt, x_sample, state_mlstm_C, state_mlstm_n, state_mlstm_m, state_mlstm_conv, state_ssm, state_ssm_conv, norm_g, w_in, m_conv_w, m_conv_b, m_w_qk, m_w_vo, m_b_o, m_w_gate, m_b_gate, m_norm_g, s_conv_w, s_conv_b, s_dt_bias, s_A_log, s_D, s_norm_g, c_v_norm_g, c_w_s, c_b_s, w_out, final_norm_g):
    bp, seq, _ = x_prompt.shape
    bs, dec_seq, _ = x_sample.shape
    depth = w_in.shape[0]
    t_lo = SAMPLE_ROWS - dec_seq
    hist_lo = t_lo - (CONV_K - 1)
    nseq = CHUNK // SAMPLE_ROWS
    W = _prepare_weights(norm_g, w_in, m_conv_w, m_conv_b, m_w_qk, m_w_vo, m_b_o, m_w_gate, m_b_gate,
                         m_norm_g, s_conv_w, s_conv_b, s_dt_bias, s_A_log, s_D, s_norm_g, c_v_norm_g,
                         c_w_s, c_b_s, w_out)
    fg = final_norm_g[None, :]
    yp = x_prompt.reshape(bp * seq, D_MODEL)
    ys = jnp.pad(x_sample, ((0, 0), (t_lo, 0), (0, 0))).reshape(bs * SAMPLE_ROWS, D_MODEL)

    pad_hist = ((0, 0), (0, 0), (hist_lo, dec_seq), (0, 0))
    hist_m = jnp.pad(state_mlstm_conv, pad_hist).reshape(depth, bs * SAMPLE_ROWS, M_WIDTH)
    hist_s = jnp.pad(state_ssm_conv, pad_hist).reshape(depth, bs * SAMPLE_ROWS, -1)
    m0rows = jnp.repeat(jnp.swapaxes(state_mlstm_m, 1, 2)[..., None], SAMPLE_ROWS, axis=2)
    n0 = state_mlstm_n[:, :, :, None, :]
    reps = C_CHUNK // SAMPLE_ROWS
    w4 = jnp.pad(c_w_s[:, :, :dec_seq, :dec_seq], ((0, 0), (0, 0), (t_lo, 0), (t_lo, 0)))
    b4 = jnp.pad(c_b_s[:, :, :dec_seq], ((0, 0), (0, 0), (t_lo, 0)))
    wmix_s = jnp.tile(w4, (1, 1, reps, reps))
    bias_s = jnp.tile(b4, (1, 1, reps))[..., None]
    bias_p = c_b_s[..., None]

    hp = _norm(0, yp, W["norm_g"])
    hs = _norm(0, ys, W["norm_g"])
    outs_p, outs_s = [], []
    c_all = h_all = None
    mc = min(M_CHUNK, seq)
    sc = min(CHUNK, seq)
    tbp = min(512, bp * seq)
    tbs = min(512, bs * SAMPLE_ROWS)
    for l in range(depth):
        final = l == depth - 1
        proj = _inproj(l, hp, W["w_all"], W["w_tail"])
        q, k, v, o, gates = _mlstm_proj(l, proj, None, W, max(1, seq // tbp), tbp)
        out_m, c1, n1, m1 = _mlstm_prompt(l, q, k, v, o, gates, proj, W, bp, seq // mc)
        out_s, h1 = _ssd_prompt(l, proj, hp, W, bp, seq // sc)
        res = _outproj(l, yp, out_m, out_s, proj, c_w_s, bias_p, W["gv"], W["wo"],
                       fg if final else W["norm_g"], final, False, CHUNK, 0)
        yp, hp = res[0], (None if final else res[1])
        pj = proj.reshape(bp, seq, PROJ_MAIN)
        outs_p.append((
            c1, n1.reshape(bp, M_HEADS, M_DH), m1[:, :M_HEADS, 0],
            pj[:, seq - (CONV_K - 1):, COL_XM:COL_XM + M_WIDTH], h1,
            pj[:, seq - (CONV_K - 1):, COL_XBC:COL_XBC + S_WIDTH + 2 * S_BC]))
        proj = _inproj(l, hs, W["w_all"], W["w_tail"])
        q, k, v, o, gates = _mlstm_proj(l, proj, hist_m, W, 1, tbs)
        out_m, c_all, n1, mrow = _mlstm_sample(l, depth, q, k, v, o, gates, proj, W, state_mlstm_C, n0,
                                               m0rows, c_all, nseq, t_lo)
        out_s, h_all = _ssd_sample(l, depth, proj, hs, hist_s, W, state_ssm, h_all, nseq, t_lo)
        res = _outproj(l, ys, out_m, out_s, proj, wmix_s, bias_s, W["gv"], W["wo"],
                       fg if final else W["norm_g"], final, True, SAMPLE_ROWS, t_lo)
        ys, hs, vn = res[0], (None if final else res[1]), res[-1]
        pj = proj.reshape(bs, SAMPLE_ROWS, PROJ_MAIN)
        outs_s.append((
            n1.reshape(bs, M_HEADS, M_DH), mrow[:, SAMPLE_ROWS - 1::SAMPLE_ROWS, 0].T,
            pj[:, SAMPLE_ROWS - (CONV_K - 1):, COL_XM:COL_XM + M_WIDTH],
            pj[:, SAMPLE_ROWS - (CONV_K - 1):, COL_XBC:COL_XBC + S_WIDTH + 2 * S_BC],
            vn.reshape(bs, SAMPLE_ROWS, C_WIDTH)[:, t_lo:]))
    p_out = [jnp.stack([s[i] for s in outs_p]) for i in range(6)]
    s_n, s_m, s_mconv, s_sconv, s_cv = [jnp.stack([s[i] for s in outs_s]) for i in range(5)]
    y_prompt = yp.reshape(bp, seq, D_MODEL)
    y_sample = ys.reshape(bs, SAMPLE_ROWS, D_MODEL)[:, t_lo:]
    return (y_prompt, y_sample, *p_out, c_all, s_n, s_m, s_mconv, h_all, s_sconv, s_cv)
```

```python
import functools

import jax
import jax.numpy as jnp
from jax import lax
from jax.experimental import pallas as pl
from jax.experimental.pallas import tpu as pltpu

F32 = jnp.float32
BF16 = jnp.bfloat16

D_MODEL = 2048
MIX_WIDTH = 2 * D_MODEL
M_WIDTH = MIX_WIDTH // 4
M_HEADS = 4
M_DH = M_WIDTH // M_HEADS
S_WIDTH = MIX_WIDTH // 2
S_DH = 64
S_HEADS = S_WIDTH // S_DH
S_GROUPS = 4
S_HPG = S_HEADS // S_GROUPS
S_STATE = 128
S_GW = S_HPG * S_DH
S_BC = S_GROUPS * S_STATE
C_WIDTH = MIX_WIDTH // 4
C_GROUPS = 4
C_DG = C_WIDTH // C_GROUPS
C_CHUNK = 128
CONV_K = 4
EPS = 1e-6

LANES = 128
SUBLANES = 8
SAMPLE_ROWS = 8
CHUNK = 128
M_CHUNK = 256
SEQ_UNROLL = 8
VMEM_LIMIT = 56 * 1024 * 1024

COL_XM, COL_ZM, COL_ZS, COL_XBC, COL_U, COL_V, COL_ZC = 0, 1024, 2048, 4096, 7168, 8192, 9216
PROJ_MAIN = 10240
PROJ_HEAD = COL_U
COL_B = COL_XBC + S_WIDTH
COL_C = COL_B + S_BC


def _dot(a, b):
    return jnp.dot(a, b, preferred_element_type=F32)


def _dot_nt(a, b):
    return lax.dot_general(a, b, (((1,), (1,)), ((), ())), preferred_element_type=F32)


def _dot_mask(mask, x):
    m = jnp.where(mask, 1.0, 0.0).astype(BF16)
    hi = x.astype(BF16)
    r1 = x - hi.astype(F32)
    mid = r1.astype(BF16)
    lo = (r1 - mid.astype(F32)).astype(BF16)
    return _dot(m, hi) + _dot(m, mid) + _dot(m, lo)


def _sigmoid(x):
    return 0.5 * jnp.tanh(0.5 * x) + 0.5


def _silu(x):
    h = 0.5 * x
    return h * jnp.tanh(h) + h


def _softplus(x):
    return jnp.maximum(x, 0.0) + jnp.log1p(jnp.exp(-jnp.abs(x)))


def _log_sigmoid(x):
    return jnp.minimum(x, 0.0) - jnp.log1p(jnp.exp(-jnp.abs(x)))


def _rms(x, g):
    return x * lax.rsqrt(jnp.mean(x * x, axis=-1, keepdims=True) + EPS) * g


def _params(sem):
    return pltpu.CompilerParams(dimension_semantics=sem, vmem_limit_bytes=VMEM_LIMIT)


def _vec_spec(l, n, col=None):
    if col is None:
        return pl.BlockSpec((None, 1, n), lambda *ids: (l, 0, 0))
    return pl.BlockSpec((None, 1, n), lambda *ids: (l, 0, col(*ids)))


def _norm_kernel(x_ref, g_ref, h_ref):
    h_ref[...] = _rms(x_ref[...], g_ref[...]).astype(BF16)


def _norm(l, x, g):
    rows = x.shape[0]
    tb = min(1024, rows)
    return pl.pallas_call(
        _norm_kernel,
        grid=(rows // tb,),
        in_specs=[pl.BlockSpec((tb, D_MODEL), lambda i: (i, 0)), _vec_spec(l, D_MODEL)],
        out_specs=pl.BlockSpec((tb, D_MODEL), lambda i: (i, 0)),
        out_shape=jax.ShapeDtypeStruct((rows, D_MODEL), BF16),
        compiler_params=_params(("arbitrary",)),
        name="norm",
    )(x, g)


def _inproj_kernel(h_ref, wa_ref, wb_ref, proj_ref, *, n_head):
    j = pl.program_id(0)

    @pl.when(j < n_head)
    def _():
        proj_ref[...] = _dot(h_ref[...], wa_ref[...])

    @pl.when(j >= n_head)
    def _():
        proj_ref[...] = _dot(h_ref[...], wb_ref[...])


def _inproj(l, h, w_all, w_tail):
    rows = h.shape[0]
    tm = min(1024, rows)
    tn = 1024
    n_head = PROJ_HEAD // tn
    return pl.pallas_call(
        functools.partial(_inproj_kernel, n_head=n_head),
        grid=(PROJ_MAIN // tn, rows // tm),
        in_specs=[
            pl.BlockSpec((tm, D_MODEL), lambda j, i: (i, 0)),
            pl.BlockSpec((None, D_MODEL, tn), lambda j, i: (l, 0, jnp.minimum(j, n_head - 1))),
            pl.BlockSpec((None, D_MODEL, tn), lambda j, i: (l, 0, jnp.maximum(j - n_head, 0))),
        ],
        out_specs=pl.BlockSpec((tm, tn), lambda j, i: (i, j)),
        out_shape=jax.ShapeDtypeStruct((rows, PROJ_MAIN), F32),
        compiler_params=_params(("arbitrary", "arbitrary")),
        name="inproj",
    )(h, w_all, w_tail)


def _conv_rows(x, first, prev_sc, cw_ref, cb_ref):
    tb = x.shape[0]

    @pl.when(first)
    def _():
        prev_sc[...] = jnp.zeros(prev_sc.shape, F32)

    xe = jnp.concatenate([prev_sc[...], x], axis=0)
    acc = cb_ref[...] + cw_ref[CONV_K - 1:CONV_K, :] * x
    for d in range(1, CONV_K):
        acc = acc + cw_ref[CONV_K - 1 - d:CONV_K - d, :] * pltpu.roll(xe, d, 0)[SUBLANES:, :]
    prev_sc[...] = x[tb - SUBLANES:, :]
    return _silu(acc)


def _mlstm_proj_kernel(*refs, has_hist, blocks_per_seq):
    if has_hist:
        (xm_ref, he_ref, cw_ref, cb_ref, wqk_ref, wvo_ref, wg_ref, bg_ref,
         q_ref, k_ref, v_ref, o_ref, gates_ref, xe_sc) = refs
    else:
        (xm_ref, cw_ref, cb_ref, wqk_ref, wvo_ref, wg_ref, bg_ref,
         q_ref, k_ref, v_ref, o_ref, gates_ref, xe_sc) = refs
    i = pl.program_id(0)
    x = xm_ref[...]
    xin = x + he_ref[...] if has_hist else x
    xmc = _conv_rows(xin, i % blocks_per_seq == 0, xe_sc, cw_ref, cb_ref)
    tb = x.shape[0]
    gates = jnp.broadcast_to(bg_ref[...], (tb, LANES))
    for h in range(M_HEADS):
        cols = slice(h * M_DH, (h + 1) * M_DH)
        qk = _dot(xmc[:, cols].astype(BF16), wqk_ref[h])
        vo = _dot(x[:, cols].astype(BF16), wvo_ref[h])
        qb = qk[:, :M_DH].astype(BF16)
        kb = qk[:, M_DH:].astype(BF16)
        vb = vo[:, :M_DH].astype(BF16)
        gates = gates + _dot(qb, wg_ref[h, 0]) + _dot(kb, wg_ref[h, 1]) + _dot(vb, wg_ref[h, 2])
        q_ref[:, cols] = qb
        k_ref[:, cols] = (qk[:, M_DH:] * (M_DH ** -0.5)).astype(BF16)
        v_ref[:, cols] = vb
        o_ref[:, cols] = vo[:, M_DH:]
    lane = lax.broadcasted_iota(jnp.int32, (tb, LANES), 1)
    gates_ref[...] = jnp.where(lane < M_HEADS, gates, _log_sigmoid(gates))


def _mlstm_proj(l, proj, hist, W, blocks_per_seq, tb):
    rows = proj.shape[0]
    has_hist = hist is not None
    in_specs = [pl.BlockSpec((tb, M_WIDTH), lambda i: (i, COL_XM // M_WIDTH))]
    args = [proj]
    if has_hist:
        in_specs.append(pl.BlockSpec((None, tb, M_WIDTH), lambda i: (l, i, 0)))
        args.append(hist)
    in_specs += [
        pl.BlockSpec((None, CONV_K, M_WIDTH), lambda i: (l, 0, 0)),
        _vec_spec(l, M_WIDTH),
        pl.BlockSpec((None, M_HEADS, M_DH, 2 * M_DH), lambda i: (l, 0, 0, 0)),
        pl.BlockSpec((None, M_HEADS, M_DH, 2 * M_DH), lambda i: (l, 0, 0, 0)),
        pl.BlockSpec((None, M_HEADS, 3, M_DH, LANES), lambda i: (l, 0, 0, 0, 0)),
        _vec_spec(l, LANES),
    ]
    args += [W["m_cw"], W["m_cb"], W["wqk"], W["wvo"], W["wg"], W["bg"]]
    row_spec = pl.BlockSpec((tb, M_WIDTH), lambda i: (i, 0))
    return pl.pallas_call(
        functools.partial(_mlstm_proj_kernel, has_hist=has_hist, blocks_per_seq=blocks_per_seq),
        grid=(rows // tb,),
        in_specs=in_specs,
        out_specs=[row_spec, row_spec, row_spec, row_spec, pl.BlockSpec((tb, LANES), lambda i: (i, 0))],
        out_shape=[
            jax.ShapeDtypeStruct((rows, M_WIDTH), BF16),
            jax.ShapeDtypeStruct((rows, M_WIDTH), BF16),
            jax.ShapeDtypeStruct((rows, M_WIDTH), BF16),
            jax.ShapeDtypeStruct((rows, M_WIDTH), F32),
            jax.ShapeDtypeStruct((rows, LANES), F32),
        ],
        scratch_shapes=[pltpu.VMEM((SUBLANES, M_WIDTH), F32)],
        compiler_params=_params(("arbitrary",)),
        name="mlstm_proj",
    )(*args)


def _mlstm_finish(hh, o, bo, ng, zm):
    hm = _sigmoid(o + bo) * hh
    return _rms(hm, ng) * _silu(zm)


def _mlstm_prep(gates_ref):
    L = gates_ref.shape[0]
    r2 = lax.broadcasted_iota(jnp.int32, (L, L), 0)
    c2 = lax.broadcasted_iota(jnp.int32, (L, L), 1)
    causal = c2 <= r2
    g = gates_ref[...]
    b_all = _dot_mask(causal, g)
    return causal, g, b_all, g.T, b_all.T


def _mlstm_head(h, prep, q_ref, k_ref, v_ref, o_ref, zm_ref, bo_ref, ng_ref, c1_ref, n1_ref, m1_ref, emit):
    causal, g, b_all, g_t, b_t = prep
    L = q_ref.shape[0]
    neg_inf = -jnp.inf
    cols = slice(h * M_DH, (h + 1) * M_DH)
    f = M_HEADS + h
    g_row = g_t[h:h + 1, :] - b_t[f:f + 1, :]
    b_col = b_all[:, f:f + 1]
    g_col = g[:, h:h + 1] - b_col
    mprev = m1_ref[0, h:h + 1, 0:1]
    gm = jnp.where(causal, g_row, neg_inf)
    m_col = jnp.maximum(mprev, jnp.max(gm, axis=1, keepdims=True))
    w_intra = jnp.exp(gm - m_col)
    w_inter = jnp.exp(mprev - m_col)
    m_new = b_col + m_col
    m_last = m_col[L - 1:L, :]
    wl_col = jnp.exp(g_col - m_last)
    wli = jnp.exp(mprev - m_last)
    q = q_ref[:, cols]
    k = k_ref[:, cols]
    v = v_ref[:, cols]
    cst = c1_ref[0, h]
    nst = n1_ref[0, h]
    s = _dot_nt(q, k) * w_intra
    num = _dot(s.astype(BF16), v) + w_inter * _dot_nt(q, cst.astype(BF16))
    den = (jnp.sum(s, axis=1, keepdims=True)
           + w_inter * jnp.sum(q.astype(F32) * nst, axis=1, keepdims=True))
    hh = num / jnp.maximum(jnp.abs(den), jnp.exp(-m_new))
    emit(cols, _mlstm_finish(hh, o_ref[:, cols], bo_ref[:, cols], ng_ref[:, cols], zm_ref[:, cols]))
    c1_ref[0, h] = wli * cst + _dot((v.astype(F32) * wl_col).T.astype(BF16), k)
    n1_ref[0, h] = wli * nst + jnp.sum(k.astype(F32) * wl_col, axis=0, keepdims=True)
    m1_ref[0, h:h + 1, :] = jnp.broadcast_to(m_new[L - 1:L, :], (1, LANES))


def _block_masks(L, rs, t_lo):
    shift = rs.bit_length() - 1
    r2 = lax.broadcasted_iota(jnp.int32, (L, L), 0)
    c2 = lax.broadcasted_iota(jnp.int32, (L, L), 1)
    same = (r2 >> shift) == (c2 >> shift)
    valid_c = (c2 & (rs - 1)) >= t_lo
    ridx = lax.broadcasted_iota(jnp.int32, (L, 1), 0)
    valid_r = (ridx & (rs - 1)) >= t_lo
    return r2, c2, same, valid_c, ridx, valid_r


def _mlstm_sample_kernel(q_ref, k_ref, v_ref, o_ref, zm_ref, gates_ref, bo_ref, ng_ref,
                         c0_ref, n0_ref, m0_ref, alias_ref,
                         out_ref, c1_ref, n1_ref, mrow_ref,
                         gt_sc, bt_sc, col_sc, q_sc, numi_sc, nrow_sc, wk_sc, *, nseq, t_lo):
    del alias_ref
    h = pl.program_id(1)
    L = q_ref.shape[0]
    rs = L // nseq
    r2, c2, same, valid_c, ridx, valid_r = _block_masks(L, rs, t_lo)
    eye = r2 == c2
    neg_inf = -jnp.inf
    g = gates_ref[...]
    b_all = _dot_mask(same & (c2 <= r2), jnp.where(valid_r, g, 0.0))
    gt_sc[...] = g.T
    bt_sc[...] = b_all.T
    lane_g = lax.broadcasted_iota(jnp.int32, (1, LANES), 1)
    i_col = jnp.sum(jnp.where(lane_g == h, g, 0.0), axis=1, keepdims=True)
    b_col = jnp.sum(jnp.where(lane_g == h + M_HEADS, b_all, 0.0), axis=1, keepdims=True)
    g_row = gt_sc[pl.ds(h, 1), :] - bt_sc[pl.ds(h + M_HEADS, 1), :]
    g_col = i_col - b_col
    mask = same & (c2 <= r2) & valid_c
    mprev = m0_ref[0]
    gm = jnp.where(mask, g_row, neg_inf)
    m_col = jnp.maximum(mprev, jnp.max(gm, axis=1, keepdims=True))
    m_row = jnp.sum(jnp.where(eye, m_col, 0.0), axis=0, keepdims=True)
    mlast_col = jnp.max(jnp.where(same, m_row, neg_inf), axis=1, keepdims=True)
    w_intra = jnp.exp(gm - m_col)
    w_inter = jnp.exp(mprev - m_col)
    m_new = b_col + m_col
    wl_col = jnp.where(valid_r, jnp.exp(g_col - mlast_col), 0.0)
    col_sc[:, 0:1] = jnp.exp(mprev - mlast_col)

    q = q_ref[...]
    k = k_ref[...]
    v = v_ref[...]
    s = _dot_nt(q, k) * w_intra
    num = _dot(s.astype(BF16), v)
    den = jnp.sum(s, axis=1, keepdims=True)
    wvt = (v.astype(F32) * wl_col).T.astype(BF16)
    wk_sc[...] = k.astype(F32) * wl_col
    qf = q.astype(F32)
    q_sc[0:L, :] = qf
    q_sc[L:L + SUBLANES, :] = jnp.zeros((SUBLANES, M_DH), F32)

    def seq_step(j, carry):
        rows = pl.ds(pl.multiple_of(j * rs, rs), rs)
        in_seq_r = (ridx >= j * rs) & (ridx < (j + 1) * rs)
        cj = c0_ref[j, 0]
        nj = n0_ref[j, 0]
        q2 = q_sc[pl.ds(pl.multiple_of(j * rs, rs), 2 * rs), :].astype(BF16)
        numi_sc[rows, :] = _dot_nt(q2, cj.astype(BF16))[:rs]
        nrow_sc[rows, :] = jnp.broadcast_to(nj, (rs, M_DH))
        wli = col_sc[pl.ds(j * rs + rs - 1, 1), 0:1]
        c1_ref[j, 0] = wli * cj + _dot(wvt, jnp.where(in_seq_r, k, jnp.zeros_like(k)))
        n1_ref[j, 0] = wli * nj + jnp.sum(wk_sc[rows, :], axis=0, keepdims=True)
        return carry

    lax.fori_loop(0, nseq, seq_step, 0, unroll=SEQ_UNROLL)
    num = num + w_inter * numi_sc[...]
    den = den + w_inter * jnp.sum(qf * nrow_sc[...], axis=1, keepdims=True)
    hh = num / jnp.maximum(jnp.abs(den), jnp.exp(-m_new))
    out_ref[...] = _mlstm_finish(hh, o_ref[...], bo_ref[...], ng_ref[...], zm_ref[...])
    mrow_ref[0] = m_new


def _mlstm_sample(l, depth, q, k, v, o, gates, proj, W, c0, n0, m0rows, c_prev, nseq, t_lo):
    L = CHUNK
    rows = q.shape[0]
    nb = rows // L
    nbatch = nb * nseq
    rowblk = lambda b, h: (b, h)
    c_spec = pl.BlockSpec((None, nseq, 1, M_DH, M_DH), lambda b, h: (l, b, h, 0, 0))
    n_spec_in = pl.BlockSpec((None, nseq, 1, 1, M_DH), lambda b, h: (l, b, h, 0, 0))
    n_spec_out = pl.BlockSpec((nseq, 1, 1, M_DH), lambda b, h: (b, h, 0, 0))
    has_prev = c_prev is not None
    in_specs = [
        pl.BlockSpec((L, M_DH), rowblk),
        pl.BlockSpec((L, M_DH), rowblk),
        pl.BlockSpec((L, M_DH), rowblk),
        pl.BlockSpec((L, M_DH), rowblk),
        pl.BlockSpec((L, M_DH), lambda b, h: (b, COL_ZM // M_DH + h)),
        pl.BlockSpec((L, LANES), lambda b, h: (b, 0)),
        _vec_spec(l, M_DH, lambda b, h: h),
        _vec_spec(l, M_DH, lambda b, h: h),
        c_spec,
        n_spec_in,
        pl.BlockSpec((None, 1, L, 1), lambda b, h: (l, h, b, 0)),
        pl.BlockSpec(memory_space=pl.ANY),
    ]
    args = [q, k, v, o, proj, gates, W["bo"], W["m_ng"], c0, n0, m0rows,
            c_prev if has_prev else jnp.zeros((SUBLANES, LANES), F32)]
    return pl.pallas_call(
        functools.partial(_mlstm_sample_kernel, nseq=nseq, t_lo=t_lo),
        grid=(nb, M_HEADS),
        in_specs=in_specs,
        out_specs=[pl.BlockSpec((L, M_DH), rowblk), c_spec, n_spec_out,
                   pl.BlockSpec((1, L, 1), lambda b, h: (h, b, 0))],
        out_shape=[
            jax.ShapeDtypeStruct((rows, M_WIDTH), F32),
            jax.ShapeDtypeStruct((depth, nbatch, M_HEADS, M_DH, M_DH), F32),
            jax.ShapeDtypeStruct((nbatch, M_HEADS, 1, M_DH), F32),
            jax.ShapeDtypeStruct((M_HEADS, rows, 1), F32),
        ],
        scratch_shapes=[
            pltpu.VMEM((LANES, L), F32),
            pltpu.VMEM((LANES, L), F32),
            pltpu.VMEM((L, LANES), F32),
            pltpu.VMEM((L + SUBLANES, M_DH), F32),
            pltpu.VMEM((L, M_DH), F32),
            pltpu.VMEM((L, M_DH), F32),
            pltpu.VMEM((L, M_DH), F32),
        ],
        input_output_aliases={11: 1} if has_prev else {},
        compiler_params=_params(("arbitrary", "arbitrary")),
        name="mlstm_sample",
    )(*args)


def _ssd_prompt_kernel(xs_ref, b_ref, c_ref, zs_ref, h_ref, wdt_ref,
                       cwx_ref, cwb_ref, cwc_ref, cbx_ref, cbb_ref, cbc_ref,
                       dtb_ref, alog_ref, dskip_ref, ng_ref, expand_ref,
                       out_ref, h1_ref, xex_sc, xeb_sc, xec_sc, ht_sc):
    c = pl.program_id(1)
    nc = pl.num_programs(1)
    L = xs_ref.shape[0]
    first = c == 0

    @pl.when(first)
    def _():
        ht_sc[...] = jnp.zeros(ht_sc.shape, F32)

    xs = _conv_rows(xs_ref[...], first, xex_sc, cwx_ref, cbx_ref)
    bm = _conv_rows(b_ref[...], first, xeb_sc, cwb_ref, cbb_ref)
    cm = _conv_rows(c_ref[...], first, xec_sc, cwc_ref, cbc_ref).astype(BF16)

    r2 = lax.broadcasted_iota(jnp.int32, (L, L), 0)
    c2 = lax.broadcasted_iota(jnp.int32, (L, L), 1)
    causal = c2 <= r2
    neg_inf = -jnp.inf
    dt = _softplus(_dot(h_ref[...], wdt_ref[...]) + dtb_ref[...])
    da = dt * (-jnp.exp(alog_ref[...]))
    cs = _dot_mask(causal, da)
    cs_t = cs.T
    ecs = jnp.exp(cs)
    wend = jnp.exp(cs[L - 1:L, :] - cs) * dt

    expand = expand_ref[...]

    def widen(a):
        hi = a.astype(BF16)
        lo = (a - hi.astype(F32)).astype(BF16)
        return _dot(hi, expand) + _dot(lo, expand)

    dte = widen(dt)
    wende = widen(wend)
    ecse = widen(ecs)
    xdt = (xs * dte).astype(BF16)
    wx = (xs * wende).astype(BF16)
    lane = lax.broadcasted_iota(jnp.int32, (1, LANES), 1)
    low_half = lane < S_DH
    zero_slab = jnp.zeros((L, LANES), BF16)
    for g in range(S_GROUPS):
        gcols = slice(g * S_GW, (g + 1) * S_GW)
        scols = slice(g * S_STATE, (g + 1) * S_STATE)
        bg = bm[:, scols]
        cg = cm[:, scols]
        cb = _dot_nt(cg, bg.astype(BF16))
        pairs = []
        for pr in range(S_HPG // 2):
            h0 = g * S_HPG + 2 * pr
            mixes = []
            for hh in (h0, h0 + 1):
                dec = jnp.exp(jnp.where(causal, cs[:, hh:hh + 1] - cs_t[hh:hh + 1, :], neg_inf))
                mixes.append((cb * dec).astype(BF16))
            slab = xdt[:, h0 * S_DH:(h0 + 2) * S_DH]
            rhs = jnp.concatenate([jnp.where(low_half, slab, zero_slab),
                                   jnp.where(low_half, zero_slab, slab)], axis=0)
            pairs.append(_dot(jnp.concatenate(mixes, axis=1), rhs))
        y_intra = jnp.concatenate(pairs, axis=1)
        ht = ht_sc[:, gcols]
        y = y_intra + ecse[:, gcols] * _dot(cg, ht.astype(BF16)) + dskip_ref[:, gcols] * xs[:, gcols]
        y = y * _silu(zs_ref[:, gcols])
        out_ref[:, gcols] = _rms(y, ng_ref[:, gcols])
        ht_sc[:, gcols] = ecse[L - 1:L, gcols] * ht + _dot(bg.T.astype(BF16), wx[:, gcols])

    @pl.when(c == nc - 1)
    def _():
        for pr in range(S_HEADS // 2):
            blk = ht_sc[:, pr * LANES:(pr + 1) * LANES].T
            h1_ref[0, 2 * pr] = blk[:S_DH]
            h1_ref[0, 2 * pr + 1] = blk[S_DH:]


def _ssd_prompt(l, proj, h, W, nb, nc):
    L = CHUNK
    rows = nb * nc * L
    row = lambda b, c: b * nc + c
    conv_w = lambda width, blk: pl.BlockSpec((None, CONV_K, width), lambda b, c: (l, 0, blk))
    return pl.pallas_call(
        _ssd_prompt_kernel,
        grid=(nb, nc),
        in_specs=[
            pl.BlockSpec((L, S_WIDTH), lambda b, c: (row(b, c), COL_XBC // S_WIDTH)),
            pl.BlockSpec((L, S_BC), lambda b, c: (row(b, c), COL_B // S_BC)),
            pl.BlockSpec((L, S_BC), lambda b, c: (row(b, c), COL_C // S_BC)),
            pl.BlockSpec((L, S_WIDTH), lambda b, c: (row(b, c), COL_ZS // S_WIDTH)),
            pl.BlockSpec((L, D_MODEL), lambda b, c: (row(b, c), 0)),
            pl.BlockSpec((None, D_MODEL, LANES), lambda b, c: (l, 0, 0)),
            conv_w(S_WIDTH, 0), conv_w(S_BC, S_WIDTH // S_BC), conv_w(S_BC, S_WIDTH // S_BC + 1),
            _vec_spec(l, S_WIDTH, lambda b, c: 0),
            _vec_spec(l, S_BC, lambda b, c: S_WIDTH // S_BC),
            _vec_spec(l, S_BC, lambda b, c: S_WIDTH // S_BC + 1),
            _vec_spec(l, LANES),
            _vec_spec(l, LANES),
            _vec_spec(l, S_WIDTH),
            _vec_spec(l, S_WIDTH),
            pl.BlockSpec((LANES, S_WIDTH), lambda b, c: (0, 0)),
        ],
        out_specs=[
            pl.BlockSpec((L, S_WIDTH), lambda b, c: (row(b, c), 0)),
            pl.BlockSpec((1, S_HEADS, S_DH, S_STATE), lambda b, c: (b, 0, 0, 0)),
        ],
        out_shape=[
            jax.ShapeDtypeStruct((rows, S_WIDTH), F32),
            jax.ShapeDtypeStruct((nb, S_HEADS, S_DH, S_STATE), F32),
        ],
        scratch_shapes=[
            pltpu.VMEM((SUBLANES, S_WIDTH), F32),
            pltpu.VMEM((SUBLANES, S_BC), F32),
            pltpu.VMEM((SUBLANES, S_BC), F32),
            pltpu.VMEM((S_STATE, S_WIDTH), F32),
        ],
        compiler_params=_params(("arbitrary", "arbitrary")),
        name="ssd_prompt",
    )(proj, proj, proj, proj, h, W["w_dt_c"], W["s_cw"], W["s_cw"], W["s_cw"], W["s_cb"], W["s_cb"], W["s_cb"],
      W["dtb_c"], W["alog_c"], W["dskip_wide"], W["s_ng"], W["expand"])


def _ssd_sample_kernel(xs_ref, b_ref, c_ref, zs_ref, h_ref, wdt_ref, hx_ref, hb_ref, hc_ref,
                       cwx_ref, cwb_ref, cwc_ref, cbx_ref, cbb_ref, cbc_ref,
                       dtb_ref, alog_ref, dskip_ref, ng_ref, expand_ref, h0_ref, alias_ref,
                       out_ref, h1_ref, xex_sc, xeb_sc, xec_sc, tot_sc, cm_sc, yi_sc, *, nseq, t_lo):
    del alias_ref
    L = xs_ref.shape[0]
    rs = L // nseq
    first = True
    xs = _conv_rows(xs_ref[...] + hx_ref[...], first, xex_sc, cwx_ref, cbx_ref)
    bm = _conv_rows(b_ref[...] + hb_ref[...], first, xeb_sc, cwb_ref, cbb_ref)
    cm = _conv_rows(c_ref[...] + hc_ref[...], first, xec_sc, cwc_ref, cbc_ref)
    bmb = bm.astype(BF16)
    cmb = cm.astype(BF16)
    cm_sc[0:L, :] = cm
    cm_sc[L:L + SUBLANES, :] = jnp.zeros((SUBLANES, S_STATE), F32)

    r2, c2, same, valid_c, ridx, valid_r = _block_masks(L, rs, t_lo)
    mask = same & (c2 <= r2) & valid_c
    neg_inf = -jnp.inf
    dt = jnp.where(valid_r, _softplus(_dot(h_ref[...], wdt_ref[...]) + dtb_ref[...]), 0.0)
    da = dt * (-jnp.exp(alog_ref[...]))
    cs = _dot_mask(mask, da)
    sfx = _dot_mask(same & (c2 > r2), da)
    tot_sc[...] = cs + sfx
    cs_t = cs.T
    expand = expand_ref[...]

    def widen(a):
        hi = a.astype(BF16)
        lo = (a - hi.astype(F32)).astype(BF16)
        return _dot(hi, expand) + _dot(lo, expand)

    dte = widen(dt)
    wende = widen(jnp.exp(sfx) * dt)
    ecse = widen(jnp.exp(cs))
    xdt = (xs * dte).astype(BF16)
    wxt = (xs * wende).T.astype(BF16)
    cb = _dot_nt(cmb, bmb)
    lane = lax.broadcasted_iota(jnp.int32, (1, LANES), 1)
    low_half = lane < S_DH
    zero_slab = jnp.zeros((L, LANES), BF16)
    pairs = []
    for pr in range(S_HPG // 2):
        mixes = []
        for hh in (2 * pr, 2 * pr + 1):
            dec = jnp.exp(jnp.where(mask, cs[:, hh:hh + 1] - cs_t[hh:hh + 1, :], neg_inf))
            mixes.append((cb * dec).astype(BF16))
        slab = xdt[:, pr * LANES:(pr + 1) * LANES]
        rhs = jnp.concatenate([jnp.where(low_half, slab, zero_slab),
                               jnp.where(low_half, zero_slab, slab)], axis=0)
        pairs.append(_dot(jnp.concatenate(mixes, axis=1), rhs))
    y_intra = jnp.concatenate(pairs, axis=1)

    def seq_step(j, carry):
        rows = pl.ds(pl.multiple_of(j * rs, rs), rs)
        in_seq_r = (ridx >= j * rs) & (ridx < (j + 1) * rs)
        hj = h0_ref[j]
        c2rows = cm_sc[pl.ds(pl.multiple_of(j * rs, rs), 2 * rs), :].astype(BF16)
        yi_sc[rows, :] = _dot_nt(c2rows, hj.reshape(S_GW, S_STATE).astype(BF16))[:rs]
        upd = _dot(wxt, jnp.where(in_seq_r, bmb, jnp.zeros_like(bmb)))
        dec_j = jnp.exp(tot_sc[pl.ds(j * rs + rs - 1, 1), :])
        for hh in range(S_HPG):
            h1_ref[j, hh] = dec_j[:, hh:hh + 1] * hj[hh] + upd[hh * S_DH:(hh + 1) * S_DH, :]
        return carry

    lax.fori_loop(0, nseq, seq_step, 0, unroll=SEQ_UNROLL)
    y = y_intra + ecse * yi_sc[...] + dskip_ref[...] * xs
    out_ref[...] = _rms(y * _silu(zs_ref[...]), ng_ref[...])


def _ssd_sample(l, depth, proj, h, hist, W, h0, h_prev, nseq, t_lo):
    L = CHUNK
    rows = proj.shape[0]
    nb = rows // L
    xblk = COL_XBC // S_GW
    bblk = COL_B // S_STATE
    cblk = COL_C // S_STATE
    hb_blk = S_WIDTH // S_STATE
    hc_blk = hb_blk + S_GROUPS
    has_prev = h_prev is not None
    conv_w = lambda width, blk: pl.BlockSpec((None, CONV_K, width), lambda b, g: (l, 0, blk(g)))
    h_spec = pl.BlockSpec((None, nseq, S_HPG, S_DH, S_STATE), lambda b, g: (l, b, g, 0, 0))
    in_specs = [
        pl.BlockSpec((L, S_GW), lambda b, g: (b, xblk + g)),
        pl.BlockSpec((L, S_STATE), lambda b, g: (b, bblk + g)),
        pl.BlockSpec((L, S_STATE), lambda b, g: (b, cblk + g)),
        pl.BlockSpec((L, S_GW), lambda b, g: (b, COL_ZS // S_GW + g)),
        pl.BlockSpec((L, D_MODEL), lambda b, g: (b, 0)),
        pl.BlockSpec((None, D_MODEL, LANES), lambda b, g: (l, 0, g)),
        pl.BlockSpec((None, L, S_GW), lambda b, g: (l, b, g)),
        pl.BlockSpec((None, L, S_STATE), lambda b, g: (l, b, hb_blk + g)),
        pl.BlockSpec((None, L, S_STATE), lambda b, g: (l, b, hc_blk + g)),
        conv_w(S_GW, lambda g: g), conv_w(S_STATE, lambda g: hb_blk + g), conv_w(S_STATE, lambda g: hc_blk + g),
        _vec_spec(l, S_GW, lambda b, g: g),
        _vec_spec(l, S_STATE, lambda b, g: hb_blk + g),
        _vec_spec(l, S_STATE, lambda b, g: hc_blk + g),
        _vec_spec(l, LANES, lambda b, g: g),
        _vec_spec(l, LANES, lambda b, g: g),
        _vec_spec(l, S_GW, lambda b, g: g),
        _vec_spec(l, S_GW, lambda b, g: g),
        pl.BlockSpec((LANES, S_GW), lambda b, g: (0, 0)),
        h_spec,
        pl.BlockSpec(memory_space=pl.ANY),
    ]
    args = [proj, proj, proj, proj, h, W["w_dt_g"], hist, hist, hist,
            W["s_cw"], W["s_cw"], W["s_cw"], W["s_cb"], W["s_cb"], W["s_cb"],
            W["dtb_g"], W["alog_g"], W["dskip_wide"], W["s_ng"], W["expand"], h0,
            h_prev if has_prev else jnp.zeros((SUBLANES, LANES), F32)]
    return pl.pallas_call(
        functools.partial(_ssd_sample_kernel, nseq=nseq, t_lo=t_lo),
        grid=(nb, S_GROUPS),
        in_specs=in_specs,
        out_specs=[pl.BlockSpec((L, S_GW), lambda b, g: (b, g)), h_spec],
        out_shape=[
            jax.ShapeDtypeStruct((rows, S_WIDTH), F32),
            jax.ShapeDtypeStruct((depth, nb * nseq, S_HEADS, S_DH, S_STATE), F32),
        ],
        scratch_shapes=[
            pltpu.VMEM((SUBLANES, S_GW), F32),
            pltpu.VMEM((SUBLANES, S_STATE), F32),
            pltpu.VMEM((SUBLANES, S_STATE), F32),
            pltpu.VMEM((L, LANES), F32),
            pltpu.VMEM((L + SUBLANES, S_STATE), F32),
            pltpu.VMEM((L, S_GW), F32),
        ],
        input_output_aliases={21: 1} if has_prev else {},
        compiler_params=_params(("arbitrary", "arbitrary")),
        name="ssd_sample",
    )(*args)


def _outproj_body(x_ref, om, os_ref, u_ref, v_ref, zc_ref, wmix_ref, bias_ref, gv_ref,
                  w0_ref, w1_ref, w2_ref, w3_ref, g_ref, y_ref, h_ref, vn_ref, oc_sc, *, final, rs, t_lo,
                  between=None):
    emit_vn = vn_ref is not None
    if between is None:
        between = lambda k: None
    L = C_CHUNK
    r2, c2, same, valid_c, _, _ = _block_masks(L, min(rs, L), t_lo)
    mask = same & (c2 <= r2) & valid_c
    for g in range(C_GROUPS):
        cols = slice(g * C_DG, (g + 1) * C_DG)
        w = jnp.where(mask, wmix_ref[g], 0.0).astype(BF16)
        for ch in range(x_ref.shape[0] // L):
            rows = slice(ch * L, (ch + 1) * L)
            vn = _rms(v_ref[rows, cols], gv_ref[:, cols])
            if emit_vn:
                vn_ref[rows, cols] = vn
            mixed = _dot(w, vn.astype(BF16)) + bias_ref[g]
            oc_sc[rows, cols] = (u_ref[rows, cols] * mixed * _silu(zc_ref[rows, cols])).astype(BF16)
    half = S_WIDTH // 2
    acc = _dot(om, w0_ref[...])
    between(0)
    acc = acc + _dot(os_ref[:, :half].astype(BF16), w1_ref[...])
    between(1)
    acc = acc + _dot(os_ref[:, half:].astype(BF16), w2_ref[...])
    between(2)
    acc = acc + _dot(oc_sc[...], w3_ref[...])
    between(3)
    y = x_ref[...] + acc
    if t_lo:
        ridx = lax.broadcasted_iota(jnp.int32, (y.shape[0], 1), 0)
        y = jnp.where((ridx & (rs - 1)) >= t_lo, y, 0.0)
    if final:
        y_ref[...] = _rms(y, g_ref[...])
    else:
        y_ref[...] = y
        h_ref[...] = _rms(y, g_ref[...]).astype(BF16)


def _outproj_kernel(*refs, final, emit_vn, rs, t_lo):
    (x_ref, om_ref, os_ref, u_ref, v_ref, zc_ref, wmix_ref, bias_ref, gv_ref,
     w0_ref, w1_ref, w2_ref, w3_ref, g_ref, y_ref) = refs[:15]
    rest = list(refs[15:])
    h_ref = None if final else rest.pop(0)
    vn_ref = rest.pop(0) if emit_vn else None
    (oc_sc,) = rest
    _outproj_body(x_ref, om_ref[...].astype(BF16), os_ref, u_ref, v_ref, zc_ref, wmix_ref, bias_ref, gv_ref,
                  w0_ref, w1_ref, w2_ref, w3_ref, g_ref, y_ref, h_ref, vn_ref, oc_sc,
                  final=final, rs=rs, t_lo=t_lo)


def _mlstm_out_kernel(*refs, final, nc):
    (q_ref, k_ref, v_ref, o_ref, zm_ref, gates_ref, bo_ref, ng_ref,
     x_ref, os_ref, u_ref, v2_ref, zc_ref, wmix_ref, bias_ref, gv_ref,
     w0_ref, w1_ref, w2_ref, w3_ref, g_ref, c1_ref, n1_ref, m1_ref, y_ref) = refs[:25]
    rest = list(refs[25:])
    h_ref = None if final else rest.pop(0)
    om_sc, oc_sc = rest
    i = pl.program_id(0)

    @pl.when(i == 0)
    def _():
        om_sc[...] = jnp.zeros(om_sc.shape, BF16)

    @pl.when(i % nc == 0)
    def _():
        c1_ref[...] = jnp.zeros(c1_ref.shape, F32)
        n1_ref[...] = jnp.zeros(n1_ref.shape, F32)
        m1_ref[...] = jnp.zeros(m1_ref.shape, F32)

    def emit(cols, val):
        om_sc[i % 2, :, cols] = val.astype(BF16)

    prep = _mlstm_prep(gates_ref)

    def head(h):
        _mlstm_head(h, prep, q_ref, k_ref, v_ref, o_ref, zm_ref, bo_ref, ng_ref, c1_ref, n1_ref, m1_ref, emit)

    _outproj_body(x_ref, om_sc[(i + 1) % 2], os_ref, u_ref, v2_ref, zc_ref, wmix_ref, bias_ref, gv_ref,
                  w0_ref, w1_ref, w2_ref, w3_ref, g_ref, y_ref, h_ref, None, oc_sc,
                  final=final, rs=CHUNK, t_lo=0, between=head)


def _mlstm_out(l, q, k, v, o, gates, proj, x, os_, wmix, bias, W, g, final, nb, nc):
    L = M_CHUNK
    nt = nb * nc
    rows = nt * L
    cur = lambda i: jnp.minimum(i, nt - 1)
    prev = lambda i: jnp.maximum(i - 1, 0)
    cur_spec = pl.BlockSpec((L, M_WIDTH), lambda i: (cur(i), 0))
    prev_spec = pl.BlockSpec((L, D_MODEL), lambda i: (prev(i), 0))
    wspec = lambda r: pl.BlockSpec((None, M_WIDTH, D_MODEL), lambda i: (l, r, 0), pipeline_mode=pl.Buffered(1))
    out_shape = [
        jax.ShapeDtypeStruct((nb + 1, M_HEADS, M_DH, M_DH), F32),
        jax.ShapeDtypeStruct((nb + 1, M_HEADS, 1, M_DH), F32),
        jax.ShapeDtypeStruct((nb + 1, SUBLANES, LANES), F32),
        jax.ShapeDtypeStruct((rows, D_MODEL), F32),
    ]
    out_specs = [
        pl.BlockSpec((1, M_HEADS, M_DH, M_DH), lambda i: (i // nc, 0, 0, 0)),
        pl.BlockSpec((1, M_HEADS, 1, M_DH), lambda i: (i // nc, 0, 0, 0)),
        pl.BlockSpec((1, SUBLANES, LANES), lambda i: (i // nc, 0, 0)),
        prev_spec,
    ]
    if not final:
        out_shape.append(jax.ShapeDtypeStruct((rows, D_MODEL), BF16))
        out_specs.append(prev_spec)
    return pl.pallas_call(
        functools.partial(_mlstm_out_kernel, final=final, nc=nc),
        grid=(nt + 1,),
        in_specs=[
            cur_spec, cur_spec, cur_spec, cur_spec,
            pl.BlockSpec((L, M_WIDTH), lambda i: (cur(i), COL_ZM // M_WIDTH)),
            pl.BlockSpec((L, LANES), lambda i: (cur(i), 0)),
            _vec_spec(l, M_WIDTH),
            _vec_spec(l, M_WIDTH),
            prev_spec,
            prev_spec,
            pl.BlockSpec((L, C_WIDTH), lambda i: (prev(i), COL_U // C_WIDTH)),
            pl.BlockSpec((L, C_WIDTH), lambda i: (prev(i), COL_V // C_WIDTH)),
            pl.BlockSpec((L, C_WIDTH), lambda i: (prev(i), COL_ZC // C_WIDTH)),
            pl.BlockSpec((None, C_GROUPS, C_CHUNK, C_CHUNK), lambda i: (l, 0, 0, 0)),
            pl.BlockSpec((None, C_GROUPS, C_CHUNK, 1), lambda i: (l, 0, 0, 0)),
            _vec_spec(l, C_WIDTH),
            wspec(0), wspec(1), wspec(2), wspec(3),
            pl.BlockSpec((1, D_MODEL), lambda i: (0, 0)) if final else _vec_spec(l + 1, D_MODEL),
        ],
        out_specs=out_specs,
        out_shape=out_shape,
        scratch_shapes=[pltpu.VMEM((2, L, M_WIDTH), BF16), pltpu.VMEM((L, C_WIDTH), BF16)],
        compiler_params=_params(("arbitrary",)),
        name="mlstm_out",
    )(q, k, v, o, proj, gates, W["bo"], W["m_ng"], x, os_, proj, proj, proj, wmix, bias, W["gv"],
      W["wo"], W["wo"], W["wo"], W["wo"], g)


def _outproj(l, x, om, os_, proj, wmix, bias, gv, wo, g, final, emit_vn, rs, t_lo):
    rows = x.shape[0]
    tm = min(256, rows)
    kb = M_WIDTH
    wspec = lambda r: pl.BlockSpec((None, kb, D_MODEL), lambda i: (l, r, 0), pipeline_mode=pl.Buffered(1))
    row_spec = pl.BlockSpec((tm, D_MODEL), lambda i: (i, 0))
    out_shape = [jax.ShapeDtypeStruct((rows, D_MODEL), F32)]
    out_specs = [row_spec]
    if not final:
        out_shape.append(jax.ShapeDtypeStruct((rows, D_MODEL), BF16))
        out_specs.append(row_spec)
    if emit_vn:
        out_shape.append(jax.ShapeDtypeStruct((rows, C_WIDTH), F32))
        out_specs.append(pl.BlockSpec((tm, C_WIDTH), lambda i: (i, 0)))
    return pl.pallas_call(
        functools.partial(_outproj_kernel, final=final, emit_vn=emit_vn, rs=rs, t_lo=t_lo),
        grid=(rows // tm,),
        in_specs=[
            row_spec,
            pl.BlockSpec((tm, M_WIDTH), lambda i: (i, 0)),
            pl.BlockSpec((tm, S_WIDTH), lambda i: (i, 0)),
            pl.BlockSpec((tm, C_WIDTH), lambda i: (i, COL_U // C_WIDTH)),
            pl.BlockSpec((tm, C_WIDTH), lambda i: (i, COL_V // C_WIDTH)),
            pl.BlockSpec((tm, C_WIDTH), lambda i: (i, COL_ZC // C_WIDTH)),
            pl.BlockSpec((None, C_GROUPS, C_CHUNK, C_CHUNK), lambda i: (l, 0, 0, 0)),
            pl.BlockSpec((None, C_GROUPS, C_CHUNK, 1), lambda i: (l, 0, 0, 0)),
            _vec_spec(l, C_WIDTH),
            wspec(0), wspec(1), wspec(2), wspec(3),
            pl.BlockSpec((1, D_MODEL), lambda i: (0, 0)) if final else _vec_spec(l + 1, D_MODEL),
        ],
        out_specs=out_specs,
        out_shape=out_shape,
        scratch_shapes=[pltpu.VMEM((tm, C_WIDTH), BF16)],
        compiler_params=_params(("arbitrary",)),
        name="outproj",
    )(x, om, os_, proj, proj, proj, wmix, bias, gv, wo, wo, wo, wo, g)


def _heads_compact(a):
    return jnp.pad(a, ((0, 0), (0, LANES - S_HEADS)))[:, None, :]


def _heads_grouped(a):
    a = a.reshape(a.shape[0], S_GROUPS, S_HPG)
    return jnp.pad(a, ((0, 0), (0, 0), (0, LANES - S_HPG))).reshape(a.shape[0], 1, S_GROUPS * LANES)


def _prepare_weights(norm_g, w_in, m_conv_w, m_conv_b, m_w_qk, m_w_vo, m_b_o, m_w_gate, m_b_gate,
                     m_norm_g, s_conv_w, s_conv_b, s_dt_bias, s_A_log, s_D, s_norm_g, c_v_norm_g,
                     c_w_s, c_b_s, w_out):
    depth = w_in.shape[0]
    vec = lambda a: a.reshape(depth, 1, -1)
    dt0 = PROJ_HEAD
    w_all = w_in.astype(BF16)
    w_dt = w_in[:, :, dt0:dt0 + S_HEADS]
    w_dt_g = jnp.pad(w_dt.reshape(depth, D_MODEL, S_GROUPS, S_HPG),
                     ((0, 0), (0, 0), (0, 0), (0, LANES - S_HPG))).reshape(depth, D_MODEL, -1).astype(BF16)
    w_dt_c = jnp.pad(w_dt, ((0, 0), (0, 0), (0, LANES - S_HEADS))).astype(BF16)
    wg = m_w_gate.reshape(depth, M_HEADS, 3, M_DH, 2 * M_HEADS)
    wg = jnp.pad(wg, ((0, 0),) * 4 + ((0, LANES - 2 * M_HEADS),)).astype(BF16)
    head_of_lane = jnp.arange(S_WIDTH) // S_DH
    expand = (jnp.arange(LANES)[:, None] == head_of_lane[None, :]).astype(BF16)
    return dict(
        norm_g=vec(norm_g), w_all=w_all, w_tail=w_all[:, :, dt0 + S_HEADS:], w_dt_c=w_dt_c, w_dt_g=w_dt_g,
        m_cw=m_conv_w, m_cb=vec(m_conv_b),
        wqk=m_w_qk.astype(BF16), wvo=m_w_vo.astype(BF16), wg=wg,
        bg=vec(jnp.pad(m_b_gate, ((0, 0), (0, LANES - 2 * M_HEADS)))),
        bo=vec(m_b_o), m_ng=vec(m_norm_g),
        s_cw=s_conv_w, s_cb=vec(s_conv_b),
        dtb_c=_heads_compact(s_dt_bias), alog_c=_heads_compact(s_A_log),
        dtb_g=_heads_grouped(s_dt_bias), alog_g=_heads_grouped(s_A_log),
        dskip_wide=vec(jnp.repeat(s_D, S_DH, axis=1)), s_ng=vec(s_norm_g), expand=expand,
        gv=vec(c_v_norm_g), wo=w_out.astype(BF16),
    )


def kernel(x_prompt, x_sample, state_mlstm_C, state_mlstm_n, state_mlstm_m, state_mlstm_conv, state_ssm, state_ssm_conv, norm_g, w_in, m_conv_w, m_conv_b, m_w_qk, m_w_vo, m_b_o, m_w_gate, m_b_gate, m_norm_g, s_conv_w, s_conv_b, s_dt_bias, s_A_log, s_D, s_norm_g, c_v_norm_g, c_w_s, c_b_s, w_out, final_norm_g):
    bp, seq, _ = x_prompt.shape
    bs, dec_seq, _ = x_sample.shape
    depth = w_in.shape[0]
    t_lo = SAMPLE_ROWS - dec_seq
    hist_lo = t_lo - (CONV_K - 1)
    nseq = CHUNK // SAMPLE_ROWS
    W = _prepare_weights(norm_g, w_in, m_conv_w, m_conv_b, m_w_qk, m_w_vo, m_b_o, m_w_gate, m_b_gate,
                         m_norm_g, s_conv_w, s_conv_b, s_dt_bias, s_A_log, s_D, s_norm_g, c_v_norm_g,
                         c_w_s, c_b_s, w_out)
    fg = final_norm_g[None, :]
    yp = x_prompt.reshape(bp * seq, D_MODEL)
    ys = jnp.pad(x_sample, ((0, 0), (t_lo, 0), (0, 0))).reshape(bs * SAMPLE_ROWS, D_MODEL)

    pad_hist = ((0, 0), (0, 0), (hist_lo, dec_seq), (0, 0))
    hist_m = jnp.pad(state_mlstm_conv, pad_hist).reshape(depth, bs * SAMPLE_ROWS, M_WIDTH)
    hist_s = jnp.pad(state_ssm_conv, pad_hist).reshape(depth, bs * SAMPLE_ROWS, -1)
    m0rows = jnp.repeat(jnp.swapaxes(state_mlstm_m, 1, 2)[..., None], SAMPLE_ROWS, axis=2)
    n0 = state_mlstm_n[:, :, :, None, :]
    reps = C_CHUNK // SAMPLE_ROWS
    w4 = jnp.pad(c_w_s[:, :, :dec_seq, :dec_seq], ((0, 0), (0, 0), (t_lo, 0), (t_lo, 0)))
    b4 = jnp.pad(c_b_s[:, :, :dec_seq], ((0, 0), (0, 0), (t_lo, 0)))
    wmix_s = jnp.tile(w4, (1, 1, reps, reps))
    bias_s = jnp.tile(b4, (1, 1, reps))[..., None]
    bias_p = c_b_s[..., None]

    hp = _norm(0, yp, W["norm_g"])
    hs = _norm(0, ys, W["norm_g"])
    outs_p, outs_s = [], []
    c_all = h_all = None
    mc = min(M_CHUNK, seq)
    sc = min(CHUNK, seq)
    tbp = min(512, bp * seq)
    tbs = min(512, bs * SAMPLE_ROWS)
    for l in range(depth):
        final = l == depth - 1
        proj = _inproj(l, hp, W["w_all"], W["w_tail"])
        q, k, v, o, gates = _mlstm_proj(l, proj, None, W, max(1, seq // tbp), tbp)
        out_s, h1 = _ssd_prompt(l, proj, hp, W, bp, seq // sc)
        res = _mlstm_out(l, q, k, v, o, gates, proj, yp, out_s, c_w_s, bias_p, W,
                         fg if final else W["norm_g"], final, bp, seq // mc)
        c1, n1, m1 = res[0][:bp], res[1][:bp], res[2][:bp]
        yp, hp = res[3], (None if final else res[4])
        pj = proj.reshape(bp, seq, PROJ_MAIN)
        outs_p.append((
            c1, n1.reshape(bp, M_HEADS, M_DH), m1[:, :M_HEADS, 0],
            pj[:, seq - (CONV_K - 1):, COL_XM:COL_XM + M_WIDTH], h1,
            pj[:, seq - (CONV_K - 1):, COL_XBC:COL_XBC + S_WIDTH + 2 * S_BC]))
        proj = _inproj(l, hs, W["w_all"], W["w_tail"])
        q, k, v, o, gates = _mlstm_proj(l, proj, hist_m, W, 1, tbs)
        out_m, c_all, n1, mrow = _mlstm_sample(l, depth, q, k, v, o, gates, proj, W, state_mlstm_C, n0,
                                               m0rows, c_all, nseq, t_lo)
        out_s, h_all = _ssd_sample(l, depth, proj, hs, hist_s, W, state_ssm, h_all, nseq, t_lo)
        res = _outproj(l, ys, out_m, out_s, proj, wmix_s, bias_s, W["gv"], W["wo"],
                       fg if final else W["norm_g"], final, True, SAMPLE_ROWS, t_lo)
        ys, hs, vn = res[0], (None if final else res[1]), res[-1]
        pj = proj.reshape(bs, SAMPLE_ROWS, PROJ_MAIN)
        outs_s.append((
            n1.reshape(bs, M_HEADS, M_DH), mrow[:, SAMPLE_ROWS - 1::SAMPLE_ROWS, 0].T,
            pj[:, SAMPLE_ROWS - (CONV_K - 1):, COL_XM:COL_XM + M_WIDTH],
            pj[:, SAMPLE_ROWS - (CONV_K - 1):, COL_XBC:COL_XBC + S_WIDTH + 2 * S_BC],
            vn.reshape(bs, SAMPLE_ROWS, C_WIDTH)[:, t_lo:]))
    p_out = [jnp.stack([s[i] for s in outs_p]) for i in range(6)]
    s_n, s_m, s_mconv, s_sconv, s_cv = [jnp.stack([s[i] for s in outs_s]) for i in range(5)]
    y_prompt = yp.reshape(bp, seq, D_MODEL)
    y_sample = ys.reshape(bs, SAMPLE_ROWS, D_MODEL)[:, t_lo:]
    return (y_prompt, y_sample, *p_out, c_all, s_n, s_m, s_mconv, h_all, s_sconv, s_cv)
```

```python
import functools

import jax
import jax.numpy as jnp
from jax import lax
from jax.experimental import pallas as pl
from jax.experimental.pallas import tpu as pltpu

F32 = jnp.float32
BF16 = jnp.bfloat16

D_MODEL = 2048
MIX_WIDTH = 2 * D_MODEL
M_WIDTH = MIX_WIDTH // 4
M_HEADS = 4
M_DH = M_WIDTH // M_HEADS
S_WIDTH = MIX_WIDTH // 2
S_DH = 64
S_HEADS = S_WIDTH // S_DH
S_GROUPS = 4
S_HPG = S_HEADS // S_GROUPS
S_STATE = 128
S_GW = S_HPG * S_DH
S_BC = S_GROUPS * S_STATE
C_WIDTH = MIX_WIDTH // 4
C_GROUPS = 4
C_DG = C_WIDTH // C_GROUPS
C_CHUNK = 128
CONV_K = 4
EPS = 1e-6

LANES = 128
SUBLANES = 8
SAMPLE_ROWS = 8
CHUNK = 128
M_CHUNK = 256
SEQ_UNROLL = 8
VMEM_LIMIT = 56 * 1024 * 1024

COL_XM, COL_ZM, COL_ZS, COL_XBC, COL_U, COL_V, COL_ZC = 0, 1024, 2048, 4096, 7168, 8192, 9216
PROJ_MAIN = 10240
PROJ_HEAD = COL_U
COL_B = COL_XBC + S_WIDTH
COL_C = COL_B + S_BC


def _dot(a, b):
    return jnp.dot(a, b, preferred_element_type=F32)


def _dot_nt(a, b):
    return lax.dot_general(a, b, (((1,), (1,)), ((), ())), preferred_element_type=F32)


def _dot_mask(mask, x):
    m = jnp.where(mask, 1.0, 0.0).astype(BF16)
    hi = x.astype(BF16)
    r1 = x - hi.astype(F32)
    mid = r1.astype(BF16)
    lo = (r1 - mid.astype(F32)).astype(BF16)
    return _dot(m, hi) + _dot(m, mid) + _dot(m, lo)


def _widen(a, expand2):
    hi = a.astype(BF16)
    lo = (a - hi.astype(F32)).astype(BF16)
    return _dot(jnp.concatenate([hi, lo], axis=1), expand2)


def _sigmoid(x):
    return 0.5 * jnp.tanh(0.5 * x) + 0.5


def _silu(x):
    h = 0.5 * x
    return h * jnp.tanh(h) + h


def _softplus(x):
    return jnp.maximum(x, 0.0) + jnp.log1p(jnp.exp(-jnp.abs(x)))


def _log_sigmoid(x):
    return jnp.minimum(x, 0.0) - jnp.log1p(jnp.exp(-jnp.abs(x)))


def _rms(x, g):
    return x * lax.rsqrt(jnp.mean(x * x, axis=-1, keepdims=True) + EPS) * g


def _params(sem):
    return pltpu.CompilerParams(dimension_semantics=sem, vmem_limit_bytes=VMEM_LIMIT)


def _vec_spec(l, n, col=None):
    if col is None:
        return pl.BlockSpec((None, 1, n), lambda *ids: (l, 0, 0))
    return pl.BlockSpec((None, 1, n), lambda *ids: (l, 0, col(*ids)))


def _norm_kernel(x_ref, g_ref, h_ref):
    h_ref[...] = _rms(x_ref[...], g_ref[...]).astype(BF16)


def _norm(l, x, g):
    rows = x.shape[0]
    tb = min(1024, rows)
    return pl.pallas_call(
        _norm_kernel,
        grid=(rows // tb,),
        in_specs=[pl.BlockSpec((tb, D_MODEL), lambda i: (i, 0)), _vec_spec(l, D_MODEL)],
        out_specs=pl.BlockSpec((tb, D_MODEL), lambda i: (i, 0)),
        out_shape=jax.ShapeDtypeStruct((rows, D_MODEL), BF16),
        compiler_params=_params(("arbitrary",)),
        name="norm",
    )(x, g)


def _inproj_kernel(h_ref, wa_ref, wb_ref, proj_ref, *, n_head):
    j = pl.program_id(0)

    @pl.when(j < n_head)
    def _():
        proj_ref[...] = _dot(h_ref[...], wa_ref[...])

    @pl.when(j >= n_head)
    def _():
        proj_ref[...] = _dot(h_ref[...], wb_ref[...])


def _inproj(l, h, w_all, w_tail):
    rows = h.shape[0]
    tm = min(1024, rows)
    tn = 1024
    n_head = PROJ_HEAD // tn
    return pl.pallas_call(
        functools.partial(_inproj_kernel, n_head=n_head),
        grid=(PROJ_MAIN // tn, rows // tm),
        in_specs=[
            pl.BlockSpec((tm, D_MODEL), lambda j, i: (i, 0)),
            pl.BlockSpec((None, D_MODEL, tn), lambda j, i: (l, 0, jnp.minimum(j, n_head - 1))),
            pl.BlockSpec((None, D_MODEL, tn), lambda j, i: (l, 0, jnp.maximum(j - n_head, 0))),
        ],
        out_specs=pl.BlockSpec((tm, tn), lambda j, i: (i, j)),
        out_shape=jax.ShapeDtypeStruct((rows, PROJ_MAIN), F32),
        compiler_params=_params(("arbitrary", "arbitrary")),
        name="inproj",
    )(h, w_all, w_tail)


def _conv_rows(x, first, prev_sc, cw_ref, cb_ref):
    assert CONV_K == 4
    tb = x.shape[0]

    @pl.when(first)
    def _():
        prev_sc[...] = jnp.zeros(prev_sc.shape, F32)

    xe = jnp.concatenate([prev_sc[...], x], axis=0)
    x1 = pltpu.roll(xe, 1, 0)
    pair = cw_ref[1:2, :] * xe + cw_ref[0:1, :] * x1
    acc = (cb_ref[...] + cw_ref[3:4, :] * x + cw_ref[2:3, :] * x1[SUBLANES:, :]
           + pltpu.roll(pair, 2, 0)[SUBLANES:, :])
    prev_sc[...] = x[tb - SUBLANES:, :]
    return _silu(acc)


def _mlstm_proj_kernel(*refs, has_hist, blocks_per_seq):
    if has_hist:
        (xm_ref, he_ref, cw_ref, cb_ref, wqk_ref, wvo_ref, wg_ref, bg_ref,
         q_ref, k_ref, v_ref, o_ref, gates_ref, xe_sc) = refs
    else:
        (xm_ref, cw_ref, cb_ref, wqk_ref, wvo_ref, wg_ref, bg_ref,
         q_ref, k_ref, v_ref, o_ref, gates_ref, xe_sc) = refs
    i = pl.program_id(0)
    x = xm_ref[...]
    xin = x + he_ref[...] if has_hist else x
    xmc = _conv_rows(xin, i % blocks_per_seq == 0, xe_sc, cw_ref, cb_ref)
    tb = x.shape[0]
    gates = jnp.broadcast_to(bg_ref[...], (tb, LANES))
    for h in range(M_HEADS):
        cols = slice(h * M_DH, (h + 1) * M_DH)
        qk = _dot(xmc[:, cols].astype(BF16), wqk_ref[h])
        vo = _dot(x[:, cols].astype(BF16), wvo_ref[h])
        qb = qk[:, :M_DH].astype(BF16)
        kb = qk[:, M_DH:].astype(BF16)
        vb = vo[:, :M_DH].astype(BF16)
        gates = gates + _dot(qb, wg_ref[h, 0]) + _dot(kb, wg_ref[h, 1]) + _dot(vb, wg_ref[h, 2])
        q_ref[:, cols] = qb
        k_ref[:, cols] = (qk[:, M_DH:] * (M_DH ** -0.5)).astype(BF16)
        v_ref[:, cols] = vb
        o_ref[:, cols] = vo[:, M_DH:]
    lane = lax.broadcasted_iota(jnp.int32, (tb, LANES), 1)
    gates_ref[...] = jnp.where(lane < M_HEADS, gates, _log_sigmoid(gates))


def _mlstm_proj(l, proj, hist, W, blocks_per_seq, tb):
    rows = proj.shape[0]
    has_hist = hist is not None
    in_specs = [pl.BlockSpec((tb, M_WIDTH), lambda i: (i, COL_XM // M_WIDTH))]
    args = [proj]
    if has_hist:
        in_specs.append(pl.BlockSpec((None, tb, M_WIDTH), lambda i: (l, i, 0)))
        args.append(hist)
    in_specs += [
        pl.BlockSpec((None, CONV_K, M_WIDTH), lambda i: (l, 0, 0)),
        _vec_spec(l, M_WIDTH),
        pl.BlockSpec((None, M_HEADS, M_DH, 2 * M_DH), lambda i: (l, 0, 0, 0)),
        pl.BlockSpec((None, M_HEADS, M_DH, 2 * M_DH), lambda i: (l, 0, 0, 0)),
        pl.BlockSpec((None, M_HEADS, 3, M_DH, LANES), lambda i: (l, 0, 0, 0, 0)),
        _vec_spec(l, LANES),
    ]
    args += [W["m_cw"], W["m_cb"], W["wqk"], W["wvo"], W["wg"], W["bg"]]
    row_spec = pl.BlockSpec((tb, M_WIDTH), lambda i: (i, 0))
    return pl.pallas_call(
        functools.partial(_mlstm_proj_kernel, has_hist=has_hist, blocks_per_seq=blocks_per_seq),
        grid=(rows // tb,),
        in_specs=in_specs,
        out_specs=[row_spec, row_spec, row_spec, row_spec, pl.BlockSpec((tb, LANES), lambda i: (i, 0))],
        out_shape=[
            jax.ShapeDtypeStruct((rows, M_WIDTH), BF16),
            jax.ShapeDtypeStruct((rows, M_WIDTH), BF16),
            jax.ShapeDtypeStruct((rows, M_WIDTH), BF16),
            jax.ShapeDtypeStruct((rows, M_WIDTH), F32),
            jax.ShapeDtypeStruct((rows, LANES), F32),
        ],
        scratch_shapes=[pltpu.VMEM((SUBLANES, M_WIDTH), F32)],
        compiler_params=_params(("arbitrary",)),
        name="mlstm_proj",
    )(*args)


def _mlstm_finish(hh, o, bo, ng, zm):
    hm = _sigmoid(o + bo) * hh
    return _rms(hm, ng) * _silu(zm)


def _mlstm_prep(gates_ref):
    L = gates_ref.shape[0]
    r2 = lax.broadcasted_iota(jnp.int32, (L, L), 0)
    c2 = lax.broadcasted_iota(jnp.int32, (L, L), 1)
    causal = c2 <= r2
    g = gates_ref[...]
    b_all = _dot_mask(causal, g)
    return causal, g, b_all, g.T, b_all.T


def _mlstm_head(h, prep, q_ref, k_ref, v_ref, o_ref, zm_ref, bo_ref, ng_ref, c1_ref, n1_ref, m1_ref, emit):
    causal, g, b_all, g_t, b_t = prep
    L = q_ref.shape[0]
    neg_inf = -jnp.inf
    cols = slice(h * M_DH, (h + 1) * M_DH)
    f = M_HEADS + h
    g_row = g_t[h:h + 1, :] - b_t[f:f + 1, :]
    b_col = b_all[:, f:f + 1]
    g_col = g[:, h:h + 1] - b_col
    mprev = m1_ref[0, h:h + 1, 0:1]
    gm = jnp.where(causal, g_row, neg_inf)
    m_col = jnp.maximum(mprev, jnp.max(gm, axis=1, keepdims=True))
    w_intra = jnp.exp(gm - m_col)
    w_inter = jnp.exp(mprev - m_col)
    m_new = b_col + m_col
    m_last = m_col[L - 1:L, :]
    wl_col = jnp.exp(g_col - m_last)
    wli = jnp.exp(mprev - m_last)
    q = q_ref[:, cols]
    k = k_ref[:, cols]
    v = v_ref[:, cols]
    cst = c1_ref[0, h]
    nst = n1_ref[0, h]
    s = _dot_nt(q, k) * w_intra
    num = _dot(s.astype(BF16), v) + w_inter * _dot_nt(q, cst.astype(BF16))
    den = (jnp.sum(s, axis=1, keepdims=True)
           + w_inter * jnp.sum(q.astype(F32) * nst, axis=1, keepdims=True))
    hh = num / jnp.maximum(jnp.abs(den), jnp.exp(-m_new))
    emit(cols, _mlstm_finish(hh, o_ref[:, cols], bo_ref[:, cols], ng_ref[:, cols], zm_ref[:, cols]))

    def update_state():
        c1_ref[0, h] = wli * cst + _dot((v.astype(F32) * wl_col).T.astype(BF16), k)
        n1_ref[0, h] = wli * nst + jnp.sum(k.astype(F32) * wl_col, axis=0, keepdims=True)
        m1_ref[0, h:h + 1, :] = jnp.broadcast_to(m_new[L - 1:L, :], (1, LANES))

    return update_state


def _block_masks(L, rs, t_lo):
    shift = rs.bit_length() - 1
    r2 = lax.broadcasted_iota(jnp.int32, (L, L), 0)
    c2 = lax.broadcasted_iota(jnp.int32, (L, L), 1)
    same = (r2 >> shift) == (c2 >> shift)
    valid_c = (c2 & (rs - 1)) >= t_lo
    ridx = lax.broadcasted_iota(jnp.int32, (L, 1), 0)
    valid_r = (ridx & (rs - 1)) >= t_lo
    return r2, c2, same, valid_c, ridx, valid_r


def _mlstm_sample_kernel(q_ref, k_ref, v_ref, o_ref, zm_ref, gates_ref, bo_ref, ng_ref,
                         c0_ref, n0_ref, m0_ref, alias_ref,
                         out_ref, c1_ref, n1_ref, mrow_ref,
                         gt_sc, bt_sc, col_sc, q_sc, numi_sc, nrow_sc, wk_sc, *, nseq, t_lo):
    del alias_ref
    h = pl.program_id(1)
    L = q_ref.shape[0]
    rs = L // nseq
    r2, c2, same, valid_c, ridx, valid_r = _block_masks(L, rs, t_lo)
    eye = r2 == c2
    neg_inf = -jnp.inf
    g = gates_ref[...]
    b_all = _dot_mask(same & (c2 <= r2), jnp.where(valid_r, g, 0.0))
    gt_sc[...] = g.T
    bt_sc[...] = b_all.T
    lane_g = lax.broadcasted_iota(jnp.int32, (1, LANES), 1)
    i_col = jnp.sum(jnp.where(lane_g == h, g, 0.0), axis=1, keepdims=True)
    b_col = jnp.sum(jnp.where(lane_g == h + M_HEADS, b_all, 0.0), axis=1, keepdims=True)
    g_row = gt_sc[pl.ds(h, 1), :] - bt_sc[pl.ds(h + M_HEADS, 1), :]
    g_col = i_col - b_col
    mask = same & (c2 <= r2) & valid_c
    mprev = m0_ref[0]
    gm = jnp.where(mask, g_row, neg_inf)
    m_col = jnp.maximum(mprev, jnp.max(gm, axis=1, keepdims=True))
    m_row = jnp.sum(jnp.where(eye, m_col, 0.0), axis=0, keepdims=True)
    mlast_col = jnp.max(jnp.where(same, m_row, neg_inf), axis=1, keepdims=True)
    w_intra = jnp.exp(gm - m_col)
    w_inter = jnp.exp(mprev - m_col)
    m_new = b_col + m_col
    wl_col = jnp.where(valid_r, jnp.exp(g_col - mlast_col), 0.0)
    col_sc[:, 0:1] = jnp.exp(mprev - mlast_col)

    q = q_ref[...]
    k = k_ref[...]
    v = v_ref[...]
    s = _dot_nt(q, k) * w_intra
    num = _dot(s.astype(BF16), v)
    den = jnp.sum(s, axis=1, keepdims=True)
    wvt = (v.astype(F32) * wl_col).T.astype(BF16)
    wk_sc[...] = k.astype(F32) * wl_col
    qf = q.astype(F32)
    q_sc[0:L, :] = qf
    q_sc[L:L + SUBLANES, :] = jnp.zeros((SUBLANES, M_DH), F32)

    def seq_step(j, carry):
        rows = pl.ds(pl.multiple_of(j * rs, rs), rs)
        in_seq_r = (ridx >= j * rs) & (ridx < (j + 1) * rs)
        cj = c0_ref[j, 0]
        nj = n0_ref[j, 0]
        q2 = q_sc[pl.ds(pl.multiple_of(j * rs, rs), 2 * rs), :].astype(BF16)
        numi_sc[rows, :] = _dot_nt(q2, cj.astype(BF16))[:rs]
        nrow_sc[rows, :] = jnp.broadcast_to(nj, (rs, M_DH))
        wli = col_sc[pl.ds(j * rs + rs - 1, 1), 0:1]
        c1_ref[j, 0] = wli * cj + _dot(wvt, jnp.where(in_seq_r, k, jnp.zeros_like(k)))
        n1_ref[j, 0] = wli * nj + jnp.sum(wk_sc[rows, :], axis=0, keepdims=True)
        return carry

    lax.fori_loop(0, nseq, seq_step, 0, unroll=SEQ_UNROLL)
    num = num + w_inter * numi_sc[...]
    den = den + w_inter * jnp.sum(qf * nrow_sc[...], axis=1, keepdims=True)
    hh = num / jnp.maximum(jnp.abs(den), jnp.exp(-m_new))
    out_ref[...] = _mlstm_finish(hh, o_ref[...], bo_ref[...], ng_ref[...], zm_ref[...])
    mrow_ref[0] = m_new


def _mlstm_sample(l, depth, q, k, v, o, gates, proj, W, c0, n0, m0rows, c_prev, nseq, t_lo):
    L = CHUNK
    rows = q.shape[0]
    nb = rows // L
    nbatch = nb * nseq
    rowblk = lambda b, h: (b, h)
    c_spec = pl.BlockSpec((None, nseq, 1, M_DH, M_DH), lambda b, h: (l, b, h, 0, 0))
    n_spec_in = pl.BlockSpec((None, nseq, 1, 1, M_DH), lambda b, h: (l, b, h, 0, 0))
    n_spec_out = pl.BlockSpec((nseq, 1, 1, M_DH), lambda b, h: (b, h, 0, 0))
    has_prev = c_prev is not None
    in_specs = [
        pl.BlockSpec((L, M_DH), rowblk),
        pl.BlockSpec((L, M_DH), rowblk),
        pl.BlockSpec((L, M_DH), rowblk),
        pl.BlockSpec((L, M_DH), rowblk),
        pl.BlockSpec((L, M_DH), lambda b, h: (b, COL_ZM // M_DH + h)),
        pl.BlockSpec((L, LANES), lambda b, h: (b, 0)),
        _vec_spec(l, M_DH, lambda b, h: h),
        _vec_spec(l, M_DH, lambda b, h: h),
        c_spec,
        n_spec_in,
        pl.BlockSpec((None, 1, L, 1), lambda b, h: (l, h, b, 0)),
        pl.BlockSpec(memory_space=pl.ANY),
    ]
    args = [q, k, v, o, proj, gates, W["bo"], W["m_ng"], c0, n0, m0rows,
            c_prev if has_prev else jnp.zeros((SUBLANES, LANES), F32)]
    return pl.pallas_call(
        functools.partial(_mlstm_sample_kernel, nseq=nseq, t_lo=t_lo),
        grid=(nb, M_HEADS),
        in_specs=in_specs,
        out_specs=[pl.BlockSpec((L, M_DH), rowblk), c_spec, n_spec_out,
                   pl.BlockSpec((1, L, 1), lambda b, h: (h, b, 0))],
        out_shape=[
            jax.ShapeDtypeStruct((rows, M_WIDTH), F32),
            jax.ShapeDtypeStruct((depth, nbatch, M_HEADS, M_DH, M_DH), F32),
            jax.ShapeDtypeStruct((nbatch, M_HEADS, 1, M_DH), F32),
            jax.ShapeDtypeStruct((M_HEADS, rows, 1), F32),
        ],
        scratch_shapes=[
            pltpu.VMEM((LANES, L), F32),
            pltpu.VMEM((LANES, L), F32),
            pltpu.VMEM((L, LANES), F32),
            pltpu.VMEM((L + SUBLANES, M_DH), F32),
            pltpu.VMEM((L, M_DH), F32),
            pltpu.VMEM((L, M_DH), F32),
            pltpu.VMEM((L, M_DH), F32),
        ],
        input_output_aliases={11: 1} if has_prev else {},
        compiler_params=_params(("arbitrary", "arbitrary")),
        name="mlstm_sample",
    )(*args)


def _ssd_prompt_kernel(xs_ref, b_ref, c_ref, zs_ref, h_ref, wdt_ref,
                       cwx_ref, cwb_ref, cwc_ref, cbx_ref, cbb_ref, cbc_ref,
                       dtb_ref, alog_ref, dskip_ref, ng_ref, expand_ref,
                       out_ref, h1_ref, xex_sc, xeb_sc, xec_sc, ht_sc):
    c = pl.program_id(1)
    nc = pl.num_programs(1)
    L = xs_ref.shape[0]
    first = c == 0

    @pl.when(first)
    def _():
        ht_sc[...] = jnp.zeros(ht_sc.shape, F32)

    xs = _conv_rows(xs_ref[...], first, xex_sc, cwx_ref, cbx_ref)
    bm = _conv_rows(b_ref[...], first, xeb_sc, cwb_ref, cbb_ref)
    cm = _conv_rows(c_ref[...], first, xec_sc, cwc_ref, cbc_ref).astype(BF16)

    r2 = lax.broadcasted_iota(jnp.int32, (L, L), 0)
    c2 = lax.broadcasted_iota(jnp.int32, (L, L), 1)
    causal = c2 <= r2
    neg_inf = -jnp.inf
    dt = _softplus(_dot(h_ref[...], wdt_ref[...]) + dtb_ref[...])
    da = dt * (-jnp.exp(alog_ref[...]))
    cs = _dot_mask(causal, da)
    cs_t = cs.T
    ecs = jnp.exp(cs)
    wend = jnp.exp(cs[L - 1:L, :] - cs) * dt

    expand = expand_ref[...]
    dte = _widen(dt, expand)
    wende = _widen(wend, expand)
    ecse = _widen(ecs, expand)
    xdt = (xs * dte).astype(BF16)
    wx = (xs * wende).astype(BF16)
    lane = lax.broadcasted_iota(jnp.int32, (1, LANES), 1)
    low_half = lane < S_DH
    zero_slab = jnp.zeros((L, LANES), BF16)
    for g in range(S_GROUPS):
        gcols = slice(g * S_GW, (g + 1) * S_GW)
        scols = slice(g * S_STATE, (g + 1) * S_STATE)
        bg = bm[:, scols]
        cg = cm[:, scols]
        cb = _dot_nt(cg, bg.astype(BF16))
        pairs = []
        for pr in range(S_HPG // 2):
            h0 = g * S_HPG + 2 * pr
            mixes = []
            for hh in (h0, h0 + 1):
                dec = jnp.exp(jnp.where(causal, cs[:, hh:hh + 1] - cs_t[hh:hh + 1, :], neg_inf))
                mixes.append((cb * dec).astype(BF16))
            slab = xdt[:, h0 * S_DH:(h0 + 2) * S_DH]
            rhs = jnp.concatenate([jnp.where(low_half, slab, zero_slab),
                                   jnp.where(low_half, zero_slab, slab)], axis=0)
            pairs.append(_dot(jnp.concatenate(mixes, axis=1), rhs))
        y_intra = jnp.concatenate(pairs, axis=1)
        ht = ht_sc[:, gcols]
        y = y_intra + ecse[:, gcols] * _dot(cg, ht.astype(BF16)) + dskip_ref[:, gcols] * xs[:, gcols]
        y = y * _silu(zs_ref[:, gcols])
        out_ref[:, gcols] = _rms(y, ng_ref[:, gcols])
        ht_sc[:, gcols] = ecse[L - 1:L, gcols] * ht + _dot(bg.T.astype(BF16), wx[:, gcols])

    @pl.when(c == nc - 1)
    def _():
        for pr in range(S_HEADS // 2):
            blk = ht_sc[:, pr * LANES:(pr + 1) * LANES].T
            h1_ref[0, 2 * pr] = blk[:S_DH]
            h1_ref[0, 2 * pr + 1] = blk[S_DH:]


def _ssd_prompt(l, proj, h, W, nb, nc):
    L = CHUNK
    rows = nb * nc * L
    row = lambda b, c: b * nc + c
    conv_w = lambda width, blk: pl.BlockSpec((None, CONV_K, width), lambda b, c: (l, 0, blk))
    return pl.pallas_call(
        _ssd_prompt_kernel,
        grid=(nb, nc),
        in_specs=[
            pl.BlockSpec((L, S_WIDTH), lambda b, c: (row(b, c), COL_XBC // S_WIDTH)),
            pl.BlockSpec((L, S_BC), lambda b, c: (row(b, c), COL_B // S_BC)),
            pl.BlockSpec((L, S_BC), lambda b, c: (row(b, c), COL_C // S_BC)),
            pl.BlockSpec((L, S_WIDTH), lambda b, c: (row(b, c), COL_ZS // S_WIDTH)),
            pl.BlockSpec((L, D_MODEL), lambda b, c: (row(b, c), 0)),
            pl.BlockSpec((None, D_MODEL, LANES), lambda b, c: (l, 0, 0)),
            conv_w(S_WIDTH, 0), conv_w(S_BC, S_WIDTH // S_BC), conv_w(S_BC, S_WIDTH // S_BC + 1),
            _vec_spec(l, S_WIDTH, lambda b, c: 0),
            _vec_spec(l, S_BC, lambda b, c: S_WIDTH // S_BC),
            _vec_spec(l, S_BC, lambda b, c: S_WIDTH // S_BC + 1),
            _vec_spec(l, LANES),
            _vec_spec(l, LANES),
            _vec_spec(l, S_WIDTH),
            _vec_spec(l, S_WIDTH),
            pl.BlockSpec((2 * LANES, S_WIDTH), lambda b, c: (0, 0)),
        ],
        out_specs=[
            pl.BlockSpec((L, S_WIDTH), lambda b, c: (row(b, c), 0)),
            pl.BlockSpec((1, S_HEADS, S_DH, S_STATE), lambda b, c: (b, 0, 0, 0)),
        ],
        out_shape=[
            jax.ShapeDtypeStruct((rows, S_WIDTH), F32),
            jax.ShapeDtypeStruct((nb, S_HEADS, S_DH, S_STATE), F32),
        ],
        scratch_shapes=[
            pltpu.VMEM((SUBLANES, S_WIDTH), F32),
            pltpu.VMEM((SUBLANES, S_BC), F32),
            pltpu.VMEM((SUBLANES, S_BC), F32),
            pltpu.VMEM((S_STATE, S_WIDTH), F32),
        ],
        compiler_params=_params(("arbitrary", "arbitrary")),
        name="ssd_prompt",
    )(proj, proj, proj, proj, h, W["w_dt_c"], W["s_cw"], W["s_cw"], W["s_cw"], W["s_cb"], W["s_cb"], W["s_cb"],
      W["dtb_c"], W["alog_c"], W["dskip_wide"], W["s_ng"], W["expand"])


def _ssd_sample_kernel(xs_ref, b_ref, c_ref, zs_ref, h_ref, wdt_ref, hx_ref, hb_ref, hc_ref,
                       cwx_ref, cwb_ref, cwc_ref, cbx_ref, cbb_ref, cbc_ref,
                       dtb_ref, alog_ref, dskip_ref, ng_ref, expand_ref, h0_ref, alias_ref,
                       out_ref, h1_ref, xex_sc, xeb_sc, xec_sc, tot_sc, cm_sc, yi_sc, *, nseq, t_lo):
    del alias_ref
    L = xs_ref.shape[0]
    rs = L // nseq
    first = True
    xs = _conv_rows(xs_ref[...] + hx_ref[...], first, xex_sc, cwx_ref, cbx_ref)
    bm = _conv_rows(b_ref[...] + hb_ref[...], first, xeb_sc, cwb_ref, cbb_ref)
    cm = _conv_rows(c_ref[...] + hc_ref[...], first, xec_sc, cwc_ref, cbc_ref)
    bmb = bm.astype(BF16)
    cmb = cm.astype(BF16)
    cm_sc[0:L, :] = cm
    cm_sc[L:L + SUBLANES, :] = jnp.zeros((SUBLANES, S_STATE), F32)

    r2, c2, same, valid_c, ridx, valid_r = _block_masks(L, rs, t_lo)
    mask = same & (c2 <= r2) & valid_c
    neg_inf = -jnp.inf
    dt = jnp.where(valid_r, _softplus(_dot(h_ref[...], wdt_ref[...]) + dtb_ref[...]), 0.0)
    da = dt * (-jnp.exp(alog_ref[...]))
    cs = _dot_mask(mask, da)
    sfx = _dot_mask(same & (c2 > r2), da)
    tot_sc[...] = cs + sfx
    cs_t = cs.T
    expand = expand_ref[...]
    dte = _widen(dt, expand)
    wende = _widen(jnp.exp(sfx) * dt, expand)
    ecse = _widen(jnp.exp(cs), expand)
    xdt = (xs * dte).astype(BF16)
    wxt = (xs * wende).T.astype(BF16)
    cb = _dot_nt(cmb, bmb)
    lane = lax.broadcasted_iota(jnp.int32, (1, LANES), 1)
    low_half = lane < S_DH
    zero_slab = jnp.zeros((L, LANES), BF16)
    pairs = []
    for pr in range(S_HPG // 2):
        mixes = []
        for hh in (2 * pr, 2 * pr + 1):
            dec = jnp.exp(jnp.where(mask, cs[:, hh:hh + 1] - cs_t[hh:hh + 1, :], neg_inf))
            mixes.append((cb * dec).astype(BF16))
        slab = xdt[:, pr * LANES:(pr + 1) * LANES]
        rhs = jnp.concatenate([jnp.where(low_half, slab, zero_slab),
                               jnp.where(low_half, zero_slab, slab)], axis=0)
        pairs.append(_dot(jnp.concatenate(mixes, axis=1), rhs))
    y_intra = jnp.concatenate(pairs, axis=1)

    def seq_step(j, carry):
        rows = pl.ds(pl.multiple_of(j * rs, rs), rs)
        in_seq_r = (ridx >= j * rs) & (ridx < (j + 1) * rs)
        hj = h0_ref[j]
        c2rows = cm_sc[pl.ds(pl.multiple_of(j * rs, rs), 2 * rs), :].astype(BF16)
        yi_sc[rows, :] = _dot_nt(c2rows, hj.reshape(S_GW, S_STATE).astype(BF16))[:rs]
        upd = _dot(wxt, jnp.where(in_seq_r, bmb, jnp.zeros_like(bmb)))
        dec_j = jnp.exp(tot_sc[pl.ds(j * rs + rs - 1, 1), :])
        for hh in range(S_HPG):
            h1_ref[j, hh] = dec_j[:, hh:hh + 1] * hj[hh] + upd[hh * S_DH:(hh + 1) * S_DH, :]
        return carry

    lax.fori_loop(0, nseq, seq_step, 0, unroll=SEQ_UNROLL)
    y = y_intra + ecse * yi_sc[...] + dskip_ref[...] * xs
    out_ref[...] = _rms(y * _silu(zs_ref[...]), ng_ref[...])


def _ssd_sample(l, depth, proj, h, hist, W, h0, h_prev, nseq, t_lo):
    L = CHUNK
    rows = proj.shape[0]
    nb = rows // L
    xblk = COL_XBC // S_GW
    bblk = COL_B // S_STATE
    cblk = COL_C // S_STATE
    hb_blk = S_WIDTH // S_STATE
    hc_blk = hb_blk + S_GROUPS
    has_prev = h_prev is not None
    conv_w = lambda width, blk: pl.BlockSpec((None, CONV_K, width), lambda b, g: (l, 0, blk(g)))
    h_spec = pl.BlockSpec((None, nseq, S_HPG, S_DH, S_STATE), lambda b, g: (l, b, g, 0, 0))
    in_specs = [
        pl.BlockSpec((L, S_GW), lambda b, g: (b, xblk + g)),
        pl.BlockSpec((L, S_STATE), lambda b, g: (b, bblk + g)),
        pl.BlockSpec((L, S_STATE), lambda b, g: (b, cblk + g)),
        pl.BlockSpec((L, S_GW), lambda b, g: (b, COL_ZS // S_GW + g)),
        pl.BlockSpec((L, D_MODEL), lambda b, g: (b, 0)),
        pl.BlockSpec((None, D_MODEL, LANES), lambda b, g: (l, 0, g)),
        pl.BlockSpec((None, L, S_GW), lambda b, g: (l, b, g)),
        pl.BlockSpec((None, L, S_STATE), lambda b, g: (l, b, hb_blk + g)),
        pl.BlockSpec((None, L, S_STATE), lambda b, g: (l, b, hc_blk + g)),
        conv_w(S_GW, lambda g: g), conv_w(S_STATE, lambda g: hb_blk + g), conv_w(S_STATE, lambda g: hc_blk + g),
        _vec_spec(l, S_GW, lambda b, g: g),
        _vec_spec(l, S_STATE, lambda b, g: hb_blk + g),
        _vec_spec(l, S_STATE, lambda b, g: hc_blk + g),
        _vec_spec(l, LANES, lambda b, g: g),
        _vec_spec(l, LANES, lambda b, g: g),
        _vec_spec(l, S_GW, lambda b, g: g),
        _vec_spec(l, S_GW, lambda b, g: g),
        pl.BlockSpec((2 * LANES, S_GW), lambda b, g: (0, 0)),
        h_spec,
        pl.BlockSpec(memory_space=pl.ANY),
    ]
    args = [proj, proj, proj, proj, h, W["w_dt_g"], hist, hist, hist,
            W["s_cw"], W["s_cw"], W["s_cw"], W["s_cb"], W["s_cb"], W["s_cb"],
            W["dtb_g"], W["alog_g"], W["dskip_wide"], W["s_ng"], W["expand"], h0,
            h_prev if has_prev else jnp.zeros((SUBLANES, LANES), F32)]
    return pl.pallas_call(
        functools.partial(_ssd_sample_kernel, nseq=nseq, t_lo=t_lo),
        grid=(nb, S_GROUPS),
        in_specs=in_specs,
        out_specs=[pl.BlockSpec((L, S_GW), lambda b, g: (b, g)), h_spec],
        out_shape=[
            jax.ShapeDtypeStruct((rows, S_WIDTH), F32),
            jax.ShapeDtypeStruct((depth, nb * nseq, S_HEADS, S_DH, S_STATE), F32),
        ],
        scratch_shapes=[
            pltpu.VMEM((SUBLANES, S_GW), F32),
            pltpu.VMEM((SUBLANES, S_STATE), F32),
            pltpu.VMEM((SUBLANES, S_STATE), F32),
            pltpu.VMEM((L, LANES), F32),
            pltpu.VMEM((L + SUBLANES, S_STATE), F32),
            pltpu.VMEM((L, S_GW), F32),
        ],
        input_output_aliases={21: 1} if has_prev else {},
        compiler_params=_params(("arbitrary", "arbitrary")),
        name="ssd_sample",
    )(*args)


def _outproj_body(x_ref, om, os_ref, u_ref, v_ref, zc_ref, wmix_ref, bias_ref, gv_ref,
                  w0_ref, w1_ref, w2_ref, w3_ref, g_ref, y_ref, h_ref, vn_ref, oc_sc, *, final, rs, t_lo,
                  between=None):
    emit_vn = vn_ref is not None
    if between is None:
        between = lambda k: None
    L = C_CHUNK
    r2, c2, same, valid_c, _, _ = _block_masks(L, min(rs, L), t_lo)
    mask = same & (c2 <= r2) & valid_c
    for g in range(C_GROUPS):
        cols = slice(g * C_DG, (g + 1) * C_DG)
        w = jnp.where(mask, wmix_ref[g], 0.0).astype(BF16)
        for ch in range(x_ref.shape[0] // L):
            rows = slice(ch * L, (ch + 1) * L)
            vn = _rms(v_ref[rows, cols], gv_ref[:, cols])
            if emit_vn:
                vn_ref[rows, cols] = vn
            mixed = _dot(w, vn.astype(BF16)) + bias_ref[g]
            oc_sc[rows, cols] = (u_ref[rows, cols] * mixed * _silu(zc_ref[rows, cols])).astype(BF16)
    kb = w0_ref.shape[0]
    hk = kb // 2
    lhs = (lambda c: om[:, c],
           lambda c: os_ref[:, c].astype(BF16),
           lambda c: os_ref[:, slice(kb + c.start, kb + c.stop)].astype(BF16),
           lambda c: oc_sc[:, c])
    acc = None
    for blk, w_ref in enumerate((w0_ref, w1_ref, w2_ref, w3_ref)):
        for part in range(2):
            c = slice(part * hk, (part + 1) * hk)
            d = _dot(lhs[blk](c), w_ref[c, :])
            acc = d if acc is None else acc + d
            between(2 * blk + part)
    y = x_ref[...] + acc
    if t_lo:
        ridx = lax.broadcasted_iota(jnp.int32, (y.shape[0], 1), 0)
        y = jnp.where((ridx & (rs - 1)) >= t_lo, y, 0.0)
    if final:
        y_ref[...] = _rms(y, g_ref[...])
    else:
        y_ref[...] = y
        h_ref[...] = _rms(y, g_ref[...]).astype(BF16)


def _outproj_kernel(*refs, final, emit_vn, rs, t_lo):
    (x_ref, om_ref, os_ref, u_ref, v_ref, zc_ref, wmix_ref, bias_ref, gv_ref,
     w0_ref, w1_ref, w2_ref, w3_ref, g_ref, y_ref) = refs[:15]
    rest = list(refs[15:])
    h_ref = None if final else rest.pop(0)
    vn_ref = rest.pop(0) if emit_vn else None
    (oc_sc,) = rest
    _outproj_body(x_ref, om_ref[...].astype(BF16), os_ref, u_ref, v_ref, zc_ref, wmix_ref, bias_ref, gv_ref,
                  w0_ref, w1_ref, w2_ref, w3_ref, g_ref, y_ref, h_ref, vn_ref, oc_sc,
                  final=final, rs=rs, t_lo=t_lo)


def _mlstm_out_kernel(*refs, final, nc):
    (q_ref, k_ref, v_ref, o_ref, zm_ref, gates_ref, bo_ref, ng_ref,
     x_ref, os_ref, u_ref, v2_ref, zc_ref, wmix_ref, bias_ref, gv_ref,
     w0_ref, w1_ref, w2_ref, w3_ref, g_ref, c1_ref, n1_ref, m1_ref, y_ref) = refs[:25]
    rest = list(refs[25:])
    h_ref = None if final else rest.pop(0)
    om_sc, oc_sc = rest
    i = pl.program_id(0)

    @pl.when(i == 0)
    def _():
        om_sc[...] = jnp.zeros(om_sc.shape, BF16)

    @pl.when(i % nc == 0)
    def _():
        c1_ref[...] = jnp.zeros(c1_ref.shape, F32)
        n1_ref[...] = jnp.zeros(n1_ref.shape, F32)
        m1_ref[...] = jnp.zeros(m1_ref.shape, F32)

    def emit(cols, val):
        om_sc[i % 2, :, cols] = val.astype(BF16)

    prep = _mlstm_prep(gates_ref)

    pending = {}

    def head(slot):
        h, second = divmod(slot, 2)
        if second:
            pending.pop(h)()
        else:
            pending[h] = _mlstm_head(h, prep, q_ref, k_ref, v_ref, o_ref, zm_ref, bo_ref, ng_ref,
                                     c1_ref, n1_ref, m1_ref, emit)

    _outproj_body(x_ref, om_sc[(i + 1) % 2], os_ref, u_ref, v2_ref, zc_ref, wmix_ref, bias_ref, gv_ref,
                  w0_ref, w1_ref, w2_ref, w3_ref, g_ref, y_ref, h_ref, None, oc_sc,
                  final=final, rs=CHUNK, t_lo=0, between=head)


def _mlstm_out(l, q, k, v, o, gates, proj, x, os_, wmix, bias, W, g, final, nb, nc):
    L = M_CHUNK
    nt = nb * nc
    rows = nt * L
    cur = lambda i: jnp.minimum(i, nt - 1)
    prev = lambda i: jnp.maximum(i - 1, 0)
    cur_spec = pl.BlockSpec((L, M_WIDTH), lambda i: (cur(i), 0))
    prev_spec = pl.BlockSpec((L, D_MODEL), lambda i: (prev(i), 0))
    wspec = lambda r: pl.BlockSpec((None, M_WIDTH, D_MODEL), lambda i: (l, r, 0), pipeline_mode=pl.Buffered(1))
    out_shape = [
        jax.ShapeDtypeStruct((nb + 1, M_HEADS, M_DH, M_DH), F32),
        jax.ShapeDtypeStruct((nb + 1, M_HEADS, 1, M_DH), F32),
        jax.ShapeDtypeStruct((nb + 1, SUBLANES, LANES), F32),
        jax.ShapeDtypeStruct((rows, D_MODEL), F32),
    ]
    out_specs = [
        pl.BlockSpec((1, M_HEADS, M_DH, M_DH), lambda i: (i // nc, 0, 0, 0)),
        pl.BlockSpec((1, M_HEADS, 1, M_DH), lambda i: (i // nc, 0, 0, 0)),
        pl.BlockSpec((1, SUBLANES, LANES), lambda i: (i // nc, 0, 0)),
        prev_spec,
    ]
    if not final:
        out_shape.append(jax.ShapeDtypeStruct((rows, D_MODEL), BF16))
        out_specs.append(prev_spec)
    return pl.pallas_call(
        functools.partial(_mlstm_out_kernel, final=final, nc=nc),
        grid=(nt + 1,),
        in_specs=[
            cur_spec, cur_spec, cur_spec, cur_spec,
            pl.BlockSpec((L, M_WIDTH), lambda i: (cur(i), COL_ZM // M_WIDTH)),
            pl.BlockSpec((L, LANES), lambda i: (cur(i), 0)),
            _vec_spec(l, M_WIDTH),
            _vec_spec(l, M_WIDTH),
            prev_spec,
            prev_spec,
            pl.BlockSpec((L, C_WIDTH), lambda i: (prev(i), COL_U // C_WIDTH)),
            pl.BlockSpec((L, C_WIDTH), lambda i: (prev(i), COL_V // C_WIDTH)),
            pl.BlockSpec((L, C_WIDTH), lambda i: (prev(i), COL_ZC // C_WIDTH)),
            pl.BlockSpec((None, C_GROUPS, C_CHUNK, C_CHUNK), lambda i: (l, 0, 0, 0)),
            pl.BlockSpec((None, C_GROUPS, C_CHUNK, 1), lambda i: (l, 0, 0, 0)),
            _vec_spec(l, C_WIDTH),
            wspec(0), wspec(1), wspec(2), wspec(3),
            pl.BlockSpec((1, D_MODEL), lambda i: (0, 0)) if final else _vec_spec(l + 1, D_MODEL),
        ],
        out_specs=out_specs,
        out_shape=out_shape,
        scratch_shapes=[pltpu.VMEM((2, L, M_WIDTH), BF16), pltpu.VMEM((L, C_WIDTH), BF16)],
        compiler_params=_params(("arbitrary",)),
        name="mlstm_out",
    )(q, k, v, o, proj, gates, W["bo"], W["m_ng"], x, os_, proj, proj, proj, wmix, bias, W["gv"],
      W["wo"], W["wo"], W["wo"], W["wo"], g)


def _outproj(l, x, om, os_, proj, wmix, bias, gv, wo, g, final, emit_vn, rs, t_lo):
    rows = x.shape[0]
    tm = min(256, rows)
    kb = M_WIDTH
    wspec = lambda r: pl.BlockSpec((None, kb, D_MODEL), lambda i: (l, r, 0), pipeline_mode=pl.Buffered(1))
    row_spec = pl.BlockSpec((tm, D_MODEL), lambda i: (i, 0))
    out_shape = [jax.ShapeDtypeStruct((rows, D_MODEL), F32)]
    out_specs = [row_spec]
    if not final:
        out_shape.append(jax.ShapeDtypeStruct((rows, D_MODEL), BF16))
        out_specs.append(row_spec)
    if emit_vn:
        out_shape.append(jax.ShapeDtypeStruct((rows, C_WIDTH), F32))
        out_specs.append(pl.BlockSpec((tm, C_WIDTH), lambda i: (i, 0)))
    return pl.pallas_call(
        functools.partial(_outproj_kernel, final=final, emit_vn=emit_vn, rs=rs, t_lo=t_lo),
        grid=(rows // tm,),
        in_specs=[
            row_spec,
            pl.BlockSpec((tm, M_WIDTH), lambda i: (i, 0)),
            pl.BlockSpec((tm, S_WIDTH), lambda i: (i, 0)),
            pl.BlockSpec((tm, C_WIDTH), lambda i: (i, COL_U // C_WIDTH)),
            pl.BlockSpec((tm, C_WIDTH), lambda i: (i, COL_V // C_WIDTH)),
            pl.BlockSpec((tm, C_WIDTH), lambda i: (i, COL_ZC // C_WIDTH)),
            pl.BlockSpec((None, C_GROUPS, C_CHUNK, C_CHUNK), lambda i: (l, 0, 0, 0)),
            pl.BlockSpec((None, C_GROUPS, C_CHUNK, 1), lambda i: (l, 0, 0, 0)),
            _vec_spec(l, C_WIDTH),
            wspec(0), wspec(1), wspec(2), wspec(3),
            pl.BlockSpec((1, D_MODEL), lambda i: (0, 0)) if final else _vec_spec(l + 1, D_MODEL),
        ],
        out_specs=out_specs,
        out_shape=out_shape,
        scratch_shapes=[pltpu.VMEM((tm, C_WIDTH), BF16)],
        compiler_params=_params(("arbitrary",)),
        name="outproj",
    )(x, om, os_, proj, proj, proj, wmix, bias, gv, wo, wo, wo, wo, g)


def _heads_compact(a):
    return jnp.pad(a, ((0, 0), (0, LANES - S_HEADS)))[:, None, :]


def _heads_grouped(a):
    a = a.reshape(a.shape[0], S_GROUPS, S_HPG)
    return jnp.pad(a, ((0, 0), (0, 0), (0, LANES - S_HPG))).reshape(a.shape[0], 1, S_GROUPS * LANES)


def _prepare_weights(norm_g, w_in, m_conv_w, m_conv_b, m_w_qk, m_w_vo, m_b_o, m_w_gate, m_b_gate,
                     m_norm_g, s_conv_w, s_conv_b, s_dt_bias, s_A_log, s_D, s_norm_g, c_v_norm_g,
                     c_w_s, c_b_s, w_out):
    depth = w_in.shape[0]
    vec = lambda a: a.reshape(depth, 1, -1)
    dt0 = PROJ_HEAD
    w_all = w_in.astype(BF16)
    w_dt = w_in[:, :, dt0:dt0 + S_HEADS]
    w_dt_g = jnp.pad(w_dt.reshape(depth, D_MODEL, S_GROUPS, S_HPG),
                     ((0, 0), (0, 0), (0, 0), (0, LANES - S_HPG))).reshape(depth, D_MODEL, -1).astype(BF16)
    w_dt_c = jnp.pad(w_dt, ((0, 0), (0, 0), (0, LANES - S_HEADS))).astype(BF16)
    wg = m_w_gate.reshape(depth, M_HEADS, 3, M_DH, 2 * M_HEADS)
    wg = jnp.pad(wg, ((0, 0),) * 4 + ((0, LANES - 2 * M_HEADS),)).astype(BF16)
    head_of_lane = jnp.arange(S_WIDTH) // S_DH
    expand = (jnp.arange(LANES)[:, None] == head_of_lane[None, :]).astype(BF16)
    expand = jnp.concatenate([expand, expand], axis=0)
    return dict(
        norm_g=vec(norm_g), w_all=w_all, w_tail=w_all[:, :, dt0 + S_HEADS:], w_dt_c=w_dt_c, w_dt_g=w_dt_g,
        m_cw=m_conv_w, m_cb=vec(m_conv_b),
        wqk=m_w_qk.astype(BF16), wvo=m_w_vo.astype(BF16), wg=wg,
        bg=vec(jnp.pad(m_b_gate, ((0, 0), (0, LANES - 2 * M_HEADS)))),
        bo=vec(m_b_o), m_ng=vec(m_norm_g),
        s_cw=s_conv_w, s_cb=vec(s_conv_b),
        dtb_c=_heads_compact(s_dt_bias), alog_c=_heads_compact(s_A_log),
        dtb_g=_heads_grouped(s_dt_bias), alog_g=_heads_grouped(s_A_log),
        dskip_wide=vec(jnp.repeat(s_D, S_DH, axis=1)), s_ng=vec(s_norm_g), expand=expand,
        gv=vec(c_v_norm_g), wo=w_out.astype(BF16),
    )


def kernel(x_prompt, x_sample, state_mlstm_C, state_mlstm_n, state_mlstm_m, state_mlstm_conv, state_ssm, state_ssm_conv, norm_g, w_in, m_conv_w, m_conv_b, m_w_qk, m_w_vo, m_b_o, m_w_gate, m_b_gate, m_norm_g, s_conv_w, s_conv_b, s_dt_bias, s_A_log, s_D, s_norm_g, c_v_norm_g, c_w_s, c_b_s, w_out, final_norm_g):
    bp, seq, _ = x_prompt.shape
    bs, dec_seq, _ = x_sample.shape
    depth = w_in.shape[0]
    t_lo = SAMPLE_ROWS - dec_seq
    hist_lo = t_lo - (CONV_K - 1)
    nseq = CHUNK // SAMPLE_ROWS
    W = _prepare_weights(norm_g, w_in, m_conv_w, m_conv_b, m_w_qk, m_w_vo, m_b_o, m_w_gate, m_b_gate,
                         m_norm_g, s_conv_w, s_conv_b, s_dt_bias, s_A_log, s_D, s_norm_g, c_v_norm_g,
                         c_w_s, c_b_s, w_out)
    fg = final_norm_g[None, :]
    yp = x_prompt.reshape(bp * seq, D_MODEL)
    ys = jnp.pad(x_sample, ((0, 0), (t_lo, 0), (0, 0))).reshape(bs * SAMPLE_ROWS, D_MODEL)

    pad_hist = ((0, 0), (0, 0), (hist_lo, dec_seq), (0, 0))
    hist_m = jnp.pad(state_mlstm_conv, pad_hist).reshape(depth, bs * SAMPLE_ROWS, M_WIDTH)
    hist_s = jnp.pad(state_ssm_conv, pad_hist).reshape(depth, bs * SAMPLE_ROWS, -1)
    m0rows = jnp.repeat(jnp.swapaxes(state_mlstm_m, 1, 2)[..., None], SAMPLE_ROWS, axis=2)
    n0 = state_mlstm_n[:, :, :, None, :]
    reps = C_CHUNK // SAMPLE_ROWS
    w4 = jnp.pad(c_w_s[:, :, :dec_seq, :dec_seq], ((0, 0), (0, 0), (t_lo, 0), (t_lo, 0)))
    b4 = jnp.pad(c_b_s[:, :, :dec_seq], ((0, 0), (0, 0), (t_lo, 0)))
    wmix_s = jnp.tile(w4, (1, 1, reps, reps))
    bias_s = jnp.tile(b4, (1, 1, reps))[..., None]
    bias_p = c_b_s[..., None]

    hp = _norm(0, yp, W["norm_g"])
    hs = _norm(0, ys, W["norm_g"])
    outs_p, outs_s = [], []
    c_all = h_all = None
    mc = min(M_CHUNK, seq)
    sc = min(CHUNK, seq)
    tbp = min(512, bp * seq)
    tbs = min(512, bs * SAMPLE_ROWS)
    for l in range(depth):
        final = l == depth - 1
        proj = _inproj(l, hp, W["w_all"], W["w_tail"])
        q, k, v, o, gates = _mlstm_proj(l, proj, None, W, max(1, seq // tbp), tbp)
        out_s, h1 = _ssd_prompt(l, proj, hp, W, bp, seq // sc)
        res = _mlstm_out(l, q, k, v, o, gates, proj, yp, out_s, c_w_s, bias_p, W,
                         fg if final else W["norm_g"], final, bp, seq // mc)
        c1, n1, m1 = res[0][:bp], res[1][:bp], res[2][:bp]
        yp, hp = res[3], (None if final else res[4])
        pj = proj.reshape(bp, seq, PROJ_MAIN)
        outs_p.append((
            c1, n1.reshape(bp, M_HEADS, M_DH), m1[:, :M_HEADS, 0],
            pj[:, seq - (CONV_K - 1):, COL_XM:COL_XM + M_WIDTH], h1,
            pj[:, seq - (CONV_K - 1):, COL_XBC:COL_XBC + S_WIDTH + 2 * S_BC]))
        proj = _inproj(l, hs, W["w_all"], W["w_tail"])
        q, k, v, o, gates = _mlstm_proj(l, proj, hist_m, W, 1, tbs)
        out_m, c_all, n1, mrow = _mlstm_sample(l, depth, q, k, v, o, gates, proj, W, state_mlstm_C, n0,
                                               m0rows, c_all, nseq, t_lo)
        out_s, h_all = _ssd_sample(l, depth, proj, hs, hist_s, W, state_ssm, h_all, nseq, t_lo)
        res = _outproj(l, ys, out_m, out_s, proj, wmix_s, bias_s, W["gv"], W["wo"],
                       fg if final else W["norm_g"], final, True, SAMPLE_ROWS, t_lo)
        ys, hs, vn = res[0], (None if final else res[1]), res[-1]
        pj = proj.reshape(bs, SAMPLE_ROWS, PROJ_MAIN)
        outs_s.append((
            n1.reshape(bs, M_HEADS, M_DH), mrow[:, SAMPLE_ROWS - 1::SAMPLE_ROWS, 0].T,
            pj[:, SAMPLE_ROWS - (CONV_K - 1):, COL_XM:COL_XM + M_WIDTH],
            pj[:, SAMPLE_ROWS - (CONV_K - 1):, COL_XBC:COL_XBC + S_WIDTH + 2 * S_BC],
            vn.reshape(bs, SAMPLE_ROWS, C_WIDTH)[:, t_lo:]))
    p_out = [jnp.stack([s[i] for s in outs_p]) for i in range(6)]
    s_n, s_m, s_mconv, s_sconv, s_cv = [jnp.stack([s[i] for s in outs_s]) for i in range(5)]
    y_prompt = yp.reshape(bp, seq, D_MODEL)
    y_sample = ys.reshape(bs, SAMPLE_ROWS, D_MODEL)[:, t_lo:]
    return (y_prompt, y_sample, *p_out, c_all, s_n, s_m, s_mconv, h_all, s_sconv, s_cv)
```

```python
import functools

import jax
import jax.numpy as jnp
from jax import lax
from jax.experimental import pallas as pl
from jax.experimental.pallas import tpu as pltpu

F32 = jnp.float32
BF16 = jnp.bfloat16

D_MODEL = 2048
MIX_WIDTH = 2 * D_MODEL
M_WIDTH = MIX_WIDTH // 4
M_HEADS = 4
M_DH = M_WIDTH // M_HEADS
S_WIDTH = MIX_WIDTH // 2
S_DH = 64
S_HEADS = S_WIDTH // S_DH
S_GROUPS = 4
S_HPG = S_HEADS // S_GROUPS
S_STATE = 128
S_GW = S_HPG * S_DH
S_BC = S_GROUPS * S_STATE
C_WIDTH = MIX_WIDTH // 4
C_GROUPS = 4
C_DG = C_WIDTH // C_GROUPS
C_CHUNK = 128
CONV_K = 4
EPS = 1e-6

LANES = 128
SUBLANES = 8
SAMPLE_ROWS = 8
CHUNK = 128
M_CHUNK = 256
SSD_SUB = 4
SEQ_UNROLL = 8
VMEM_LIMIT = 56 * 1024 * 1024

COL_XM, COL_ZM, COL_ZS, COL_XBC, COL_U, COL_V, COL_ZC = 0, 1024, 2048, 4096, 7168, 8192, 9216
PROJ_MAIN = 10240
PROJ_HEAD = COL_U
COL_B = COL_XBC + S_WIDTH
COL_C = COL_B + S_BC


def _dot(a, b):
    return jnp.dot(a, b, preferred_element_type=F32)


def _dot_nt(a, b):
    return lax.dot_general(a, b, (((1,), (1,)), ((), ())), preferred_element_type=F32)


def _dot_mask(mask, x):
    m = jnp.where(mask, 1.0, 0.0).astype(BF16)
    hi = x.astype(BF16)
    r1 = x - hi.astype(F32)
    mid = r1.astype(BF16)
    lo = (r1 - mid.astype(F32)).astype(BF16)
    return _dot(m, hi) + _dot(m, mid) + _dot(m, lo)


def _widen(a, expand2):
    hi = a.astype(BF16)
    lo = (a - hi.astype(F32)).astype(BF16)
    return _dot(jnp.concatenate([hi, lo], axis=1), expand2)


def _sigmoid(x):
    return 0.5 * jnp.tanh(0.5 * x) + 0.5


def _silu(x):
    h = 0.5 * x
    return h * jnp.tanh(h) + h


def _softplus(x):
    return jnp.maximum(x, 0.0) + jnp.log1p(jnp.exp(-jnp.abs(x)))


def _log_sigmoid(x):
    return jnp.minimum(x, 0.0) - jnp.log1p(jnp.exp(-jnp.abs(x)))


def _rms(x, g):
    return x * lax.rsqrt(jnp.mean(x * x, axis=-1, keepdims=True) + EPS) * g


def _params(sem):
    return pltpu.CompilerParams(dimension_semantics=sem, vmem_limit_bytes=VMEM_LIMIT)


def _vec_spec(l, n, col=None):
    if col is None:
        return pl.BlockSpec((None, 1, n), lambda *ids: (l, 0, 0))
    return pl.BlockSpec((None, 1, n), lambda *ids: (l, 0, col(*ids)))


def _norm_kernel(x_ref, g_ref, h_ref):
    h_ref[...] = _rms(x_ref[...], g_ref[...]).astype(BF16)


def _norm(l, x, g):
    rows = x.shape[0]
    tb = min(1024, rows)
    return pl.pallas_call(
        _norm_kernel,
        grid=(rows // tb,),
        in_specs=[pl.BlockSpec((tb, D_MODEL), lambda i: (i, 0)), _vec_spec(l, D_MODEL)],
        out_specs=pl.BlockSpec((tb, D_MODEL), lambda i: (i, 0)),
        out_shape=jax.ShapeDtypeStruct((rows, D_MODEL), BF16),
        compiler_params=_params(("arbitrary",)),
        name="norm",
    )(x, g)


def _inproj_kernel(h_ref, wa_ref, wb_ref, proj_ref, *, n_head):
    j = pl.program_id(0)

    @pl.when(j < n_head)
    def _():
        proj_ref[...] = _dot(h_ref[...], wa_ref[...])

    @pl.when(j >= n_head)
    def _():
        proj_ref[...] = _dot(h_ref[...], wb_ref[...])


def _inproj(l, h, w_all, w_tail):
    rows = h.shape[0]
    tm = min(1024, rows)
    tn = 1024
    n_head = PROJ_HEAD // tn
    return pl.pallas_call(
        functools.partial(_inproj_kernel, n_head=n_head),
        grid=(PROJ_MAIN // tn, rows // tm),
        in_specs=[
            pl.BlockSpec((tm, D_MODEL), lambda j, i: (i, 0)),
            pl.BlockSpec((None, D_MODEL, tn), lambda j, i: (l, 0, jnp.minimum(j, n_head - 1))),
            pl.BlockSpec((None, D_MODEL, tn), lambda j, i: (l, 0, jnp.maximum(j - n_head, 0))),
        ],
        out_specs=pl.BlockSpec((tm, tn), lambda j, i: (i, j)),
        out_shape=jax.ShapeDtypeStruct((rows, PROJ_MAIN), F32),
        compiler_params=_params(("arbitrary", "arbitrary")),
        name="inproj",
    )(h, w_all, w_tail)


def _conv_rows(x, first, prev_sc, cw_ref, cb_ref):
    assert CONV_K == 4
    tb = x.shape[0]

    if first is not False:
        @pl.when(first)
        def _():
            prev_sc[...] = jnp.zeros(prev_sc.shape, F32)

    xe = jnp.concatenate([prev_sc[...], x], axis=0)
    x1 = pltpu.roll(xe, 1, 0)
    pair = cw_ref[1:2, :] * xe + cw_ref[0:1, :] * x1
    acc = (cb_ref[...] + cw_ref[3:4, :] * x + cw_ref[2:3, :] * x1[SUBLANES:, :]
           + pltpu.roll(pair, 2, 0)[SUBLANES:, :])
    prev_sc[...] = x[tb - SUBLANES:, :]
    return _silu(acc)


def _mlstm_proj_kernel(*refs, has_hist, blocks_per_seq):
    if has_hist:
        (xm_ref, he_ref, cw_ref, cb_ref, wqk_ref, wvo_ref, wg_ref, bg_ref,
         q_ref, k_ref, v_ref, o_ref, gates_ref, xe_sc) = refs
    else:
        (xm_ref, cw_ref, cb_ref, wqk_ref, wvo_ref, wg_ref, bg_ref,
         q_ref, k_ref, v_ref, o_ref, gates_ref, xe_sc) = refs
    i = pl.program_id(0)
    x = xm_ref[...]
    xin = x + he_ref[...] if has_hist else x
    xmc = _conv_rows(xin, i % blocks_per_seq == 0, xe_sc, cw_ref, cb_ref)
    tb = x.shape[0]
    gates = jnp.broadcast_to(bg_ref[...], (tb, LANES))
    for h in range(M_HEADS):
        cols = slice(h * M_DH, (h + 1) * M_DH)
        qk = _dot(xmc[:, cols].astype(BF16), wqk_ref[h])
        vo = _dot(x[:, cols].astype(BF16), wvo_ref[h])
        qb = qk[:, :M_DH].astype(BF16)
        kb = qk[:, M_DH:].astype(BF16)
        vb = vo[:, :M_DH].astype(BF16)
        gates = gates + _dot(qb, wg_ref[h, 0]) + _dot(kb, wg_ref[h, 1]) + _dot(vb, wg_ref[h, 2])
        q_ref[:, cols] = qb
        k_ref[:, cols] = (qk[:, M_DH:] * (M_DH ** -0.5)).astype(BF16)
        v_ref[:, cols] = vb
        o_ref[:, cols] = vo[:, M_DH:]
    lane = lax.broadcasted_iota(jnp.int32, (tb, LANES), 1)
    gates_ref[...] = jnp.where(lane < M_HEADS, gates, _log_sigmoid(gates))


def _mlstm_proj(l, proj, hist, W, blocks_per_seq, tb):
    rows = proj.shape[0]
    has_hist = hist is not None
    in_specs = [pl.BlockSpec((tb, M_WIDTH), lambda i: (i, COL_XM // M_WIDTH))]
    args = [proj]
    if has_hist:
        in_specs.append(pl.BlockSpec((None, tb, M_WIDTH), lambda i: (l, i, 0)))
        args.append(hist)
    in_specs += [
        pl.BlockSpec((None, CONV_K, M_WIDTH), lambda i: (l, 0, 0)),
        _vec_spec(l, M_WIDTH),
        pl.BlockSpec((None, M_HEADS, M_DH, 2 * M_DH), lambda i: (l, 0, 0, 0)),
        pl.BlockSpec((None, M_HEADS, M_DH, 2 * M_DH), lambda i: (l, 0, 0, 0)),
        pl.BlockSpec((None, M_HEADS, 3, M_DH, LANES), lambda i: (l, 0, 0, 0, 0)),
        _vec_spec(l, LANES),
    ]
    args += [W["m_cw"], W["m_cb"], W["wqk"], W["wvo"], W["wg"], W["bg"]]
    row_spec = pl.BlockSpec((tb, M_WIDTH), lambda i: (i, 0))
    return pl.pallas_call(
        functools.partial(_mlstm_proj_kernel, has_hist=has_hist, blocks_per_seq=blocks_per_seq),
        grid=(rows // tb,),
        in_specs=in_specs,
        out_specs=[row_spec, row_spec, row_spec, row_spec, pl.BlockSpec((tb, LANES), lambda i: (i, 0))],
        out_shape=[
            jax.ShapeDtypeStruct((rows, M_WIDTH), BF16),
            jax.ShapeDtypeStruct((rows, M_WIDTH), BF16),
            jax.ShapeDtypeStruct((rows, M_WIDTH), BF16),
            jax.ShapeDtypeStruct((rows, M_WIDTH), F32),
            jax.ShapeDtypeStruct((rows, LANES), F32),
        ],
        scratch_shapes=[pltpu.VMEM((SUBLANES, M_WIDTH), F32)],
        compiler_params=_params(("arbitrary",)),
        name="mlstm_proj",
    )(*args)


def _mlstm_finish(hh, o, bo, ng, zm):
    hm = _sigmoid(o + bo) * hh
    return _rms(hm, ng) * _silu(zm)


def _mlstm_prep(gates_ref):
    L = gates_ref.shape[0]
    r2 = lax.broadcasted_iota(jnp.int32, (L, L), 0)
    c2 = lax.broadcasted_iota(jnp.int32, (L, L), 1)
    causal = c2 <= r2
    g = gates_ref[...]
    b_all = _dot_mask(causal, g)
    return causal, g, b_all, g.T, b_all.T


def _mlstm_head(h, prep, q_ref, k_ref, v_ref, o_ref, zm_ref, bo_ref, ng_ref, c1_ref, n1_ref, m1_ref, emit):
    causal, g, b_all, g_t, b_t = prep
    L = q_ref.shape[0]
    neg_inf = -jnp.inf
    cols = slice(h * M_DH, (h + 1) * M_DH)
    f = M_HEADS + h
    g_row = g_t[h:h + 1, :] - b_t[f:f + 1, :]
    b_col = b_all[:, f:f + 1]
    g_col = g[:, h:h + 1] - b_col
    mprev = m1_ref[0, h:h + 1, 0:1]
    gm = jnp.where(causal, g_row, neg_inf)
    m_col = jnp.maximum(mprev, jnp.max(gm, axis=1, keepdims=True))
    w_intra = jnp.exp(gm - m_col)
    w_inter = jnp.exp(mprev - m_col)
    m_new = b_col + m_col
    m_last = m_col[L - 1:L, :]
    wl_col = jnp.exp(g_col - m_last)
    wli = jnp.exp(mprev - m_last)
    q = q_ref[:, cols]
    k = k_ref[:, cols]
    v = v_ref[:, cols]
    cst = c1_ref[0, h]
    nst = n1_ref[0, h]
    s = _dot_nt(q, k) * w_intra
    num = _dot(s.astype(BF16), v) + w_inter * _dot_nt(q, cst.astype(BF16))
    den = (jnp.sum(s, axis=1, keepdims=True)
           + w_inter * jnp.sum(q.astype(F32) * nst, axis=1, keepdims=True))
    hh = num / jnp.maximum(jnp.abs(den), jnp.exp(-m_new))
    emit(cols, _mlstm_finish(hh, o_ref[:, cols], bo_ref[:, cols], ng_ref[:, cols], zm_ref[:, cols]))

    def update_state():
        c1_ref[0, h] = wli * cst + _dot((v.astype(F32) * wl_col).T.astype(BF16), k)
        n1_ref[0, h] = wli * nst + jnp.sum(k.astype(F32) * wl_col, axis=0, keepdims=True)
        m1_ref[0, h:h + 1, :] = jnp.broadcast_to(m_new[L - 1:L, :], (1, LANES))

    return update_state


def _block_masks(L, rs, t_lo):
    shift = rs.bit_length() - 1
    r2 = lax.broadcasted_iota(jnp.int32, (L, L), 0)
    c2 = lax.broadcasted_iota(jnp.int32, (L, L), 1)
    same = (r2 >> shift) == (c2 >> shift)
    valid_c = (c2 & (rs - 1)) >= t_lo
    ridx = lax.broadcasted_iota(jnp.int32, (L, 1), 0)
    valid_r = (ridx & (rs - 1)) >= t_lo
    return r2, c2, same, valid_c, ridx, valid_r


def _mlstm_sample_kernel(q_ref, k_ref, v_ref, o_ref, zm_ref, gates_ref, bo_ref, ng_ref,
                         c0_ref, n0_ref, m0_ref, alias_ref,
                         out_ref, c1_ref, n1_ref, mrow_ref,
                         gt_sc, bt_sc, col_sc, q_sc, numi_sc, nrow_sc, wk_sc, *, nseq, t_lo):
    del alias_ref
    h = pl.program_id(1)
    L = q_ref.shape[0]
    rs = L // nseq
    r2, c2, same, valid_c, ridx, valid_r = _block_masks(L, rs, t_lo)
    eye = r2 == c2
    neg_inf = -jnp.inf
    g = gates_ref[...]
    b_all = _dot_mask(same & (c2 <= r2), jnp.where(valid_r, g, 0.0))
    gt_sc[...] = g.T
    bt_sc[...] = b_all.T
    lane_g = lax.broadcasted_iota(jnp.int32, (1, LANES), 1)
    i_col = jnp.sum(jnp.where(lane_g == h, g, 0.0), axis=1, keepdims=True)
    b_col = jnp.sum(jnp.where(lane_g == h + M_HEADS, b_all, 0.0), axis=1, keepdims=True)
    g_row = gt_sc[pl.ds(h, 1), :] - bt_sc[pl.ds(h + M_HEADS, 1), :]
    g_col = i_col - b_col
    mask = same & (c2 <= r2) & valid_c
    mprev = m0_ref[0]
    gm = jnp.where(mask, g_row, neg_inf)
    m_col = jnp.maximum(mprev, jnp.max(gm, axis=1, keepdims=True))
    m_row = jnp.sum(jnp.where(eye, m_col, 0.0), axis=0, keepdims=True)
    mlast_col = jnp.max(jnp.where(same, m_row, neg_inf), axis=1, keepdims=True)
    w_intra = jnp.exp(gm - m_col)
    w_inter = jnp.exp(mprev - m_col)
    m_new = b_col + m_col
    wl_col = jnp.where(valid_r, jnp.exp(g_col - mlast_col), 0.0)
    col_sc[:, 0:1] = jnp.exp(mprev - mlast_col)

    q = q_ref[...]
    k = k_ref[...]
    v = v_ref[...]
    s = _dot_nt(q, k) * w_intra
    num = _dot(s.astype(BF16), v)
    den = jnp.sum(s, axis=1, keepdims=True)
    wvt = (v.astype(F32) * wl_col).T.astype(BF16)
    wk_sc[...] = k.astype(F32) * wl_col
    qf = q.astype(F32)
    q_sc[0:L, :] = qf
    q_sc[L:L + SUBLANES, :] = jnp.zeros((SUBLANES, M_DH), F32)

    def seq_step(j, carry):
        rows = pl.ds(pl.multiple_of(j * rs, rs), rs)
        in_seq_r = (ridx >= j * rs) & (ridx < (j + 1) * rs)
        cj = c0_ref[j, 0]
        nj = n0_ref[j, 0]
        q2 = q_sc[pl.ds(pl.multiple_of(j * rs, rs), 2 * rs), :].astype(BF16)
        numi_sc[rows, :] = _dot_nt(q2, cj.astype(BF16))[:rs]
        nrow_sc[rows, :] = jnp.broadcast_to(nj, (rs, M_DH))
        wli = col_sc[pl.ds(j * rs + rs - 1, 1), 0:1]
        c1_ref[j, 0] = wli * cj + _dot(wvt, jnp.where(in_seq_r, k, jnp.zeros_like(k)))
        n1_ref[j, 0] = wli * nj + jnp.sum(wk_sc[rows, :], axis=0, keepdims=True)
        return carry

    lax.fori_loop(0, nseq, seq_step, 0, unroll=SEQ_UNROLL)
    num = num + w_inter * numi_sc[...]
    den = den + w_inter * jnp.sum(qf * nrow_sc[...], axis=1, keepdims=True)
    hh = num / jnp.maximum(jnp.abs(den), jnp.exp(-m_new))
    out_ref[...] = _mlstm_finish(hh, o_ref[...], bo_ref[...], ng_ref[...], zm_ref[...])
    mrow_ref[0] = m_new


def _mlstm_sample(l, depth, q, k, v, o, gates, proj, W, c0, n0, m0rows, c_prev, nseq, t_lo):
    L = CHUNK
    rows = q.shape[0]
    nb = rows // L
    nbatch = nb * nseq
    rowblk = lambda b, h: (b, h)
    c_spec = pl.BlockSpec((None, nseq, 1, M_DH, M_DH), lambda b, h: (l, b, h, 0, 0))
    n_spec_in = pl.BlockSpec((None, nseq, 1, 1, M_DH), lambda b, h: (l, b, h, 0, 0))
    n_spec_out = pl.BlockSpec((nseq, 1, 1, M_DH), lambda b, h: (b, h, 0, 0))
    has_prev = c_prev is not None
    in_specs = [
        pl.BlockSpec((L, M_DH), rowblk),
        pl.BlockSpec((L, M_DH), rowblk),
        pl.BlockSpec((L, M_DH), rowblk),
        pl.BlockSpec((L, M_DH), rowblk),
        pl.BlockSpec((L, M_DH), lambda b, h: (b, COL_ZM // M_DH + h)),
        pl.BlockSpec((L, LANES), lambda b, h: (b, 0)),
        _vec_spec(l, M_DH, lambda b, h: h),
        _vec_spec(l, M_DH, lambda b, h: h),
        c_spec,
        n_spec_in,
        pl.BlockSpec((None, 1, L, 1), lambda b, h: (l, h, b, 0)),
        pl.BlockSpec(memory_space=pl.ANY),
    ]
    args = [q, k, v, o, proj, gates, W["bo"], W["m_ng"], c0, n0, m0rows,
            c_prev if has_prev else jnp.zeros((SUBLANES, LANES), F32)]
    return pl.pallas_call(
        functools.partial(_mlstm_sample_kernel, nseq=nseq, t_lo=t_lo),
        grid=(nb, M_HEADS),
        in_specs=in_specs,
        out_specs=[pl.BlockSpec((L, M_DH), rowblk), c_spec, n_spec_out,
                   pl.BlockSpec((1, L, 1), lambda b, h: (h, b, 0))],
        out_shape=[
            jax.ShapeDtypeStruct((rows, M_WIDTH), F32),
            jax.ShapeDtypeStruct((depth, nbatch, M_HEADS, M_DH, M_DH), F32),
            jax.ShapeDtypeStruct((nbatch, M_HEADS, 1, M_DH), F32),
            jax.ShapeDtypeStruct((M_HEADS, rows, 1), F32),
        ],
        scratch_shapes=[
            pltpu.VMEM((LANES, L), F32),
            pltpu.VMEM((LANES, L), F32),
            pltpu.VMEM((L, LANES), F32),
            pltpu.VMEM((L + SUBLANES, M_DH), F32),
            pltpu.VMEM((L, M_DH), F32),
            pltpu.VMEM((L, M_DH), F32),
            pltpu.VMEM((L, M_DH), F32),
        ],
        input_output_aliases={11: 1} if has_prev else {},
        compiler_params=_params(("arbitrary", "arbitrary")),
        name="mlstm_sample",
    )(*args)


def _ssd_prompt_kernel(xs_ref, b_ref, c_ref, zs_ref, h_ref, wdt_ref,
                       cwx_ref, cwb_ref, cwc_ref, cbx_ref, cbb_ref, cbc_ref,
                       dtb_ref, alog_ref, dskip_ref, ng_ref, expand_ref,
                       out_ref, h1_ref, xex_sc, xeb_sc, xec_sc, ht_sc):
    c = pl.program_id(1)
    nc = pl.num_programs(1)
    L = CHUNK

    @pl.when(c == 0)
    def _():
        ht_sc[...] = jnp.zeros(ht_sc.shape, F32)

    for sub in range(xs_ref.shape[0] // L):
        _ssd_prompt_chunk(slice(sub * L, (sub + 1) * L), (c == 0) if sub == 0 else False,
                          xs_ref, b_ref, c_ref, zs_ref, h_ref, wdt_ref,
                          cwx_ref, cwb_ref, cwc_ref, cbx_ref, cbb_ref, cbc_ref,
                          dtb_ref, alog_ref, dskip_ref, ng_ref, expand_ref,
                          out_ref, xex_sc, xeb_sc, xec_sc, ht_sc)

    @pl.when(c == nc - 1)
    def _():
        for pr in range(S_HEADS // 2):
            blk = ht_sc[:, pr * LANES:(pr + 1) * LANES].T
            h1_ref[0, 2 * pr] = blk[:S_DH]
            h1_ref[0, 2 * pr + 1] = blk[S_DH:]


def _ssd_prompt_chunk(rows, first, xs_ref, b_ref, c_ref, zs_ref, h_ref, wdt_ref,
                      cwx_ref, cwb_ref, cwc_ref, cbx_ref, cbb_ref, cbc_ref,
                      dtb_ref, alog_ref, dskip_ref, ng_ref, expand_ref,
                      out_ref, xex_sc, xeb_sc, xec_sc, ht_sc):
    L = CHUNK
    xs = _conv_rows(xs_ref[rows, :], first, xex_sc, cwx_ref, cbx_ref)
    bm = _conv_rows(b_ref[rows, :], first, xeb_sc, cwb_ref, cbb_ref)
    cm = _conv_rows(c_ref[rows, :], first, xec_sc, cwc_ref, cbc_ref).astype(BF16)

    r2 = lax.broadcasted_iota(jnp.int32, (L, L), 0)
    c2 = lax.broadcasted_iota(jnp.int32, (L, L), 1)
    causal = c2 <= r2
    neg_inf = -jnp.inf
    dt = _softplus(_dot(h_ref[rows, :], wdt_ref[...]) + dtb_ref[...])
    da = dt * (-jnp.exp(alog_ref[...]))
    cs = _dot_mask(causal, da)
    cs_t = cs.T
    ecs = jnp.exp(cs)
    wend = jnp.exp(cs[L - 1:L, :] - cs) * dt

    expand = expand_ref[...]
    dte = _widen(dt, expand)
    wende = _widen(wend, expand)
    ecse = _widen(ecs, expand)
    xdt = (xs * dte).astype(BF16)
    wx = (xs * wende).astype(BF16)
    lane = lax.broadcasted_iota(jnp.int32, (1, LANES), 1)
    low_half = lane < S_DH
    zero_slab = jnp.zeros((L, LANES), BF16)
    for g in range(S_GROUPS):
        gcols = slice(g * S_GW, (g + 1) * S_GW)
        scols = slice(g * S_STATE, (g + 1) * S_STATE)
        bg = bm[:, scols]
        cg = cm[:, scols]
        cb = _dot_nt(cg, bg.astype(BF16))
        pairs = []
        for pr in range(S_HPG // 2):
            h0 = g * S_HPG + 2 * pr
            mixes = []
            for hh in (h0, h0 + 1):
                dec = jnp.exp(jnp.where(causal, cs[:, hh:hh + 1] - cs_t[hh:hh + 1, :], neg_inf))
                mixes.append((cb * dec).astype(BF16))
            slab = xdt[:, h0 * S_DH:(h0 + 2) * S_DH]
            rhs = jnp.concatenate([jnp.where(low_half, slab, zero_slab),
                                   jnp.where(low_half, zero_slab, slab)], axis=0)
            pairs.append(_dot(jnp.concatenate(mixes, axis=1), rhs))
        y_intra = jnp.concatenate(pairs, axis=1)
        ht = ht_sc[:, gcols]
        y = y_intra + ecse[:, gcols] * _dot(cg, ht.astype(BF16)) + dskip_ref[:, gcols] * xs[:, gcols]
        y = y * _silu(zs_ref[rows, gcols])
        out_ref[rows, gcols] = _rms(y, ng_ref[:, gcols])
        ht_sc[:, gcols] = ecse[L - 1:L, gcols] * ht + _dot(bg.T.astype(BF16), wx[:, gcols])


def _ssd_prompt(l, proj, h, W, nb, nc):
    L = CHUNK * SSD_SUB
    rows = nb * nc * L
    row = lambda b, c: b * nc + c
    conv_w = lambda width, blk: pl.BlockSpec((None, CONV_K, width), lambda b, c: (l, 0, blk))
    return pl.pallas_call(
        _ssd_prompt_kernel,
        grid=(nb, nc),
        in_specs=[
            pl.BlockSpec((L, S_WIDTH), lambda b, c: (row(b, c), COL_XBC // S_WIDTH)),
            pl.BlockSpec((L, S_BC), lambda b, c: (row(b, c), COL_B // S_BC)),
            pl.BlockSpec((L, S_BC), lambda b, c: (row(b, c), COL_C // S_BC)),
            pl.BlockSpec((L, S_WIDTH), lambda b, c: (row(b, c), COL_ZS // S_WIDTH)),
            pl.BlockSpec((L, D_MODEL), lambda b, c: (row(b, c), 0)),
            pl.BlockSpec((None, D_MODEL, LANES), lambda b, c: (l, 0, 0)),
            conv_w(S_WIDTH, 0), conv_w(S_BC, S_WIDTH // S_BC), conv_w(S_BC, S_WIDTH // S_BC + 1),
            _vec_spec(l, S_WIDTH, lambda b, c: 0),
            _vec_spec(l, S_BC, lambda b, c: S_WIDTH // S_BC),
            _vec_spec(l, S_BC, lambda b, c: S_WIDTH // S_BC + 1),
            _vec_spec(l, LANES),
            _vec_spec(l, LANES),
            _vec_spec(l, S_WIDTH),
            _vec_spec(l, S_WIDTH),
            pl.BlockSpec((2 * LANES, S_WIDTH), lambda b, c: (0, 0)),
        ],
        out_specs=[
            pl.BlockSpec((L, S_WIDTH), lambda b, c: (row(b, c), 0)),
            pl.BlockSpec((1, S_HEADS, S_DH, S_STATE), lambda b, c: (b, 0, 0, 0)),
        ],
        out_shape=[
            jax.ShapeDtypeStruct((rows, S_WIDTH), F32),
            jax.ShapeDtypeStruct((nb, S_HEADS, S_DH, S_STATE), F32),
        ],
        scratch_shapes=[
            pltpu.VMEM((SUBLANES, S_WIDTH), F32),
            pltpu.VMEM((SUBLANES, S_BC), F32),
            pltpu.VMEM((SUBLANES, S_BC), F32),
            pltpu.VMEM((S_STATE, S_WIDTH), F32),
        ],
        compiler_params=_params(("arbitrary", "arbitrary")),
        name="ssd_prompt",
    )(proj, proj, proj, proj, h, W["w_dt_c"], W["s_cw"], W["s_cw"], W["s_cw"], W["s_cb"], W["s_cb"], W["s_cb"],
      W["dtb_c"], W["alog_c"], W["dskip_wide"], W["s_ng"], W["expand"])


def _ssd_sample_kernel(xs_ref, b_ref, c_ref, zs_ref, h_ref, wdt_ref, hx_ref, hb_ref, hc_ref,
                       cwx_ref, cwb_ref, cwc_ref, cbx_ref, cbb_ref, cbc_ref,
                       dtb_ref, alog_ref, dskip_ref, ng_ref, expand_ref, h0_ref, alias_ref,
                       out_ref, h1_ref, xex_sc, xeb_sc, xec_sc, tot_sc, cm_sc, yi_sc, *, nseq, t_lo):
    del alias_ref
    L = xs_ref.shape[0]
    rs = L // nseq
    first = True
    xs = _conv_rows(xs_ref[...] + hx_ref[...], first, xex_sc, cwx_ref, cbx_ref)
    bm = _conv_rows(b_ref[...] + hb_ref[...], first, xeb_sc, cwb_ref, cbb_ref)
    cm = _conv_rows(c_ref[...] + hc_ref[...], first, xec_sc, cwc_ref, cbc_ref)
    bmb = bm.astype(BF16)
    cmb = cm.astype(BF16)
    cm_sc[0:L, :] = cm
    cm_sc[L:L + SUBLANES, :] = jnp.zeros((SUBLANES, S_STATE), F32)

    r2, c2, same, valid_c, ridx, valid_r = _block_masks(L, rs, t_lo)
    mask = same & (c2 <= r2) & valid_c
    neg_inf = -jnp.inf
    dt = jnp.where(valid_r, _softplus(_dot(h_ref[...], wdt_ref[...]) + dtb_ref[...]), 0.0)
    da = dt * (-jnp.exp(alog_ref[...]))
    cs = _dot_mask(mask, da)
    sfx = _dot_mask(same & (c2 > r2), da)
    tot_sc[...] = cs + sfx
    cs_t = cs.T
    expand = expand_ref[...]
    dte = _widen(dt, expand)
    wende = _widen(jnp.exp(sfx) * dt, expand)
    ecse = _widen(jnp.exp(cs), expand)
    xdt = (xs * dte).astype(BF16)
    wxt = (xs * wende).T.astype(BF16)
    cb = _dot_nt(cmb, bmb)
    lane = lax.broadcasted_iota(jnp.int32, (1, LANES), 1)
    low_half = lane < S_DH
    zero_slab = jnp.zeros((L, LANES), BF16)
    pairs = []
    for pr in range(S_HPG // 2):
        mixes = []
        for hh in (2 * pr, 2 * pr + 1):
            dec = jnp.exp(jnp.where(mask, cs[:, hh:hh + 1] - cs_t[hh:hh + 1, :], neg_inf))
            mixes.append((cb * dec).astype(BF16))
        slab = xdt[:, pr * LANES:(pr + 1) * LANES]
        rhs = jnp.concatenate([jnp.where(low_half, slab, zero_slab),
                               jnp.where(low_half, zero_slab, slab)], axis=0)
        pairs.append(_dot(jnp.concatenate(mixes, axis=1), rhs))
    y_intra = jnp.concatenate(pairs, axis=1)

    def seq_step(j, carry):
        rows = pl.ds(pl.multiple_of(j * rs, rs), rs)
        in_seq_r = (ridx >= j * rs) & (ridx < (j + 1) * rs)
        hj = h0_ref[j]
        c2rows = cm_sc[pl.ds(pl.multiple_of(j * rs, rs), 2 * rs), :].astype(BF16)
        yi_sc[rows, :] = _dot_nt(c2rows, hj.reshape(S_GW, S_STATE).astype(BF16))[:rs]
        upd = _dot(wxt, jnp.where(in_seq_r, bmb, jnp.zeros_like(bmb)))
        dec_j = jnp.exp(tot_sc[pl.ds(j * rs + rs - 1, 1), :])
        for hh in range(S_HPG):
            h1_ref[j, hh] = dec_j[:, hh:hh + 1] * hj[hh] + upd[hh * S_DH:(hh + 1) * S_DH, :]
        return carry

    lax.fori_loop(0, nseq, seq_step, 0, unroll=SEQ_UNROLL)
    y = y_intra + ecse * yi_sc[...] + dskip_ref[...] * xs
    out_ref[...] = _rms(y * _silu(zs_ref[...]), ng_ref[...])


def _ssd_sample(l, depth, proj, h, hist, W, h0, h_prev, nseq, t_lo):
    L = CHUNK
    rows = proj.shape[0]
    nb = rows // L
    xblk = COL_XBC // S_GW
    bblk = COL_B // S_STATE
    cblk = COL_C // S_STATE
    hb_blk = S_WIDTH // S_STATE
    hc_blk = hb_blk + S_GROUPS
    has_prev = h_prev is not None
    conv_w = lambda width, blk: pl.BlockSpec((None, CONV_K, width), lambda b, g: (l, 0, blk(g)))
    h_spec = pl.BlockSpec((None, nseq, S_HPG, S_DH, S_STATE), lambda b, g: (l, b, g, 0, 0))
    in_specs = [
        pl.BlockSpec((L, S_GW), lambda b, g: (b, xblk + g)),
        pl.BlockSpec((L, S_STATE), lambda b, g: (b, bblk + g)),
        pl.BlockSpec((L, S_STATE), lambda b, g: (b, cblk + g)),
        pl.BlockSpec((L, S_GW), lambda b, g: (b, COL_ZS // S_GW + g)),
        pl.BlockSpec((L, D_MODEL), lambda b, g: (b, 0)),
        pl.BlockSpec((None, D_MODEL, LANES), lambda b, g: (l, 0, g)),
        pl.BlockSpec((None, L, S_GW), lambda b, g: (l, b, g)),
        pl.BlockSpec((None, L, S_STATE), lambda b, g: (l, b, hb_blk + g)),
        pl.BlockSpec((None, L, S_STATE), lambda b, g: (l, b, hc_blk + g)),
        conv_w(S_GW, lambda g: g), conv_w(S_STATE, lambda g: hb_blk + g), conv_w(S_STATE, lambda g: hc_blk + g),
        _vec_spec(l, S_GW, lambda b, g: g),
        _vec_spec(l, S_STATE, lambda b, g: hb_blk + g),
        _vec_spec(l, S_STATE, lambda b, g: hc_blk + g),
        _vec_spec(l, LANES, lambda b, g: g),
        _vec_spec(l, LANES, lambda b, g: g),
        _vec_spec(l, S_GW, lambda b, g: g),
        _vec_spec(l, S_GW, lambda b, g: g),
        pl.BlockSpec((2 * LANES, S_GW), lambda b, g: (0, 0)),
        h_spec,
        pl.BlockSpec(memory_space=pl.ANY),
    ]
    args = [proj, proj, proj, proj, h, W["w_dt_g"], hist, hist, hist,
            W["s_cw"], W["s_cw"], W["s_cw"], W["s_cb"], W["s_cb"], W["s_cb"],
            W["dtb_g"], W["alog_g"], W["dskip_wide"], W["s_ng"], W["expand"], h0,
            h_prev if has_prev else jnp.zeros((SUBLANES, LANES), F32)]
    return pl.pallas_call(
        functools.partial(_ssd_sample_kernel, nseq=nseq, t_lo=t_lo),
        grid=(nb, S_GROUPS),
        in_specs=in_specs,
        out_specs=[pl.BlockSpec((L, S_GW), lambda b, g: (b, g)), h_spec],
        out_shape=[
            jax.ShapeDtypeStruct((rows, S_WIDTH), F32),
            jax.ShapeDtypeStruct((depth, nb * nseq, S_HEADS, S_DH, S_STATE), F32),
        ],
        scratch_shapes=[
            pltpu.VMEM((SUBLANES, S_GW), F32),
            pltpu.VMEM((SUBLANES, S_STATE), F32),
            pltpu.VMEM((SUBLANES, S_STATE), F32),
            pltpu.VMEM((L, LANES), F32),
            pltpu.VMEM((L + SUBLANES, S_STATE), F32),
            pltpu.VMEM((L, S_GW), F32),
        ],
        input_output_aliases={21: 1} if has_prev else {},
        compiler_params=_params(("arbitrary", "arbitrary")),
        name="ssd_sample",
    )(*args)


def _outproj_body(x_ref, om, os_ref, u_ref, v_ref, zc_ref, wmix_ref, bias_ref, gv_ref,
                  w0_ref, w1_ref, w2_ref, w3_ref, g_ref, y_ref, h_ref, vn_ref, oc_sc, *, final, rs, t_lo,
                  between=None):
    emit_vn = vn_ref is not None
    if between is None:
        between = lambda k: None
    L = C_CHUNK
    r2, c2, same, valid_c, _, _ = _block_masks(L, min(rs, L), t_lo)
    mask = same & (c2 <= r2) & valid_c
    for g in range(C_GROUPS):
        cols = slice(g * C_DG, (g + 1) * C_DG)
        w = jnp.where(mask, wmix_ref[g], 0.0).astype(BF16)
        for ch in range(x_ref.shape[0] // L):
            rows = slice(ch * L, (ch + 1) * L)
            vn = _rms(v_ref[rows, cols], gv_ref[:, cols])
            if emit_vn:
                vn_ref[rows, cols] = vn
            mixed = _dot(w, vn.astype(BF16)) + bias_ref[g]
            oc_sc[rows, cols] = (u_ref[rows, cols] * mixed * _silu(zc_ref[rows, cols])).astype(BF16)
    kb = w0_ref.shape[0]
    hk = kb // 2
    lhs = (lambda c: om[:, c],
           lambda c: os_ref[:, c].astype(BF16),
           lambda c: os_ref[:, slice(kb + c.start, kb + c.stop)].astype(BF16),
           lambda c: oc_sc[:, c])
    acc = None
    for blk, w_ref in enumerate((w0_ref, w1_ref, w2_ref, w3_ref)):
        for part in range(2):
            c = slice(part * hk, (part + 1) * hk)
            d = _dot(lhs[blk](c), w_ref[c, :])
            acc = d if acc is None else acc + d
            between(2 * blk + part)
    y = x_ref[...] + acc
    if t_lo:
        ridx = lax.broadcasted_iota(jnp.int32, (y.shape[0], 1), 0)
        y = jnp.where((ridx & (rs - 1)) >= t_lo, y, 0.0)
    if final:
        y_ref[...] = _rms(y, g_ref[...])
    else:
        y_ref[...] = y
        h_ref[...] = _rms(y, g_ref[...]).astype(BF16)


def _outproj_kernel(*refs, final, emit_vn, rs, t_lo):
    (x_ref, om_ref, os_ref, u_ref, v_ref, zc_ref, wmix_ref, bias_ref, gv_ref,
     w0_ref, w1_ref, w2_ref, w3_ref, g_ref, y_ref) = refs[:15]
    rest = list(refs[15:])
    h_ref = None if final else rest.pop(0)
    vn_ref = rest.pop(0) if emit_vn else None
    (oc_sc,) = rest
    _outproj_body(x_ref, om_ref[...].astype(BF16), os_ref, u_ref, v_ref, zc_ref, wmix_ref, bias_ref, gv_ref,
                  w0_ref, w1_ref, w2_ref, w3_ref, g_ref, y_ref, h_ref, vn_ref, oc_sc,
                  final=final, rs=rs, t_lo=t_lo)


def _mlstm_out_kernel(*refs, final, nc):
    (q_ref, k_ref, v_ref, o_ref, zm_ref, gates_ref, bo_ref, ng_ref,
     x_ref, os_ref, u_ref, v2_ref, zc_ref, wmix_ref, bias_ref, gv_ref,
     w0_ref, w1_ref, w2_ref, w3_ref, g_ref, c1_ref, n1_ref, m1_ref, y_ref) = refs[:25]
    rest = list(refs[25:])
    h_ref = None if final else rest.pop(0)
    om_sc, oc_sc = rest
    i = pl.program_id(0)

    @pl.when(i == 0)
    def _():
        om_sc[...] = jnp.zeros(om_sc.shape, BF16)

    @pl.when(i % nc == 0)
    def _():
        c1_ref[...] = jnp.zeros(c1_ref.shape, F32)
        n1_ref[...] = jnp.zeros(n1_ref.shape, F32)
        m1_ref[...] = jnp.zeros(m1_ref.shape, F32)

    def emit(cols, val):
        om_sc[i % 2, :, cols] = val.astype(BF16)

    prep = _mlstm_prep(gates_ref)

    pending = {}

    def head(slot):
        h, second = divmod(slot, 2)
        if second:
            pending.pop(h)()
        else:
            pending[h] = _mlstm_head(h, prep, q_ref, k_ref, v_ref, o_ref, zm_ref, bo_ref, ng_ref,
                                     c1_ref, n1_ref, m1_ref, emit)

    _outproj_body(x_ref, om_sc[(i + 1) % 2], os_ref, u_ref, v2_ref, zc_ref, wmix_ref, bias_ref, gv_ref,
                  w0_ref, w1_ref, w2_ref, w3_ref, g_ref, y_ref, h_ref, None, oc_sc,
                  final=final, rs=CHUNK, t_lo=0, between=head)


def _mlstm_out(l, q, k, v, o, gates, proj, x, os_, wmix, bias, W, g, final, nb, nc):
    L = M_CHUNK
    nt = nb * nc
    rows = nt * L
    cur = lambda i: jnp.minimum(i, nt - 1)
    prev = lambda i: jnp.maximum(i - 1, 0)
    cur_spec = pl.BlockSpec((L, M_WIDTH), lambda i: (cur(i), 0))
    prev_spec = pl.BlockSpec((L, D_MODEL), lambda i: (prev(i), 0))
    wspec = lambda r: pl.BlockSpec((None, M_WIDTH, D_MODEL), lambda i: (l, r, 0), pipeline_mode=pl.Buffered(1))
    out_shape = [
        jax.ShapeDtypeStruct((nb + 1, M_HEADS, M_DH, M_DH), F32),
        jax.ShapeDtypeStruct((nb + 1, M_HEADS, 1, M_DH), F32),
        jax.ShapeDtypeStruct((nb + 1, SUBLANES, LANES), F32),
        jax.ShapeDtypeStruct((rows, D_MODEL), F32),
    ]
    out_specs = [
        pl.BlockSpec((1, M_HEADS, M_DH, M_DH), lambda i: (i // nc, 0, 0, 0)),
        pl.BlockSpec((1, M_HEADS, 1, M_DH), lambda i: (i // nc, 0, 0, 0)),
        pl.BlockSpec((1, SUBLANES, LANES), lambda i: (i // nc, 0, 0)),
        prev_spec,
    ]
    if not final:
        out_shape.append(jax.ShapeDtypeStruct((rows, D_MODEL), BF16))
        out_specs.append(prev_spec)
    return pl.pallas_call(
        functools.partial(_mlstm_out_kernel, final=final, nc=nc),
        grid=(nt + 1,),
        in_specs=[
            cur_spec, cur_spec, cur_spec, cur_spec,
            pl.BlockSpec((L, M_WIDTH), lambda i: (cur(i), COL_ZM // M_WIDTH)),
            pl.BlockSpec((L, LANES), lambda i: (cur(i), 0)),
            _vec_spec(l, M_WIDTH),
            _vec_spec(l, M_WIDTH),
            prev_spec,
            prev_spec,
            pl.BlockSpec((L, C_WIDTH), lambda i: (prev(i), COL_U // C_WIDTH)),
            pl.BlockSpec((L, C_WIDTH), lambda i: (prev(i), COL_V // C_WIDTH)),
            pl.BlockSpec((L, C_WIDTH), lambda i: (prev(i), COL_ZC // C_WIDTH)),
            pl.BlockSpec((None, C_GROUPS, C_CHUNK, C_CHUNK), lambda i: (l, 0, 0, 0)),
            pl.BlockSpec((None, C_GROUPS, C_CHUNK, 1), lambda i: (l, 0, 0, 0)),
            _vec_spec(l, C_WIDTH),
            wspec(0), wspec(1), wspec(2), wspec(3),
            pl.BlockSpec((1, D_MODEL), lambda i: (0, 0)) if final else _vec_spec(l + 1, D_MODEL),
        ],
        out_specs=out_specs,
        out_shape=out_shape,
        scratch_shapes=[pltpu.VMEM((2, L, M_WIDTH), BF16), pltpu.VMEM((L, C_WIDTH), BF16)],
        compiler_params=_params(("arbitrary",)),
        name="mlstm_out",
    )(q, k, v, o, proj, gates, W["bo"], W["m_ng"], x, os_, proj, proj, proj, wmix, bias, W["gv"],
      W["wo"], W["wo"], W["wo"], W["wo"], g)


def _outproj(l, x, om, os_, proj, wmix, bias, gv, wo, g, final, emit_vn, rs, t_lo):
    rows = x.shape[0]
    tm = min(256, rows)
    kb = M_WIDTH
    wspec = lambda r: pl.BlockSpec((None, kb, D_MODEL), lambda i: (l, r, 0), pipeline_mode=pl.Buffered(1))
    row_spec = pl.BlockSpec((tm, D_MODEL), lambda i: (i, 0))
    out_shape = [jax.ShapeDtypeStruct((rows, D_MODEL), F32)]
    out_specs = [row_spec]
    if not final:
        out_shape.append(jax.ShapeDtypeStruct((rows, D_MODEL), BF16))
        out_specs.append(row_spec)
    if emit_vn:
        out_shape.append(jax.ShapeDtypeStruct((rows, C_WIDTH), F32))
        out_specs.append(pl.BlockSpec((tm, C_WIDTH), lambda i: (i, 0)))
    return pl.pallas_call(
        functools.partial(_outproj_kernel, final=final, emit_vn=emit_vn, rs=rs, t_lo=t_lo),
        grid=(rows // tm,),
        in_specs=[
            row_spec,
            pl.BlockSpec((tm, M_WIDTH), lambda i: (i, 0)),
            pl.BlockSpec((tm, S_WIDTH), lambda i: (i, 0)),
            pl.BlockSpec((tm, C_WIDTH), lambda i: (i, COL_U // C_WIDTH)),
            pl.BlockSpec((tm, C_WIDTH), lambda i: (i, COL_V // C_WIDTH)),
            pl.BlockSpec((tm, C_WIDTH), lambda i: (i, COL_ZC // C_WIDTH)),
            pl.BlockSpec((None, C_GROUPS, C_CHUNK, C_CHUNK), lambda i: (l, 0, 0, 0)),
            pl.BlockSpec((None, C_GROUPS, C_CHUNK, 1), lambda i: (l, 0, 0, 0)),
            _vec_spec(l, C_WIDTH),
            wspec(0), wspec(1), wspec(2), wspec(3),
            pl.BlockSpec((1, D_MODEL), lambda i: (0, 0)) if final else _vec_spec(l + 1, D_MODEL),
        ],
        out_specs=out_specs,
        out_shape=out_shape,
        scratch_shapes=[pltpu.VMEM((tm, C_WIDTH), BF16)],
        compiler_params=_params(("arbitrary",)),
        name="outproj",
    )(x, om, os_, proj, proj, proj, wmix, bias, gv, wo, wo, wo, wo, g)


def _heads_compact(a):
    return jnp.pad(a, ((0, 0), (0, LANES - S_HEADS)))[:, None, :]


def _heads_grouped(a):
    a = a.reshape(a.shape[0], S_GROUPS, S_HPG)
    return jnp.pad(a, ((0, 0), (0, 0), (0, LANES - S_HPG))).reshape(a.shape[0], 1, S_GROUPS * LANES)


def _prepare_weights(norm_g, w_in, m_conv_w, m_conv_b, m_w_qk, m_w_vo, m_b_o, m_w_gate, m_b_gate,
                     m_norm_g, s_conv_w, s_conv_b, s_dt_bias, s_A_log, s_D, s_norm_g, c_v_norm_g,
                     c_w_s, c_b_s, w_out):
    depth = w_in.shape[0]
    vec = lambda a: a.reshape(depth, 1, -1)
    dt0 = PROJ_HEAD
    w_all = w_in.astype(BF16)
    w_dt = w_in[:, :, dt0:dt0 + S_HEADS]
    w_dt_g = jnp.pad(w_dt.reshape(depth, D_MODEL, S_GROUPS, S_HPG),
                     ((0, 0), (0, 0), (0, 0), (0, LANES - S_HPG))).reshape(depth, D_MODEL, -1).astype(BF16)
    w_dt_c = jnp.pad(w_dt, ((0, 0), (0, 0), (0, LANES - S_HEADS))).astype(BF16)
    wg = m_w_gate.reshape(depth, M_HEADS, 3, M_DH, 2 * M_HEADS)
    wg = jnp.pad(wg, ((0, 0),) * 4 + ((0, LANES - 2 * M_HEADS),)).astype(BF16)
    head_of_lane = jnp.arange(S_WIDTH) // S_DH
    expand = (jnp.arange(LANES)[:, None] == head_of_lane[None, :]).astype(BF16)
    expand = jnp.concatenate([expand, expand], axis=0)
    return dict(
        norm_g=vec(norm_g), w_all=w_all, w_tail=w_all[:, :, dt0 + S_HEADS:], w_dt_c=w_dt_c, w_dt_g=w_dt_g,
        m_cw=m_conv_w, m_cb=vec(m_conv_b),
        wqk=m_w_qk.astype(BF16), wvo=m_w_vo.astype(BF16), wg=wg,
        bg=vec(jnp.pad(m_b_gate, ((0, 0), (0, LANES - 2 * M_HEADS)))),
        bo=vec(m_b_o), m_ng=vec(m_norm_g),
        s_cw=s_conv_w, s_cb=vec(s_conv_b),
        dtb_c=_heads_compact(s_dt_bias), alog_c=_heads_compact(s_A_log),
        dtb_g=_heads_grouped(s_dt_bias), alog_g=_heads_grouped(s_A_log),
        dskip_wide=vec(jnp.repeat(s_D, S_DH, axis=1)), s_ng=vec(s_norm_g), expand=expand,
        gv=vec(c_v_norm_g), wo=w_out.astype(BF16),
    )


def kernel(x_prompt, x_sample, state_mlstm_C, state_mlstm_n, state_mlstm_m, state_mlstm_conv, state_ssm, state_ssm_conv, norm_g, w_in, m_conv_w, m_conv_b, m_w_qk, m_w_vo, m_b_o, m_w_gate, m_b_gate, m_norm_g, s_conv_w, s_conv_b, s_dt_bias, s_A_log, s_D, s_norm_g, c_v_norm_g, c_w_s, c_b_s, w_out, final_norm_g):
    bp, seq, _ = x_prompt.shape
    bs, dec_seq, _ = x_sample.shape
    depth = w_in.shape[0]
    t_lo = SAMPLE_ROWS - dec_seq
    hist_lo = t_lo - (CONV_K - 1)
    nseq = CHUNK // SAMPLE_ROWS
    W = _prepare_weights(norm_g, w_in, m_conv_w, m_conv_b, m_w_qk, m_w_vo, m_b_o, m_w_gate, m_b_gate,
                         m_norm_g, s_conv_w, s_conv_b, s_dt_bias, s_A_log, s_D, s_norm_g, c_v_norm_g,
                         c_w_s, c_b_s, w_out)
    fg = final_norm_g[None, :]
    yp = x_prompt.reshape(bp * seq, D_MODEL)
    ys = jnp.pad(x_sample, ((0, 0), (t_lo, 0), (0, 0))).reshape(bs * SAMPLE_ROWS, D_MODEL)

    pad_hist = ((0, 0), (0, 0), (hist_lo, dec_seq), (0, 0))
    hist_m = jnp.pad(state_mlstm_conv, pad_hist).reshape(depth, bs * SAMPLE_ROWS, M_WIDTH)
    hist_s = jnp.pad(state_ssm_conv, pad_hist).reshape(depth, bs * SAMPLE_ROWS, -1)
    m0rows = jnp.repeat(jnp.swapaxes(state_mlstm_m, 1, 2)[..., None], SAMPLE_ROWS, axis=2)
    n0 = state_mlstm_n[:, :, :, None, :]
    reps = C_CHUNK // SAMPLE_ROWS
    w4 = jnp.pad(c_w_s[:, :, :dec_seq, :dec_seq], ((0, 0), (0, 0), (t_lo, 0), (t_lo, 0)))
    b4 = jnp.pad(c_b_s[:, :, :dec_seq], ((0, 0), (0, 0), (t_lo, 0)))
    wmix_s = jnp.tile(w4, (1, 1, reps, reps))
    bias_s = jnp.tile(b4, (1, 1, reps))[..., None]
    bias_p = c_b_s[..., None]

    hp = _norm(0, yp, W["norm_g"])
    hs = _norm(0, ys, W["norm_g"])
    outs_p, outs_s = [], []
    c_all = h_all = None
    mc = min(M_CHUNK, seq)
    sc = min(CHUNK * SSD_SUB, seq)
    tbp = min(512, bp * seq)
    tbs = min(512, bs * SAMPLE_ROWS)
    for l in range(depth):
        final = l == depth - 1
        proj = _inproj(l, hp, W["w_all"], W["w_tail"])
        q, k, v, o, gates = _mlstm_proj(l, proj, None, W, max(1, seq // tbp), tbp)
        out_s, h1 = _ssd_prompt(l, proj, hp, W, bp, seq // sc)
        res = _mlstm_out(l, q, k, v, o, gates, proj, yp, out_s, c_w_s, bias_p, W,
                         fg if final else W["norm_g"], final, bp, seq // mc)
        c1, n1, m1 = res[0][:bp], res[1][:bp], res[2][:bp]
        yp, hp = res[3], (None if final else res[4])
        pj = proj.reshape(bp, seq, PROJ_MAIN)
        outs_p.append((
            c1, n1.reshape(bp, M_HEADS, M_DH), m1[:, :M_HEADS, 0],
            pj[:, seq - (CONV_K - 1):, COL_XM:COL_XM + M_WIDTH], h1,
            pj[:, seq - (CONV_K - 1):, COL_XBC:COL_XBC + S_WIDTH + 2 * S_BC]))
        proj = _inproj(l, hs, W["w_all"], W["w_tail"])
        q, k, v, o, gates = _mlstm_proj(l, proj, hist_m, W, 1, tbs)
        out_m, c_all, n1, mrow = _mlstm_sample(l, depth, q, k, v, o, gates, proj, W, state_mlstm_C, n0,
                                               m0rows, c_all, nseq, t_lo)
        out_s, h_all = _ssd_sample(l, depth, proj, hs, hist_s, W, state_ssm, h_all, nseq, t_lo)
        res = _outproj(l, ys, out_m, out_s, proj, wmix_s, bias_s, W["gv"], W["wo"],
                       fg if final else W["norm_g"], final, True, SAMPLE_ROWS, t_lo)
        ys, hs, vn = res[0], (None if final else res[1]), res[-1]
        pj = proj.reshape(bs, SAMPLE_ROWS, PROJ_MAIN)
        outs_s.append((
            n1.reshape(bs, M_HEADS, M_DH), mrow[:, SAMPLE_ROWS - 1::SAMPLE_ROWS, 0].T,
            pj[:, SAMPLE_ROWS - (CONV_K - 1):, COL_XM:COL_XM + M_WIDTH],
            pj[:, SAMPLE_ROWS - (CONV_K - 1):, COL_XBC:COL_XBC + S_WIDTH + 2 * S_BC],
            vn.reshape(bs, SAMPLE_ROWS, C_WIDTH)[:, t_lo:]))
    p_out = [jnp.stack([s[i] for s in outs_p]) for i in range(6)]
    s_n, s_m, s_mconv, s_sconv, s_cv = [jnp.stack([s[i] for s in outs_s]) for i in range(5)]
    y_prompt = yp.reshape(bp, seq, D_MODEL)
    y_sample = ys.reshape(bs, SAMPLE_ROWS, D_MODEL)[:, t_lo:]
    return (y_prompt, y_sample, *p_out, c_all, s_n, s_m, s_mconv, h_all, s_sconv, s_cv)
```

```python
import functools

import jax
import jax.numpy as jnp
from jax import lax
from jax.experimental import pallas as pl
from jax.experimental.pallas import tpu as pltpu

F32 = jnp.float32
BF16 = jnp.bfloat16

D_MODEL = 2048
MIX_WIDTH = 2 * D_MODEL
M_WIDTH = MIX_WIDTH // 4
M_HEADS = 4
M_DH = M_WIDTH // M_HEADS
S_WIDTH = MIX_WIDTH // 2
S_DH = 64
S_HEADS = S_WIDTH // S_DH
S_GROUPS = 4
S_HPG = S_HEADS // S_GROUPS
S_STATE = 128
S_GW = S_HPG * S_DH
S_BC = S_GROUPS * S_STATE
C_WIDTH = MIX_WIDTH // 4
C_GROUPS = 4
C_DG = C_WIDTH // C_GROUPS
C_CHUNK = 128
CONV_K = 4
EPS = 1e-6

LANES = 128
SUBLANES = 8
SAMPLE_ROWS = 8
CHUNK = 128
M_CHUNK = 256
SSD_SUB = 4
SEQ_UNROLL = 16
VMEM_LIMIT = 56 * 1024 * 1024

COL_XM, COL_ZM, COL_ZS, COL_XBC, COL_U, COL_V, COL_ZC = 0, 1024, 2048, 4096, 7168, 8192, 9216
PROJ_MAIN = 10240
PROJ_HEAD = COL_U
COL_B = COL_XBC + S_WIDTH
COL_C = COL_B + S_BC


def _dot(a, b):
    return jnp.dot(a, b, preferred_element_type=F32)


def _dot_nt(a, b):
    return lax.dot_general(a, b, (((1,), (1,)), ((), ())), preferred_element_type=F32)


def _dot_mask(mask, x):
    m = jnp.where(mask, 1.0, 0.0).astype(BF16)
    hi = x.astype(BF16)
    r1 = x - hi.astype(F32)
    mid = r1.astype(BF16)
    lo = (r1 - mid.astype(F32)).astype(BF16)
    return _dot(m, hi) + _dot(m, mid) + _dot(m, lo)


def _widen(a, expand2):
    hi = a.astype(BF16)
    lo = (a - hi.astype(F32)).astype(BF16)
    return _dot(jnp.concatenate([hi, lo], axis=1), expand2)


def _sigmoid(x):
    return 0.5 * jnp.tanh(0.5 * x) + 0.5


def _silu(x):
    h = 0.5 * x
    return h * jnp.tanh(h) + h


def _softplus(x):
    return jnp.maximum(x, 0.0) + jnp.log1p(jnp.exp(-jnp.abs(x)))


def _log_sigmoid(x):
    return jnp.minimum(x, 0.0) - jnp.log1p(jnp.exp(-jnp.abs(x)))


def _rms(x, g):
    return x * lax.rsqrt(jnp.mean(x * x, axis=-1, keepdims=True) + EPS) * g


def _params(sem):
    return pltpu.CompilerParams(dimension_semantics=sem, vmem_limit_bytes=VMEM_LIMIT)


def _vec_spec(l, n, col=None):
    if col is None:
        return pl.BlockSpec((None, 1, n), lambda *ids: (l, 0, 0))
    return pl.BlockSpec((None, 1, n), lambda *ids: (l, 0, col(*ids)))


def _norm_kernel(x_ref, g_ref, h_ref):
    h_ref[...] = _rms(x_ref[...], g_ref[...]).astype(BF16)


def _norm(l, x, g):
    rows = x.shape[0]
    tb = min(1024, rows)
    return pl.pallas_call(
        _norm_kernel,
        grid=(rows // tb,),
        in_specs=[pl.BlockSpec((tb, D_MODEL), lambda i: (i, 0)), _vec_spec(l, D_MODEL)],
        out_specs=pl.BlockSpec((tb, D_MODEL), lambda i: (i, 0)),
        out_shape=jax.ShapeDtypeStruct((rows, D_MODEL), BF16),
        compiler_params=_params(("arbitrary",)),
        name="norm",
    )(x, g)


def _inproj_kernel(h_ref, wa_ref, wb_ref, proj_ref, *, n_head):
    j = pl.program_id(0)

    @pl.when(j < n_head)
    def _():
        proj_ref[...] = _dot(h_ref[...], wa_ref[...])

    @pl.when(j >= n_head)
    def _():
        proj_ref[...] = _dot(h_ref[...], wb_ref[...])


def _inproj(l, h, w_all, w_tail):
    rows = h.shape[0]
    tm = min(1024, rows)
    tn = 1024
    n_head = PROJ_HEAD // tn
    return pl.pallas_call(
        functools.partial(_inproj_kernel, n_head=n_head),
        grid=(PROJ_MAIN // tn, rows // tm),
        in_specs=[
            pl.BlockSpec((tm, D_MODEL), lambda j, i: (i, 0)),
            pl.BlockSpec((None, D_MODEL, tn), lambda j, i: (l, 0, jnp.minimum(j, n_head - 1))),
            pl.BlockSpec((None, D_MODEL, tn), lambda j, i: (l, 0, jnp.maximum(j - n_head, 0))),
        ],
        out_specs=pl.BlockSpec((tm, tn), lambda j, i: (i, j)),
        out_shape=jax.ShapeDtypeStruct((rows, PROJ_MAIN), F32),
        compiler_params=_params(("arbitrary", "arbitrary")),
        name="inproj",
    )(h, w_all, w_tail)


def _conv_rows(x, first, prev_sc, cw_ref, cb_ref):
    assert CONV_K == 4
    tb = x.shape[0]

    if first is not False:
        @pl.when(first)
        def _():
            prev_sc[...] = jnp.zeros(prev_sc.shape, F32)

    xe = jnp.concatenate([prev_sc[...], x], axis=0)
    x1 = pltpu.roll(xe, 1, 0)
    pair = cw_ref[1:2, :] * xe + cw_ref[0:1, :] * x1
    acc = (cb_ref[...] + cw_ref[3:4, :] * x + cw_ref[2:3, :] * x1[SUBLANES:, :]
           + pltpu.roll(pair, 2, 0)[SUBLANES:, :])
    prev_sc[...] = x[tb - SUBLANES:, :]
    return _silu(acc)


def _mlstm_proj_kernel(*refs, has_hist, blocks_per_seq):
    if has_hist:
        (xm_ref, he_ref, cw_ref, cb_ref, wqk_ref, wvo_ref, wg_ref, bg_ref,
         q_ref, k_ref, v_ref, o_ref, gates_ref, xe_sc) = refs
    else:
        (xm_ref, cw_ref, cb_ref, wqk_ref, wvo_ref, wg_ref, bg_ref,
         q_ref, k_ref, v_ref, o_ref, gates_ref, xe_sc) = refs
    i = pl.program_id(0)
    x = xm_ref[...]
    xin = x + he_ref[...] if has_hist else x
    xmc = _conv_rows(xin, i % blocks_per_seq == 0, xe_sc, cw_ref, cb_ref)
    tb = x.shape[0]
    gates = jnp.broadcast_to(bg_ref[...], (tb, LANES))
    for h in range(M_HEADS):
        cols = slice(h * M_DH, (h + 1) * M_DH)
        qk = _dot(xmc[:, cols].astype(BF16), wqk_ref[h])
        vo = _dot(x[:, cols].astype(BF16), wvo_ref[h])
        qb = qk[:, :M_DH].astype(BF16)
        kb = qk[:, M_DH:].astype(BF16)
        vb = vo[:, :M_DH].astype(BF16)
        gates = gates + _dot(qb, wg_ref[h, 0]) + _dot(kb, wg_ref[h, 1]) + _dot(vb, wg_ref[h, 2])
        q_ref[:, cols] = qb
        k_ref[:, cols] = (qk[:, M_DH:] * (M_DH ** -0.5)).astype(BF16)
        v_ref[:, cols] = vb
        o_ref[:, cols] = vo[:, M_DH:]
    lane = lax.broadcasted_iota(jnp.int32, (tb, LANES), 1)
    gates_ref[...] = jnp.where(lane < M_HEADS, gates, _log_sigmoid(gates))


def _mlstm_proj(l, proj, hist, W, blocks_per_seq, tb):
    rows = proj.shape[0]
    has_hist = hist is not None
    in_specs = [pl.BlockSpec((tb, M_WIDTH), lambda i: (i, COL_XM // M_WIDTH))]
    args = [proj]
    if has_hist:
        in_specs.append(pl.BlockSpec((None, tb, M_WIDTH), lambda i: (l, i, 0)))
        args.append(hist)
    in_specs += [
        pl.BlockSpec((None, CONV_K, M_WIDTH), lambda i: (l, 0, 0)),
        _vec_spec(l, M_WIDTH),
        pl.BlockSpec((None, M_HEADS, M_DH, 2 * M_DH), lambda i: (l, 0, 0, 0)),
        pl.BlockSpec((None, M_HEADS, M_DH, 2 * M_DH), lambda i: (l, 0, 0, 0)),
        pl.BlockSpec((None, M_HEADS, 3, M_DH, LANES), lambda i: (l, 0, 0, 0, 0)),
        _vec_spec(l, LANES),
    ]
    args += [W["m_cw"], W["m_cb"], W["wqk"], W["wvo"], W["wg"], W["bg"]]
    row_spec = pl.BlockSpec((tb, M_WIDTH), lambda i: (i, 0))
    return pl.pallas_call(
        functools.partial(_mlstm_proj_kernel, has_hist=has_hist, blocks_per_seq=blocks_per_seq),
        grid=(rows // tb,),
        in_specs=in_specs,
        out_specs=[row_spec, row_spec, row_spec, row_spec, pl.BlockSpec((tb, LANES), lambda i: (i, 0))],
        out_shape=[
            jax.ShapeDtypeStruct((rows, M_WIDTH), BF16),
            jax.ShapeDtypeStruct((rows, M_WIDTH), BF16),
            jax.ShapeDtypeStruct((rows, M_WIDTH), BF16),
            jax.ShapeDtypeStruct((rows, M_WIDTH), F32),
            jax.ShapeDtypeStruct((rows, LANES), F32),
        ],
        scratch_shapes=[pltpu.VMEM((SUBLANES, M_WIDTH), F32)],
        compiler_params=_params(("arbitrary",)),
        name="mlstm_proj",
    )(*args)


def _mlstm_finish(hh, o, bo, ng, zm):
    hm = _sigmoid(o + bo) * hh
    return _rms(hm, ng) * _silu(zm)


def _mlstm_prep(gates_ref):
    L = gates_ref.shape[0]
    r2 = lax.broadcasted_iota(jnp.int32, (L, L), 0)
    c2 = lax.broadcasted_iota(jnp.int32, (L, L), 1)
    causal = c2 <= r2
    g = gates_ref[...]
    b_all = _dot_mask(causal, g)
    return causal, g, b_all, g.T, b_all.T


def _mlstm_head(h, prep, q_ref, k_ref, v_ref, o_ref, zm_ref, bo_ref, ng_ref, c1_ref, n1_ref, m1_ref, emit):
    causal, g, b_all, g_t, b_t = prep
    L = q_ref.shape[0]
    neg_inf = -jnp.inf
    cols = slice(h * M_DH, (h + 1) * M_DH)
    f = M_HEADS + h
    g_row = g_t[h:h + 1, :] - b_t[f:f + 1, :]
    b_col = b_all[:, f:f + 1]
    g_col = g[:, h:h + 1] - b_col
    mprev = m1_ref[0, h:h + 1, 0:1]
    gm = jnp.where(causal, g_row, neg_inf)
    m_col = jnp.maximum(mprev, jnp.max(gm, axis=1, keepdims=True))
    w_intra = jnp.exp(gm - m_col)
    w_inter = jnp.exp(mprev - m_col)
    m_new = b_col + m_col
    m_last = m_col[L - 1:L, :]
    wl_col = jnp.exp(g_col - m_last)
    wli = jnp.exp(mprev - m_last)
    q = q_ref[:, cols]
    k = k_ref[:, cols]
    v = v_ref[:, cols]
    cst = c1_ref[0, h]
    nst = n1_ref[0, h]
    s = _dot_nt(q, k) * w_intra
    num = _dot(s.astype(BF16), v) + w_inter * _dot_nt(q, cst.astype(BF16))
    den = (jnp.sum(s, axis=1, keepdims=True)
           + w_inter * jnp.sum(q.astype(F32) * nst, axis=1, keepdims=True))
    hh = num / jnp.maximum(jnp.abs(den), jnp.exp(-m_new))
    emit(cols, _mlstm_finish(hh, o_ref[:, cols], bo_ref[:, cols], ng_ref[:, cols], zm_ref[:, cols]))

    def update_state():
        c1_ref[0, h] = wli * cst + _dot((v.astype(F32) * wl_col).T.astype(BF16), k)
        n1_ref[0, h] = wli * nst + jnp.sum(k.astype(F32) * wl_col, axis=0, keepdims=True)
        m1_ref[0, h:h + 1, :] = jnp.broadcast_to(m_new[L - 1:L, :], (1, LANES))

    return update_state


def _block_masks(L, rs, t_lo):
    shift = rs.bit_length() - 1
    r2 = lax.broadcasted_iota(jnp.int32, (L, L), 0)
    c2 = lax.broadcasted_iota(jnp.int32, (L, L), 1)
    same = (r2 >> shift) == (c2 >> shift)
    valid_c = (c2 & (rs - 1)) >= t_lo
    ridx = lax.broadcasted_iota(jnp.int32, (L, 1), 0)
    valid_r = (ridx & (rs - 1)) >= t_lo
    return r2, c2, same, valid_c, ridx, valid_r


def _mlstm_sample_kernel(q_ref, k_ref, v_ref, o_ref, zm_ref, gates_ref, bo_ref, ng_ref,
                         c0_ref, n0_ref, m0_ref, alias_ref,
                         out_ref, c1_ref, n1_ref, mrow_ref,
                         gt_sc, bt_sc, col_sc, q_sc, numi_sc, nrow_sc, wk_sc, *, nseq, t_lo):
    del alias_ref
    h = pl.program_id(1)
    L = q_ref.shape[0]
    rs = L // nseq
    r2, c2, same, valid_c, ridx, valid_r = _block_masks(L, rs, t_lo)
    eye = r2 == c2
    neg_inf = -jnp.inf
    g = gates_ref[...]
    b_all = _dot_mask(same & (c2 <= r2), jnp.where(valid_r, g, 0.0))
    gt_sc[...] = g.T
    bt_sc[...] = b_all.T
    lane_g = lax.broadcasted_iota(jnp.int32, (1, LANES), 1)
    i_col = jnp.sum(jnp.where(lane_g == h, g, 0.0), axis=1, keepdims=True)
    b_col = jnp.sum(jnp.where(lane_g == h + M_HEADS, b_all, 0.0), axis=1, keepdims=True)
    g_row = gt_sc[pl.ds(h, 1), :] - bt_sc[pl.ds(h + M_HEADS, 1), :]
    g_col = i_col - b_col
    mask = same & (c2 <= r2) & valid_c
    mprev = m0_ref[0]
    gm = jnp.where(mask, g_row, neg_inf)
    m_col = jnp.maximum(mprev, jnp.max(gm, axis=1, keepdims=True))
    m_row = jnp.sum(jnp.where(eye, m_col, 0.0), axis=0, keepdims=True)
    mlast_col = jnp.max(jnp.where(same, m_row, neg_inf), axis=1, keepdims=True)
    w_intra = jnp.exp(gm - m_col)
    w_inter = jnp.exp(mprev - m_col)
    m_new = b_col + m_col
    wl_col = jnp.where(valid_r, jnp.exp(g_col - mlast_col), 0.0)
    col_sc[:, 0:1] = jnp.exp(mprev - mlast_col)

    q = q_ref[...]
    k = k_ref[...]
    v = v_ref[...]
    s = _dot_nt(q, k) * w_intra
    num = _dot(s.astype(BF16), v)
    den = jnp.sum(s, axis=1, keepdims=True)
    wvt = (v.astype(F32) * wl_col).T.astype(BF16)
    wk_sc[...] = k.astype(F32) * wl_col
    qf = q.astype(F32)
    q_sc[0:L, :] = qf
    q_sc[L:L + SUBLANES, :] = jnp.zeros((SUBLANES, M_DH), F32)

    def seq_step(j, carry):
        rows = pl.ds(pl.multiple_of(j * rs, rs), rs)
        in_seq_r = (ridx >= j * rs) & (ridx < (j + 1) * rs)
        cj = c0_ref[j, 0]
        nj = n0_ref[j, 0]
        q2 = q_sc[pl.ds(pl.multiple_of(j * rs, rs), 2 * rs), :].astype(BF16)
        numi_sc[rows, :] = _dot_nt(q2, cj.astype(BF16))[:rs]
        nrow_sc[rows, :] = jnp.broadcast_to(nj, (rs, M_DH))
        wli = col_sc[pl.ds(j * rs + rs - 1, 1), 0:1]
        c1_ref[j, 0] = wli * cj + _dot(wvt, jnp.where(in_seq_r, k, jnp.zeros_like(k)))
        n1_ref[j, 0] = wli * nj + jnp.sum(wk_sc[rows, :], axis=0, keepdims=True)
        return carry

    lax.fori_loop(0, nseq, seq_step, 0, unroll=SEQ_UNROLL)
    num = num + w_inter * numi_sc[...]
    den = den + w_inter * jnp.sum(qf * nrow_sc[...], axis=1, keepdims=True)
    hh = num / jnp.maximum(jnp.abs(den), jnp.exp(-m_new))
    out_ref[...] = _mlstm_finish(hh, o_ref[...], bo_ref[...], ng_ref[...], zm_ref[...]).astype(BF16)
    mrow_ref[0] = m_new


def _mlstm_sample(l, depth, q, k, v, o, gates, proj, W, c0, n0, m0rows, c_prev, nseq, t_lo):
    L = CHUNK
    rows = q.shape[0]
    nb = rows // L
    nbatch = nb * nseq
    rowblk = lambda b, h: (b, h)
    c_spec = pl.BlockSpec((None, nseq, 1, M_DH, M_DH), lambda b, h: (l, b, h, 0, 0))
    n_spec_in = pl.BlockSpec((None, nseq, 1, 1, M_DH), lambda b, h: (l, b, h, 0, 0))
    n_spec_out = pl.BlockSpec((nseq, 1, 1, M_DH), lambda b, h: (b, h, 0, 0))
    has_prev = c_prev is not None
    in_specs = [
        pl.BlockSpec((L, M_DH), rowblk),
        pl.BlockSpec((L, M_DH), rowblk),
        pl.BlockSpec((L, M_DH), rowblk),
        pl.BlockSpec((L, M_DH), rowblk),
        pl.BlockSpec((L, M_DH), lambda b, h: (b, COL_ZM // M_DH + h)),
        pl.BlockSpec((L, LANES), lambda b, h: (b, 0)),
        _vec_spec(l, M_DH, lambda b, h: h),
        _vec_spec(l, M_DH, lambda b, h: h),
        c_spec,
        n_spec_in,
        pl.BlockSpec((None, 1, L, 1), lambda b, h: (l, h, b, 0)),
        pl.BlockSpec(memory_space=pl.ANY),
    ]
    args = [q, k, v, o, proj, gates, W["bo"], W["m_ng"], c0, n0, m0rows,
            c_prev if has_prev else jnp.zeros((SUBLANES, LANES), F32)]
    return pl.pallas_call(
        functools.partial(_mlstm_sample_kernel, nseq=nseq, t_lo=t_lo),
        grid=(nb, M_HEADS),
        in_specs=in_specs,
        out_specs=[pl.BlockSpec((L, M_DH), rowblk), c_spec, n_spec_out,
                   pl.BlockSpec((1, L, 1), lambda b, h: (h, b, 0))],
        out_shape=[
            jax.ShapeDtypeStruct((rows, M_WIDTH), BF16),
            jax.ShapeDtypeStruct((depth, nbatch, M_HEADS, M_DH, M_DH), F32),
            jax.ShapeDtypeStruct((nbatch, M_HEADS, 1, M_DH), F32),
            jax.ShapeDtypeStruct((M_HEADS, rows, 1), F32),
        ],
        scratch_shapes=[
            pltpu.VMEM((LANES, L), F32),
            pltpu.VMEM((LANES, L), F32),
            pltpu.VMEM((L, LANES), F32),
            pltpu.VMEM((L + SUBLANES, M_DH), F32),
            pltpu.VMEM((L, M_DH), F32),
            pltpu.VMEM((L, M_DH), F32),
            pltpu.VMEM((L, M_DH), F32),
        ],
        input_output_aliases={11: 1} if has_prev else {},
        compiler_params=_params(("arbitrary", "arbitrary")),
        name="mlstm_sample",
    )(*args)


def _ssd_prompt_kernel(xs_ref, b_ref, c_ref, zs_ref, h_ref, wdt_ref,
                       cwx_ref, cwb_ref, cwc_ref, cbx_ref, cbb_ref, cbc_ref,
                       dtb_ref, alog_ref, dskip_ref, ng_ref, expand_ref,
                       out_ref, h1_ref, xex_sc, xeb_sc, xec_sc, ht_sc):
    c = pl.program_id(1)
    nc = pl.num_programs(1)
    L = CHUNK

    @pl.when(c == 0)
    def _():
        ht_sc[...] = jnp.zeros(ht_sc.shape, F32)

    for sub in range(xs_ref.shape[0] // L):
        _ssd_prompt_chunk(slice(sub * L, (sub + 1) * L), (c == 0) if sub == 0 else False,
                          xs_ref, b_ref, c_ref, zs_ref, h_ref, wdt_ref,
                          cwx_ref, cwb_ref, cwc_ref, cbx_ref, cbb_ref, cbc_ref,
                          dtb_ref, alog_ref, dskip_ref, ng_ref, expand_ref,
                          out_ref, xex_sc, xeb_sc, xec_sc, ht_sc)

    @pl.when(c == nc - 1)
    def _():
        for pr in range(S_HEADS // 2):
            blk = ht_sc[:, pr * LANES:(pr + 1) * LANES].T
            h1_ref[0, 2 * pr] = blk[:S_DH]
            h1_ref[0, 2 * pr + 1] = blk[S_DH:]


def _ssd_prompt_chunk(rows, first, xs_ref, b_ref, c_ref, zs_ref, h_ref, wdt_ref,
                      cwx_ref, cwb_ref, cwc_ref, cbx_ref, cbb_ref, cbc_ref,
                      dtb_ref, alog_ref, dskip_ref, ng_ref, expand_ref,
                      out_ref, xex_sc, xeb_sc, xec_sc, ht_sc):
    L = CHUNK
    xs = _conv_rows(xs_ref[rows, :], first, xex_sc, cwx_ref, cbx_ref)
    bm = _conv_rows(b_ref[rows, :], first, xeb_sc, cwb_ref, cbb_ref)
    cm = _conv_rows(c_ref[rows, :], first, xec_sc, cwc_ref, cbc_ref).astype(BF16)

    r2 = lax.broadcasted_iota(jnp.int32, (L, L), 0)
    c2 = lax.broadcasted_iota(jnp.int32, (L, L), 1)
    causal = c2 <= r2
    neg_inf = -jnp.inf
    dt = _softplus(_dot(h_ref[rows, :], wdt_ref[...]) + dtb_ref[...])
    da = dt * (-jnp.exp(alog_ref[...]))
    cs = _dot_mask(causal, da)
    cs_t = cs.T
    ecs = jnp.exp(cs)
    wend = jnp.exp(cs[L - 1:L, :] - cs) * dt

    expand = expand_ref[...]
    dte = _widen(dt, expand)
    wende = _widen(wend, expand)
    ecse = _widen(ecs, expand)
    xdt = (xs * dte).astype(BF16)
    wx = (xs * wende).astype(BF16)
    lane = lax.broadcasted_iota(jnp.int32, (1, LANES), 1)
    low_half = lane < S_DH
    zero_slab = jnp.zeros((L, LANES), BF16)
    for g in range(S_GROUPS):
        gcols = slice(g * S_GW, (g + 1) * S_GW)
        scols = slice(g * S_STATE, (g + 1) * S_STATE)
        bg = bm[:, scols]
        cg = cm[:, scols]
        cb = _dot_nt(cg, bg.astype(BF16))
        pairs = []
        for pr in range(S_HPG // 2):
            h0 = g * S_HPG + 2 * pr
            mixes = []
            for hh in (h0, h0 + 1):
                dec = jnp.exp(jnp.where(causal, cs[:, hh:hh + 1] - cs_t[hh:hh + 1, :], neg_inf))
                mixes.append((cb * dec).astype(BF16))
            slab = xdt[:, h0 * S_DH:(h0 + 2) * S_DH]
            rhs = jnp.concatenate([jnp.where(low_half, slab, zero_slab),
                                   jnp.where(low_half, zero_slab, slab)], axis=0)
            pairs.append(_dot(jnp.concatenate(mixes, axis=1), rhs))
        y_intra = jnp.concatenate(pairs, axis=1)
        ht = ht_sc[:, gcols]
        y = y_intra + ecse[:, gcols] * _dot(cg, ht.astype(BF16)) + dskip_ref[:, gcols] * xs[:, gcols]
        y = y * _silu(zs_ref[rows, gcols])
        out_ref[rows, gcols] = _rms(y, ng_ref[:, gcols]).astype(BF16)
        ht_sc[:, gcols] = ecse[L - 1:L, gcols] * ht + _dot(bg.T.astype(BF16), wx[:, gcols])


def _ssd_prompt(l, proj, h, W, nb, nc):
    L = CHUNK * SSD_SUB
    rows = nb * nc * L
    row = lambda b, c: b * nc + c
    conv_w = lambda width, blk: pl.BlockSpec((None, CONV_K, width), lambda b, c: (l, 0, blk))
    return pl.pallas_call(
        _ssd_prompt_kernel,
        grid=(nb, nc),
        in_specs=[
            pl.BlockSpec((L, S_WIDTH), lambda b, c: (row(b, c), COL_XBC // S_WIDTH)),
            pl.BlockSpec((L, S_BC), lambda b, c: (row(b, c), COL_B // S_BC)),
            pl.BlockSpec((L, S_BC), lambda b, c: (row(b, c), COL_C // S_BC)),
            pl.BlockSpec((L, S_WIDTH), lambda b, c: (row(b, c), COL_ZS // S_WIDTH)),
            pl.BlockSpec((L, D_MODEL), lambda b, c: (row(b, c), 0)),
            pl.BlockSpec((None, D_MODEL, LANES), lambda b, c: (l, 0, 0)),
            conv_w(S_WIDTH, 0), conv_w(S_BC, S_WIDTH // S_BC), conv_w(S_BC, S_WIDTH // S_BC + 1),
            _vec_spec(l, S_WIDTH, lambda b, c: 0),
            _vec_spec(l, S_BC, lambda b, c: S_WIDTH // S_BC),
            _vec_spec(l, S_BC, lambda b, c: S_WIDTH // S_BC + 1),
            _vec_spec(l, LANES),
            _vec_spec(l, LANES),
            _vec_spec(l, S_WIDTH),
            _vec_spec(l, S_WIDTH),
            pl.BlockSpec((2 * LANES, S_WIDTH), lambda b, c: (0, 0)),
        ],
        out_specs=[
            pl.BlockSpec((L, S_WIDTH), lambda b, c: (row(b, c), 0)),
            pl.BlockSpec((1, S_HEADS, S_DH, S_STATE), lambda b, c: (b, 0, 0, 0)),
        ],
        out_shape=[
            jax.ShapeDtypeStruct((rows, S_WIDTH), BF16),
            jax.ShapeDtypeStruct((nb, S_HEADS, S_DH, S_STATE), F32),
        ],
        scratch_shapes=[
            pltpu.VMEM((SUBLANES, S_WIDTH), F32),
            pltpu.VMEM((SUBLANES, S_BC), F32),
            pltpu.VMEM((SUBLANES, S_BC), F32),
            pltpu.VMEM((S_STATE, S_WIDTH), F32),
        ],
        compiler_params=_params(("arbitrary", "arbitrary")),
        name="ssd_prompt",
    )(proj, proj, proj, proj, h, W["w_dt_c"], W["s_cw"], W["s_cw"], W["s_cw"], W["s_cb"], W["s_cb"], W["s_cb"],
      W["dtb_c"], W["alog_c"], W["dskip_wide"], W["s_ng"], W["expand"])


def _ssd_sample_kernel(xs_ref, b_ref, c_ref, zs_ref, h_ref, wdt_ref, hx_ref, hb_ref, hc_ref,
                       cwx_ref, cwb_ref, cwc_ref, cbx_ref, cbb_ref, cbc_ref,
                       dtb_ref, alog_ref, dskip_ref, ng_ref, expand_ref, h0_ref, alias_ref,
                       out_ref, h1_ref, xex_sc, xeb_sc, xec_sc, tot_sc, cm_sc, yi_sc, *, nseq, t_lo):
    del alias_ref
    L = xs_ref.shape[0]
    rs = L // nseq
    first = True
    xs = _conv_rows(xs_ref[...] + hx_ref[...], first, xex_sc, cwx_ref, cbx_ref)
    bm = _conv_rows(b_ref[...] + hb_ref[...], first, xeb_sc, cwb_ref, cbb_ref)
    cm = _conv_rows(c_ref[...] + hc_ref[...], first, xec_sc, cwc_ref, cbc_ref)
    bmb = bm.astype(BF16)
    cmb = cm.astype(BF16)
    cm_sc[0:L, :] = cm
    cm_sc[L:L + SUBLANES, :] = jnp.zeros((SUBLANES, S_STATE), F32)

    r2, c2, same, valid_c, ridx, valid_r = _block_masks(L, rs, t_lo)
    mask = same & (c2 <= r2) & valid_c
    neg_inf = -jnp.inf
    dt = jnp.where(valid_r, _softplus(_dot(h_ref[...], wdt_ref[...]) + dtb_ref[...]), 0.0)
    da = dt * (-jnp.exp(alog_ref[...]))
    cs = _dot_mask(mask, da)
    sfx = _dot_mask(same & (c2 > r2), da)
    tot_sc[...] = cs + sfx
    cs_t = cs.T
    expand = expand_ref[...]
    dte = _widen(dt, expand)
    wende = _widen(jnp.exp(sfx) * dt, expand)
    ecse = _widen(jnp.exp(cs), expand)
    xdt = (xs * dte).astype(BF16)
    wxt = (xs * wende).T.astype(BF16)
    cb = _dot_nt(cmb, bmb)
    lane = lax.broadcasted_iota(jnp.int32, (1, LANES), 1)
    low_half = lane < S_DH
    zero_slab = jnp.zeros((L, LANES), BF16)
    pairs = []
    for pr in range(S_HPG // 2):
        mixes = []
        for hh in (2 * pr, 2 * pr + 1):
            dec = jnp.exp(jnp.where(mask, cs[:, hh:hh + 1] - cs_t[hh:hh + 1, :], neg_inf))
            mixes.append((cb * dec).astype(BF16))
        slab = xdt[:, pr * LANES:(pr + 1) * LANES]
        rhs = jnp.concatenate([jnp.where(low_half, slab, zero_slab),
                               jnp.where(low_half, zero_slab, slab)], axis=0)
        pairs.append(_dot(jnp.concatenate(mixes, axis=1), rhs))
    y_intra = jnp.concatenate(pairs, axis=1)

    def seq_step(j, carry):
        rows = pl.ds(pl.multiple_of(j * rs, rs), rs)
        in_seq_r = (ridx >= j * rs) & (ridx < (j + 1) * rs)
        hj = h0_ref[j]
        c2rows = cm_sc[pl.ds(pl.multiple_of(j * rs, rs), 2 * rs), :].astype(BF16)
        yi_sc[rows, :] = _dot_nt(c2rows, hj.reshape(S_GW, S_STATE).astype(BF16))[:rs]
        upd = _dot(wxt, jnp.where(in_seq_r, bmb, jnp.zeros_like(bmb)))
        dec_j = jnp.exp(tot_sc[pl.ds(j * rs + rs - 1, 1), :])
        for hh in range(S_HPG):
            h1_ref[j, hh] = dec_j[:, hh:hh + 1] * hj[hh] + upd[hh * S_DH:(hh + 1) * S_DH, :]
        return carry

    lax.fori_loop(0, nseq, seq_step, 0, unroll=SEQ_UNROLL)
    y = y_intra + ecse * yi_sc[...] + dskip_ref[...] * xs
    out_ref[...] = _rms(y * _silu(zs_ref[...]), ng_ref[...]).astype(BF16)


def _ssd_sample(l, depth, proj, h, hist, W, h0, h_prev, nseq, t_lo):
    L = CHUNK
    rows = proj.shape[0]
    nb = rows // L
    xblk = COL_XBC // S_GW
    bblk = COL_B // S_STATE
    cblk = COL_C // S_STATE
    hb_blk = S_WIDTH // S_STATE
    hc_blk = hb_blk + S_GROUPS
    has_prev = h_prev is not None
    conv_w = lambda width, blk: pl.BlockSpec((None, CONV_K, width), lambda b, g: (l, 0, blk(g)))
    h_spec = pl.BlockSpec((None, nseq, S_HPG, S_DH, S_STATE), lambda b, g: (l, b, g, 0, 0))
    in_specs = [
        pl.BlockSpec((L, S_GW), lambda b, g: (b, xblk + g)),
        pl.BlockSpec((L, S_STATE), lambda b, g: (b, bblk + g)),
        pl.BlockSpec((L, S_STATE), lambda b, g: (b, cblk + g)),
        pl.BlockSpec((L, S_GW), lambda b, g: (b, COL_ZS // S_GW + g)),
        pl.BlockSpec((L, D_MODEL), lambda b, g: (b, 0)),
        pl.BlockSpec((None, D_MODEL, LANES), lambda b, g: (l, 0, g)),
        pl.BlockSpec((None, L, S_GW), lambda b, g: (l, b, g)),
        pl.BlockSpec((None, L, S_STATE), lambda b, g: (l, b, hb_blk + g)),
        pl.BlockSpec((None, L, S_STATE), lambda b, g: (l, b, hc_blk + g)),
        conv_w(S_GW, lambda g: g), conv_w(S_STATE, lambda g: hb_blk + g), conv_w(S_STATE, lambda g: hc_blk + g),
        _vec_spec(l, S_GW, lambda b, g: g),
        _vec_spec(l, S_STATE, lambda b, g: hb_blk + g),
        _vec_spec(l, S_STATE, lambda b, g: hc_blk + g),
        _vec_spec(l, LANES, lambda b, g: g),
        _vec_spec(l, LANES, lambda b, g: g),
        _vec_spec(l, S_GW, lambda b, g: g),
        _vec_spec(l, S_GW, lambda b, g: g),
        pl.BlockSpec((2 * LANES, S_GW), lambda b, g: (0, 0)),
        h_spec,
        pl.BlockSpec(memory_space=pl.ANY),
    ]
    args = [proj, proj, proj, proj, h, W["w_dt_g"], hist, hist, hist,
            W["s_cw"], W["s_cw"], W["s_cw"], W["s_cb"], W["s_cb"], W["s_cb"],
            W["dtb_g"], W["alog_g"], W["dskip_wide"], W["s_ng"], W["expand"], h0,
            h_prev if has_prev else jnp.zeros((SUBLANES, LANES), F32)]
    return pl.pallas_call(
        functools.partial(_ssd_sample_kernel, nseq=nseq, t_lo=t_lo),
        grid=(nb, S_GROUPS),
        in_specs=in_specs,
        out_specs=[pl.BlockSpec((L, S_GW), lambda b, g: (b, g)), h_spec],
        out_shape=[
            jax.ShapeDtypeStruct((rows, S_WIDTH), BF16),
            jax.ShapeDtypeStruct((depth, nb * nseq, S_HEADS, S_DH, S_STATE), F32),
        ],
        scratch_shapes=[
            pltpu.VMEM((SUBLANES, S_GW), F32),
            pltpu.VMEM((SUBLANES, S_STATE), F32),
            pltpu.VMEM((SUBLANES, S_STATE), F32),
            pltpu.VMEM((L, LANES), F32),
            pltpu.VMEM((L + SUBLANES, S_STATE), F32),
            pltpu.VMEM((L, S_GW), F32),
        ],
        input_output_aliases={21: 1} if has_prev else {},
        compiler_params=_params(("arbitrary", "arbitrary")),
        name="ssd_sample",
    )(*args)


def _outproj_body(x_ref, om, os_ref, u_ref, v_ref, zc_ref, wmix_ref, bias_ref, gv_ref,
                  w0_ref, w1_ref, w2_ref, w3_ref, g_ref, y_ref, h_ref, vn_ref, oc_sc, *, final, rs, t_lo,
                  between=None):
    emit_vn = vn_ref is not None
    if between is None:
        between = lambda k: None
    L = C_CHUNK
    r2, c2, same, valid_c, _, _ = _block_masks(L, min(rs, L), t_lo)
    mask = same & (c2 <= r2) & valid_c
    for g in range(C_GROUPS):
        cols = slice(g * C_DG, (g + 1) * C_DG)
        w = jnp.where(mask, wmix_ref[g], 0.0).astype(BF16)
        for ch in range(x_ref.shape[0] // L):
            rows = slice(ch * L, (ch + 1) * L)
            vn = _rms(v_ref[rows, cols], gv_ref[:, cols])
            if emit_vn:
                vn_ref[rows, cols] = vn
            mixed = _dot(w, vn.astype(BF16)) + bias_ref[g]
            oc_sc[rows, cols] = (u_ref[rows, cols] * mixed * _silu(zc_ref[rows, cols])).astype(BF16)
    kb = w0_ref.shape[0]
    hk = kb // 2
    lhs = (lambda c: om[:, c],
           lambda c: os_ref[:, c].astype(BF16),
           lambda c: os_ref[:, slice(kb + c.start, kb + c.stop)].astype(BF16),
           lambda c: oc_sc[:, c])
    acc = None
    for blk, w_ref in enumerate((w0_ref, w1_ref, w2_ref, w3_ref)):
        for part in range(2):
            c = slice(part * hk, (part + 1) * hk)
            d = _dot(lhs[blk](c), w_ref[c, :])
            acc = d if acc is None else acc + d
            between(2 * blk + part)
    y = x_ref[...] + acc
    if t_lo:
        ridx = lax.broadcasted_iota(jnp.int32, (y.shape[0], 1), 0)
        y = jnp.where((ridx & (rs - 1)) >= t_lo, y, 0.0)
    if final:
        y_ref[...] = _rms(y, g_ref[...])
    else:
        y_ref[...] = y
        h_ref[...] = _rms(y, g_ref[...]).astype(BF16)


def _outproj_kernel(*refs, final, emit_vn, rs, t_lo):
    (x_ref, om_ref, os_ref, u_ref, v_ref, zc_ref, wmix_ref, bias_ref, gv_ref,
     w0_ref, w1_ref, w2_ref, w3_ref, g_ref, y_ref) = refs[:15]
    rest = list(refs[15:])
    h_ref = None if final else rest.pop(0)
    vn_ref = rest.pop(0) if emit_vn else None
    (oc_sc,) = rest
    _outproj_body(x_ref, om_ref[...].astype(BF16), os_ref, u_ref, v_ref, zc_ref, wmix_ref, bias_ref, gv_ref,
                  w0_ref, w1_ref, w2_ref, w3_ref, g_ref, y_ref, h_ref, vn_ref, oc_sc,
                  final=final, rs=rs, t_lo=t_lo)


def _mlstm_out_kernel(*refs, final, nc):
    (q_ref, k_ref, v_ref, o_ref, zm_ref, gates_ref, bo_ref, ng_ref,
     x_ref, os_ref, u_ref, v2_ref, zc_ref, wmix_ref, bias_ref, gv_ref,
     w0_ref, w1_ref, w2_ref, w3_ref, g_ref, c1_ref, n1_ref, m1_ref, y_ref) = refs[:25]
    rest = list(refs[25:])
    h_ref = None if final else rest.pop(0)
    om_sc, oc_sc = rest
    i = pl.program_id(0)

    @pl.when(i == 0)
    def _():
        om_sc[...] = jnp.zeros(om_sc.shape, BF16)

    @pl.when(i % nc == 0)
    def _():
        c1_ref[...] = jnp.zeros(c1_ref.shape, F32)
        n1_ref[...] = jnp.zeros(n1_ref.shape, F32)
        m1_ref[...] = jnp.zeros(m1_ref.shape, F32)

    def emit(cols, val):
        om_sc[i % 2, :, cols] = val.astype(BF16)

    prep = _mlstm_prep(gates_ref)

    pending = {}

    def head(slot):
        h, second = divmod(slot, 2)
        if second:
            pending.pop(h)()
        else:
            pending[h] = _mlstm_head(h, prep, q_ref, k_ref, v_ref, o_ref, zm_ref, bo_ref, ng_ref,
                                     c1_ref, n1_ref, m1_ref, emit)

    _outproj_body(x_ref, om_sc[(i + 1) % 2], os_ref, u_ref, v2_ref, zc_ref, wmix_ref, bias_ref, gv_ref,
                  w0_ref, w1_ref, w2_ref, w3_ref, g_ref, y_ref, h_ref, None, oc_sc,
                  final=final, rs=CHUNK, t_lo=0, between=head)


def _mlstm_out(l, q, k, v, o, gates, proj, x, os_, wmix, bias, W, g, final, nb, nc):
    L = M_CHUNK
    nt = nb * nc
    rows = nt * L
    cur = lambda i: jnp.minimum(i, nt - 1)
    prev = lambda i: jnp.maximum(i - 1, 0)
    cur_spec = pl.BlockSpec((L, M_WIDTH), lambda i: (cur(i), 0))
    prev_spec = pl.BlockSpec((L, D_MODEL), lambda i: (prev(i), 0))
    wspec = lambda r: pl.BlockSpec((None, M_WIDTH, D_MODEL), lambda i: (l, r, 0), pipeline_mode=pl.Buffered(1))
    out_shape = [
        jax.ShapeDtypeStruct((nb + 1, M_HEADS, M_DH, M_DH), F32),
        jax.ShapeDtypeStruct((nb + 1, M_HEADS, 1, M_DH), F32),
        jax.ShapeDtypeStruct((nb + 1, SUBLANES, LANES), F32),
        jax.ShapeDtypeStruct((rows, D_MODEL), F32),
    ]
    out_specs = [
        pl.BlockSpec((1, M_HEADS, M_DH, M_DH), lambda i: (i // nc, 0, 0, 0)),
        pl.BlockSpec((1, M_HEADS, 1, M_DH), lambda i: (i // nc, 0, 0, 0)),
        pl.BlockSpec((1, SUBLANES, LANES), lambda i: (i // nc, 0, 0)),
        prev_spec,
    ]
    if not final:
        out_shape.append(jax.ShapeDtypeStruct((rows, D_MODEL), BF16))
        out_specs.append(prev_spec)
    return pl.pallas_call(
        functools.partial(_mlstm_out_kernel, final=final, nc=nc),
        grid=(nt + 1,),
        in_specs=[
            cur_spec, cur_spec, cur_spec, cur_spec,
            pl.BlockSpec((L, M_WIDTH), lambda i: (cur(i), COL_ZM // M_WIDTH)),
            pl.BlockSpec((L, LANES), lambda i: (cur(i), 0)),
            _vec_spec(l, M_WIDTH),
            _vec_spec(l, M_WIDTH),
            prev_spec,
            prev_spec,
            pl.BlockSpec((L, C_WIDTH), lambda i: (prev(i), COL_U // C_WIDTH)),
            pl.BlockSpec((L, C_WIDTH), lambda i: (prev(i), COL_V // C_WIDTH)),
            pl.BlockSpec((L, C_WIDTH), lambda i: (prev(i), COL_ZC // C_WIDTH)),
            pl.BlockSpec((None, C_GROUPS, C_CHUNK, C_CHUNK), lambda i: (l, 0, 0, 0)),
            pl.BlockSpec((None, C_GROUPS, C_CHUNK, 1), lambda i: (l, 0, 0, 0)),
            _vec_spec(l, C_WIDTH),
            wspec(0), wspec(1), wspec(2), wspec(3),
            pl.BlockSpec((1, D_MODEL), lambda i: (0, 0)) if final else _vec_spec(l + 1, D_MODEL),
        ],
        out_specs=out_specs,
        out_shape=out_shape,
        scratch_shapes=[pltpu.VMEM((2, L, M_WIDTH), BF16), pltpu.VMEM((L, C_WIDTH), BF16)],
        compiler_params=_params(("arbitrary",)),
        name="mlstm_out",
    )(q, k, v, o, proj, gates, W["bo"], W["m_ng"], x, os_, proj, proj, proj, wmix, bias, W["gv"],
      W["wo"], W["wo"], W["wo"], W["wo"], g)


def _outproj(l, x, om, os_, proj, wmix, bias, gv, wo, g, final, emit_vn, rs, t_lo):
    rows = x.shape[0]
    tm = min(256, rows)
    kb = M_WIDTH
    wspec = lambda r: pl.BlockSpec((None, kb, D_MODEL), lambda i: (l, r, 0), pipeline_mode=pl.Buffered(1))
    row_spec = pl.BlockSpec((tm, D_MODEL), lambda i: (i, 0))
    out_shape = [jax.ShapeDtypeStruct((rows, D_MODEL), F32)]
    out_specs = [row_spec]
    if not final:
        out_shape.append(jax.ShapeDtypeStruct((rows, D_MODEL), BF16))
        out_specs.append(row_spec)
    if emit_vn:
        out_shape.append(jax.ShapeDtypeStruct((rows, C_WIDTH), F32))
        out_specs.append(pl.BlockSpec((tm, C_WIDTH), lambda i: (i, 0)))
    return pl.pallas_call(
        functools.partial(_outproj_kernel, final=final, emit_vn=emit_vn, rs=rs, t_lo=t_lo),
        grid=(rows // tm,),
        in_specs=[
            row_spec,
            pl.BlockSpec((tm, M_WIDTH), lambda i: (i, 0)),
            pl.BlockSpec((tm, S_WIDTH), lambda i: (i, 0)),
            pl.BlockSpec((tm, C_WIDTH), lambda i: (i, COL_U // C_WIDTH)),
            pl.BlockSpec((tm, C_WIDTH), lambda i: (i, COL_V // C_WIDTH)),
            pl.BlockSpec((tm, C_WIDTH), lambda i: (i, COL_ZC // C_WIDTH)),
            pl.BlockSpec((None, C_GROUPS, C_CHUNK, C_CHUNK), lambda i: (l, 0, 0, 0)),
            pl.BlockSpec((None, C_GROUPS, C_CHUNK, 1), lambda i: (l, 0, 0, 0)),
            _vec_spec(l, C_WIDTH),
            wspec(0), wspec(1), wspec(2), wspec(3),
            pl.BlockSpec((1, D_MODEL), lambda i: (0, 0)) if final else _vec_spec(l + 1, D_MODEL),
        ],
        out_specs=out_specs,
        out_shape=out_shape,
        scratch_shapes=[pltpu.VMEM((tm, C_WIDTH), BF16)],
        compiler_params=_params(("arbitrary",)),
        name="outproj",
    )(x, om, os_, proj, proj, proj, wmix, bias, gv, wo, wo, wo, wo, g)


def _heads_compact(a):
    return jnp.pad(a, ((0, 0), (0, LANES - S_HEADS)))[:, None, :]


def _heads_grouped(a):
    a = a.reshape(a.shape[0], S_GROUPS, S_HPG)
    return jnp.pad(a, ((0, 0), (0, 0), (0, LANES - S_HPG))).reshape(a.shape[0], 1, S_GROUPS * LANES)


def _prepare_weights(norm_g, w_in, m_conv_w, m_conv_b, m_w_qk, m_w_vo, m_b_o, m_w_gate, m_b_gate,
                     m_norm_g, s_conv_w, s_conv_b, s_dt_bias, s_A_log, s_D, s_norm_g, c_v_norm_g,
                     c_w_s, c_b_s, w_out):
    depth = w_in.shape[0]
    vec = lambda a: a.reshape(depth, 1, -1)
    dt0 = PROJ_HEAD
    w_all = w_in.astype(BF16)
    w_dt = w_in[:, :, dt0:dt0 + S_HEADS]
    w_dt_g = jnp.pad(w_dt.reshape(depth, D_MODEL, S_GROUPS, S_HPG),
                     ((0, 0), (0, 0), (0, 0), (0, LANES - S_HPG))).reshape(depth, D_MODEL, -1).astype(BF16)
    w_dt_c = jnp.pad(w_dt, ((0, 0), (0, 0), (0, LANES - S_HEADS))).astype(BF16)
    wg = m_w_gate.reshape(depth, M_HEADS, 3, M_DH, 2 * M_HEADS)
    wg = jnp.pad(wg, ((0, 0),) * 4 + ((0, LANES - 2 * M_HEADS),)).astype(BF16)
    head_of_lane = jnp.arange(S_WIDTH) // S_DH
    expand = (jnp.arange(LANES)[:, None] == head_of_lane[None, :]).astype(BF16)
    expand = jnp.concatenate([expand, expand], axis=0)
    return dict(
        norm_g=vec(norm_g), w_all=w_all, w_tail=w_all[:, :, dt0 + S_HEADS:], w_dt_c=w_dt_c, w_dt_g=w_dt_g,
        m_cw=m_conv_w, m_cb=vec(m_conv_b),
        wqk=m_w_qk.astype(BF16), wvo=m_w_vo.astype(BF16), wg=wg,
        bg=vec(jnp.pad(m_b_gate, ((0, 0), (0, LANES - 2 * M_HEADS)))),
        bo=vec(m_b_o), m_ng=vec(m_norm_g),
        s_cw=s_conv_w, s_cb=vec(s_conv_b),
        dtb_c=_heads_compact(s_dt_bias), alog_c=_heads_compact(s_A_log),
        dtb_g=_heads_grouped(s_dt_bias), alog_g=_heads_grouped(s_A_log),
        dskip_wide=vec(jnp.repeat(s_D, S_DH, axis=1)), s_ng=vec(s_norm_g), expand=expand,
        gv=vec(c_v_norm_g), wo=w_out.astype(BF16),
    )


def kernel(x_prompt, x_sample, state_mlstm_C, state_mlstm_n, state_mlstm_m, state_mlstm_conv, state_ssm, state_ssm_conv, norm_g, w_in, m_conv_w, m_conv_b, m_w_qk, m_w_vo, m_b_o, m_w_gate, m_b_gate, m_norm_g, s_conv_w, s_conv_b, s_dt_bias, s_A_log, s_D, s_norm_g, c_v_norm_g, c_w_s, c_b_s, w_out, final_norm_g):
    bp, seq, _ = x_prompt.shape
    bs, dec_seq, _ = x_sample.shape
    depth = w_in.shape[0]
    t_lo = SAMPLE_ROWS - dec_seq
    hist_lo = t_lo - (CONV_K - 1)
    nseq = CHUNK // SAMPLE_ROWS
    W = _prepare_weights(norm_g, w_in, m_conv_w, m_conv_b, m_w_qk, m_w_vo, m_b_o, m_w_gate, m_b_gate,
                         m_norm_g, s_conv_w, s_conv_b, s_dt_bias, s_A_log, s_D, s_norm_g, c_v_norm_g,
                         c_w_s, c_b_s, w_out)
    fg = final_norm_g[None, :]
    yp = x_prompt.reshape(bp * seq, D_MODEL)
    ys = jnp.pad(x_sample, ((0, 0), (t_lo, 0), (0, 0))).reshape(bs * SAMPLE_ROWS, D_MODEL)

    pad_hist = ((0, 0), (0, 0), (hist_lo, dec_seq), (0, 0))
    hist_m = jnp.pad(state_mlstm_conv, pad_hist).reshape(depth, bs * SAMPLE_ROWS, M_WIDTH)
    hist_s = jnp.pad(state_ssm_conv, pad_hist).reshape(depth, bs * SAMPLE_ROWS, -1)
    m0rows = jnp.repeat(jnp.swapaxes(state_mlstm_m, 1, 2)[..., None], SAMPLE_ROWS, axis=2)
    n0 = state_mlstm_n[:, :, :, None, :]
    reps = C_CHUNK // SAMPLE_ROWS
    w4 = jnp.pad(c_w_s[:, :, :dec_seq, :dec_seq], ((0, 0), (0, 0), (t_lo, 0), (t_lo, 0)))
    b4 = jnp.pad(c_b_s[:, :, :dec_seq], ((0, 0), (0, 0), (t_lo, 0)))
    wmix_s = jnp.tile(w4, (1, 1, reps, reps))
    bias_s = jnp.tile(b4, (1, 1, reps))[..., None]
    bias_p = c_b_s[..., None]

    hp = _norm(0, yp, W["norm_g"])
    hs = _norm(0, ys, W["norm_g"])
    outs_p, outs_s = [], []
    c_all = h_all = None
    mc = min(M_CHUNK, seq)
    sc = min(CHUNK * SSD_SUB, seq)
    tbp = min(512, bp * seq)
    tbs = min(512, bs * SAMPLE_ROWS)
    for l in range(depth):
        final = l == depth - 1
        proj = _inproj(l, hp, W["w_all"], W["w_tail"])
        q, k, v, o, gates = _mlstm_proj(l, proj, None, W, max(1, seq // tbp), tbp)
        out_s, h1 = _ssd_prompt(l, proj, hp, W, bp, seq // sc)
        res = _mlstm_out(l, q, k, v, o, gates, proj, yp, out_s, c_w_s, bias_p, W,
                         fg if final else W["norm_g"], final, bp, seq // mc)
        c1, n1, m1 = res[0][:bp], res[1][:bp], res[2][:bp]
        yp, hp = res[3], (None if final else res[4])
        pj = proj.reshape(bp, seq, PROJ_MAIN)
        outs_p.append((
            c1, n1.reshape(bp, M_HEADS, M_DH), m1[:, :M_HEADS, 0],
            pj[:, seq - (CONV_K - 1):, COL_XM:COL_XM + M_WIDTH], h1,
            pj[:, seq - (CONV_K - 1):, COL_XBC:COL_XBC + S_WIDTH + 2 * S_BC]))
        proj = _inproj(l, hs, W["w_all"], W["w_tail"])
        q, k, v, o, gates = _mlstm_proj(l, proj, hist_m, W, 1, tbs)
        out_m, c_all, n1, mrow = _mlstm_sample(l, depth, q, k, v, o, gates, proj, W, state_mlstm_C, n0,
                                               m0rows, c_all, nseq, t_lo)
        out_s, h_all = _ssd_sample(l, depth, proj, hs, hist_s, W, state_ssm, h_all, nseq, t_lo)
        res = _outproj(l, ys, out_m, out_s, proj, wmix_s, bias_s, W["gv"], W["wo"],
                       fg if final else W["norm_g"], final, True, SAMPLE_ROWS, t_lo)
        ys, hs, vn = res[0], (None if final else res[1]), res[-1]
        pj = proj.reshape(bs, SAMPLE_ROWS, PROJ_MAIN)
        outs_s.append((
            n1.reshape(bs, M_HEADS, M_DH), mrow[:, SAMPLE_ROWS - 1::SAMPLE_ROWS, 0].T,
            pj[:, SAMPLE_ROWS - (CONV_K - 1):, COL_XM:COL_XM + M_WIDTH],
            pj[:, SAMPLE_ROWS - (CONV_K - 1):, COL_XBC:COL_XBC + S_WIDTH + 2 * S_BC],
            vn.reshape(bs, SAMPLE_ROWS, C_WIDTH)[:, t_lo:]))
    p_out = [jnp.stack([s[i] for s in outs_p]) for i in range(6)]
    s_n, s_m, s_mconv, s_sconv, s_cv = [jnp.stack([s[i] for s in outs_s]) for i in range(5)]
    y_prompt = yp.reshape(bp, seq, D_MODEL)
    y_sample = ys.reshape(bs, SAMPLE_ROWS, D_MODEL)[:, t_lo:]
    return (y_prompt, y_sample, *p_out, c_all, s_n, s_m, s_mconv, h_all, s_sconv, s_cv)
```

```python
import functools

import jax
import jax.numpy as jnp
from jax import lax
from jax.experimental import pallas as pl
from jax.experimental.pallas import tpu as pltpu

F32 = jnp.float32
BF16 = jnp.bfloat16

D_MODEL = 2048
MIX_WIDTH = 2 * D_MODEL
M_WIDTH = MIX_WIDTH // 4
M_HEADS = 4
M_DH = M_WIDTH // M_HEADS
S_WIDTH = MIX_WIDTH // 2
S_DH = 64
S_HEADS = S_WIDTH // S_DH
S_GROUPS = 4
S_HPG = S_HEADS // S_GROUPS
S_STATE = 128
S_GW = S_HPG * S_DH
S_BC = S_GROUPS * S_STATE
C_WIDTH = MIX_WIDTH // 4
C_GROUPS = 4
C_DG = C_WIDTH // C_GROUPS
C_CHUNK = 128
CONV_K = 4
EPS = 1e-6

LANES = 128
SUBLANES = 8
SAMPLE_ROWS = 8
CHUNK = 128
M_CHUNK = 256
SSD_SUB = 4
SEQ_UNROLL = 16
VMEM_LIMIT = 56 * 1024 * 1024

COL_XM, COL_ZM, COL_ZS, COL_XBC, COL_U, COL_V, COL_ZC = 0, 1024, 2048, 4096, 7168, 8192, 9216
PROJ_MAIN = 10240
PROJ_HEAD = COL_U
COL_B = COL_XBC + S_WIDTH
COL_C = COL_B + S_BC


def _dot(a, b):
    return jnp.dot(a, b, preferred_element_type=F32)


def _dot_nt(a, b):
    return lax.dot_general(a, b, (((1,), (1,)), ((), ())), preferred_element_type=F32)


def _dot_mask(mask, x):
    m = jnp.where(mask, 1.0, 0.0).astype(BF16)
    hi = x.astype(BF16)
    r1 = x - hi.astype(F32)
    mid = r1.astype(BF16)
    lo = (r1 - mid.astype(F32)).astype(BF16)
    return _dot(m, hi) + _dot(m, mid) + _dot(m, lo)


def _widen(a, expand2):
    hi = a.astype(BF16)
    lo = (a - hi.astype(F32)).astype(BF16)
    return _dot(jnp.concatenate([hi, lo], axis=1), expand2)


def _sigmoid(x):
    return 0.5 * jnp.tanh(0.5 * x) + 0.5


def _silu(x):
    h = 0.5 * x
    return h * jnp.tanh(h) + h


def _softplus(x):
    return jnp.maximum(x, 0.0) + jnp.log1p(jnp.exp(-jnp.abs(x)))


def _log_sigmoid(x):
    return jnp.minimum(x, 0.0) - jnp.log1p(jnp.exp(-jnp.abs(x)))


def _rms(x, g):
    return x * lax.rsqrt(jnp.mean(x * x, axis=-1, keepdims=True) + EPS) * g


def _params(sem):
    return pltpu.CompilerParams(dimension_semantics=sem, vmem_limit_bytes=VMEM_LIMIT)


def _vec_spec(l, n, col=None):
    if col is None:
        return pl.BlockSpec((None, 1, n), lambda *ids: (l, 0, 0))
    return pl.BlockSpec((None, 1, n), lambda *ids: (l, 0, col(*ids)))


def _norm_kernel(x_ref, g_ref, h_ref):
    h_ref[...] = _rms(x_ref[...], g_ref[...]).astype(BF16)


def _norm(l, x, g):
    rows = x.shape[0]
    tb = min(1024, rows)
    return pl.pallas_call(
        _norm_kernel,
        grid=(rows // tb,),
        in_specs=[pl.BlockSpec((tb, D_MODEL), lambda i: (i, 0)), _vec_spec(l, D_MODEL)],
        out_specs=pl.BlockSpec((tb, D_MODEL), lambda i: (i, 0)),
        out_shape=jax.ShapeDtypeStruct((rows, D_MODEL), BF16),
        compiler_params=_params(("arbitrary",)),
        name="norm",
    )(x, g)


def _inproj_kernel(h_ref, wa_ref, wb_ref, proj_ref, *, n_head):
    j = pl.program_id(0)

    @pl.when(j < n_head)
    def _():
        proj_ref[...] = _dot(h_ref[...], wa_ref[...])

    @pl.when(j >= n_head)
    def _():
        proj_ref[...] = _dot(h_ref[...], wb_ref[...])


def _inproj(l, h, w_all, w_tail):
    rows = h.shape[0]
    tm = min(1024, rows)
    tn = 1024
    n_head = PROJ_HEAD // tn
    return pl.pallas_call(
        functools.partial(_inproj_kernel, n_head=n_head),
        grid=(PROJ_MAIN // tn, rows // tm),
        in_specs=[
            pl.BlockSpec((tm, D_MODEL), lambda j, i: (i, 0)),
            pl.BlockSpec((None, D_MODEL, tn), lambda j, i: (l, 0, jnp.minimum(j, n_head - 1))),
            pl.BlockSpec((None, D_MODEL, tn), lambda j, i: (l, 0, jnp.maximum(j - n_head, 0))),
        ],
        out_specs=pl.BlockSpec((tm, tn), lambda j, i: (i, j)),
        out_shape=jax.ShapeDtypeStruct((rows, PROJ_MAIN), F32),
        compiler_params=_params(("arbitrary", "arbitrary")),
        name="inproj",
    )(h, w_all, w_tail)


def _conv_rows(x, first, prev_sc, cw_ref, cb_ref):
    assert CONV_K == 4
    tb = x.shape[0]

    if first is not False:
        @pl.when(first)
        def _():
            prev_sc[...] = jnp.zeros(prev_sc.shape, F32)

    xe = jnp.concatenate([prev_sc[...], x], axis=0)
    x1 = pltpu.roll(xe, 1, 0)
    pair = cw_ref[1:2, :] * xe + cw_ref[0:1, :] * x1
    acc = (cb_ref[...] + cw_ref[3:4, :] * x + cw_ref[2:3, :] * x1[SUBLANES:, :]
           + pltpu.roll(pair, 2, 0)[SUBLANES:, :])
    prev_sc[...] = x[tb - SUBLANES:, :]
    return _silu(acc)


def _mlstm_proj_kernel(*refs, has_hist, blocks_per_seq):
    if has_hist:
        (xm_ref, he_ref, cw_ref, cb_ref, wqk_ref, wvo_ref, wg_ref, bg_ref,
         q_ref, k_ref, v_ref, o_ref, gates_ref, xe_sc) = refs
    else:
        (xm_ref, cw_ref, cb_ref, wqk_ref, wvo_ref, wg_ref, bg_ref,
         q_ref, k_ref, v_ref, o_ref, gates_ref, xe_sc) = refs
    i = pl.program_id(0)
    x = xm_ref[...]
    xin = x + he_ref[...] if has_hist else x
    xmc = _conv_rows(xin, i % blocks_per_seq == 0, xe_sc, cw_ref, cb_ref)
    tb = x.shape[0]
    gates = jnp.broadcast_to(bg_ref[...], (tb, LANES))
    for h in range(M_HEADS):
        cols = slice(h * M_DH, (h + 1) * M_DH)
        qk = _dot(xmc[:, cols].astype(BF16), wqk_ref[h])
        vo = _dot(x[:, cols].astype(BF16), wvo_ref[h])
        qb = qk[:, :M_DH].astype(BF16)
        kb = qk[:, M_DH:].astype(BF16)
        vb = vo[:, :M_DH].astype(BF16)
        gates = gates + _dot(qb, wg_ref[h, 0]) + _dot(kb, wg_ref[h, 1]) + _dot(vb, wg_ref[h, 2])
        q_ref[:, cols] = qb
        k_ref[:, cols] = (qk[:, M_DH:] * (M_DH ** -0.5)).astype(BF16)
        v_ref[:, cols] = vb
        o_ref[:, cols] = vo[:, M_DH:]
    lane = lax.broadcasted_iota(jnp.int32, (tb, LANES), 1)
    gates_ref[...] = jnp.where(lane < M_HEADS, gates, _log_sigmoid(gates))


def _mlstm_proj(l, proj, hist, W, blocks_per_seq, tb):
    rows = proj.shape[0]
    has_hist = hist is not None
    in_specs = [pl.BlockSpec((tb, M_WIDTH), lambda i: (i, COL_XM // M_WIDTH))]
    args = [proj]
    if has_hist:
        in_specs.append(pl.BlockSpec((None, tb, M_WIDTH), lambda i: (l, i, 0)))
        args.append(hist)
    in_specs += [
        pl.BlockSpec((None, CONV_K, M_WIDTH), lambda i: (l, 0, 0)),
        _vec_spec(l, M_WIDTH),
        pl.BlockSpec((None, M_HEADS, M_DH, 2 * M_DH), lambda i: (l, 0, 0, 0)),
        pl.BlockSpec((None, M_HEADS, M_DH, 2 * M_DH), lambda i: (l, 0, 0, 0)),
        pl.BlockSpec((None, M_HEADS, 3, M_DH, LANES), lambda i: (l, 0, 0, 0, 0)),
        _vec_spec(l, LANES),
    ]
    args += [W["m_cw"], W["m_cb"], W["wqk"], W["wvo"], W["wg"], W["bg"]]
    row_spec = pl.BlockSpec((tb, M_WIDTH), lambda i: (i, 0))
    return pl.pallas_call(
        functools.partial(_mlstm_proj_kernel, has_hist=has_hist, blocks_per_seq=blocks_per_seq),
        grid=(rows // tb,),
        in_specs=in_specs,
        out_specs=[row_spec, row_spec, row_spec, row_spec, pl.BlockSpec((tb, LANES), lambda i: (i, 0))],
        out_shape=[
            jax.ShapeDtypeStruct((rows, M_WIDTH), BF16),
            jax.ShapeDtypeStruct((rows, M_WIDTH), BF16),
            jax.ShapeDtypeStruct((rows, M_WIDTH), BF16),
            jax.ShapeDtypeStruct((rows, M_WIDTH), F32),
            jax.ShapeDtypeStruct((rows, LANES), F32),
        ],
        scratch_shapes=[pltpu.VMEM((SUBLANES, M_WIDTH), F32)],
        compiler_params=_params(("arbitrary",)),
        name="mlstm_proj",
    )(*args)


def _mlstm_finish(hh, o, bo, ng, zm):
    hm = _sigmoid(o + bo) * hh
    return _rms(hm, ng) * _silu(zm)


def _mlstm_prep(gates_ref):
    L = gates_ref.shape[0]
    r2 = lax.broadcasted_iota(jnp.int32, (L, L), 0)
    c2 = lax.broadcasted_iota(jnp.int32, (L, L), 1)
    causal = c2 <= r2
    g = gates_ref[...]
    b_all = _dot_mask(causal, g)
    return causal, g, b_all, g.T, b_all.T


def _mlstm_head(h, prep, q_ref, k_ref, v_ref, o_ref, zm_ref, bo_ref, ng_ref, c1_ref, n1_ref, m1_ref, emit):
    causal, g, b_all, g_t, b_t = prep
    L = q_ref.shape[0]
    neg_inf = -jnp.inf
    cols = slice(h * M_DH, (h + 1) * M_DH)
    f = M_HEADS + h
    g_row = g_t[h:h + 1, :] - b_t[f:f + 1, :]
    b_col = b_all[:, f:f + 1]
    g_col = g[:, h:h + 1] - b_col
    mprev = m1_ref[0, h:h + 1, 0:1]
    gm = jnp.where(causal, g_row, neg_inf)
    m_col = jnp.maximum(mprev, jnp.max(gm, axis=1, keepdims=True))
    w_intra = jnp.exp(gm - m_col)
    w_inter = jnp.exp(mprev - m_col)
    m_new = b_col + m_col
    m_last = m_col[L - 1:L, :]
    wl_col = jnp.exp(g_col - m_last)
    wli = jnp.exp(mprev - m_last)
    q = q_ref[:, cols]
    k = k_ref[:, cols]
    v = v_ref[:, cols]
    cst = c1_ref[0, h]
    nst = n1_ref[0, h]
    s = _dot_nt(q, k) * w_intra
    num = _dot(s.astype(BF16), v) + w_inter * _dot_nt(q, cst.astype(BF16))
    den = (jnp.sum(s, axis=1, keepdims=True)
           + w_inter * jnp.sum(q.astype(F32) * nst, axis=1, keepdims=True))
    hh = num / jnp.maximum(jnp.abs(den), jnp.exp(-m_new))
    emit(cols, _mlstm_finish(hh, o_ref[:, cols], bo_ref[:, cols], ng_ref[:, cols], zm_ref[:, cols]))

    def update_state():
        c1_ref[0, h] = wli * cst + _dot((v.astype(F32) * wl_col).T.astype(BF16), k)
        n1_ref[0, h] = wli * nst + jnp.sum(k.astype(F32) * wl_col, axis=0, keepdims=True)
        m1_ref[0, h:h + 1, :] = jnp.broadcast_to(m_new[L - 1:L, :], (1, LANES))

    return update_state


def _block_masks(L, rs, t_lo):
    shift = rs.bit_length() - 1
    r2 = lax.broadcasted_iota(jnp.int32, (L, L), 0)
    c2 = lax.broadcasted_iota(jnp.int32, (L, L), 1)
    same = (r2 >> shift) == (c2 >> shift)
    valid_c = (c2 & (rs - 1)) >= t_lo
    ridx = lax.broadcasted_iota(jnp.int32, (L, 1), 0)
    valid_r = (ridx & (rs - 1)) >= t_lo
    return r2, c2, same, valid_c, ridx, valid_r


def _mlstm_sample_kernel(q_ref, k_ref, v_ref, o_ref, zm_ref, gates_ref, bo_ref, ng_ref,
                         c0_ref, n0_ref, m0_ref, alias_ref,
                         out_ref, c1_ref, n1_ref, mrow_ref,
                         gt_sc, bt_sc, col_sc, q_sc, numi_sc, nrow_sc, wk_sc, *, nseq, t_lo):
    del alias_ref
    h = pl.program_id(0)
    L = q_ref.shape[0]
    rs = L // nseq
    r2, c2, same, valid_c, ridx, valid_r = _block_masks(L, rs, t_lo)
    eye = r2 == c2
    neg_inf = -jnp.inf
    g = gates_ref[...]
    b_all = _dot_mask(same & (c2 <= r2), jnp.where(valid_r, g, 0.0))
    gt_sc[...] = g.T
    bt_sc[...] = b_all.T
    lane_g = lax.broadcasted_iota(jnp.int32, (1, LANES), 1)
    i_col = jnp.sum(jnp.where(lane_g == h, g, 0.0), axis=1, keepdims=True)
    b_col = jnp.sum(jnp.where(lane_g == h + M_HEADS, b_all, 0.0), axis=1, keepdims=True)
    g_row = gt_sc[pl.ds(h, 1), :] - bt_sc[pl.ds(h + M_HEADS, 1), :]
    g_col = i_col - b_col
    mask = same & (c2 <= r2) & valid_c
    mprev = m0_ref[0]
    gm = jnp.where(mask, g_row, neg_inf)
    m_col = jnp.maximum(mprev, jnp.max(gm, axis=1, keepdims=True))
    m_row = jnp.sum(jnp.where(eye, m_col, 0.0), axis=0, keepdims=True)
    mlast_col = jnp.max(jnp.where(same, m_row, neg_inf), axis=1, keepdims=True)
    w_intra = jnp.exp(gm - m_col)
    w_inter = jnp.exp(mprev - m_col)
    m_new = b_col + m_col
    wl_col = jnp.where(valid_r, jnp.exp(g_col - mlast_col), 0.0)
    col_sc[:, 0:1] = jnp.exp(mprev - mlast_col)

    q = q_ref[...]
    k = k_ref[...]
    v = v_ref[...]
    s = _dot_nt(q, k) * w_intra
    num = _dot(s.astype(BF16), v)
    den = jnp.sum(s, axis=1, keepdims=True)
    wvt = (v.astype(F32) * wl_col).T.astype(BF16)
    wk_sc[...] = k.astype(F32) * wl_col
    qf = q.astype(F32)
    q_sc[0:L, :] = qf
    q_sc[L:L + SUBLANES, :] = jnp.zeros((SUBLANES, M_DH), F32)

    def seq_step(j, carry):
        rows = pl.ds(pl.multiple_of(j * rs, rs), rs)
        in_seq_r = (ridx >= j * rs) & (ridx < (j + 1) * rs)
        cj = c0_ref[j, 0]
        nj = n0_ref[j, 0]
        q2 = q_sc[pl.ds(pl.multiple_of(j * rs, rs), 2 * rs), :].astype(BF16)
        numi_sc[rows, :] = _dot_nt(q2, cj.astype(BF16))[:rs]
        nrow_sc[rows, :] = jnp.broadcast_to(nj, (rs, M_DH))
        wli = col_sc[pl.ds(j * rs + rs - 1, 1), 0:1]
        c1_ref[j, 0] = wli * cj + _dot(wvt, jnp.where(in_seq_r, k, jnp.zeros_like(k)))
        n1_ref[j, 0] = wli * nj + jnp.sum(wk_sc[rows, :], axis=0, keepdims=True)
        return carry

    lax.fori_loop(0, nseq, seq_step, 0, unroll=SEQ_UNROLL)
    num = num + w_inter * numi_sc[...]
    den = den + w_inter * jnp.sum(qf * nrow_sc[...], axis=1, keepdims=True)
    hh = num / jnp.maximum(jnp.abs(den), jnp.exp(-m_new))
    out_ref[...] = _mlstm_finish(hh, o_ref[...], bo_ref[...], ng_ref[...], zm_ref[...]).astype(BF16)
    mrow_ref[0] = m_new


def _mlstm_sample(l, depth, q, k, v, o, gates, proj, W, c0, n0, m0rows, c_prev, nseq, t_lo):
    L = CHUNK
    rows = q.shape[0]
    nb = rows // L
    nbatch = nb * nseq
    rowblk = lambda h, b: (b, h)
    c_spec = pl.BlockSpec((None, nseq, 1, M_DH, M_DH), lambda h, b: (l, b, h, 0, 0))
    n_spec_in = pl.BlockSpec((None, nseq, 1, 1, M_DH), lambda h, b: (l, b, h, 0, 0))
    n_spec_out = pl.BlockSpec((nseq, 1, 1, M_DH), lambda h, b: (b, h, 0, 0))
    has_prev = c_prev is not None
    in_specs = [
        pl.BlockSpec((L, M_DH), rowblk),
        pl.BlockSpec((L, M_DH), rowblk),
        pl.BlockSpec((L, M_DH), rowblk),
        pl.BlockSpec((L, M_DH), rowblk),
        pl.BlockSpec((L, M_DH), lambda h, b: (b, COL_ZM // M_DH + h)),
        pl.BlockSpec((L, LANES), lambda h, b: (b, 0)),
        _vec_spec(l, M_DH, lambda h, b: h),
        _vec_spec(l, M_DH, lambda h, b: h),
        c_spec,
        n_spec_in,
        pl.BlockSpec((None, 1, L, 1), lambda h, b: (l, h, b, 0)),
        pl.BlockSpec(memory_space=pl.ANY),
    ]
    args = [q, k, v, o, proj, gates, W["bo"], W["m_ng"], c0, n0, m0rows,
            c_prev if has_prev else jnp.zeros((SUBLANES, LANES), F32)]
    return pl.pallas_call(
        functools.partial(_mlstm_sample_kernel, nseq=nseq, t_lo=t_lo),
        grid=(M_HEADS, nb),
        in_specs=in_specs,
        out_specs=[pl.BlockSpec((L, M_DH), rowblk), c_spec, n_spec_out,
                   pl.BlockSpec((1, L, 1), lambda h, b: (h, b, 0))],
        out_shape=[
            jax.ShapeDtypeStruct((rows, M_WIDTH), BF16),
            jax.ShapeDtypeStruct((depth, nbatch, M_HEADS, M_DH, M_DH), F32),
            jax.ShapeDtypeStruct((nbatch, M_HEADS, 1, M_DH), F32),
            jax.ShapeDtypeStruct((M_HEADS, rows, 1), F32),
        ],
        scratch_shapes=[
            pltpu.VMEM((LANES, L), F32),
            pltpu.VMEM((LANES, L), F32),
            pltpu.VMEM((L, LANES), F32),
            pltpu.VMEM((L + SUBLANES, M_DH), F32),
            pltpu.VMEM((L, M_DH), F32),
            pltpu.VMEM((L, M_DH), F32),
            pltpu.VMEM((L, M_DH), F32),
        ],
        input_output_aliases={11: 1} if has_prev else {},
        compiler_params=_params(("arbitrary", "arbitrary")),
        name="mlstm_sample",
    )(*args)


def _ssd_prompt_kernel(xs_ref, b_ref, c_ref, zs_ref, h_ref, wdt_ref,
                       cwx_ref, cwb_ref, cwc_ref, cbx_ref, cbb_ref, cbc_ref,
                       dtb_ref, alog_ref, dskip_ref, ng_ref, expand_ref,
                       out_ref, h1_ref, xex_sc, xeb_sc, xec_sc, ht_sc):
    c = pl.program_id(1)
    nc = pl.num_programs(1)
    L = CHUNK

    @pl.when(c == 0)
    def _():
        ht_sc[...] = jnp.zeros(ht_sc.shape, F32)

    for sub in range(xs_ref.shape[0] // L):
        _ssd_prompt_chunk(slice(sub * L, (sub + 1) * L), (c == 0) if sub == 0 else False,
                          xs_ref, b_ref, c_ref, zs_ref, h_ref, wdt_ref,
                          cwx_ref, cwb_ref, cwc_ref, cbx_ref, cbb_ref, cbc_ref,
                          dtb_ref, alog_ref, dskip_ref, ng_ref, expand_ref,
                          out_ref, xex_sc, xeb_sc, xec_sc, ht_sc)

    @pl.when(c == nc - 1)
    def _():
        for pr in range(S_HEADS // 2):
            blk = ht_sc[:, pr * LANES:(pr + 1) * LANES].T
            h1_ref[0, 2 * pr] = blk[:S_DH]
            h1_ref[0, 2 * pr + 1] = blk[S_DH:]


def _ssd_prompt_chunk(rows, first, xs_ref, b_ref, c_ref, zs_ref, h_ref, wdt_ref,
                      cwx_ref, cwb_ref, cwc_ref, cbx_ref, cbb_ref, cbc_ref,
                      dtb_ref, alog_ref, dskip_ref, ng_ref, expand_ref,
                      out_ref, xex_sc, xeb_sc, xec_sc, ht_sc):
    L = CHUNK
    xs = _conv_rows(xs_ref[rows, :], first, xex_sc, cwx_ref, cbx_ref)
    bm = _conv_rows(b_ref[rows, :], first, xeb_sc, cwb_ref, cbb_ref)
    cm = _conv_rows(c_ref[rows, :], first, xec_sc, cwc_ref, cbc_ref).astype(BF16)

    r2 = lax.broadcasted_iota(jnp.int32, (L, L), 0)
    c2 = lax.broadcasted_iota(jnp.int32, (L, L), 1)
    causal = c2 <= r2
    neg_inf = -jnp.inf
    dt = _softplus(_dot(h_ref[rows, :], wdt_ref[...]) + dtb_ref[...])
    da = dt * (-jnp.exp(alog_ref[...]))
    cs = _dot_mask(causal, da)
    cs_t = cs.T
    ecs = jnp.exp(cs)
    wend = jnp.exp(cs[L - 1:L, :] - cs) * dt

    expand = expand_ref[...]
    dte = _widen(dt, expand)
    wende = _widen(wend, expand)
    ecse = _widen(ecs, expand)
    xdt = (xs * dte).astype(BF16)
    wx = (xs * wende).astype(BF16)
    lane = lax.broadcasted_iota(jnp.int32, (1, LANES), 1)
    low_half = lane < S_DH
    zero_slab = jnp.zeros((L, LANES), BF16)
    for g in range(S_GROUPS):
        gcols = slice(g * S_GW, (g + 1) * S_GW)
        scols = slice(g * S_STATE, (g + 1) * S_STATE)
        bg = bm[:, scols]
        cg = cm[:, scols]
        cb = _dot_nt(cg, bg.astype(BF16))
        pairs = []
        for pr in range(S_HPG // 2):
            h0 = g * S_HPG + 2 * pr
            mixes = []
            for hh in (h0, h0 + 1):
                dec = jnp.exp(jnp.where(causal, cs[:, hh:hh + 1] - cs_t[hh:hh + 1, :], neg_inf))
                mixes.append((cb * dec).astype(BF16))
            slab = xdt[:, h0 * S_DH:(h0 + 2) * S_DH]
            rhs = jnp.concatenate([jnp.where(low_half, slab, zero_slab),
                                   jnp.where(low_half, zero_slab, slab)], axis=0)
            pairs.append(_dot(jnp.concatenate(mixes, axis=1), rhs))
        y_intra = jnp.concatenate(pairs, axis=1)
        ht = ht_sc[:, gcols]
        y = y_intra + ecse[:, gcols] * _dot(cg, ht.astype(BF16)) + dskip_ref[:, gcols] * xs[:, gcols]
        y = y * _silu(zs_ref[rows, gcols])
        out_ref[rows, gcols] = _rms(y, ng_ref[:, gcols]).astype(BF16)
        ht_sc[:, gcols] = ecse[L - 1:L, gcols] * ht + _dot(bg.T.astype(BF16), wx[:, gcols])


def _ssd_prompt(l, proj, h, W, nb, nc):
    L = CHUNK * SSD_SUB
    rows = nb * nc * L
    row = lambda b, c: b * nc + c
    conv_w = lambda width, blk: pl.BlockSpec((None, CONV_K, width), lambda b, c: (l, 0, blk))
    return pl.pallas_call(
        _ssd_prompt_kernel,
        grid=(nb, nc),
        in_specs=[
            pl.BlockSpec((L, S_WIDTH), lambda b, c: (row(b, c), COL_XBC // S_WIDTH)),
            pl.BlockSpec((L, S_BC), lambda b, c: (row(b, c), COL_B // S_BC)),
            pl.BlockSpec((L, S_BC), lambda b, c: (row(b, c), COL_C // S_BC)),
            pl.BlockSpec((L, S_WIDTH), lambda b, c: (row(b, c), COL_ZS // S_WIDTH)),
            pl.BlockSpec((L, D_MODEL), lambda b, c: (row(b, c), 0)),
            pl.BlockSpec((None, D_MODEL, LANES), lambda b, c: (l, 0, 0)),
            conv_w(S_WIDTH, 0), conv_w(S_BC, S_WIDTH // S_BC), conv_w(S_BC, S_WIDTH // S_BC + 1),
            _vec_spec(l, S_WIDTH, lambda b, c: 0),
            _vec_spec(l, S_BC, lambda b, c: S_WIDTH // S_BC),
            _vec_spec(l, S_BC, lambda b, c: S_WIDTH // S_BC + 1),
            _vec_spec(l, LANES),
            _vec_spec(l, LANES),
            _vec_spec(l, S_WIDTH),
            _vec_spec(l, S_WIDTH),
            pl.BlockSpec((2 * LANES, S_WIDTH), lambda b, c: (0, 0)),
        ],
        out_specs=[
            pl.BlockSpec((L, S_WIDTH), lambda b, c: (row(b, c), 0)),
            pl.BlockSpec((1, S_HEADS, S_DH, S_STATE), lambda b, c: (b, 0, 0, 0)),
        ],
        out_shape=[
            jax.ShapeDtypeStruct((rows, S_WIDTH), BF16),
            jax.ShapeDtypeStruct((nb, S_HEADS, S_DH, S_STATE), F32),
        ],
        scratch_shapes=[
            pltpu.VMEM((SUBLANES, S_WIDTH), F32),
            pltpu.VMEM((SUBLANES, S_BC), F32),
            pltpu.VMEM((SUBLANES, S_BC), F32),
            pltpu.VMEM((S_STATE, S_WIDTH), F32),
        ],
        compiler_params=_params(("arbitrary", "arbitrary")),
        name="ssd_prompt",
    )(proj, proj, proj, proj, h, W["w_dt_c"], W["s_cw"], W["s_cw"], W["s_cw"], W["s_cb"], W["s_cb"], W["s_cb"],
      W["dtb_c"], W["alog_c"], W["dskip_wide"], W["s_ng"], W["expand"])


def _ssd_sample_kernel(xs_ref, b_ref, c_ref, zs_ref, h_ref, wdt_ref, hx_ref, hb_ref, hc_ref,
                       cwx_ref, cwb_ref, cwc_ref, cbx_ref, cbb_ref, cbc_ref,
                       dtb_ref, alog_ref, dskip_ref, ng_ref, expand_ref, h0_ref, alias_ref,
                       out_ref, h1_ref, xex_sc, xeb_sc, xec_sc, tot_sc, cm_sc, yi_sc, *, nseq, t_lo):
    del alias_ref
    L = xs_ref.shape[0]
    rs = L // nseq
    first = True
    xs = _conv_rows(xs_ref[...] + hx_ref[...], first, xex_sc, cwx_ref, cbx_ref)
    bm = _conv_rows(b_ref[...] + hb_ref[...], first, xeb_sc, cwb_ref, cbb_ref)
    cm = _conv_rows(c_ref[...] + hc_ref[...], first, xec_sc, cwc_ref, cbc_ref)
    bmb = bm.astype(BF16)
    cmb = cm.astype(BF16)
    cm_sc[0:L, :] = cm
    cm_sc[L:L + SUBLANES, :] = jnp.zeros((SUBLANES, S_STATE), F32)

    r2, c2, same, valid_c, ridx, valid_r = _block_masks(L, rs, t_lo)
    mask = same & (c2 <= r2) & valid_c
    neg_inf = -jnp.inf
    dt = jnp.where(valid_r, _softplus(_dot(h_ref[...], wdt_ref[...]) + dtb_ref[...]), 0.0)
    da = dt * (-jnp.exp(alog_ref[...]))
    cs = _dot_mask(mask, da)
    sfx = _dot_mask(same & (c2 > r2), da)
    tot_sc[...] = cs + sfx
    cs_t = cs.T
    expand = expand_ref[...]
    dte = _widen(dt, expand)
    wende = _widen(jnp.exp(sfx) * dt, expand)
    ecse = _widen(jnp.exp(cs), expand)
    xdt = (xs * dte).astype(BF16)
    wxt = (xs * wende).T.astype(BF16)
    cb = _dot_nt(cmb, bmb)
    lane = lax.broadcasted_iota(jnp.int32, (1, LANES), 1)
    low_half = lane < S_DH
    zero_slab = jnp.zeros((L, LANES), BF16)
    pairs = []
    for pr in range(S_HPG // 2):
        mixes = []
        for hh in (2 * pr, 2 * pr + 1):
            dec = jnp.exp(jnp.where(mask, cs[:, hh:hh + 1] - cs_t[hh:hh + 1, :], neg_inf))
            mixes.append((cb * dec).astype(BF16))
        slab = xdt[:, pr * LANES:(pr + 1) * LANES]
        rhs = jnp.concatenate([jnp.where(low_half, slab, zero_slab),
                               jnp.where(low_half, zero_slab, slab)], axis=0)
        pairs.append(_dot(jnp.concatenate(mixes, axis=1), rhs))
    y_intra = jnp.concatenate(pairs, axis=1)

    def seq_step(j, carry):
        rows = pl.ds(pl.multiple_of(j * rs, rs), rs)
        in_seq_r = (ridx >= j * rs) & (ridx < (j + 1) * rs)
        hj = h0_ref[j]
        c2rows = cm_sc[pl.ds(pl.multiple_of(j * rs, rs), 2 * rs), :].astype(BF16)
        yi_sc[rows, :] = _dot_nt(c2rows, hj.reshape(S_GW, S_STATE).astype(BF16))[:rs]
        upd = _dot(wxt, jnp.where(in_seq_r, bmb, jnp.zeros_like(bmb)))
        dec_j = jnp.exp(tot_sc[pl.ds(j * rs + rs - 1, 1), :])
        for hh in range(S_HPG):
            h1_ref[j, hh] = dec_j[:, hh:hh + 1] * hj[hh] + upd[hh * S_DH:(hh + 1) * S_DH, :]
        return carry

    lax.fori_loop(0, nseq, seq_step, 0, unroll=SEQ_UNROLL)
    y = y_intra + ecse * yi_sc[...] + dskip_ref[...] * xs
    out_ref[...] = _rms(y * _silu(zs_ref[...]), ng_ref[...]).astype(BF16)


def _ssd_sample(l, depth, proj, h, hist, W, h0, h_prev, nseq, t_lo):
    L = CHUNK
    rows = proj.shape[0]
    nb = rows // L
    xblk = COL_XBC // S_GW
    bblk = COL_B // S_STATE
    cblk = COL_C // S_STATE
    hb_blk = S_WIDTH // S_STATE
    hc_blk = hb_blk + S_GROUPS
    has_prev = h_prev is not None
    conv_w = lambda width, blk: pl.BlockSpec((None, CONV_K, width), lambda g, b: (l, 0, blk(g)))
    h_spec = pl.BlockSpec((None, nseq, S_HPG, S_DH, S_STATE), lambda g, b: (l, b, g, 0, 0))
    in_specs = [
        pl.BlockSpec((L, S_GW), lambda g, b: (b, xblk + g)),
        pl.BlockSpec((L, S_STATE), lambda g, b: (b, bblk + g)),
        pl.BlockSpec((L, S_STATE), lambda g, b: (b, cblk + g)),
        pl.BlockSpec((L, S_GW), lambda g, b: (b, COL_ZS // S_GW + g)),
        pl.BlockSpec((L, D_MODEL), lambda g, b: (b, 0)),
        pl.BlockSpec((None, D_MODEL, LANES), lambda g, b: (l, 0, g)),
        pl.BlockSpec((None, L, S_GW), lambda g, b: (l, b, g)),
        pl.BlockSpec((None, L, S_STATE), lambda g, b: (l, b, hb_blk + g)),
        pl.BlockSpec((None, L, S_STATE), lambda g, b: (l, b, hc_blk + g)),
        conv_w(S_GW, lambda g: g), conv_w(S_STATE, lambda g: hb_blk + g), conv_w(S_STATE, lambda g: hc_blk + g),
        _vec_spec(l, S_GW, lambda g, b: g),
        _vec_spec(l, S_STATE, lambda g, b: hb_blk + g),
        _vec_spec(l, S_STATE, lambda g, b: hc_blk + g),
        _vec_spec(l, LANES, lambda g, b: g),
        _vec_spec(l, LANES, lambda g, b: g),
        _vec_spec(l, S_GW, lambda g, b: g),
        _vec_spec(l, S_GW, lambda g, b: g),
        pl.BlockSpec((2 * LANES, S_GW), lambda g, b: (0, 0)),
        h_spec,
        pl.BlockSpec(memory_space=pl.ANY),
    ]
    args = [proj, proj, proj, proj, h, W["w_dt_g"], hist, hist, hist,
            W["s_cw"], W["s_cw"], W["s_cw"], W["s_cb"], W["s_cb"], W["s_cb"],
            W["dtb_g"], W["alog_g"], W["dskip_wide"], W["s_ng"], W["expand"], h0,
            h_prev if has_prev else jnp.zeros((SUBLANES, LANES), F32)]
    return pl.pallas_call(
        functools.partial(_ssd_sample_kernel, nseq=nseq, t_lo=t_lo),
        grid=(S_GROUPS, nb),
        in_specs=in_specs,
        out_specs=[pl.BlockSpec((L, S_GW), lambda g, b: (b, g)), h_spec],
        out_shape=[
            jax.ShapeDtypeStruct((rows, S_WIDTH), BF16),
            jax.ShapeDtypeStruct((depth, nb * nseq, S_HEADS, S_DH, S_STATE), F32),
        ],
        scratch_shapes=[
            pltpu.VMEM((SUBLANES, S_GW), F32),
            pltpu.VMEM((SUBLANES, S_STATE), F32),
            pltpu.VMEM((SUBLANES, S_STATE), F32),
            pltpu.VMEM((L, LANES), F32),
            pltpu.VMEM((L + SUBLANES, S_STATE), F32),
            pltpu.VMEM((L, S_GW), F32),
        ],
        input_output_aliases={21: 1} if has_prev else {},
        compiler_params=_params(("arbitrary", "arbitrary")),
        name="ssd_sample",
    )(*args)


def _outproj_body(x_ref, om, os_ref, u_ref, v_ref, zc_ref, wmix_ref, bias_ref, gv_ref,
                  w0_ref, w1_ref, w2_ref, w3_ref, g_ref, y_ref, h_ref, vn_ref, oc_sc, *, final, rs, t_lo,
                  between=None):
    emit_vn = vn_ref is not None
    if between is None:
        between = lambda k: None
    L = C_CHUNK
    r2, c2, same, valid_c, _, _ = _block_masks(L, min(rs, L), t_lo)
    mask = same & (c2 <= r2) & valid_c
    for g in range(C_GROUPS):
        cols = slice(g * C_DG, (g + 1) * C_DG)
        w = jnp.where(mask, wmix_ref[g], 0.0).astype(BF16)
        for ch in range(x_ref.shape[0] // L):
            rows = slice(ch * L, (ch + 1) * L)
            vn = _rms(v_ref[rows, cols], gv_ref[:, cols])
            if emit_vn:
                vn_ref[rows, cols] = vn
            mixed = _dot(w, vn.astype(BF16)) + bias_ref[g]
            oc_sc[rows, cols] = (u_ref[rows, cols] * mixed * _silu(zc_ref[rows, cols])).astype(BF16)
    kb = w0_ref.shape[0]
    hk = kb // 2
    lhs = (lambda c: om[:, c],
           lambda c: os_ref[:, c].astype(BF16),
           lambda c: os_ref[:, slice(kb + c.start, kb + c.stop)].astype(BF16),
           lambda c: oc_sc[:, c])
    acc = None
    for blk, w_ref in enumerate((w0_ref, w1_ref, w2_ref, w3_ref)):
        for part in range(2):
            c = slice(part * hk, (part + 1) * hk)
            d = _dot(lhs[blk](c), w_ref[c, :])
            acc = d if acc is None else acc + d
            between(2 * blk + part)
    y = x_ref[...] + acc
    if t_lo:
        ridx = lax.broadcasted_iota(jnp.int32, (y.shape[0], 1), 0)
        y = jnp.where((ridx & (rs - 1)) >= t_lo, y, 0.0)
    if final:
        y_ref[...] = _rms(y, g_ref[...])
    else:
        y_ref[...] = y
        h_ref[...] = _rms(y, g_ref[...]).astype(BF16)


def _outproj_kernel(*refs, final, emit_vn, rs, t_lo):
    (x_ref, om_ref, os_ref, u_ref, v_ref, zc_ref, wmix_ref, bias_ref, gv_ref,
     w0_ref, w1_ref, w2_ref, w3_ref, g_ref, y_ref) = refs[:15]
    rest = list(refs[15:])
    h_ref = None if final else rest.pop(0)
    vn_ref = rest.pop(0) if emit_vn else None
    (oc_sc,) = rest
    _outproj_body(x_ref, om_ref[...].astype(BF16), os_ref, u_ref, v_ref, zc_ref, wmix_ref, bias_ref, gv_ref,
                  w0_ref, w1_ref, w2_ref, w3_ref, g_ref, y_ref, h_ref, vn_ref, oc_sc,
                  final=final, rs=rs, t_lo=t_lo)


def _mlstm_out_kernel(*refs, final, nc):
    (q_ref, k_ref, v_ref, o_ref, zm_ref, gates_ref, bo_ref, ng_ref,
     x_ref, os_ref, u_ref, v2_ref, zc_ref, wmix_ref, bias_ref, gv_ref,
     w0_ref, w1_ref, w2_ref, w3_ref, g_ref, c1_ref, n1_ref, m1_ref, y_ref) = refs[:25]
    rest = list(refs[25:])
    h_ref = None if final else rest.pop(0)
    om_sc, oc_sc = rest
    i = pl.program_id(0)

    @pl.when(i == 0)
    def _():
        om_sc[...] = jnp.zeros(om_sc.shape, BF16)

    @pl.when(i % nc == 0)
    def _():
        c1_ref[...] = jnp.zeros(c1_ref.shape, F32)
        n1_ref[...] = jnp.zeros(n1_ref.shape, F32)
        m1_ref[...] = jnp.zeros(m1_ref.shape, F32)

    def emit(cols, val):
        om_sc[i % 2, :, cols] = val.astype(BF16)

    prep = _mlstm_prep(gates_ref)

    pending = {}

    def head(slot):
        h, second = divmod(slot, 2)
        if second:
            pending.pop(h)()
        else:
            pending[h] = _mlstm_head(h, prep, q_ref, k_ref, v_ref, o_ref, zm_ref, bo_ref, ng_ref,
                                     c1_ref, n1_ref, m1_ref, emit)

    _outproj_body(x_ref, om_sc[(i + 1) % 2], os_ref, u_ref, v2_ref, zc_ref, wmix_ref, bias_ref, gv_ref,
                  w0_ref, w1_ref, w2_ref, w3_ref, g_ref, y_ref, h_ref, None, oc_sc,
                  final=final, rs=CHUNK, t_lo=0, between=head)


def _mlstm_out(l, q, k, v, o, gates, proj, x, os_, wmix, bias, W, g, final, nb, nc):
    L = M_CHUNK
    nt = nb * nc
    rows = nt * L
    cur = lambda i: jnp.minimum(i, nt - 1)
    prev = lambda i: jnp.maximum(i - 1, 0)
    cur_spec = pl.BlockSpec((L, M_WIDTH), lambda i: (cur(i), 0))
    prev_spec = pl.BlockSpec((L, D_MODEL), lambda i: (prev(i), 0))
    wspec = lambda r: pl.BlockSpec((None, M_WIDTH, D_MODEL), lambda i: (l, r, 0), pipeline_mode=pl.Buffered(1))
    out_shape = [
        jax.ShapeDtypeStruct((nb + 1, M_HEADS, M_DH, M_DH), F32),
        jax.ShapeDtypeStruct((nb + 1, M_HEADS, 1, M_DH), F32),
        jax.ShapeDtypeStruct((nb + 1, SUBLANES, LANES), F32),
        jax.ShapeDtypeStruct((rows, D_MODEL), F32),
    ]
    out_specs = [
        pl.BlockSpec((1, M_HEADS, M_DH, M_DH), lambda i: (i // nc, 0, 0, 0)),
        pl.BlockSpec((1, M_HEADS, 1, M_DH), lambda i: (i // nc, 0, 0, 0)),
        pl.BlockSpec((1, SUBLANES, LANES), lambda i: (i // nc, 0, 0)),
        prev_spec,
    ]
    if not final:
        out_shape.append(jax.ShapeDtypeStruct((rows, D_MODEL), BF16))
        out_specs.append(prev_spec)
    return pl.pallas_call(
        functools.partial(_mlstm_out_kernel, final=final, nc=nc),
        grid=(nt + 1,),
        in_specs=[
            cur_spec, cur_spec, cur_spec, cur_spec,
            pl.BlockSpec((L, M_WIDTH), lambda i: (cur(i), COL_ZM // M_WIDTH)),
            pl.BlockSpec((L, LANES), lambda i: (cur(i), 0)),
            _vec_spec(l, M_WIDTH),
            _vec_spec(l, M_WIDTH),
            prev_spec,
            prev_spec,
            pl.BlockSpec((L, C_WIDTH), lambda i: (prev(i), COL_U // C_WIDTH)),
            pl.BlockSpec((L, C_WIDTH), lambda i: (prev(i), COL_V // C_WIDTH)),
            pl.BlockSpec((L, C_WIDTH), lambda i: (prev(i), COL_ZC // C_WIDTH)),
            pl.BlockSpec((None, C_GROUPS, C_CHUNK, C_CHUNK), lambda i: (l, 0, 0, 0)),
            pl.BlockSpec((None, C_GROUPS, C_CHUNK, 1), lambda i: (l, 0, 0, 0)),
            _vec_spec(l, C_WIDTH),
            wspec(0), wspec(1), wspec(2), wspec(3),
            pl.BlockSpec((1, D_MODEL), lambda i: (0, 0)) if final else _vec_spec(l + 1, D_MODEL),
        ],
        out_specs=out_specs,
        out_shape=out_shape,
        scratch_shapes=[pltpu.VMEM((2, L, M_WIDTH), BF16), pltpu.VMEM((L, C_WIDTH), BF16)],
        compiler_params=_params(("arbitrary",)),
        name="mlstm_out",
    )(q, k, v, o, proj, gates, W["bo"], W["m_ng"], x, os_, proj, proj, proj, wmix, bias, W["gv"],
      W["wo"], W["wo"], W["wo"], W["wo"], g)


def _outproj(l, x, om, os_, proj, wmix, bias, gv, wo, g, final, emit_vn, rs, t_lo):
    rows = x.shape[0]
    tm = min(256, rows)
    kb = M_WIDTH
    wspec = lambda r: pl.BlockSpec((None, kb, D_MODEL), lambda i: (l, r, 0), pipeline_mode=pl.Buffered(1))
    row_spec = pl.BlockSpec((tm, D_MODEL), lambda i: (i, 0))
    out_shape = [jax.ShapeDtypeStruct((rows, D_MODEL), F32)]
    out_specs = [row_spec]
    if not final:
        out_shape.append(jax.ShapeDtypeStruct((rows, D_MODEL), BF16))
        out_specs.append(row_spec)
    if emit_vn:
        out_shape.append(jax.ShapeDtypeStruct((rows, C_WIDTH), F32))
        out_specs.append(pl.BlockSpec((tm, C_WIDTH), lambda i: (i, 0)))
    return pl.pallas_call(
        functools.partial(_outproj_kernel, final=final, emit_vn=emit_vn, rs=rs, t_lo=t_lo),
        grid=(rows // tm,),
        in_specs=[
            row_spec,
            pl.BlockSpec((tm, M_WIDTH), lambda i: (i, 0)),
            pl.BlockSpec((tm, S_WIDTH), lambda i: (i, 0)),
            pl.BlockSpec((tm, C_WIDTH), lambda i: (i, COL_U // C_WIDTH)),
            pl.BlockSpec((tm, C_WIDTH), lambda i: (i, COL_V // C_WIDTH)),
            pl.BlockSpec((tm, C_WIDTH), lambda i: (i, COL_ZC // C_WIDTH)),
            pl.BlockSpec((None, C_GROUPS, C_CHUNK, C_CHUNK), lambda i: (l, 0, 0, 0)),
            pl.BlockSpec((None, C_GROUPS, C_CHUNK, 1), lambda i: (l, 0, 0, 0)),
            _vec_spec(l, C_WIDTH),
            wspec(0), wspec(1), wspec(2), wspec(3),
            pl.BlockSpec((1, D_MODEL), lambda i: (0, 0)) if final else _vec_spec(l + 1, D_MODEL),
        ],
        out_specs=out_specs,
        out_shape=out_shape,
        scratch_shapes=[pltpu.VMEM((tm, C_WIDTH), BF16)],
        compiler_params=_params(("arbitrary",)),
        name="outproj",
    )(x, om, os_, proj, proj, proj, wmix, bias, gv, wo, wo, wo, wo, g)


def _heads_compact(a):
    return jnp.pad(a, ((0, 0), (0, LANES - S_HEADS)))[:, None, :]


def _heads_grouped(a):
    a = a.reshape(a.shape[0], S_GROUPS, S_HPG)
    return jnp.pad(a, ((0, 0), (0, 0), (0, LANES - S_HPG))).reshape(a.shape[0], 1, S_GROUPS * LANES)


def _prepare_weights(norm_g, w_in, m_conv_w, m_conv_b, m_w_qk, m_w_vo, m_b_o, m_w_gate, m_b_gate,
                     m_norm_g, s_conv_w, s_conv_b, s_dt_bias, s_A_log, s_D, s_norm_g, c_v_norm_g,
                     c_w_s, c_b_s, w_out):
    depth = w_in.shape[0]
    vec = lambda a: a.reshape(depth, 1, -1)
    dt0 = PROJ_HEAD
    w_all = w_in.astype(BF16)
    w_dt = w_in[:, :, dt0:dt0 + S_HEADS]
    w_dt_g = jnp.pad(w_dt.reshape(depth, D_MODEL, S_GROUPS, S_HPG),
                     ((0, 0), (0, 0), (0, 0), (0, LANES - S_HPG))).reshape(depth, D_MODEL, -1).astype(BF16)
    w_dt_c = jnp.pad(w_dt, ((0, 0), (0, 0), (0, LANES - S_HEADS))).astype(BF16)
    wg = m_w_gate.reshape(depth, M_HEADS, 3, M_DH, 2 * M_HEADS)
    wg = jnp.pad(wg, ((0, 0),) * 4 + ((0, LANES - 2 * M_HEADS),)).astype(BF16)
    head_of_lane = jnp.arange(S_WIDTH) // S_DH
    expand = (jnp.arange(LANES)[:, None] == head_of_lane[None, :]).astype(BF16)
    expand = jnp.concatenate([expand, expand], axis=0)
    return dict(
        norm_g=vec(norm_g), w_all=w_all, w_tail=w_all[:, :, dt0 + S_HEADS:], w_dt_c=w_dt_c, w_dt_g=w_dt_g,
        m_cw=m_conv_w, m_cb=vec(m_conv_b),
        wqk=m_w_qk.astype(BF16), wvo=m_w_vo.astype(BF16), wg=wg,
        bg=vec(jnp.pad(m_b_gate, ((0, 0), (0, LANES - 2 * M_HEADS)))),
        bo=vec(m_b_o), m_ng=vec(m_norm_g),
        s_cw=s_conv_w, s_cb=vec(s_conv_b),
        dtb_c=_heads_compact(s_dt_bias), alog_c=_heads_compact(s_A_log),
        dtb_g=_heads_grouped(s_dt_bias), alog_g=_heads_grouped(s_A_log),
        dskip_wide=vec(jnp.repeat(s_D, S_DH, axis=1)), s_ng=vec(s_norm_g), expand=expand,
        gv=vec(c_v_norm_g), wo=w_out.astype(BF16),
    )


def kernel(x_prompt, x_sample, state_mlstm_C, state_mlstm_n, state_mlstm_m, state_mlstm_conv, state_ssm, state_ssm_conv, norm_g, w_in, m_conv_w, m_conv_b, m_w_qk, m_w_vo, m_b_o, m_w_gate, m_b_gate, m_norm_g, s_conv_w, s_conv_b, s_dt_bias, s_A_log, s_D, s_norm_g, c_v_norm_g, c_w_s, c_b_s, w_out, final_norm_g):
    bp, seq, _ = x_prompt.shape
    bs, dec_seq, _ = x_sample.shape
    depth = w_in.shape[0]
    t_lo = SAMPLE_ROWS - dec_seq
    hist_lo = t_lo - (CONV_K - 1)
    nseq = CHUNK // SAMPLE_ROWS
    W = _prepare_weights(norm_g, w_in, m_conv_w, m_conv_b, m_w_qk, m_w_vo, m_b_o, m_w_gate, m_b_gate,
                         m_norm_g, s_conv_w, s_conv_b, s_dt_bias, s_A_log, s_D, s_norm_g, c_v_norm_g,
                         c_w_s, c_b_s, w_out)
    fg = final_norm_g[None, :]
    yp = x_prompt.reshape(bp * seq, D_MODEL)
    ys = jnp.pad(x_sample, ((0, 0), (t_lo, 0), (0, 0))).reshape(bs * SAMPLE_ROWS, D_MODEL)

    pad_hist = ((0, 0), (0, 0), (hist_lo, dec_seq), (0, 0))
    hist_m = jnp.pad(state_mlstm_conv, pad_hist).reshape(depth, bs * SAMPLE_ROWS, M_WIDTH)
    hist_s = jnp.pad(state_ssm_conv, pad_hist).reshape(depth, bs * SAMPLE_ROWS, -1)
    m0rows = jnp.repeat(jnp.swapaxes(state_mlstm_m, 1, 2)[..., None], SAMPLE_ROWS, axis=2)
    n0 = state_mlstm_n[:, :, :, None, :]
    reps = C_CHUNK // SAMPLE_ROWS
    w4 = jnp.pad(c_w_s[:, :, :dec_seq, :dec_seq], ((0, 0), (0, 0), (t_lo, 0), (t_lo, 0)))
    b4 = jnp.pad(c_b_s[:, :, :dec_seq], ((0, 0), (0, 0), (t_lo, 0)))
    wmix_s = jnp.tile(w4, (1, 1, reps, reps))
    bias_s = jnp.tile(b4, (1, 1, reps))[..., None]
    bias_p = c_b_s[..., None]

    hp = _norm(0, yp, W["norm_g"])
    hs = _norm(0, ys, W["norm_g"])
    outs_p, outs_s = [], []
    c_all = h_all = None
    mc = min(M_CHUNK, seq)
    sc = min(CHUNK * SSD_SUB, seq)
    tbp = min(512, bp * seq)
    tbs = min(512, bs * SAMPLE_ROWS)
    for l in range(depth):
        final = l == depth - 1
        proj = _inproj(l, hp, W["w_all"], W["w_tail"])
        q, k, v, o, gates = _mlstm_proj(l, proj, None, W, max(1, seq // tbp), tbp)
        out_s, h1 = _ssd_prompt(l, proj, hp, W, bp, seq // sc)
        res = _mlstm_out(l, q, k, v, o, gates, proj, yp, out_s, c_w_s, bias_p, W,
                         fg if final else W["norm_g"], final, bp, seq // mc)
        c1, n1, m1 = res[0][:bp], res[1][:bp], res[2][:bp]
        yp, hp = res[3], (None if final else res[4])
        pj = proj.reshape(bp, seq, PROJ_MAIN)
        outs_p.append((
            c1, n1.reshape(bp, M_HEADS, M_DH), m1[:, :M_HEADS, 0],
            pj[:, seq - (CONV_K - 1):, COL_XM:COL_XM + M_WIDTH], h1,
            pj[:, seq - (CONV_K - 1):, COL_XBC:COL_XBC + S_WIDTH + 2 * S_BC]))
        proj = _inproj(l, hs, W["w_all"], W["w_tail"])
        q, k, v, o, gates = _mlstm_proj(l, proj, hist_m, W, 1, tbs)
        out_m, c_all, n1, mrow = _mlstm_sample(l, depth, q, k, v, o, gates, proj, W, state_mlstm_C, n0,
                                               m0rows, c_all, nseq, t_lo)
        out_s, h_all = _ssd_sample(l, depth, proj, hs, hist_s, W, state_ssm, h_all, nseq, t_lo)
        res = _outproj(l, ys, out_m, out_s, proj, wmix_s, bias_s, W["gv"], W["wo"],
                       fg if final else W["norm_g"], final, True, SAMPLE_ROWS, t_lo)
        ys, hs, vn = res[0], (None if final else res[1]), res[-1]
        pj = proj.reshape(bs, SAMPLE_ROWS, PROJ_MAIN)
        outs_s.append((
            n1.reshape(bs, M_HEADS, M_DH), mrow[:, SAMPLE_ROWS - 1::SAMPLE_ROWS, 0].T,
            pj[:, SAMPLE_ROWS - (CONV_K - 1):, COL_XM:COL_XM + M_WIDTH],
            pj[:, SAMPLE_ROWS - (CONV_K - 1):, COL_XBC:COL_XBC + S_WIDTH + 2 * S_BC],
            vn.reshape(bs, SAMPLE_ROWS, C_WIDTH)[:, t_lo:]))
    p_out = [jnp.stack([s[i] for s in outs_p]) for i in range(6)]
    s_n, s_m, s_mconv, s_sconv, s_cv = [jnp.stack([s[i] for s in outs_s]) for i in range(5)]
    y_prompt = yp.reshape(bp, seq, D_MODEL)
    y_sample = ys.reshape(bs, SAMPLE_ROWS, D_MODEL)[:, t_lo:]
    return (y_prompt, y_sample, *p_out, c_all, s_n, s_m, s_mconv, h_all, s_sconv, s_cv)
```

```python
import functools

import jax
import jax.numpy as jnp
from jax import lax
from jax.experimental import pallas as pl
from jax.experimental.pallas import tpu as pltpu

F32 = jnp.float32
BF16 = jnp.bfloat16

D_MODEL = 2048
MIX_WIDTH = 2 * D_MODEL
M_WIDTH = MIX_WIDTH // 4
M_HEADS = 4
M_DH = M_WIDTH // M_HEADS
S_WIDTH = MIX_WIDTH // 2
S_DH = 64
S_HEADS = S_WIDTH // S_DH
S_GROUPS = 4
S_HPG = S_HEADS // S_GROUPS
S_STATE = 128
S_GW = S_HPG * S_DH
S_BC = S_GROUPS * S_STATE
C_WIDTH = MIX_WIDTH // 4
C_GROUPS = 4
C_DG = C_WIDTH // C_GROUPS
C_CHUNK = 128
CONV_K = 4
EPS = 1e-6

LANES = 128
SUBLANES = 8
SAMPLE_ROWS = 8
CHUNK = 128
M_CHUNK = 256
SSD_SUB = 4
SEQ_UNROLL = 16
NORM_ROWS = 1024
INPROJ_TILE = 1024
PROJ_ROWS = 512
VMEM_LIMIT = 56 * 1024 * 1024

COL_XM, COL_ZM, COL_ZS, COL_XBC, COL_U, COL_V, COL_ZC = 0, 1024, 2048, 4096, 7168, 8192, 9216
PROJ_MAIN = 10240
PROJ_HEAD = COL_U
COL_B = COL_XBC + S_WIDTH
COL_C = COL_B + S_BC


def _dot(a, b):
    return jnp.dot(a, b, preferred_element_type=F32)


def _dot_nt(a, b):
    return lax.dot_general(a, b, (((1,), (1,)), ((), ())), preferred_element_type=F32)


def _dot_mask(mask, x):
    m = jnp.where(mask, 1.0, 0.0).astype(BF16)
    hi = x.astype(BF16)
    r1 = x - hi.astype(F32)
    mid = r1.astype(BF16)
    lo = (r1 - mid.astype(F32)).astype(BF16)
    return _dot(m, hi) + _dot(m, mid) + _dot(m, lo)


def _widen(a, expand2):
    hi = a.astype(BF16)
    lo = (a - hi.astype(F32)).astype(BF16)
    return _dot(jnp.concatenate([hi, lo], axis=1), expand2)


def _sigmoid(x):
    return 0.5 * jnp.tanh(0.5 * x) + 0.5


def _silu(x):
    h = 0.5 * x
    return h * jnp.tanh(h) + h


def _softplus(x):
    return jnp.maximum(x, 0.0) + jnp.log1p(jnp.exp(-jnp.abs(x)))


def _log_sigmoid(x):
    return jnp.minimum(x, 0.0) - jnp.log1p(jnp.exp(-jnp.abs(x)))


def _rms(x, g):
    return x * lax.rsqrt(jnp.mean(x * x, axis=-1, keepdims=True) + EPS) * g


def _params(sem):
    return pltpu.CompilerParams(dimension_semantics=sem, vmem_limit_bytes=VMEM_LIMIT)


def _vec_spec(l, n, col=None):
    if col is None:
        return pl.BlockSpec((None, 1, n), lambda *ids: (l, 0, 0))
    return pl.BlockSpec((None, 1, n), lambda *ids: (l, 0, col(*ids)))


def _norm_kernel(x_ref, g_ref, h_ref):
    h_ref[...] = _rms(x_ref[...], g_ref[...]).astype(BF16)


def _norm(l, x, g):
    rows = x.shape[0]
    tb = min(NORM_ROWS, rows)
    return pl.pallas_call(
        _norm_kernel,
        grid=(rows // tb,),
        in_specs=[pl.BlockSpec((tb, D_MODEL), lambda i: (i, 0)), _vec_spec(l, D_MODEL)],
        out_specs=pl.BlockSpec((tb, D_MODEL), lambda i: (i, 0)),
        out_shape=jax.ShapeDtypeStruct((rows, D_MODEL), BF16),
        compiler_params=_params(("arbitrary",)),
        name="norm",
    )(x, g)


def _inproj_kernel(h_ref, wa_ref, wb_ref, proj_ref, *, n_head):
    j = pl.program_id(0)

    @pl.when(j < n_head)
    def _():
        proj_ref[...] = _dot(h_ref[...], wa_ref[...])

    @pl.when(j >= n_head)
    def _():
        proj_ref[...] = _dot(h_ref[...], wb_ref[...])


def _inproj(l, h, w_all, w_tail):
    rows = h.shape[0]
    tm = min(INPROJ_TILE, rows)
    tn = INPROJ_TILE
    n_head = PROJ_HEAD // tn
    return pl.pallas_call(
        functools.partial(_inproj_kernel, n_head=n_head),
        grid=(PROJ_MAIN // tn, rows // tm),
        in_specs=[
            pl.BlockSpec((tm, D_MODEL), lambda j, i: (i, 0)),
            pl.BlockSpec((None, D_MODEL, tn), lambda j, i: (l, 0, jnp.minimum(j, n_head - 1))),
            pl.BlockSpec((None, D_MODEL, tn), lambda j, i: (l, 0, jnp.maximum(j - n_head, 0))),
        ],
        out_specs=pl.BlockSpec((tm, tn), lambda j, i: (i, j)),
        out_shape=jax.ShapeDtypeStruct((rows, PROJ_MAIN), F32),
        compiler_params=_params(("arbitrary", "arbitrary")),
        name="inproj",
    )(h, w_all, w_tail)


def _conv_rows(x, first, prev_sc, cw_ref, cb_ref):
    assert CONV_K == 4
    tb = x.shape[0]

    if first is not False:
        @pl.when(first)
        def _():
            prev_sc[...] = jnp.zeros(prev_sc.shape, F32)

    xe = jnp.concatenate([prev_sc[...], x], axis=0)
    x1 = pltpu.roll(xe, 1, 0)
    pair = cw_ref[1:2, :] * xe + cw_ref[0:1, :] * x1
    acc = (cb_ref[...] + cw_ref[3:4, :] * x + cw_ref[2:3, :] * x1[SUBLANES:, :]
           + pltpu.roll(pair, 2, 0)[SUBLANES:, :])
    prev_sc[...] = x[tb - SUBLANES:, :]
    return _silu(acc)


def _mlstm_proj_kernel(*refs, has_hist, blocks_per_seq):
    if has_hist:
        (xm_ref, he_ref, cw_ref, cb_ref, wqk_ref, wvo_ref, wg_ref, bg_ref,
         q_ref, k_ref, v_ref, o_ref, gates_ref, xe_sc) = refs
    else:
        (xm_ref, cw_ref, cb_ref, wqk_ref, wvo_ref, wg_ref, bg_ref,
         q_ref, k_ref, v_ref, o_ref, gates_ref, xe_sc) = refs
    i = pl.program_id(0)
    x = xm_ref[...]
    xin = x + he_ref[...] if has_hist else x
    xmc = _conv_rows(xin, i % blocks_per_seq == 0, xe_sc, cw_ref, cb_ref)
    tb = x.shape[0]
    gates = jnp.broadcast_to(bg_ref[...], (tb, LANES))
    for h in range(M_HEADS):
        cols = slice(h * M_DH, (h + 1) * M_DH)
        qk = _dot(xmc[:, cols].astype(BF16), wqk_ref[h])
        vo = _dot(x[:, cols].astype(BF16), wvo_ref[h])
        qb = qk[:, :M_DH].astype(BF16)
        kb = qk[:, M_DH:].astype(BF16)
        vb = vo[:, :M_DH].astype(BF16)
        gates = gates + _dot(qb, wg_ref[h, 0]) + _dot(kb, wg_ref[h, 1]) + _dot(vb, wg_ref[h, 2])
        q_ref[:, cols] = qb
        k_ref[:, cols] = (qk[:, M_DH:] * (M_DH ** -0.5)).astype(BF16)
        v_ref[:, cols] = vb
        o_ref[:, cols] = vo[:, M_DH:]
    lane = lax.broadcasted_iota(jnp.int32, (tb, LANES), 1)
    gates_ref[...] = jnp.where(lane < M_HEADS, gates, _log_sigmoid(gates))


def _mlstm_proj(l, proj, hist, W, blocks_per_seq, tb):
    rows = proj.shape[0]
    has_hist = hist is not None
    in_specs = [pl.BlockSpec((tb, M_WIDTH), lambda i: (i, COL_XM // M_WIDTH))]
    args = [proj]
    if has_hist:
        in_specs.append(pl.BlockSpec((None, tb, M_WIDTH), lambda i: (l, i, 0)))
        args.append(hist)
    in_specs += [
        pl.BlockSpec((None, CONV_K, M_WIDTH), lambda i: (l, 0, 0)),
        _vec_spec(l, M_WIDTH),
        pl.BlockSpec((None, M_HEADS, M_DH, 2 * M_DH), lambda i: (l, 0, 0, 0)),
        pl.BlockSpec((None, M_HEADS, M_DH, 2 * M_DH), lambda i: (l, 0, 0, 0)),
        pl.BlockSpec((None, M_HEADS, 3, M_DH, LANES), lambda i: (l, 0, 0, 0, 0)),
        _vec_spec(l, LANES),
    ]
    args += [W["m_cw"], W["m_cb"], W["wqk"], W["wvo"], W["wg"], W["bg"]]
    row_spec = pl.BlockSpec((tb, M_WIDTH), lambda i: (i, 0))
    return pl.pallas_call(
        functools.partial(_mlstm_proj_kernel, has_hist=has_hist, blocks_per_seq=blocks_per_seq),
        grid=(rows // tb,),
        in_specs=in_specs,
        out_specs=[row_spec, row_spec, row_spec, row_spec, pl.BlockSpec((tb, LANES), lambda i: (i, 0))],
        out_shape=[
            jax.ShapeDtypeStruct((rows, M_WIDTH), BF16),
            jax.ShapeDtypeStruct((rows, M_WIDTH), BF16),
            jax.ShapeDtypeStruct((rows, M_WIDTH), BF16),
            jax.ShapeDtypeStruct((rows, M_WIDTH), F32),
            jax.ShapeDtypeStruct((rows, LANES), F32),
        ],
        scratch_shapes=[pltpu.VMEM((SUBLANES, M_WIDTH), F32)],
        compiler_params=_params(("arbitrary",)),
        name="mlstm_proj",
    )(*args)


def _mlstm_finish(hh, o, bo, ng, zm):
    hm = _sigmoid(o + bo) * hh
    return _rms(hm, ng) * _silu(zm)


def _mlstm_prep(gates_ref):
    L = gates_ref.shape[0]
    r2 = lax.broadcasted_iota(jnp.int32, (L, L), 0)
    c2 = lax.broadcasted_iota(jnp.int32, (L, L), 1)
    causal = c2 <= r2
    g = gates_ref[...]
    b_all = _dot_mask(causal, g)
    return causal, g, b_all, g.T, b_all.T


def _mlstm_head(h, prep, q_ref, k_ref, v_ref, o_ref, zm_ref, bo_ref, ng_ref, c1_ref, n1_ref, m1_ref, emit):
    causal, g, b_all, g_t, b_t = prep
    L = q_ref.shape[0]
    neg_inf = -jnp.inf
    cols = slice(h * M_DH, (h + 1) * M_DH)
    f = M_HEADS + h
    g_row = g_t[h:h + 1, :] - b_t[f:f + 1, :]
    b_col = b_all[:, f:f + 1]
    g_col = g[:, h:h + 1] - b_col
    mprev = m1_ref[0, h:h + 1, 0:1]
    gm = jnp.where(causal, g_row, neg_inf)
    m_col = jnp.maximum(mprev, jnp.max(gm, axis=1, keepdims=True))
    w_intra = jnp.exp(gm - m_col)
    w_inter = jnp.exp(mprev - m_col)
    m_new = b_col + m_col
    m_last = m_col[L - 1:L, :]
    wl_col = jnp.exp(g_col - m_last)
    wli = jnp.exp(mprev - m_last)
    q = q_ref[:, cols]
    k = k_ref[:, cols]
    v = v_ref[:, cols]
    cst = c1_ref[0, h]
    nst = n1_ref[0, h]
    s = _dot_nt(q, k) * w_intra
    num = _dot(s.astype(BF16), v) + w_inter * _dot_nt(q, cst.astype(BF16))
    den = (jnp.sum(s, axis=1, keepdims=True)
           + w_inter * jnp.sum(q.astype(F32) * nst, axis=1, keepdims=True))
    hh = num / jnp.maximum(jnp.abs(den), jnp.exp(-m_new))
    emit(cols, _mlstm_finish(hh, o_ref[:, cols], bo_ref[:, cols], ng_ref[:, cols], zm_ref[:, cols]))

    def update_state():
        c1_ref[0, h] = wli * cst + _dot((v.astype(F32) * wl_col).T.astype(BF16), k)
        n1_ref[0, h] = wli * nst + jnp.sum(k.astype(F32) * wl_col, axis=0, keepdims=True)
        m1_ref[0, h:h + 1, :] = jnp.broadcast_to(m_new[L - 1:L, :], (1, LANES))

    return update_state


def _block_masks(L, rs, t_lo):
    shift = rs.bit_length() - 1
    r2 = lax.broadcasted_iota(jnp.int32, (L, L), 0)
    c2 = lax.broadcasted_iota(jnp.int32, (L, L), 1)
    same = (r2 >> shift) == (c2 >> shift)
    valid_c = (c2 & (rs - 1)) >= t_lo
    ridx = lax.broadcasted_iota(jnp.int32, (L, 1), 0)
    valid_r = (ridx & (rs - 1)) >= t_lo
    return r2, c2, same, valid_c, ridx, valid_r


def _mlstm_sample_kernel(q_ref, k_ref, v_ref, o_ref, zm_ref, gates_ref, bo_ref, ng_ref,
                         c0_ref, n0_ref, m0_ref, alias_ref,
                         out_ref, c1_ref, n1_ref, mrow_ref,
                         gt_sc, bt_sc, col_sc, q_sc, numi_sc, nrow_sc, wk_sc, *, nseq, t_lo):
    del alias_ref
    h = pl.program_id(1)
    L = q_ref.shape[0]
    rs = L // nseq
    r2, c2, same, valid_c, ridx, valid_r = _block_masks(L, rs, t_lo)
    eye = r2 == c2
    neg_inf = -jnp.inf
    g = gates_ref[...]
    b_all = _dot_mask(same & (c2 <= r2), jnp.where(valid_r, g, 0.0))
    gt_sc[...] = g.T
    bt_sc[...] = b_all.T
    lane_g = lax.broadcasted_iota(jnp.int32, (1, LANES), 1)
    i_col = jnp.sum(jnp.where(lane_g == h, g, 0.0), axis=1, keepdims=True)
    b_col = jnp.sum(jnp.where(lane_g == h + M_HEADS, b_all, 0.0), axis=1, keepdims=True)
    g_row = gt_sc[pl.ds(h, 1), :] - bt_sc[pl.ds(h + M_HEADS, 1), :]
    g_col = i_col - b_col
    mask = same & (c2 <= r2) & valid_c
    mprev = m0_ref[0]
    gm = jnp.where(mask, g_row, neg_inf)
    m_col = jnp.maximum(mprev, jnp.max(gm, axis=1, keepdims=True))
    m_row = jnp.sum(jnp.where(eye, m_col, 0.0), axis=0, keepdims=True)
    mlast_col = jnp.max(jnp.where(same, m_row, neg_inf), axis=1, keepdims=True)
    w_intra = jnp.exp(gm - m_col)
    w_inter = jnp.exp(mprev - m_col)
    m_new = b_col + m_col
    wl_col = jnp.where(valid_r, jnp.exp(g_col - mlast_col), 0.0)
    col_sc[:, 0:1] = jnp.exp(mprev - mlast_col)

    q = q_ref[...]
    k = k_ref[...]
    v = v_ref[...]
    s = _dot_nt(q, k) * w_intra
    num = _dot(s.astype(BF16), v)
    den = jnp.sum(s, axis=1, keepdims=True)
    wvt = (v.astype(F32) * wl_col).T.astype(BF16)
    wk_sc[...] = k.astype(F32) * wl_col
    qf = q.astype(F32)
    q_sc[0:L, :] = qf
    q_sc[L:L + SUBLANES, :] = jnp.zeros((SUBLANES, M_DH), F32)

    def seq_step(j, carry):
        rows = pl.ds(pl.multiple_of(j * rs, rs), rs)
        in_seq_r = (ridx >= j * rs) & (ridx < (j + 1) * rs)
        cj = c0_ref[j, 0]
        nj = n0_ref[j, 0]
        q2 = q_sc[pl.ds(pl.multiple_of(j * rs, rs), 2 * rs), :].astype(BF16)
        numi_sc[rows, :] = _dot_nt(q2, cj.astype(BF16))[:rs]
        nrow_sc[rows, :] = jnp.broadcast_to(nj, (rs, M_DH))
        wli = col_sc[pl.ds(j * rs + rs - 1, 1), 0:1]
        c1_ref[j, 0] = wli * cj + _dot(wvt, jnp.where(in_seq_r, k, jnp.zeros_like(k)))
        n1_ref[j, 0] = wli * nj + jnp.sum(wk_sc[rows, :], axis=0, keepdims=True)
        return carry

    lax.fori_loop(0, nseq, seq_step, 0, unroll=SEQ_UNROLL)
    num = num + w_inter * numi_sc[...]
    den = den + w_inter * jnp.sum(qf * nrow_sc[...], axis=1, keepdims=True)
    hh = num / jnp.maximum(jnp.abs(den), jnp.exp(-m_new))
    out_ref[...] = _mlstm_finish(hh, o_ref[...], bo_ref[...], ng_ref[...], zm_ref[...]).astype(BF16)
    mrow_ref[0] = m_new


def _mlstm_sample(l, depth, q, k, v, o, gates, proj, W, c0, n0, m0rows, c_prev, nseq, t_lo):
    L = CHUNK
    rows = q.shape[0]
    nb = rows // L
    nbatch = nb * nseq
    rowblk = lambda b, h: (b, h)
    c_spec = pl.BlockSpec((None, nseq, 1, M_DH, M_DH), lambda b, h: (l, b, h, 0, 0))
    n_spec_in = pl.BlockSpec((None, nseq, 1, 1, M_DH), lambda b, h: (l, b, h, 0, 0))
    n_spec_out = pl.BlockSpec((nseq, 1, 1, M_DH), lambda b, h: (b, h, 0, 0))
    has_prev = c_prev is not None
    in_specs = [
        pl.BlockSpec((L, M_DH), rowblk),
        pl.BlockSpec((L, M_DH), rowblk),
        pl.BlockSpec((L, M_DH), rowblk),
        pl.BlockSpec((L, M_DH), rowblk),
        pl.BlockSpec((L, M_DH), lambda b, h: (b, COL_ZM // M_DH + h)),
        pl.BlockSpec((L, LANES), lambda b, h: (b, 0)),
        _vec_spec(l, M_DH, lambda b, h: h),
        _vec_spec(l, M_DH, lambda b, h: h),
        c_spec,
        n_spec_in,
        pl.BlockSpec((None, 1, L, 1), lambda b, h: (l, h, b, 0)),
        pl.BlockSpec(memory_space=pl.ANY),
    ]
    args = [q, k, v, o, proj, gates, W["bo"], W["m_ng"], c0, n0, m0rows,
            c_prev if has_prev else jnp.zeros((SUBLANES, LANES), F32)]
    return pl.pallas_call(
        functools.partial(_mlstm_sample_kernel, nseq=nseq, t_lo=t_lo),
        grid=(nb, M_HEADS),
        in_specs=in_specs,
        out_specs=[pl.BlockSpec((L, M_DH), rowblk), c_spec, n_spec_out,
                   pl.BlockSpec((1, L, 1), lambda b, h: (h, b, 0))],
        out_shape=[
            jax.ShapeDtypeStruct((rows, M_WIDTH), BF16),
            jax.ShapeDtypeStruct((depth, nbatch, M_HEADS, M_DH, M_DH), F32),
            jax.ShapeDtypeStruct((nbatch, M_HEADS, 1, M_DH), F32),
            jax.ShapeDtypeStruct((M_HEADS, rows, 1), F32),
        ],
        scratch_shapes=[
            pltpu.VMEM((LANES, L), F32),
            pltpu.VMEM((LANES, L), F32),
            pltpu.VMEM((L, LANES), F32),
            pltpu.VMEM((L + SUBLANES, M_DH), F32),
            pltpu.VMEM((L, M_DH), F32),
            pltpu.VMEM((L, M_DH), F32),
            pltpu.VMEM((L, M_DH), F32),
        ],
        input_output_aliases={11: 1} if has_prev else {},
        compiler_params=_params(("arbitrary", "arbitrary")),
        name="mlstm_sample",
    )(*args)


def _ssd_prompt_kernel(xs_ref, b_ref, c_ref, zs_ref, h_ref, wdt_ref,
                       cwx_ref, cwb_ref, cwc_ref, cbx_ref, cbb_ref, cbc_ref,
                       dtb_ref, alog_ref, dskip_ref, ng_ref, expand_ref,
                       out_ref, h1_ref, xex_sc, xeb_sc, xec_sc, ht_sc):
    c = pl.program_id(1)
    nc = pl.num_programs(1)
    L = CHUNK

    @pl.when(c == 0)
    def _():
        ht_sc[...] = jnp.zeros(ht_sc.shape, F32)

    for sub in range(xs_ref.shape[0] // L):
        _ssd_prompt_chunk(slice(sub * L, (sub + 1) * L), (c == 0) if sub == 0 else False,
                          xs_ref, b_ref, c_ref, zs_ref, h_ref, wdt_ref,
                          cwx_ref, cwb_ref, cwc_ref, cbx_ref, cbb_ref, cbc_ref,
                          dtb_ref, alog_ref, dskip_ref, ng_ref, expand_ref,
                          out_ref, xex_sc, xeb_sc, xec_sc, ht_sc)

    @pl.when(c == nc - 1)
    def _():
        for pr in range(S_HEADS // 2):
            blk = ht_sc[:, pr * LANES:(pr + 1) * LANES].T
            h1_ref[0, 2 * pr] = blk[:S_DH]
            h1_ref[0, 2 * pr + 1] = blk[S_DH:]


def _ssd_prompt_chunk(rows, first, xs_ref, b_ref, c_ref, zs_ref, h_ref, wdt_ref,
                      cwx_ref, cwb_ref, cwc_ref, cbx_ref, cbb_ref, cbc_ref,
                      dtb_ref, alog_ref, dskip_ref, ng_ref, expand_ref,
                      out_ref, xex_sc, xeb_sc, xec_sc, ht_sc):
    L = CHUNK
    xs = _conv_rows(xs_ref[rows, :], first, xex_sc, cwx_ref, cbx_ref)
    bm = _conv_rows(b_ref[rows, :], first, xeb_sc, cwb_ref, cbb_ref)
    cm = _conv_rows(c_ref[rows, :], first, xec_sc, cwc_ref, cbc_ref).astype(BF16)

    r2 = lax.broadcasted_iota(jnp.int32, (L, L), 0)
    c2 = lax.broadcasted_iota(jnp.int32, (L, L), 1)
    causal = c2 <= r2
    neg_inf = -jnp.inf
    dt = _softplus(_dot(h_ref[rows, :], wdt_ref[...]) + dtb_ref[...])
    da = dt * (-jnp.exp(alog_ref[...]))
    cs = _dot_mask(causal, da)
    cs_t = cs.T
    ecs = jnp.exp(cs)
    wend = jnp.exp(cs[L - 1:L, :] - cs) * dt

    expand = expand_ref[...]
    dte = _widen(dt, expand)
    wende = _widen(wend, expand)
    ecse = _widen(ecs, expand)
    xdt = (xs * dte).astype(BF16)
    wx = (xs * wende).astype(BF16)
    lane = lax.broadcasted_iota(jnp.int32, (1, LANES), 1)
    low_half = lane < S_DH
    zero_slab = jnp.zeros((L, LANES), BF16)
    for g in range(S_GROUPS):
        gcols = slice(g * S_GW, (g + 1) * S_GW)
        scols = slice(g * S_STATE, (g + 1) * S_STATE)
        bg = bm[:, scols]
        cg = cm[:, scols]
        cb = _dot_nt(cg, bg.astype(BF16))
        pairs = []
        for pr in range(S_HPG // 2):
            h0 = g * S_HPG + 2 * pr
            mixes = []
            for hh in (h0, h0 + 1):
                dec = jnp.exp(jnp.where(causal, cs[:, hh:hh + 1] - cs_t[hh:hh + 1, :], neg_inf))
                mixes.append((cb * dec).astype(BF16))
            slab = xdt[:, h0 * S_DH:(h0 + 2) * S_DH]
            rhs = jnp.concatenate([jnp.where(low_half, slab, zero_slab),
                                   jnp.where(low_half, zero_slab, slab)], axis=0)
            pairs.append(_dot(jnp.concatenate(mixes, axis=1), rhs))
        y_intra = jnp.concatenate(pairs, axis=1)
        ht = ht_sc[:, gcols]
        y = y_intra + ecse[:, gcols] * _dot(cg, ht.astype(BF16)) + dskip_ref[:, gcols] * xs[:, gcols]
        y = y * _silu(zs_ref[rows, gcols])
        out_ref[rows, gcols] = _rms(y, ng_ref[:, gcols]).astype(BF16)
        ht_sc[:, gcols] = ecse[L - 1:L, gcols] * ht + _dot(bg.T.astype(BF16), wx[:, gcols])


def _ssd_prompt(l, proj, h, W, nb, nc):
    L = CHUNK * SSD_SUB
    rows = nb * nc * L
    row = lambda b, c: b * nc + c
    conv_w = lambda width, blk: pl.BlockSpec((None, CONV_K, width), lambda b, c: (l, 0, blk))
    return pl.pallas_call(
        _ssd_prompt_kernel,
        grid=(nb, nc),
        in_specs=[
            pl.BlockSpec((L, S_WIDTH), lambda b, c: (row(b, c), COL_XBC // S_WIDTH)),
            pl.BlockSpec((L, S_BC), lambda b, c: (row(b, c), COL_B // S_BC)),
            pl.BlockSpec((L, S_BC), lambda b, c: (row(b, c), COL_C // S_BC)),
            pl.BlockSpec((L, S_WIDTH), lambda b, c: (row(b, c), COL_ZS // S_WIDTH)),
            pl.BlockSpec((L, D_MODEL), lambda b, c: (row(b, c), 0)),
            pl.BlockSpec((None, D_MODEL, LANES), lambda b, c: (l, 0, 0)),
            conv_w(S_WIDTH, 0), conv_w(S_BC, S_WIDTH // S_BC), conv_w(S_BC, S_WIDTH // S_BC + 1),
            _vec_spec(l, S_WIDTH, lambda b, c: 0),
            _vec_spec(l, S_BC, lambda b, c: S_WIDTH // S_BC),
            _vec_spec(l, S_BC, lambda b, c: S_WIDTH // S_BC + 1),
            _vec_spec(l, LANES),
            _vec_spec(l, LANES),
            _vec_spec(l, S_WIDTH),
            _vec_spec(l, S_WIDTH),
            pl.BlockSpec((2 * LANES, S_WIDTH), lambda b, c: (0, 0)),
        ],
        out_specs=[
            pl.BlockSpec((L, S_WIDTH), lambda b, c: (row(b, c), 0)),
            pl.BlockSpec((1, S_HEADS, S_DH, S_STATE), lambda b, c: (b, 0, 0, 0)),
        ],
        out_shape=[
            jax.ShapeDtypeStruct((rows, S_WIDTH), BF16),
            jax.ShapeDtypeStruct((nb, S_HEADS, S_DH, S_STATE), F32),
        ],
        scratch_shapes=[
            pltpu.VMEM((SUBLANES, S_WIDTH), F32),
            pltpu.VMEM((SUBLANES, S_BC), F32),
            pltpu.VMEM((SUBLANES, S_BC), F32),
            pltpu.VMEM((S_STATE, S_WIDTH), F32),
        ],
        compiler_params=_params(("arbitrary", "arbitrary")),
        name="ssd_prompt",
    )(proj, proj, proj, proj, h, W["w_dt_c"], W["s_cw"], W["s_cw"], W["s_cw"], W["s_cb"], W["s_cb"], W["s_cb"],
      W["dtb_c"], W["alog_c"], W["dskip_wide"], W["s_ng"], W["expand"])


def _ssd_sample_kernel(xs_ref, b_ref, c_ref, zs_ref, h_ref, wdt_ref, hx_ref, hb_ref, hc_ref,
                       cwx_ref, cwb_ref, cwc_ref, cbx_ref, cbb_ref, cbc_ref,
                       dtb_ref, alog_ref, dskip_ref, ng_ref, expand_ref, h0_ref, alias_ref,
                       out_ref, h1_ref, xex_sc, xeb_sc, xec_sc, tot_sc, cm_sc, yi_sc, *, nseq, t_lo):
    del alias_ref
    L = xs_ref.shape[0]
    rs = L // nseq
    first = True
    xs = _conv_rows(xs_ref[...] + hx_ref[...], first, xex_sc, cwx_ref, cbx_ref)
    bm = _conv_rows(b_ref[...] + hb_ref[...], first, xeb_sc, cwb_ref, cbb_ref)
    cm = _conv_rows(c_ref[...] + hc_ref[...], first, xec_sc, cwc_ref, cbc_ref)
    bmb = bm.astype(BF16)
    cmb = cm.astype(BF16)
    cm_sc[0:L, :] = cm
    cm_sc[L:L + SUBLANES, :] = jnp.zeros((SUBLANES, S_STATE), F32)

    r2, c2, same, valid_c, ridx, valid_r = _block_masks(L, rs, t_lo)
    mask = same & (c2 <= r2) & valid_c
    neg_inf = -jnp.inf
    dt = jnp.where(valid_r, _softplus(_dot(h_ref[...], wdt_ref[...]) + dtb_ref[...]), 0.0)
    da = dt * (-jnp.exp(alog_ref[...]))
    cs = _dot_mask(mask, da)
    sfx = _dot_mask(same & (c2 > r2), da)
    tot_sc[...] = cs + sfx
    cs_t = cs.T
    expand = expand_ref[...]
    dte = _widen(dt, expand)
    wende = _widen(jnp.exp(sfx) * dt, expand)
    ecse = _widen(jnp.exp(cs), expand)
    xdt = (xs * dte).astype(BF16)
    wxt = (xs * wende).T.astype(BF16)
    cb = _dot_nt(cmb, bmb)
    lane = lax.broadcasted_iota(jnp.int32, (1, LANES), 1)
    low_half = lane < S_DH
    zero_slab = jnp.zeros((L, LANES), BF16)
    pairs = []
    for pr in range(S_HPG // 2):
        mixes = []
        for hh in (2 * pr, 2 * pr + 1):
            dec = jnp.exp(jnp.where(mask, cs[:, hh:hh + 1] - cs_t[hh:hh + 1, :], neg_inf))
            mixes.append((cb * dec).astype(BF16))
        slab = xdt[:, pr * LANES:(pr + 1) * LANES]
        rhs = jnp.concatenate([jnp.where(low_half, slab, zero_slab),
                               jnp.where(low_half, zero_slab, slab)], axis=0)
        pairs.append(_dot(jnp.concatenate(mixes, axis=1), rhs))
    y_intra = jnp.concatenate(pairs, axis=1)

    def seq_step(j, carry):
        rows = pl.ds(pl.multiple_of(j * rs, rs), rs)
        in_seq_r = (ridx >= j * rs) & (ridx < (j + 1) * rs)
        hj = h0_ref[j]
        c2rows = cm_sc[pl.ds(pl.multiple_of(j * rs, rs), 2 * rs), :].astype(BF16)
        yi_sc[rows, :] = _dot_nt(c2rows, hj.reshape(S_GW, S_STATE).astype(BF16))[:rs]
        upd = _dot(wxt, jnp.where(in_seq_r, bmb, jnp.zeros_like(bmb)))
        dec_j = jnp.exp(tot_sc[pl.ds(j * rs + rs - 1, 1), :])
        for hh in range(S_HPG):
            h1_ref[j, hh] = dec_j[:, hh:hh + 1] * hj[hh] + upd[hh * S_DH:(hh + 1) * S_DH, :]
        return carry

    lax.fori_loop(0, nseq, seq_step, 0, unroll=SEQ_UNROLL)
    y = y_intra + ecse * yi_sc[...] + dskip_ref[...] * xs
    out_ref[...] = _rms(y * _silu(zs_ref[...]), ng_ref[...]).astype(BF16)


def _ssd_sample(l, depth, proj, h, hist, W, h0, h_prev, nseq, t_lo):
    L = CHUNK
    rows = proj.shape[0]
    nb = rows // L
    xblk = COL_XBC // S_GW
    bblk = COL_B // S_STATE
    cblk = COL_C // S_STATE
    hb_blk = S_WIDTH // S_STATE
    hc_blk = hb_blk + S_GROUPS
    has_prev = h_prev is not None
    conv_w = lambda width, blk: pl.BlockSpec((None, CONV_K, width), lambda b, g: (l, 0, blk(g)))
    h_spec = pl.BlockSpec((None, nseq, S_HPG, S_DH, S_STATE), lambda b, g: (l, b, g, 0, 0))
    in_specs = [
        pl.BlockSpec((L, S_GW), lambda b, g: (b, xblk + g)),
        pl.BlockSpec((L, S_STATE), lambda b, g: (b, bblk + g)),
        pl.BlockSpec((L, S_STATE), lambda b, g: (b, cblk + g)),
        pl.BlockSpec((L, S_GW), lambda b, g: (b, COL_ZS // S_GW + g)),
        pl.BlockSpec((L, D_MODEL), lambda b, g: (b, 0)),
        pl.BlockSpec((None, D_MODEL, LANES), lambda b, g: (l, 0, g)),
        pl.BlockSpec((None, L, S_GW), lambda b, g: (l, b, g)),
        pl.BlockSpec((None, L, S_STATE), lambda b, g: (l, b, hb_blk + g)),
        pl.BlockSpec((None, L, S_STATE), lambda b, g: (l, b, hc_blk + g)),
        conv_w(S_GW, lambda g: g), conv_w(S_STATE, lambda g: hb_blk + g), conv_w(S_STATE, lambda g: hc_blk + g),
        _vec_spec(l, S_GW, lambda b, g: g),
        _vec_spec(l, S_STATE, lambda b, g: hb_blk + g),
        _vec_spec(l, S_STATE, lambda b, g: hc_blk + g),
        _vec_spec(l, LANES, lambda b, g: g),
        _vec_spec(l, LANES, lambda b, g: g),
        _vec_spec(l, S_GW, lambda b, g: g),
        _vec_spec(l, S_GW, lambda b, g: g),
        pl.BlockSpec((2 * LANES, S_GW), lambda b, g: (0, 0)),
        h_spec,
        pl.BlockSpec(memory_space=pl.ANY),
    ]
    args = [proj, proj, proj, proj, h, W["w_dt_g"], hist, hist, hist,
            W["s_cw"], W["s_cw"], W["s_cw"], W["s_cb"], W["s_cb"], W["s_cb"],
            W["dtb_g"], W["alog_g"], W["dskip_wide"], W["s_ng"], W["expand"], h0,
            h_prev if has_prev else jnp.zeros((SUBLANES, LANES), F32)]
    return pl.pallas_call(
        functools.partial(_ssd_sample_kernel, nseq=nseq, t_lo=t_lo),
        grid=(nb, S_GROUPS),
        in_specs=in_specs,
        out_specs=[pl.BlockSpec((L, S_GW), lambda b, g: (b, g)), h_spec],
        out_shape=[
            jax.ShapeDtypeStruct((rows, S_WIDTH), BF16),
            jax.ShapeDtypeStruct((depth, nb * nseq, S_HEADS, S_DH, S_STATE), F32),
        ],
        scratch_shapes=[
            pltpu.VMEM((SUBLANES, S_GW), F32),
            pltpu.VMEM((SUBLANES, S_STATE), F32),
            pltpu.VMEM((SUBLANES, S_STATE), F32),
            pltpu.VMEM((L, LANES), F32),
            pltpu.VMEM((L + SUBLANES, S_STATE), F32),
            pltpu.VMEM((L, S_GW), F32),
        ],
        input_output_aliases={21: 1} if has_prev else {},
        compiler_params=_params(("arbitrary", "arbitrary")),
        name="ssd_sample",
    )(*args)


def _outproj_body(x_ref, om, os_ref, u_ref, v_ref, zc_ref, wmix_ref, bias_ref, gv_ref,
                  w0_ref, w1_ref, w2_ref, w3_ref, g_ref, y_ref, h_ref, vn_ref, oc_sc, *, final, rs, t_lo,
                  between=None):
    emit_vn = vn_ref is not None
    if between is None:
        between = lambda k: None
    L = C_CHUNK
    r2, c2, same, valid_c, _, _ = _block_masks(L, min(rs, L), t_lo)
    mask = same & (c2 <= r2) & valid_c
    for g in range(C_GROUPS):
        cols = slice(g * C_DG, (g + 1) * C_DG)
        w = jnp.where(mask, wmix_ref[g], 0.0).astype(BF16)
        for ch in range(x_ref.shape[0] // L):
            rows = slice(ch * L, (ch + 1) * L)
            vn = _rms(v_ref[rows, cols], gv_ref[:, cols])
            if emit_vn:
                vn_ref[rows, cols] = vn
            mixed = _dot(w, vn.astype(BF16)) + bias_ref[g]
            oc_sc[rows, cols] = (u_ref[rows, cols] * mixed * _silu(zc_ref[rows, cols])).astype(BF16)
    kb = w0_ref.shape[0]
    hk = kb // 2
    lhs = (lambda c: om[:, c],
           lambda c: os_ref[:, c].astype(BF16),
           lambda c: os_ref[:, slice(kb + c.start, kb + c.stop)].astype(BF16),
           lambda c: oc_sc[:, c])
    acc = None
    for blk, w_ref in enumerate((w0_ref, w1_ref, w2_ref, w3_ref)):
        for part in range(2):
            c = slice(part * hk, (part + 1) * hk)
            d = _dot(lhs[blk](c), w_ref[c, :])
            acc = d if acc is None else acc + d
            between(2 * blk + part)
    y = x_ref[...] + acc
    if t_lo:
        ridx = lax.broadcasted_iota(jnp.int32, (y.shape[0], 1), 0)
        y = jnp.where((ridx & (rs - 1)) >= t_lo, y, 0.0)
    if final:
        y_ref[...] = _rms(y, g_ref[...])
    else:
        y_ref[...] = y
        h_ref[...] = _rms(y, g_ref[...]).astype(BF16)


def _outproj_kernel(*refs, final, emit_vn, rs, t_lo):
    (x_ref, om_ref, os_ref, u_ref, v_ref, zc_ref, wmix_ref, bias_ref, gv_ref,
     w0_ref, w1_ref, w2_ref, w3_ref, g_ref, y_ref) = refs[:15]
    rest = list(refs[15:])
    h_ref = None if final else rest.pop(0)
    vn_ref = rest.pop(0) if emit_vn else None
    (oc_sc,) = rest
    _outproj_body(x_ref, om_ref[...].astype(BF16), os_ref, u_ref, v_ref, zc_ref, wmix_ref, bias_ref, gv_ref,
                  w0_ref, w1_ref, w2_ref, w3_ref, g_ref, y_ref, h_ref, vn_ref, oc_sc,
                  final=final, rs=rs, t_lo=t_lo)


def _mlstm_out_kernel(*refs, final, nc):
    (q_ref, k_ref, v_ref, o_ref, zm_ref, gates_ref, bo_ref, ng_ref,
     x_ref, os_ref, u_ref, v2_ref, zc_ref, wmix_ref, bias_ref, gv_ref,
     w0_ref, w1_ref, w2_ref, w3_ref, g_ref, c1_ref, n1_ref, m1_ref, y_ref) = refs[:25]
    rest = list(refs[25:])
    h_ref = None if final else rest.pop(0)
    om_sc, oc_sc = rest
    i = pl.program_id(0)

    @pl.when(i == 0)
    def _():
        om_sc[...] = jnp.zeros(om_sc.shape, BF16)

    @pl.when(i % nc == 0)
    def _():
        c1_ref[...] = jnp.zeros(c1_ref.shape, F32)
        n1_ref[...] = jnp.zeros(n1_ref.shape, F32)
        m1_ref[...] = jnp.zeros(m1_ref.shape, F32)

    def emit(cols, val):
        om_sc[i % 2, :, cols] = val.astype(BF16)

    prep = _mlstm_prep(gates_ref)

    pending = {}

    def head(slot):
        h, second = divmod(slot, 2)
        if second:
            pending.pop(h)()
        else:
            pending[h] = _mlstm_head(h, prep, q_ref, k_ref, v_ref, o_ref, zm_ref, bo_ref, ng_ref,
                                     c1_ref, n1_ref, m1_ref, emit)

    _outproj_body(x_ref, om_sc[(i + 1) % 2], os_ref, u_ref, v2_ref, zc_ref, wmix_ref, bias_ref, gv_ref,
                  w0_ref, w1_ref, w2_ref, w3_ref, g_ref, y_ref, h_ref, None, oc_sc,
                  final=final, rs=CHUNK, t_lo=0, between=head)


def _mlstm_out(l, q, k, v, o, gates, proj, x, os_, wmix, bias, W, g, final, nb, nc):
    L = M_CHUNK
    nt = nb * nc
    rows = nt * L
    cur = lambda i: jnp.minimum(i, nt - 1)
    prev = lambda i: jnp.maximum(i - 1, 0)
    cur_spec = pl.BlockSpec((L, M_WIDTH), lambda i: (cur(i), 0))
    prev_spec = pl.BlockSpec((L, D_MODEL), lambda i: (prev(i), 0))
    wspec = lambda r: pl.BlockSpec((None, M_WIDTH, D_MODEL), lambda i: (l, r, 0), pipeline_mode=pl.Buffered(1))
    out_shape = [
        jax.ShapeDtypeStruct((nb + 1, M_HEADS, M_DH, M_DH), F32),
        jax.ShapeDtypeStruct((nb + 1, M_HEADS, 1, M_DH), F32),
        jax.ShapeDtypeStruct((nb + 1, SUBLANES, LANES), F32),
        jax.ShapeDtypeStruct((rows, D_MODEL), F32),
    ]
    out_specs = [
        pl.BlockSpec((1, M_HEADS, M_DH, M_DH), lambda i: (i // nc, 0, 0, 0)),
        pl.BlockSpec((1, M_HEADS, 1, M_DH), lambda i: (i // nc, 0, 0, 0)),
        pl.BlockSpec((1, SUBLANES, LANES), lambda i: (i // nc, 0, 0)),
        prev_spec,
    ]
    if not final:
        out_shape.append(jax.ShapeDtypeStruct((rows, D_MODEL), BF16))
        out_specs.append(prev_spec)
    return pl.pallas_call(
        functools.partial(_mlstm_out_kernel, final=final, nc=nc),
        grid=(nt + 1,),
        in_specs=[
            cur_spec, cur_spec, cur_spec, cur_spec,
            pl.BlockSpec((L, M_WIDTH), lambda i: (cur(i), COL_ZM // M_WIDTH)),
            pl.BlockSpec((L, LANES), lambda i: (cur(i), 0)),
            _vec_spec(l, M_WIDTH),
            _vec_spec(l, M_WIDTH),
            prev_spec,
            prev_spec,
            pl.BlockSpec((L, C_WIDTH), lambda i: (prev(i), COL_U // C_WIDTH)),
            pl.BlockSpec((L, C_WIDTH), lambda i: (prev(i), COL_V // C_WIDTH)),
            pl.BlockSpec((L, C_WIDTH), lambda i: (prev(i), COL_ZC // C_WIDTH)),
            pl.BlockSpec((None, C_GROUPS, C_CHUNK, C_CHUNK), lambda i: (l, 0, 0, 0)),
            pl.BlockSpec((None, C_GROUPS, C_CHUNK, 1), lambda i: (l, 0, 0, 0)),
            _vec_spec(l, C_WIDTH),
            wspec(0), wspec(1), wspec(2), wspec(3),
            pl.BlockSpec((1, D_MODEL), lambda i: (0, 0)) if final else _vec_spec(l + 1, D_MODEL),
        ],
        out_specs=out_specs,
        out_shape=out_shape,
        scratch_shapes=[pltpu.VMEM((2, L, M_WIDTH), BF16), pltpu.VMEM((L, C_WIDTH), BF16)],
        compiler_params=_params(("arbitrary",)),
        name="mlstm_out",
    )(q, k, v, o, proj, gates, W["bo"], W["m_ng"], x, os_, proj, proj, proj, wmix, bias, W["gv"],
      W["wo"], W["wo"], W["wo"], W["wo"], g)


def _outproj(l, x, om, os_, proj, wmix, bias, gv, wo, g, final, emit_vn, rs, t_lo):
    rows = x.shape[0]
    tm = min(M_CHUNK, rows)
    kb = M_WIDTH
    wspec = lambda r: pl.BlockSpec((None, kb, D_MODEL), lambda i: (l, r, 0), pipeline_mode=pl.Buffered(1))
    row_spec = pl.BlockSpec((tm, D_MODEL), lambda i: (i, 0))
    out_shape = [jax.ShapeDtypeStruct((rows, D_MODEL), F32)]
    out_specs = [row_spec]
    if not final:
        out_shape.append(jax.ShapeDtypeStruct((rows, D_MODEL), BF16))
        out_specs.append(row_spec)
    if emit_vn:
        out_shape.append(jax.ShapeDtypeStruct((rows, C_WIDTH), F32))
        out_specs.append(pl.BlockSpec((tm, C_WIDTH), lambda i: (i, 0)))
    return pl.pallas_call(
        functools.partial(_outproj_kernel, final=final, emit_vn=emit_vn, rs=rs, t_lo=t_lo),
        grid=(rows // tm,),
        in_specs=[
            row_spec,
            pl.BlockSpec((tm, M_WIDTH), lambda i: (i, 0)),
            pl.BlockSpec((tm, S_WIDTH), lambda i: (i, 0)),
            pl.BlockSpec((tm, C_WIDTH), lambda i: (i, COL_U // C_WIDTH)),
            pl.BlockSpec((tm, C_WIDTH), lambda i: (i, COL_V // C_WIDTH)),
            pl.BlockSpec((tm, C_WIDTH), lambda i: (i, COL_ZC // C_WIDTH)),
            pl.BlockSpec((None, C_GROUPS, C_CHUNK, C_CHUNK), lambda i: (l, 0, 0, 0)),
            pl.BlockSpec((None, C_GROUPS, C_CHUNK, 1), lambda i: (l, 0, 0, 0)),
            _vec_spec(l, C_WIDTH),
            wspec(0), wspec(1), wspec(2), wspec(3),
            pl.BlockSpec((1, D_MODEL), lambda i: (0, 0)) if final else _vec_spec(l + 1, D_MODEL),
        ],
        out_specs=out_specs,
        out_shape=out_shape,
        scratch_shapes=[pltpu.VMEM((tm, C_WIDTH), BF16)],
        compiler_params=_params(("arbitrary",)),
        name="outproj",
    )(x, om, os_, proj, proj, proj, wmix, bias, gv, wo, wo, wo, wo, g)


def _heads_compact(a):
    return jnp.pad(a, ((0, 0), (0, LANES - S_HEADS)))[:, None, :]


def _heads_grouped(a):
    a = a.reshape(a.shape[0], S_GROUPS, S_HPG)
    return jnp.pad(a, ((0, 0), (0, 0), (0, LANES - S_HPG))).reshape(a.shape[0], 1, S_GROUPS * LANES)


def _prepare_weights(norm_g, w_in, m_conv_w, m_conv_b, m_w_qk, m_w_vo, m_b_o, m_w_gate, m_b_gate,
                     m_norm_g, s_conv_w, s_conv_b, s_dt_bias, s_A_log, s_D, s_norm_g, c_v_norm_g,
                     c_w_s, c_b_s, w_out):
    depth = w_in.shape[0]
    vec = lambda a: a.reshape(depth, 1, -1)
    dt0 = PROJ_HEAD
    w_all = w_in.astype(BF16)
    w_dt = w_in[:, :, dt0:dt0 + S_HEADS]
    w_dt_g = jnp.pad(w_dt.reshape(depth, D_MODEL, S_GROUPS, S_HPG),
                     ((0, 0), (0, 0), (0, 0), (0, LANES - S_HPG))).reshape(depth, D_MODEL, -1).astype(BF16)
    w_dt_c = jnp.pad(w_dt, ((0, 0), (0, 0), (0, LANES - S_HEADS))).astype(BF16)
    wg = m_w_gate.reshape(depth, M_HEADS, 3, M_DH, 2 * M_HEADS)
    wg = jnp.pad(wg, ((0, 0),) * 4 + ((0, LANES - 2 * M_HEADS),)).astype(BF16)
    head_of_lane = jnp.arange(S_WIDTH) // S_DH
    expand = (jnp.arange(LANES)[:, None] == head_of_lane[None, :]).astype(BF16)
    expand = jnp.concatenate([expand, expand], axis=0)
    return dict(
        norm_g=vec(norm_g), w_all=w_all, w_tail=w_all[:, :, dt0 + S_HEADS:], w_dt_c=w_dt_c, w_dt_g=w_dt_g,
        m_cw=m_conv_w, m_cb=vec(m_conv_b),
        wqk=m_w_qk.astype(BF16), wvo=m_w_vo.astype(BF16), wg=wg,
        bg=vec(jnp.pad(m_b_gate, ((0, 0), (0, LANES - 2 * M_HEADS)))),
        bo=vec(m_b_o), m_ng=vec(m_norm_g),
        s_cw=s_conv_w, s_cb=vec(s_conv_b),
        dtb_c=_heads_compact(s_dt_bias), alog_c=_heads_compact(s_A_log),
        dtb_g=_heads_grouped(s_dt_bias), alog_g=_heads_grouped(s_A_log),
        dskip_wide=vec(jnp.repeat(s_D, S_DH, axis=1)), s_ng=vec(s_norm_g), expand=expand,
        gv=vec(c_v_norm_g), wo=w_out.astype(BF16),
    )


def kernel(x_prompt, x_sample, state_mlstm_C, state_mlstm_n, state_mlstm_m, state_mlstm_conv, state_ssm, state_ssm_conv, norm_g, w_in, m_conv_w, m_conv_b, m_w_qk, m_w_vo, m_b_o, m_w_gate, m_b_gate, m_norm_g, s_conv_w, s_conv_b, s_dt_bias, s_A_log, s_D, s_norm_g, c_v_norm_g, c_w_s, c_b_s, w_out, final_norm_g):
    bp, seq, _ = x_prompt.shape
    bs, dec_seq, _ = x_sample.shape
    depth = w_in.shape[0]
    t_lo = SAMPLE_ROWS - dec_seq
    hist_lo = t_lo - (CONV_K - 1)
    nseq = CHUNK // SAMPLE_ROWS
    W = _prepare_weights(norm_g, w_in, m_conv_w, m_conv_b, m_w_qk, m_w_vo, m_b_o, m_w_gate, m_b_gate,
                         m_norm_g, s_conv_w, s_conv_b, s_dt_bias, s_A_log, s_D, s_norm_g, c_v_norm_g,
                         c_w_s, c_b_s, w_out)
    fg = final_norm_g[None, :]
    yp = x_prompt.reshape(bp * seq, D_MODEL)
    ys = jnp.pad(x_sample, ((0, 0), (t_lo, 0), (0, 0))).reshape(bs * SAMPLE_ROWS, D_MODEL)

    pad_hist = ((0, 0), (0, 0), (hist_lo, dec_seq), (0, 0))
    hist_m = jnp.pad(state_mlstm_conv, pad_hist).reshape(depth, bs * SAMPLE_ROWS, M_WIDTH)
    hist_s = jnp.pad(state_ssm_conv, pad_hist).reshape(depth, bs * SAMPLE_ROWS, -1)
    m0rows = jnp.repeat(jnp.swapaxes(state_mlstm_m, 1, 2)[..., None], SAMPLE_ROWS, axis=2)
    n0 = state_mlstm_n[:, :, :, None, :]
    reps = C_CHUNK // SAMPLE_ROWS
    w4 = jnp.pad(c_w_s[:, :, :dec_seq, :dec_seq], ((0, 0), (0, 0), (t_lo, 0), (t_lo, 0)))
    b4 = jnp.pad(c_b_s[:, :, :dec_seq], ((0, 0), (0, 0), (t_lo, 0)))
    wmix_s = jnp.tile(w4, (1, 1, reps, reps))
    bias_s = jnp.tile(b4, (1, 1, reps))[..., None]
    bias_p = c_b_s[..., None]

    hp = _norm(0, yp, W["norm_g"])
    hs = _norm(0, ys, W["norm_g"])
    outs_p, outs_s = [], []
    c_all = h_all = None
    mc = min(M_CHUNK, seq)
    sc = min(CHUNK * SSD_SUB, seq)
    tbp = min(PROJ_ROWS, bp * seq)
    tbs = min(PROJ_ROWS, bs * SAMPLE_ROWS)
    for l in range(depth):
        final = l == depth - 1
        proj = _inproj(l, hp, W["w_all"], W["w_tail"])
        q, k, v, o, gates = _mlstm_proj(l, proj, None, W, max(1, seq // tbp), tbp)
        out_s, h1 = _ssd_prompt(l, proj, hp, W, bp, seq // sc)
        res = _mlstm_out(l, q, k, v, o, gates, proj, yp, out_s, c_w_s, bias_p, W,
                         fg if final else W["norm_g"], final, bp, seq // mc)
        c1, n1, m1 = res[0][:bp], res[1][:bp], res[2][:bp]
        yp, hp = res[3], (None if final else res[4])
        pj = proj.reshape(bp, seq, PROJ_MAIN)
        outs_p.append((
            c1, n1.reshape(bp, M_HEADS, M_DH), m1[:, :M_HEADS, 0],
            pj[:, seq - (CONV_K - 1):, COL_XM:COL_XM + M_WIDTH], h1,
            pj[:, seq - (CONV_K - 1):, COL_XBC:COL_XBC + S_WIDTH + 2 * S_BC]))
        proj = _inproj(l, hs, W["w_all"], W["w_tail"])
        q, k, v, o, gates = _mlstm_proj(l, proj, hist_m, W, 1, tbs)
        out_m, c_all, n1, mrow = _mlstm_sample(l, depth, q, k, v, o, gates, proj, W, state_mlstm_C, n0,
                                               m0rows, c_all, nseq, t_lo)
        out_s, h_all = _ssd_sample(l, depth, proj, hs, hist_s, W, state_ssm, h_all, nseq, t_lo)
        res = _outproj(l, ys, out_m, out_s, proj, wmix_s, bias_s, W["gv"], W["wo"],
                       fg if final else W["norm_g"], final, True, SAMPLE_ROWS, t_lo)
        ys, hs, vn = res[0], (None if final else res[1]), res[-1]
        pj = proj.reshape(bs, SAMPLE_ROWS, PROJ_MAIN)
        outs_s.append((
            n1.reshape(bs, M_HEADS, M_DH), mrow[:, SAMPLE_ROWS - 1::SAMPLE_ROWS, 0].T,
            pj[:, SAMPLE_ROWS - (CONV_K - 1):, COL_XM:COL_XM + M_WIDTH],
            pj[:, SAMPLE_ROWS - (CONV_K - 1):, COL_XBC:COL_XBC + S_WIDTH + 2 * S_BC],
            vn.reshape(bs, SAMPLE_ROWS, C_WIDTH)[:, t_lo:]))
    p_out = [jnp.stack([s[i] for s in outs_p]) for i in range(6)]
    s_n, s_m, s_mconv, s_sconv, s_cv = [jnp.stack([s[i] for s in outs_s]) for i in range(5)]
    y_prompt = yp.reshape(bp, seq, D_MODEL)
    y_sample = ys.reshape(bs, SAMPLE_ROWS, D_MODEL)[:, t_lo:]
    return (y_prompt, y_sample, *p_out, c_all, s_n, s_m, s_mconv, h_all, s_sconv, s_cv)
```

```python
import functools

import jax
import jax.numpy as jnp
from jax import lax
from jax.experimental import pallas as pl
from jax.experimental.pallas import tpu as pltpu

F32 = jnp.float32
BF16 = jnp.bfloat16

D_MODEL = 2048
MIX_WIDTH = 2 * D_MODEL
M_WIDTH = MIX_WIDTH // 4
M_HEADS = 4
M_DH = M_WIDTH // M_HEADS
S_WIDTH = MIX_WIDTH // 2
S_DH = 64
S_HEADS = S_WIDTH // S_DH
S_GROUPS = 4
S_HPG = S_HEADS // S_GROUPS
S_STATE = 128
S_GW = S_HPG * S_DH
S_BC = S_GROUPS * S_STATE
C_WIDTH = MIX_WIDTH // 4
C_GROUPS = 4
C_DG = C_WIDTH // C_GROUPS
C_CHUNK = 128
CONV_K = 4
EPS = 1e-6

LANES = 128
SUBLANES = 8
SAMPLE_ROWS = 8
CHUNK = 128
M_CHUNK = 256
SSD_SUB = 4
SEQ_UNROLL = 16
STATE_SLOTS = 3
NORM_ROWS = 1024
INPROJ_TILE = 1024
PROJ_ROWS = 512
VMEM_LIMIT = 56 * 1024 * 1024

COL_XM, COL_ZM, COL_ZS, COL_XBC, COL_U, COL_V, COL_ZC = 0, 1024, 2048, 4096, 7168, 8192, 9216
PROJ_MAIN = 10240
PROJ_HEAD = COL_U
COL_B = COL_XBC + S_WIDTH
COL_C = COL_B + S_BC


def _dot(a, b):
    return jnp.dot(a, b, preferred_element_type=F32)


def _dot_nt(a, b):
    return lax.dot_general(a, b, (((1,), (1,)), ((), ())), preferred_element_type=F32)


def _dot_mask(mask, x):
    m = jnp.where(mask, 1.0, 0.0).astype(BF16)
    hi = x.astype(BF16)
    r1 = x - hi.astype(F32)
    mid = r1.astype(BF16)
    lo = (r1 - mid.astype(F32)).astype(BF16)
    return _dot(m, hi) + _dot(m, mid) + _dot(m, lo)


def _widen(a, expand2):
    hi = a.astype(BF16)
    lo = (a - hi.astype(F32)).astype(BF16)
    return _dot(jnp.concatenate([hi, lo], axis=1), expand2)


def _sigmoid(x):
    return 0.5 * jnp.tanh(0.5 * x) + 0.5


def _silu(x):
    h = 0.5 * x
    return h * jnp.tanh(h) + h


def _softplus(x):
    return jnp.maximum(x, 0.0) + jnp.log1p(jnp.exp(-jnp.abs(x)))


def _log_sigmoid(x):
    return jnp.minimum(x, 0.0) - jnp.log1p(jnp.exp(-jnp.abs(x)))


def _rms(x, g):
    return x * lax.rsqrt(jnp.mean(x * x, axis=-1, keepdims=True) + EPS) * g


def _params(sem):
    return pltpu.CompilerParams(dimension_semantics=sem, vmem_limit_bytes=VMEM_LIMIT)


def _vec_spec(l, n, col=None):
    if col is None:
        return pl.BlockSpec((None, 1, n), lambda *ids: (l, 0, 0))
    return pl.BlockSpec((None, 1, n), lambda *ids: (l, 0, col(*ids)))


def _norm_kernel(x_ref, g_ref, h_ref):
    h_ref[...] = _rms(x_ref[...], g_ref[...]).astype(BF16)


def _norm(l, x, g):
    rows = x.shape[0]
    tb = min(NORM_ROWS, rows)
    return pl.pallas_call(
        _norm_kernel,
        grid=(rows // tb,),
        in_specs=[pl.BlockSpec((tb, D_MODEL), lambda i: (i, 0)), _vec_spec(l, D_MODEL)],
        out_specs=pl.BlockSpec((tb, D_MODEL), lambda i: (i, 0)),
        out_shape=jax.ShapeDtypeStruct((rows, D_MODEL), BF16),
        compiler_params=_params(("arbitrary",)),
        name="norm",
    )(x, g)


def _inproj_kernel(h_ref, wa_ref, wb_ref, proj_ref, *, n_head):
    j = pl.program_id(0)

    @pl.when(j < n_head)
    def _():
        proj_ref[...] = _dot(h_ref[...], wa_ref[...])

    @pl.when(j >= n_head)
    def _():
        proj_ref[...] = _dot(h_ref[...], wb_ref[...])


def _inproj(l, h, w_all, w_tail):
    rows = h.shape[0]
    tm = min(INPROJ_TILE, rows)
    tn = INPROJ_TILE
    n_head = PROJ_HEAD // tn
    return pl.pallas_call(
        functools.partial(_inproj_kernel, n_head=n_head),
        grid=(PROJ_MAIN // tn, rows // tm),
        in_specs=[
            pl.BlockSpec((tm, D_MODEL), lambda j, i: (i, 0)),
            pl.BlockSpec((None, D_MODEL, tn), lambda j, i: (l, 0, jnp.minimum(j, n_head - 1))),
            pl.BlockSpec((None, D_MODEL, tn), lambda j, i: (l, 0, jnp.maximum(j - n_head, 0))),
        ],
        out_specs=pl.BlockSpec((tm, tn), lambda j, i: (i, j)),
        out_shape=jax.ShapeDtypeStruct((rows, PROJ_MAIN), F32),
        compiler_params=_params(("arbitrary", "arbitrary")),
        name="inproj",
    )(h, w_all, w_tail)


def _conv_rows(x, first, prev_sc, cw_ref, cb_ref):
    assert CONV_K == 4
    tb = x.shape[0]

    if first is not False:
        @pl.when(first)
        def _():
            prev_sc[...] = jnp.zeros(prev_sc.shape, F32)

    xe = jnp.concatenate([prev_sc[...], x], axis=0)
    x1 = pltpu.roll(xe, 1, 0)
    pair = cw_ref[1:2, :] * xe + cw_ref[0:1, :] * x1
    acc = (cb_ref[...] + cw_ref[3:4, :] * x + cw_ref[2:3, :] * x1[SUBLANES:, :]
           + pltpu.roll(pair, 2, 0)[SUBLANES:, :])
    prev_sc[...] = x[tb - SUBLANES:, :]
    return _silu(acc)


def _mlstm_proj_kernel(*refs, has_hist, blocks_per_seq):
    if has_hist:
        (xm_ref, he_ref, cw_ref, cb_ref, wqk_ref, wvo_ref, wg_ref, bg_ref,
         q_ref, k_ref, v_ref, o_ref, gates_ref, xe_sc) = refs
    else:
        (xm_ref, cw_ref, cb_ref, wqk_ref, wvo_ref, wg_ref, bg_ref,
         q_ref, k_ref, v_ref, o_ref, gates_ref, xe_sc) = refs
    i = pl.program_id(0)
    x = xm_ref[...]
    xin = x + he_ref[...] if has_hist else x
    xmc = _conv_rows(xin, i % blocks_per_seq == 0, xe_sc, cw_ref, cb_ref)
    tb = x.shape[0]
    gates = jnp.broadcast_to(bg_ref[...], (tb, LANES))
    for h in range(M_HEADS):
        cols = slice(h * M_DH, (h + 1) * M_DH)
        qk = _dot(xmc[:, cols].astype(BF16), wqk_ref[h])
        vo = _dot(x[:, cols].astype(BF16), wvo_ref[h])
        qb = qk[:, :M_DH].astype(BF16)
        kb = qk[:, M_DH:].astype(BF16)
        vb = vo[:, :M_DH].astype(BF16)
        gates = gates + _dot(qb, wg_ref[h, 0]) + _dot(kb, wg_ref[h, 1]) + _dot(vb, wg_ref[h, 2])
        q_ref[:, cols] = qb
        k_ref[:, cols] = (qk[:, M_DH:] * (M_DH ** -0.5)).astype(BF16)
        v_ref[:, cols] = vb
        o_ref[:, cols] = vo[:, M_DH:]
    lane = lax.broadcasted_iota(jnp.int32, (tb, LANES), 1)
    gates_ref[...] = jnp.where(lane < M_HEADS, gates, _log_sigmoid(gates))


def _mlstm_proj(l, proj, hist, W, blocks_per_seq, tb):
    rows = proj.shape[0]
    has_hist = hist is not None
    in_specs = [pl.BlockSpec((tb, M_WIDTH), lambda i: (i, COL_XM // M_WIDTH))]
    args = [proj]
    if has_hist:
        in_specs.append(pl.BlockSpec((None, tb, M_WIDTH), lambda i: (l, i, 0)))
        args.append(hist)
    in_specs += [
        pl.BlockSpec((None, CONV_K, M_WIDTH), lambda i: (l, 0, 0)),
        _vec_spec(l, M_WIDTH),
        pl.BlockSpec((None, M_HEADS, M_DH, 2 * M_DH), lambda i: (l, 0, 0, 0)),
        pl.BlockSpec((None, M_HEADS, M_DH, 2 * M_DH), lambda i: (l, 0, 0, 0)),
        pl.BlockSpec((None, M_HEADS, 3, M_DH, LANES), lambda i: (l, 0, 0, 0, 0)),
        _vec_spec(l, LANES),
    ]
    args += [W["m_cw"], W["m_cb"], W["wqk"], W["wvo"], W["wg"], W["bg"]]
    row_spec = pl.BlockSpec((tb, M_WIDTH), lambda i: (i, 0))
    return pl.pallas_call(
        functools.partial(_mlstm_proj_kernel, has_hist=has_hist, blocks_per_seq=blocks_per_seq),
        grid=(rows // tb,),
        in_specs=in_specs,
        out_specs=[row_spec, row_spec, row_spec, row_spec, pl.BlockSpec((tb, LANES), lambda i: (i, 0))],
        out_shape=[
            jax.ShapeDtypeStruct((rows, M_WIDTH), BF16),
            jax.ShapeDtypeStruct((rows, M_WIDTH), BF16),
            jax.ShapeDtypeStruct((rows, M_WIDTH), BF16),
            jax.ShapeDtypeStruct((rows, M_WIDTH), F32),
            jax.ShapeDtypeStruct((rows, LANES), F32),
        ],
        scratch_shapes=[pltpu.VMEM((SUBLANES, M_WIDTH), F32)],
        compiler_params=_params(("arbitrary",)),
        name="mlstm_proj",
    )(*args)


def _mlstm_finish(hh, o, bo, ng, zm):
    hm = _sigmoid(o + bo) * hh
    return _rms(hm, ng) * _silu(zm)


def _mlstm_prep(gates_ref):
    L = gates_ref.shape[0]
    r2 = lax.broadcasted_iota(jnp.int32, (L, L), 0)
    c2 = lax.broadcasted_iota(jnp.int32, (L, L), 1)
    causal = c2 <= r2
    g = gates_ref[...]
    b_all = _dot_mask(causal, g)
    return causal, g, b_all, g.T, b_all.T


def _mlstm_head(h, prep, q_ref, k_ref, v_ref, o_ref, zm_ref, bo_ref, ng_ref, c1_ref, n1_ref, m1_ref, emit):
    causal, g, b_all, g_t, b_t = prep
    L = q_ref.shape[0]
    neg_inf = -jnp.inf
    cols = slice(h * M_DH, (h + 1) * M_DH)
    f = M_HEADS + h
    g_row = g_t[h:h + 1, :] - b_t[f:f + 1, :]
    b_col = b_all[:, f:f + 1]
    g_col = g[:, h:h + 1] - b_col
    mprev = m1_ref[0, h:h + 1, 0:1]
    gm = jnp.where(causal, g_row, neg_inf)
    m_col = jnp.maximum(mprev, jnp.max(gm, axis=1, keepdims=True))
    w_intra = jnp.exp(gm - m_col)
    w_inter = jnp.exp(mprev - m_col)
    m_new = b_col + m_col
    m_last = m_col[L - 1:L, :]
    wl_col = jnp.exp(g_col - m_last)
    wli = jnp.exp(mprev - m_last)
    q = q_ref[:, cols]
    k = k_ref[:, cols]
    v = v_ref[:, cols]
    cst = c1_ref[0, h]
    nst = n1_ref[0, h]
    s = _dot_nt(q, k) * w_intra
    num = _dot(s.astype(BF16), v) + w_inter * _dot_nt(q, cst.astype(BF16))
    den = (jnp.sum(s, axis=1, keepdims=True)
           + w_inter * jnp.sum(q.astype(F32) * nst, axis=1, keepdims=True))
    hh = num / jnp.maximum(jnp.abs(den), jnp.exp(-m_new))
    emit(cols, _mlstm_finish(hh, o_ref[:, cols], bo_ref[:, cols], ng_ref[:, cols], zm_ref[:, cols]))

    def update_state():
        c1_ref[0, h] = wli * cst + _dot((v.astype(F32) * wl_col).T.astype(BF16), k)
        n1_ref[0, h] = wli * nst + jnp.sum(k.astype(F32) * wl_col, axis=0, keepdims=True)
        m1_ref[0, h:h + 1, :] = jnp.broadcast_to(m_new[L - 1:L, :], (1, LANES))

    return update_state


def _block_masks(L, rs, t_lo):
    shift = rs.bit_length() - 1
    r2 = lax.broadcasted_iota(jnp.int32, (L, L), 0)
    c2 = lax.broadcasted_iota(jnp.int32, (L, L), 1)
    same = (r2 >> shift) == (c2 >> shift)
    valid_c = (c2 & (rs - 1)) >= t_lo
    ridx = lax.broadcasted_iota(jnp.int32, (L, 1), 0)
    valid_r = (ridx & (rs - 1)) >= t_lo
    return r2, c2, same, valid_c, ridx, valid_r


def _mlstm_sample_kernel(q_ref, k_ref, v_ref, o_ref, zm_ref, gates_ref, bo_ref, ng_ref,
                         c0_ref, n0_ref, m0_ref, alias_ref,
                         out_ref, c1_ref, n1_ref, mrow_ref,
                         gt_sc, bt_sc, col_sc, q_sc, numi_sc, nrow_sc, wk_sc, cbuf, csem, *, nseq, t_lo, layer):
    del alias_ref
    h = pl.program_id(1)
    L = q_ref.shape[0]
    rs = L // nseq

    step = pl.program_id(0) * M_HEADS + h
    nsteps = pl.num_programs(0) * M_HEADS

    def state_copy(s):
        return pltpu.make_async_copy(
            c0_ref.at[layer, pl.ds((s // M_HEADS) * nseq, nseq), s % M_HEADS],
            cbuf.at[s % STATE_SLOTS], csem.at[s % STATE_SLOTS])

    @pl.when(step == 0)
    def _():
        state_copy(step).start()
        state_copy(step + 1).start()

    @pl.when(step + 2 < nsteps)
    def _():
        state_copy(step + 2).start()

    state_copy(step).wait()
    slot = step % STATE_SLOTS
    r2, c2, same, valid_c, ridx, valid_r = _block_masks(L, rs, t_lo)
    eye = r2 == c2
    neg_inf = -jnp.inf
    g = gates_ref[...]
    b_all = _dot_mask(same & (c2 <= r2), jnp.where(valid_r, g, 0.0))
    gt_sc[...] = g.T
    bt_sc[...] = b_all.T
    lane_g = lax.broadcasted_iota(jnp.int32, (1, LANES), 1)
    i_col = jnp.sum(jnp.where(lane_g == h, g, 0.0), axis=1, keepdims=True)
    b_col = jnp.sum(jnp.where(lane_g == h + M_HEADS, b_all, 0.0), axis=1, keepdims=True)
    g_row = gt_sc[pl.ds(h, 1), :] - bt_sc[pl.ds(h + M_HEADS, 1), :]
    g_col = i_col - b_col
    mask = same & (c2 <= r2) & valid_c
    mprev = m0_ref[0]
    gm = jnp.where(mask, g_row, neg_inf)
    m_col = jnp.maximum(mprev, jnp.max(gm, axis=1, keepdims=True))
    m_row = jnp.sum(jnp.where(eye, m_col, 0.0), axis=0, keepdims=True)
    mlast_col = jnp.max(jnp.where(same, m_row, neg_inf), axis=1, keepdims=True)
    w_intra = jnp.exp(gm - m_col)
    w_inter = jnp.exp(mprev - m_col)
    m_new = b_col + m_col
    wl_col = jnp.where(valid_r, jnp.exp(g_col - mlast_col), 0.0)
    col_sc[:, 0:1] = jnp.exp(mprev - mlast_col)

    q = q_ref[...]
    k = k_ref[...]
    v = v_ref[...]
    s = _dot_nt(q, k) * w_intra
    num = _dot(s.astype(BF16), v)
    den = jnp.sum(s, axis=1, keepdims=True)
    wvt = (v.astype(F32) * wl_col).T.astype(BF16)
    wk_sc[...] = k.astype(F32) * wl_col
    qf = q.astype(F32)
    q_sc[0:L, :] = qf
    q_sc[L:L + SUBLANES, :] = jnp.zeros((SUBLANES, M_DH), F32)

    def seq_step(j, carry):
        rows = pl.ds(pl.multiple_of(j * rs, rs), rs)
        in_seq_r = (ridx >= j * rs) & (ridx < (j + 1) * rs)
        cj = cbuf[slot, j]
        nj = n0_ref[j, 0]
        q2 = q_sc[pl.ds(pl.multiple_of(j * rs, rs), 2 * rs), :].astype(BF16)
        numi_sc[rows, :] = _dot_nt(q2, cj.astype(BF16))[:rs]
        nrow_sc[rows, :] = jnp.broadcast_to(nj, (rs, M_DH))
        wli = col_sc[pl.ds(j * rs + rs - 1, 1), 0:1]
        c1_ref[j, 0] = wli * cj + _dot(wvt, jnp.where(in_seq_r, k, jnp.zeros_like(k)))
        n1_ref[j, 0] = wli * nj + jnp.sum(wk_sc[rows, :], axis=0, keepdims=True)
        return carry

    lax.fori_loop(0, nseq, seq_step, 0, unroll=SEQ_UNROLL)
    num = num + w_inter * numi_sc[...]
    den = den + w_inter * jnp.sum(qf * nrow_sc[...], axis=1, keepdims=True)
    hh = num / jnp.maximum(jnp.abs(den), jnp.exp(-m_new))
    out_ref[...] = _mlstm_finish(hh, o_ref[...], bo_ref[...], ng_ref[...], zm_ref[...]).astype(BF16)
    mrow_ref[0] = m_new


def _mlstm_sample(l, depth, q, k, v, o, gates, proj, W, c0, n0, m0rows, c_prev, nseq, t_lo):
    L = CHUNK
    rows = q.shape[0]
    nb = rows // L
    nbatch = nb * nseq
    rowblk = lambda b, h: (b, h)
    c_spec = pl.BlockSpec((None, nseq, 1, M_DH, M_DH), lambda b, h: (l, b, h, 0, 0))
    n_spec_in = pl.BlockSpec((None, nseq, 1, 1, M_DH), lambda b, h: (l, b, h, 0, 0))
    n_spec_out = pl.BlockSpec((nseq, 1, 1, M_DH), lambda b, h: (b, h, 0, 0))
    has_prev = c_prev is not None
    in_specs = [
        pl.BlockSpec((L, M_DH), rowblk),
        pl.BlockSpec((L, M_DH), rowblk),
        pl.BlockSpec((L, M_DH), rowblk),
        pl.BlockSpec((L, M_DH), rowblk),
        pl.BlockSpec((L, M_DH), lambda b, h: (b, COL_ZM // M_DH + h)),
        pl.BlockSpec((L, LANES), lambda b, h: (b, 0)),
        _vec_spec(l, M_DH, lambda b, h: h),
        _vec_spec(l, M_DH, lambda b, h: h),
        pl.BlockSpec(memory_space=pl.ANY),
        n_spec_in,
        pl.BlockSpec((None, 1, L, 1), lambda b, h: (l, h, b, 0)),
        pl.BlockSpec(memory_space=pl.ANY),
    ]
    args = [q, k, v, o, proj, gates, W["bo"], W["m_ng"], c0, n0, m0rows,
            c_prev if has_prev else jnp.zeros((SUBLANES, LANES), F32)]
    return pl.pallas_call(
        functools.partial(_mlstm_sample_kernel, nseq=nseq, t_lo=t_lo, layer=l),
        grid=(nb, M_HEADS),
        in_specs=in_specs,
        out_specs=[pl.BlockSpec((L, M_DH), rowblk), c_spec, n_spec_out,
                   pl.BlockSpec((1, L, 1), lambda b, h: (h, b, 0))],
        out_shape=[
            jax.ShapeDtypeStruct((rows, M_WIDTH), BF16),
            jax.ShapeDtypeStruct((depth, nbatch, M_HEADS, M_DH, M_DH), F32),
            jax.ShapeDtypeStruct((nbatch, M_HEADS, 1, M_DH), F32),
            jax.ShapeDtypeStruct((M_HEADS, rows, 1), F32),
        ],
        scratch_shapes=[
            pltpu.VMEM((LANES, L), F32),
            pltpu.VMEM((LANES, L), F32),
            pltpu.VMEM((L, LANES), F32),
            pltpu.VMEM((L + SUBLANES, M_DH), F32),
            pltpu.VMEM((L, M_DH), F32),
            pltpu.VMEM((L, M_DH), F32),
            pltpu.VMEM((L, M_DH), F32),
            pltpu.VMEM((STATE_SLOTS, nseq, M_DH, M_DH), F32),
            pltpu.SemaphoreType.DMA((STATE_SLOTS,)),
        ],
        input_output_aliases={11: 1} if has_prev else {},
        compiler_params=_params(("arbitrary", "arbitrary")),
        name="mlstm_sample",
    )(*args)


def _ssd_prompt_kernel(xs_ref, b_ref, c_ref, zs_ref, h_ref, wdt_ref,
                       cwx_ref, cwb_ref, cwc_ref, cbx_ref, cbb_ref, cbc_ref,
                       dtb_ref, alog_ref, dskip_ref, ng_ref, expand_ref,
                       out_ref, h1_ref, xex_sc, xeb_sc, xec_sc, ht_sc):
    c = pl.program_id(1)
    nc = pl.num_programs(1)
    L = CHUNK

    @pl.when(c == 0)
    def _():
        ht_sc[...] = jnp.zeros(ht_sc.shape, F32)

    for sub in range(xs_ref.shape[0] // L):
        _ssd_prompt_chunk(slice(sub * L, (sub + 1) * L), (c == 0) if sub == 0 else False,
                          xs_ref, b_ref, c_ref, zs_ref, h_ref, wdt_ref,
                          cwx_ref, cwb_ref, cwc_ref, cbx_ref, cbb_ref, cbc_ref,
                          dtb_ref, alog_ref, dskip_ref, ng_ref, expand_ref,
                          out_ref, xex_sc, xeb_sc, xec_sc, ht_sc)

    @pl.when(c == nc - 1)
    def _():
        for pr in range(S_HEADS // 2):
            blk = ht_sc[:, pr * LANES:(pr + 1) * LANES].T
            h1_ref[0, 2 * pr] = blk[:S_DH]
            h1_ref[0, 2 * pr + 1] = blk[S_DH:]


def _ssd_prompt_chunk(rows, first, xs_ref, b_ref, c_ref, zs_ref, h_ref, wdt_ref,
                      cwx_ref, cwb_ref, cwc_ref, cbx_ref, cbb_ref, cbc_ref,
                      dtb_ref, alog_ref, dskip_ref, ng_ref, expand_ref,
                      out_ref, xex_sc, xeb_sc, xec_sc, ht_sc):
    L = CHUNK
    xs = _conv_rows(xs_ref[rows, :], first, xex_sc, cwx_ref, cbx_ref)
    bm = _conv_rows(b_ref[rows, :], first, xeb_sc, cwb_ref, cbb_ref)
    cm = _conv_rows(c_ref[rows, :], first, xec_sc, cwc_ref, cbc_ref).astype(BF16)

    r2 = lax.broadcasted_iota(jnp.int32, (L, L), 0)
    c2 = lax.broadcasted_iota(jnp.int32, (L, L), 1)
    causal = c2 <= r2
    neg_inf = -jnp.inf
    dt = _softplus(_dot(h_ref[rows, :], wdt_ref[...]) + dtb_ref[...])
    da = dt * (-jnp.exp(alog_ref[...]))
    cs = _dot_mask(causal, da)
    cs_t = cs.T
    ecs = jnp.exp(cs)
    wend = jnp.exp(cs[L - 1:L, :] - cs) * dt

    expand = expand_ref[...]
    dte = _widen(dt, expand)
    wende = _widen(wend, expand)
    ecse = _widen(ecs, expand)
    xdt = (xs * dte).astype(BF16)
    wx = (xs * wende).astype(BF16)
    lane = lax.broadcasted_iota(jnp.int32, (1, LANES), 1)
    low_half = lane < S_DH
    zero_slab = jnp.zeros((L, LANES), BF16)
    for g in range(S_GROUPS):
        gcols = slice(g * S_GW, (g + 1) * S_GW)
        scols = slice(g * S_STATE, (g + 1) * S_STATE)
        bg = bm[:, scols]
        cg = cm[:, scols]
        cb = _dot_nt(cg, bg.astype(BF16))
        pairs = []
        for pr in range(S_HPG // 2):
            h0 = g * S_HPG + 2 * pr
            mixes = []
            for hh in (h0, h0 + 1):
                dec = jnp.exp(jnp.where(causal, cs[:, hh:hh + 1] - cs_t[hh:hh + 1, :], neg_inf))
                mixes.append((cb * dec).astype(BF16))
            slab = xdt[:, h0 * S_DH:(h0 + 2) * S_DH]
            rhs = jnp.concatenate([jnp.where(low_half, slab, zero_slab),
                                   jnp.where(low_half, zero_slab, slab)], axis=0)
            pairs.append(_dot(jnp.concatenate(mixes, axis=1), rhs))
        y_intra = jnp.concatenate(pairs, axis=1)
        ht = ht_sc[:, gcols]
        y = y_intra + ecse[:, gcols] * _dot(cg, ht.astype(BF16)) + dskip_ref[:, gcols] * xs[:, gcols]
        y = y * _silu(zs_ref[rows, gcols])
        out_ref[rows, gcols] = _rms(y, ng_ref[:, gcols]).astype(BF16)
        ht_sc[:, gcols] = ecse[L - 1:L, gcols] * ht + _dot(bg.T.astype(BF16), wx[:, gcols])


def _ssd_prompt(l, proj, h, W, nb, nc):
    L = CHUNK * SSD_SUB
    rows = nb * nc * L
    row = lambda b, c: b * nc + c
    conv_w = lambda width, blk: pl.BlockSpec((None, CONV_K, width), lambda b, c: (l, 0, blk))
    return pl.pallas_call(
        _ssd_prompt_kernel,
        grid=(nb, nc),
        in_specs=[
            pl.BlockSpec((L, S_WIDTH), lambda b, c: (row(b, c), COL_XBC // S_WIDTH)),
            pl.BlockSpec((L, S_BC), lambda b, c: (row(b, c), COL_B // S_BC)),
            pl.BlockSpec((L, S_BC), lambda b, c: (row(b, c), COL_C // S_BC)),
            pl.BlockSpec((L, S_WIDTH), lambda b, c: (row(b, c), COL_ZS // S_WIDTH)),
            pl.BlockSpec((L, D_MODEL), lambda b, c: (row(b, c), 0)),
            pl.BlockSpec((None, D_MODEL, LANES), lambda b, c: (l, 0, 0)),
            conv_w(S_WIDTH, 0), conv_w(S_BC, S_WIDTH // S_BC), conv_w(S_BC, S_WIDTH // S_BC + 1),
            _vec_spec(l, S_WIDTH, lambda b, c: 0),
            _vec_spec(l, S_BC, lambda b, c: S_WIDTH // S_BC),
            _vec_spec(l, S_BC, lambda b, c: S_WIDTH // S_BC + 1),
            _vec_spec(l, LANES),
            _vec_spec(l, LANES),
            _vec_spec(l, S_WIDTH),
            _vec_spec(l, S_WIDTH),
            pl.BlockSpec((2 * LANES, S_WIDTH), lambda b, c: (0, 0)),
        ],
        out_specs=[
            pl.BlockSpec((L, S_WIDTH), lambda b, c: (row(b, c), 0)),
            pl.BlockSpec((1, S_HEADS, S_DH, S_STATE), lambda b, c: (b, 0, 0, 0)),
        ],
        out_shape=[
            jax.ShapeDtypeStruct((rows, S_WIDTH), BF16),
            jax.ShapeDtypeStruct((nb, S_HEADS, S_DH, S_STATE), F32),
        ],
        scratch_shapes=[
            pltpu.VMEM((SUBLANES, S_WIDTH), F32),
            pltpu.VMEM((SUBLANES, S_BC), F32),
            pltpu.VMEM((SUBLANES, S_BC), F32),
            pltpu.VMEM((S_STATE, S_WIDTH), F32),
        ],
        compiler_params=_params(("arbitrary", "arbitrary")),
        name="ssd_prompt",
    )(proj, proj, proj, proj, h, W["w_dt_c"], W["s_cw"], W["s_cw"], W["s_cw"], W["s_cb"], W["s_cb"], W["s_cb"],
      W["dtb_c"], W["alog_c"], W["dskip_wide"], W["s_ng"], W["expand"])


def _ssd_sample_kernel(xs_ref, b_ref, c_ref, zs_ref, h_ref, wdt_ref, hx_ref, hb_ref, hc_ref,
                       cwx_ref, cwb_ref, cwc_ref, cbx_ref, cbb_ref, cbc_ref,
                       dtb_ref, alog_ref, dskip_ref, ng_ref, expand_ref, h0_ref, alias_ref,
                       out_ref, h1_ref, xex_sc, xeb_sc, xec_sc, tot_sc, cm_sc, yi_sc, *, nseq, t_lo):
    del alias_ref
    L = xs_ref.shape[0]
    rs = L // nseq
    first = True
    xs = _conv_rows(xs_ref[...] + hx_ref[...], first, xex_sc, cwx_ref, cbx_ref)
    bm = _conv_rows(b_ref[...] + hb_ref[...], first, xeb_sc, cwb_ref, cbb_ref)
    cm = _conv_rows(c_ref[...] + hc_ref[...], first, xec_sc, cwc_ref, cbc_ref)
    bmb = bm.astype(BF16)
    cmb = cm.astype(BF16)
    cm_sc[0:L, :] = cm
    cm_sc[L:L + SUBLANES, :] = jnp.zeros((SUBLANES, S_STATE), F32)

    r2, c2, same, valid_c, ridx, valid_r = _block_masks(L, rs, t_lo)
    mask = same & (c2 <= r2) & valid_c
    neg_inf = -jnp.inf
    dt = jnp.where(valid_r, _softplus(_dot(h_ref[...], wdt_ref[...]) + dtb_ref[...]), 0.0)
    da = dt * (-jnp.exp(alog_ref[...]))
    cs = _dot_mask(mask, da)
    sfx = _dot_mask(same & (c2 > r2), da)
    tot_sc[...] = cs + sfx
    cs_t = cs.T
    expand = expand_ref[...]
    dte = _widen(dt, expand)
    wende = _widen(jnp.exp(sfx) * dt, expand)
    ecse = _widen(jnp.exp(cs), expand)
    xdt = (xs * dte).astype(BF16)
    wxt = (xs * wende).T.astype(BF16)
    cb = _dot_nt(cmb, bmb)
    lane = lax.broadcasted_iota(jnp.int32, (1, LANES), 1)
    low_half = lane < S_DH
    zero_slab = jnp.zeros((L, LANES), BF16)
    pairs = []
    for pr in range(S_HPG // 2):
        mixes = []
        for hh in (2 * pr, 2 * pr + 1):
            dec = jnp.exp(jnp.where(mask, cs[:, hh:hh + 1] - cs_t[hh:hh + 1, :], neg_inf))
            mixes.append((cb * dec).astype(BF16))
        slab = xdt[:, pr * LANES:(pr + 1) * LANES]
        rhs = jnp.concatenate([jnp.where(low_half, slab, zero_slab),
                               jnp.where(low_half, zero_slab, slab)], axis=0)
        pairs.append(_dot(jnp.concatenate(mixes, axis=1), rhs))
    y_intra = jnp.concatenate(pairs, axis=1)

    def seq_step(j, carry):
        rows = pl.ds(pl.multiple_of(j * rs, rs), rs)
        in_seq_r = (ridx >= j * rs) & (ridx < (j + 1) * rs)
        hj = h0_ref[j]
        c2rows = cm_sc[pl.ds(pl.multiple_of(j * rs, rs), 2 * rs), :].astype(BF16)
        yi_sc[rows, :] = _dot_nt(c2rows, hj.reshape(S_GW, S_STATE).astype(BF16))[:rs]
        upd = _dot(wxt, jnp.where(in_seq_r, bmb, jnp.zeros_like(bmb)))
        dec_j = jnp.exp(tot_sc[pl.ds(j * rs + rs - 1, 1), :])
        for hh in range(S_HPG):
            h1_ref[j, hh] = dec_j[:, hh:hh + 1] * hj[hh] + upd[hh * S_DH:(hh + 1) * S_DH, :]
        return carry

    lax.fori_loop(0, nseq, seq_step, 0, unroll=SEQ_UNROLL)
    y = y_intra + ecse * yi_sc[...] + dskip_ref[...] * xs
    out_ref[...] = _rms(y * _silu(zs_ref[...]), ng_ref[...]).astype(BF16)


def _ssd_sample(l, depth, proj, h, hist, W, h0, h_prev, nseq, t_lo):
    L = CHUNK
    rows = proj.shape[0]
    nb = rows // L
    xblk = COL_XBC // S_GW
    bblk = COL_B // S_STATE
    cblk = COL_C // S_STATE
    hb_blk = S_WIDTH // S_STATE
    hc_blk = hb_blk + S_GROUPS
    has_prev = h_prev is not None
    conv_w = lambda width, blk: pl.BlockSpec((None, CONV_K, width), lambda b, g: (l, 0, blk(g)))
    h_spec = pl.BlockSpec((None, nseq, S_HPG, S_DH, S_STATE), lambda b, g: (l, b, g, 0, 0))
    in_specs = [
        pl.BlockSpec((L, S_GW), lambda b, g: (b, xblk + g)),
        pl.BlockSpec((L, S_STATE), lambda b, g: (b, bblk + g)),
        pl.BlockSpec((L, S_STATE), lambda b, g: (b, cblk + g)),
        pl.BlockSpec((L, S_GW), lambda b, g: (b, COL_ZS // S_GW + g)),
        pl.BlockSpec((L, D_MODEL), lambda b, g: (b, 0)),
        pl.BlockSpec((None, D_MODEL, LANES), lambda b, g: (l, 0, g)),
        pl.BlockSpec((None, L, S_GW), lambda b, g: (l, b, g)),
        pl.BlockSpec((None, L, S_STATE), lambda b, g: (l, b, hb_blk + g)),
        pl.BlockSpec((None, L, S_STATE), lambda b, g: (l, b, hc_blk + g)),
        conv_w(S_GW, lambda g: g), conv_w(S_STATE, lambda g: hb_blk + g), conv_w(S_STATE, lambda g: hc_blk + g),
        _vec_spec(l, S_GW, lambda b, g: g),
        _vec_spec(l, S_STATE, lambda b, g: hb_blk + g),
        _vec_spec(l, S_STATE, lambda b, g: hc_blk + g),
        _vec_spec(l, LANES, lambda b, g: g),
        _vec_spec(l, LANES, lambda b, g: g),
        _vec_spec(l, S_GW, lambda b, g: g),
        _vec_spec(l, S_GW, lambda b, g: g),
        pl.BlockSpec((2 * LANES, S_GW), lambda b, g: (0, 0)),
        h_spec,
        pl.BlockSpec(memory_space=pl.ANY),
    ]
    args = [proj, proj, proj, proj, h, W["w_dt_g"], hist, hist, hist,
            W["s_cw"], W["s_cw"], W["s_cw"], W["s_cb"], W["s_cb"], W["s_cb"],
            W["dtb_g"], W["alog_g"], W["dskip_wide"], W["s_ng"], W["expand"], h0,
            h_prev if has_prev else jnp.zeros((SUBLANES, LANES), F32)]
    return pl.pallas_call(
        functools.partial(_ssd_sample_kernel, nseq=nseq, t_lo=t_lo),
        grid=(nb, S_GROUPS),
        in_specs=in_specs,
        out_specs=[pl.BlockSpec((L, S_GW), lambda b, g: (b, g)), h_spec],
        out_shape=[
            jax.ShapeDtypeStruct((rows, S_WIDTH), BF16),
            jax.ShapeDtypeStruct((depth, nb * nseq, S_HEADS, S_DH, S_STATE), F32),
        ],
        scratch_shapes=[
            pltpu.VMEM((SUBLANES, S_GW), F32),
            pltpu.VMEM((SUBLANES, S_STATE), F32),
            pltpu.VMEM((SUBLANES, S_STATE), F32),
            pltpu.VMEM((L, LANES), F32),
            pltpu.VMEM((L + SUBLANES, S_STATE), F32),
            pltpu.VMEM((L, S_GW), F32),
        ],
        input_output_aliases={21: 1} if has_prev else {},
        compiler_params=_params(("arbitrary", "arbitrary")),
        name="ssd_sample",
    )(*args)


def _outproj_body(x_ref, om, os_ref, u_ref, v_ref, zc_ref, wmix_ref, bias_ref, gv_ref,
                  w0_ref, w1_ref, w2_ref, w3_ref, g_ref, y_ref, h_ref, vn_ref, oc_sc, *, final, rs, t_lo,
                  between=None):
    emit_vn = vn_ref is not None
    if between is None:
        between = lambda k: None
    L = C_CHUNK
    r2, c2, same, valid_c, _, _ = _block_masks(L, min(rs, L), t_lo)
    mask = same & (c2 <= r2) & valid_c
    for g in range(C_GROUPS):
        cols = slice(g * C_DG, (g + 1) * C_DG)
        w = jnp.where(mask, wmix_ref[g], 0.0).astype(BF16)
        for ch in range(x_ref.shape[0] // L):
            rows = slice(ch * L, (ch + 1) * L)
            vn = _rms(v_ref[rows, cols], gv_ref[:, cols])
            if emit_vn:
                vn_ref[rows, cols] = vn
            mixed = _dot(w, vn.astype(BF16)) + bias_ref[g]
            oc_sc[rows, cols] = (u_ref[rows, cols] * mixed * _silu(zc_ref[rows, cols])).astype(BF16)
    kb = w0_ref.shape[0]
    hk = kb // 2
    lhs = (lambda c: om[:, c],
           lambda c: os_ref[:, c].astype(BF16),
           lambda c: os_ref[:, slice(kb + c.start, kb + c.stop)].astype(BF16),
           lambda c: oc_sc[:, c])
    acc = None
    for blk, w_ref in enumerate((w0_ref, w1_ref, w2_ref, w3_ref)):
        for part in range(2):
            c = slice(part * hk, (part + 1) * hk)
            d = _dot(lhs[blk](c), w_ref[c, :])
            acc = d if acc is None else acc + d
            between(2 * blk + part)
    y = x_ref[...] + acc
    if t_lo:
        ridx = lax.broadcasted_iota(jnp.int32, (y.shape[0], 1), 0)
        y = jnp.where((ridx & (rs - 1)) >= t_lo, y, 0.0)
    if final:
        y_ref[...] = _rms(y, g_ref[...])
    else:
        y_ref[...] = y
        h_ref[...] = _rms(y, g_ref[...]).astype(BF16)


def _outproj_kernel(*refs, final, emit_vn, rs, t_lo):
    (x_ref, om_ref, os_ref, u_ref, v_ref, zc_ref, wmix_ref, bias_ref, gv_ref,
     w0_ref, w1_ref, w2_ref, w3_ref, g_ref, y_ref) = refs[:15]
    rest = list(refs[15:])
    h_ref = None if final else rest.pop(0)
    vn_ref = rest.pop(0) if emit_vn else None
    (oc_sc,) = rest
    _outproj_body(x_ref, om_ref[...].astype(BF16), os_ref, u_ref, v_ref, zc_ref, wmix_ref, bias_ref, gv_ref,
                  w0_ref, w1_ref, w2_ref, w3_ref, g_ref, y_ref, h_ref, vn_ref, oc_sc,
                  final=final, rs=rs, t_lo=t_lo)


def _mlstm_out_kernel(*refs, final, nc):
    (q_ref, k_ref, v_ref, o_ref, zm_ref, gates_ref, bo_ref, ng_ref,
     x_ref, os_ref, u_ref, v2_ref, zc_ref, wmix_ref, bias_ref, gv_ref,
     w0_ref, w1_ref, w2_ref, w3_ref, g_ref, c1_ref, n1_ref, m1_ref, y_ref) = refs[:25]
    rest = list(refs[25:])
    h_ref = None if final else rest.pop(0)
    om_sc, oc_sc = rest
    i = pl.program_id(0)

    @pl.when(i == 0)
    def _():
        om_sc[...] = jnp.zeros(om_sc.shape, BF16)

    @pl.when(i % nc == 0)
    def _():
        c1_ref[...] = jnp.zeros(c1_ref.shape, F32)
        n1_ref[...] = jnp.zeros(n1_ref.shape, F32)
        m1_ref[...] = jnp.zeros(m1_ref.shape, F32)

    def emit(cols, val):
        om_sc[i % 2, :, cols] = val.astype(BF16)

    prep = _mlstm_prep(gates_ref)

    pending = {}

    def head(slot):
        h, second = divmod(slot, 2)
        if second:
            pending.pop(h)()
        else:
            pending[h] = _mlstm_head(h, prep, q_ref, k_ref, v_ref, o_ref, zm_ref, bo_ref, ng_ref,
                                     c1_ref, n1_ref, m1_ref, emit)

    _outproj_body(x_ref, om_sc[(i + 1) % 2], os_ref, u_ref, v2_ref, zc_ref, wmix_ref, bias_ref, gv_ref,
                  w0_ref, w1_ref, w2_ref, w3_ref, g_ref, y_ref, h_ref, None, oc_sc,
                  final=final, rs=CHUNK, t_lo=0, between=head)


def _mlstm_out(l, q, k, v, o, gates, proj, x, os_, wmix, bias, W, g, final, nb, nc):
    L = M_CHUNK
    nt = nb * nc
    rows = nt * L
    cur = lambda i: jnp.minimum(i, nt - 1)
    prev = lambda i: jnp.maximum(i - 1, 0)
    cur_spec = pl.BlockSpec((L, M_WIDTH), lambda i: (cur(i), 0))
    prev_spec = pl.BlockSpec((L, D_MODEL), lambda i: (prev(i), 0))
    wspec = lambda r: pl.BlockSpec((None, M_WIDTH, D_MODEL), lambda i: (l, r, 0), pipeline_mode=pl.Buffered(1))
    out_shape = [
        jax.ShapeDtypeStruct((nb + 1, M_HEADS, M_DH, M_DH), F32),
        jax.ShapeDtypeStruct((nb + 1, M_HEADS, 1, M_DH), F32),
        jax.ShapeDtypeStruct((nb + 1, SUBLANES, LANES), F32),
        jax.ShapeDtypeStruct((rows, D_MODEL), F32),
    ]
    out_specs = [
        pl.BlockSpec((1, M_HEADS, M_DH, M_DH), lambda i: (i // nc, 0, 0, 0)),
        pl.BlockSpec((1, M_HEADS, 1, M_DH), lambda i: (i // nc, 0, 0, 0)),
        pl.BlockSpec((1, SUBLANES, LANES), lambda i: (i // nc, 0, 0)),
        prev_spec,
    ]
    if not final:
        out_shape.append(jax.ShapeDtypeStruct((rows, D_MODEL), BF16))
        out_specs.append(prev_spec)
    return pl.pallas_call(
        functools.partial(_mlstm_out_kernel, final=final, nc=nc),
        grid=(nt + 1,),
        in_specs=[
            cur_spec, cur_spec, cur_spec, cur_spec,
            pl.BlockSpec((L, M_WIDTH), lambda i: (cur(i), COL_ZM // M_WIDTH)),
            pl.BlockSpec((L, LANES), lambda i: (cur(i), 0)),
            _vec_spec(l, M_WIDTH),
            _vec_spec(l, M_WIDTH),
            prev_spec,
            prev_spec,
            pl.BlockSpec((L, C_WIDTH), lambda i: (prev(i), COL_U // C_WIDTH)),
            pl.BlockSpec((L, C_WIDTH), lambda i: (prev(i), COL_V // C_WIDTH)),
            pl.BlockSpec((L, C_WIDTH), lambda i: (prev(i), COL_ZC // C_WIDTH)),
            pl.BlockSpec((None, C_GROUPS, C_CHUNK, C_CHUNK), lambda i: (l, 0, 0, 0)),
            pl.BlockSpec((None, C_GROUPS, C_CHUNK, 1), lambda i: (l, 0, 0, 0)),
            _vec_spec(l, C_WIDTH),
            wspec(0), wspec(1), wspec(2), wspec(3),
            pl.BlockSpec((1, D_MODEL), lambda i: (0, 0)) if final else _vec_spec(l + 1, D_MODEL),
        ],
        out_specs=out_specs,
        out_shape=out_shape,
        scratch_shapes=[pltpu.VMEM((2, L, M_WIDTH), BF16), pltpu.VMEM((L, C_WIDTH), BF16)],
        compiler_params=_params(("arbitrary",)),
        name="mlstm_out",
    )(q, k, v, o, proj, gates, W["bo"], W["m_ng"], x, os_, proj, proj, proj, wmix, bias, W["gv"],
      W["wo"], W["wo"], W["wo"], W["wo"], g)


def _outproj(l, x, om, os_, proj, wmix, bias, gv, wo, g, final, emit_vn, rs, t_lo):
    rows = x.shape[0]
    tm = min(M_CHUNK, rows)
    kb = M_WIDTH
    wspec = lambda r: pl.BlockSpec((None, kb, D_MODEL), lambda i: (l, r, 0), pipeline_mode=pl.Buffered(1))
    row_spec = pl.BlockSpec((tm, D_MODEL), lambda i: (i, 0))
    out_shape = [jax.ShapeDtypeStruct((rows, D_MODEL), F32)]
    out_specs = [row_spec]
    if not final:
        out_shape.append(jax.ShapeDtypeStruct((rows, D_MODEL), BF16))
        out_specs.append(row_spec)
    if emit_vn:
        out_shape.append(jax.ShapeDtypeStruct((rows, C_WIDTH), F32))
        out_specs.append(pl.BlockSpec((tm, C_WIDTH), lambda i: (i, 0)))
    return pl.pallas_call(
        functools.partial(_outproj_kernel, final=final, emit_vn=emit_vn, rs=rs, t_lo=t_lo),
        grid=(rows // tm,),
        in_specs=[
            row_spec,
            pl.BlockSpec((tm, M_WIDTH), lambda i: (i, 0)),
            pl.BlockSpec((tm, S_WIDTH), lambda i: (i, 0)),
            pl.BlockSpec((tm, C_WIDTH), lambda i: (i, COL_U // C_WIDTH)),
            pl.BlockSpec((tm, C_WIDTH), lambda i: (i, COL_V // C_WIDTH)),
            pl.BlockSpec((tm, C_WIDTH), lambda i: (i, COL_ZC // C_WIDTH)),
            pl.BlockSpec((None, C_GROUPS, C_CHUNK, C_CHUNK), lambda i: (l, 0, 0, 0)),
            pl.BlockSpec((None, C_GROUPS, C_CHUNK, 1), lambda i: (l, 0, 0, 0)),
            _vec_spec(l, C_WIDTH),
            wspec(0), wspec(1), wspec(2), wspec(3),
            pl.BlockSpec((1, D_MODEL), lambda i: (0, 0)) if final else _vec_spec(l + 1, D_MODEL),
        ],
        out_specs=out_specs,
        out_shape=out_shape,
        scratch_shapes=[pltpu.VMEM((tm, C_WIDTH), BF16)],
        compiler_params=_params(("arbitrary",)),
        name="outproj",
    )(x, om, os_, proj, proj, proj, wmix, bias, gv, wo, wo, wo, wo, g)


def _heads_compact(a):
    return jnp.pad(a, ((0, 0), (0, LANES - S_HEADS)))[:, None, :]


def _heads_grouped(a):
    a = a.reshape(a.shape[0], S_GROUPS, S_HPG)
    return jnp.pad(a, ((0, 0), (0, 0), (0, LANES - S_HPG))).reshape(a.shape[0], 1, S_GROUPS * LANES)


def _prepare_weights(norm_g, w_in, m_conv_w, m_conv_b, m_w_qk, m_w_vo, m_b_o, m_w_gate, m_b_gate,
                     m_norm_g, s_conv_w, s_conv_b, s_dt_bias, s_A_log, s_D, s_norm_g, c_v_norm_g,
                     c_w_s, c_b_s, w_out):
    depth = w_in.shape[0]
    vec = lambda a: a.reshape(depth, 1, -1)
    dt0 = PROJ_HEAD
    w_all = w_in.astype(BF16)
    w_dt = w_in[:, :, dt0:dt0 + S_HEADS]
    w_dt_g = jnp.pad(w_dt.reshape(depth, D_MODEL, S_GROUPS, S_HPG),
                     ((0, 0), (0, 0), (0, 0), (0, LANES - S_HPG))).reshape(depth, D_MODEL, -1).astype(BF16)
    w_dt_c = jnp.pad(w_dt, ((0, 0), (0, 0), (0, LANES - S_HEADS))).astype(BF16)
    wg = m_w_gate.reshape(depth, M_HEADS, 3, M_DH, 2 * M_HEADS)
    wg = jnp.pad(wg, ((0, 0),) * 4 + ((0, LANES - 2 * M_HEADS),)).astype(BF16)
    head_of_lane = jnp.arange(S_WIDTH) // S_DH
    expand = (jnp.arange(LANES)[:, None] == head_of_lane[None, :]).astype(BF16)
    expand = jnp.concatenate([expand, expand], axis=0)
    return dict(
        norm_g=vec(norm_g), w_all=w_all, w_tail=w_all[:, :, dt0 + S_HEADS:], w_dt_c=w_dt_c, w_dt_g=w_dt_g,
        m_cw=m_conv_w, m_cb=vec(m_conv_b),
        wqk=m_w_qk.astype(BF16), wvo=m_w_vo.astype(BF16), wg=wg,
        bg=vec(jnp.pad(m_b_gate, ((0, 0), (0, LANES - 2 * M_HEADS)))),
        bo=vec(m_b_o), m_ng=vec(m_norm_g),
        s_cw=s_conv_w, s_cb=vec(s_conv_b),
        dtb_c=_heads_compact(s_dt_bias), alog_c=_heads_compact(s_A_log),
        dtb_g=_heads_grouped(s_dt_bias), alog_g=_heads_grouped(s_A_log),
        dskip_wide=vec(jnp.repeat(s_D, S_DH, axis=1)), s_ng=vec(s_norm_g), expand=expand,
        gv=vec(c_v_norm_g), wo=w_out.astype(BF16),
    )


def kernel(x_prompt, x_sample, state_mlstm_C, state_mlstm_n, state_mlstm_m, state_mlstm_conv, state_ssm, state_ssm_conv, norm_g, w_in, m_conv_w, m_conv_b, m_w_qk, m_w_vo, m_b_o, m_w_gate, m_b_gate, m_norm_g, s_conv_w, s_conv_b, s_dt_bias, s_A_log, s_D, s_norm_g, c_v_norm_g, c_w_s, c_b_s, w_out, final_norm_g):
    bp, seq, _ = x_prompt.shape
    bs, dec_seq, _ = x_sample.shape
    depth = w_in.shape[0]
    t_lo = SAMPLE_ROWS - dec_seq
    hist_lo = t_lo - (CONV_K - 1)
    nseq = CHUNK // SAMPLE_ROWS
    W = _prepare_weights(norm_g, w_in, m_conv_w, m_conv_b, m_w_qk, m_w_vo, m_b_o, m_w_gate, m_b_gate,
                         m_norm_g, s_conv_w, s_conv_b, s_dt_bias, s_A_log, s_D, s_norm_g, c_v_norm_g,
                         c_w_s, c_b_s, w_out)
    fg = final_norm_g[None, :]
    yp = x_prompt.reshape(bp * seq, D_MODEL)
    ys = jnp.pad(x_sample, ((0, 0), (t_lo, 0), (0, 0))).reshape(bs * SAMPLE_ROWS, D_MODEL)

    pad_hist = ((0, 0), (0, 0), (hist_lo, dec_seq), (0, 0))
    hist_m = jnp.pad(state_mlstm_conv, pad_hist).reshape(depth, bs * SAMPLE_ROWS, M_WIDTH)
    hist_s = jnp.pad(state_ssm_conv, pad_hist).reshape(depth, bs * SAMPLE_ROWS, -1)
    m0rows = jnp.repeat(jnp.swapaxes(state_mlstm_m, 1, 2)[..., None], SAMPLE_ROWS, axis=2)
    n0 = state_mlstm_n[:, :, :, None, :]
    reps = C_CHUNK // SAMPLE_ROWS
    w4 = jnp.pad(c_w_s[:, :, :dec_seq, :dec_seq], ((0, 0), (0, 0), (t_lo, 0), (t_lo, 0)))
    b4 = jnp.pad(c_b_s[:, :, :dec_seq], ((0, 0), (0, 0), (t_lo, 0)))
    wmix_s = jnp.tile(w4, (1, 1, reps, reps))
    bias_s = jnp.tile(b4, (1, 1, reps))[..., None]
    bias_p = c_b_s[..., None]

    hp = _norm(0, yp, W["norm_g"])
    hs = _norm(0, ys, W["norm_g"])
    outs_p, outs_s = [], []
    c_all = h_all = None
    mc = min(M_CHUNK, seq)
    sc = min(CHUNK * SSD_SUB, seq)
    tbp = min(PROJ_ROWS, bp * seq)
    tbs = min(PROJ_ROWS, bs * SAMPLE_ROWS)
    for l in range(depth):
        final = l == depth - 1
        proj = _inproj(l, hp, W["w_all"], W["w_tail"])
        q, k, v, o, gates = _mlstm_proj(l, proj, None, W, max(1, seq // tbp), tbp)
        out_s, h1 = _ssd_prompt(l, proj, hp, W, bp, seq // sc)
        res = _mlstm_out(l, q, k, v, o, gates, proj, yp, out_s, c_w_s, bias_p, W,
                         fg if final else W["norm_g"], final, bp, seq // mc)
        c1, n1, m1 = res[0][:bp], res[1][:bp], res[2][:bp]
        yp, hp = res[3], (None if final else res[4])
        pj = proj.reshape(bp, seq, PROJ_MAIN)
        outs_p.append((
            c1, n1.reshape(bp, M_HEADS, M_DH), m1[:, :M_HEADS, 0],
            pj[:, seq - (CONV_K - 1):, COL_XM:COL_XM + M_WIDTH], h1,
            pj[:, seq - (CONV_K - 1):, COL_XBC:COL_XBC + S_WIDTH + 2 * S_BC]))
        proj = _inproj(l, hs, W["w_all"], W["w_tail"])
        q, k, v, o, gates = _mlstm_proj(l, proj, hist_m, W, 1, tbs)
        out_m, c_all, n1, mrow = _mlstm_sample(l, depth, q, k, v, o, gates, proj, W, state_mlstm_C, n0,
                                               m0rows, c_all, nseq, t_lo)
        out_s, h_all = _ssd_sample(l, depth, proj, hs, hist_s, W, state_ssm, h_all, nseq, t_lo)
        res = _outproj(l, ys, out_m, out_s, proj, wmix_s, bias_s, W["gv"], W["wo"],
                       fg if final else W["norm_g"], final, True, SAMPLE_ROWS, t_lo)
        ys, hs, vn = res[0], (None if final else res[1]), res[-1]
        pj = proj.reshape(bs, SAMPLE_ROWS, PROJ_MAIN)
        outs_s.append((
            n1.reshape(bs, M_HEADS, M_DH), mrow[:, SAMPLE_ROWS - 1::SAMPLE_ROWS, 0].T,
            pj[:, SAMPLE_ROWS - (CONV_K - 1):, COL_XM:COL_XM + M_WIDTH],
            pj[:, SAMPLE_ROWS - (CONV_K - 1):, COL_XBC:COL_XBC + S_WIDTH + 2 * S_BC],
            vn.reshape(bs, SAMPLE_ROWS, C_WIDTH)[:, t_lo:]))
    p_out = [jnp.stack([s[i] for s in outs_p]) for i in range(6)]
    s_n, s_m, s_mconv, s_sconv, s_cv = [jnp.stack([s[i] for s in outs_s]) for i in range(5)]
    y_prompt = yp.reshape(bp, seq, D_MODEL)
    y_sample = ys.reshape(bs, SAMPLE_ROWS, D_MODEL)[:, t_lo:]
    return (y_prompt, y_sample, *p_out, c_all, s_n, s_m, s_mconv, h_all, s_sconv, s_cv)
```

```python
import functools

import jax
import jax.numpy as jnp
from jax import lax
from jax.experimental import pallas as pl
from jax.experimental.pallas import tpu as pltpu

F32 = jnp.float32
BF16 = jnp.bfloat16

D_MODEL = 2048
MIX_WIDTH = 2 * D_MODEL
M_WIDTH = MIX_WIDTH // 4
M_HEADS = 4
M_DH = M_WIDTH // M_HEADS
S_WIDTH = MIX_WIDTH // 2
S_DH = 64
S_HEADS = S_WIDTH // S_DH
S_GROUPS = 4
S_HPG = S_HEADS // S_GROUPS
S_STATE = 128
S_GW = S_HPG * S_DH
S_BC = S_GROUPS * S_STATE
C_WIDTH = MIX_WIDTH // 4
C_GROUPS = 4
C_DG = C_WIDTH // C_GROUPS
C_CHUNK = 128
CONV_K = 4
EPS = 1e-6

LANES = 128
SUBLANES = 8
SAMPLE_ROWS = 8
CHUNK = 128
M_CHUNK = 256
SSD_SUB = 4
SEQ_UNROLL = 16
STATE_SLOTS = 3
NORM_ROWS = 1024
INPROJ_TILE = 1024
PROJ_ROWS = 512
VMEM_LIMIT = 56 * 1024 * 1024

COL_XM, COL_ZM, COL_ZS, COL_XBC, COL_U, COL_V, COL_ZC = 0, 1024, 2048, 4096, 7168, 8192, 9216
PROJ_MAIN = 10240
PROJ_HEAD = COL_U
COL_B = COL_XBC + S_WIDTH
COL_C = COL_B + S_BC


def _dot(a, b):
    return jnp.dot(a, b, preferred_element_type=F32)


def _dot_nt(a, b):
    return lax.dot_general(a, b, (((1,), (1,)), ((), ())), preferred_element_type=F32)


def _dot_mask(mask, x):
    m = jnp.where(mask, 1.0, 0.0).astype(BF16)
    hi = x.astype(BF16)
    r1 = x - hi.astype(F32)
    mid = r1.astype(BF16)
    lo = (r1 - mid.astype(F32)).astype(BF16)
    return _dot(m, hi) + _dot(m, mid) + _dot(m, lo)


def _widen(a, expand2):
    hi = a.astype(BF16)
    lo = (a - hi.astype(F32)).astype(BF16)
    return _dot(jnp.concatenate([hi, lo], axis=1), expand2)


def _sigmoid(x):
    return 0.5 * jnp.tanh(0.5 * x) + 0.5


def _silu(x):
    h = 0.5 * x
    return h * jnp.tanh(h) + h


def _softplus(x):
    return jnp.maximum(x, 0.0) + jnp.log1p(jnp.exp(-jnp.abs(x)))


def _log_sigmoid(x):
    return jnp.minimum(x, 0.0) - jnp.log1p(jnp.exp(-jnp.abs(x)))


def _rms(x, g):
    return x * lax.rsqrt(jnp.mean(x * x, axis=-1, keepdims=True) + EPS) * g


def _params(sem):
    return pltpu.CompilerParams(dimension_semantics=sem, vmem_limit_bytes=VMEM_LIMIT)


def _vec_spec(l, n, col=None):
    if col is None:
        return pl.BlockSpec((None, 1, n), lambda *ids: (l, 0, 0))
    return pl.BlockSpec((None, 1, n), lambda *ids: (l, 0, col(*ids)))


def _norm_kernel(x_ref, g_ref, h_ref):
    h_ref[...] = _rms(x_ref[...], g_ref[...]).astype(BF16)


def _norm(l, x, g):
    rows = x.shape[0]
    tb = min(NORM_ROWS, rows)
    return pl.pallas_call(
        _norm_kernel,
        grid=(rows // tb,),
        in_specs=[pl.BlockSpec((tb, D_MODEL), lambda i: (i, 0)), _vec_spec(l, D_MODEL)],
        out_specs=pl.BlockSpec((tb, D_MODEL), lambda i: (i, 0)),
        out_shape=jax.ShapeDtypeStruct((rows, D_MODEL), BF16),
        compiler_params=_params(("arbitrary",)),
        name="norm",
    )(x, g)


def _inproj_kernel(h_ref, wa_ref, wb_ref, proj_ref, *, n_head):
    j = pl.program_id(0)

    @pl.when(j < n_head)
    def _():
        proj_ref[...] = _dot(h_ref[...], wa_ref[...])

    @pl.when(j >= n_head)
    def _():
        proj_ref[...] = _dot(h_ref[...], wb_ref[...])


def _inproj(l, h, w_all, w_tail):
    rows = h.shape[0]
    tm = min(INPROJ_TILE, rows)
    tn = INPROJ_TILE
    n_head = PROJ_HEAD // tn
    return pl.pallas_call(
        functools.partial(_inproj_kernel, n_head=n_head),
        grid=(PROJ_MAIN // tn, rows // tm),
        in_specs=[
            pl.BlockSpec((tm, D_MODEL), lambda j, i: (i, 0)),
            pl.BlockSpec((None, D_MODEL, tn), lambda j, i: (l, 0, jnp.minimum(j, n_head - 1))),
            pl.BlockSpec((None, D_MODEL, tn), lambda j, i: (l, 0, jnp.maximum(j - n_head, 0))),
        ],
        out_specs=pl.BlockSpec((tm, tn), lambda j, i: (i, j)),
        out_shape=jax.ShapeDtypeStruct((rows, PROJ_MAIN), F32),
        compiler_params=_params(("arbitrary", "arbitrary")),
        name="inproj",
    )(h, w_all, w_tail)


def _conv_rows(x, first, prev_sc, cw_ref, cb_ref):
    assert CONV_K == 4
    tb = x.shape[0]

    if first is not False:
        @pl.when(first)
        def _():
            prev_sc[...] = jnp.zeros(prev_sc.shape, F32)

    xe = jnp.concatenate([prev_sc[...], x], axis=0)
    x1 = pltpu.roll(xe, 1, 0)
    pair = cw_ref[1:2, :] * xe + cw_ref[0:1, :] * x1
    acc = (cb_ref[...] + cw_ref[3:4, :] * x + cw_ref[2:3, :] * x1[SUBLANES:, :]
           + pltpu.roll(pair, 2, 0)[SUBLANES:, :])
    prev_sc[...] = x[tb - SUBLANES:, :]
    return _silu(acc)


def _mlstm_proj_kernel(*refs, has_hist, blocks_per_seq):
    if has_hist:
        (xm_ref, he_ref, cw_ref, cb_ref, wqk_ref, wvo_ref, wg_ref, bg_ref,
         q_ref, k_ref, v_ref, o_ref, gates_ref, xe_sc) = refs
    else:
        (xm_ref, cw_ref, cb_ref, wqk_ref, wvo_ref, wg_ref, bg_ref,
         q_ref, k_ref, v_ref, o_ref, gates_ref, xe_sc) = refs
    i = pl.program_id(0)
    x = xm_ref[...]
    xin = x + he_ref[...] if has_hist else x
    xmc = _conv_rows(xin, i % blocks_per_seq == 0, xe_sc, cw_ref, cb_ref)
    tb = x.shape[0]
    gates = jnp.broadcast_to(bg_ref[...], (tb, LANES))
    for h in range(M_HEADS):
        cols = slice(h * M_DH, (h + 1) * M_DH)
        qk = _dot(xmc[:, cols].astype(BF16), wqk_ref[h])
        vo = _dot(x[:, cols].astype(BF16), wvo_ref[h])
        qb = qk[:, :M_DH].astype(BF16)
        kb = qk[:, M_DH:].astype(BF16)
        vb = vo[:, :M_DH].astype(BF16)
        gates = gates + _dot(qb, wg_ref[h, 0]) + _dot(kb, wg_ref[h, 1]) + _dot(vb, wg_ref[h, 2])
        q_ref[:, cols] = qb
        k_ref[:, cols] = (qk[:, M_DH:] * (M_DH ** -0.5)).astype(BF16)
        v_ref[:, cols] = vb
        o_ref[:, cols] = vo[:, M_DH:]
    lane = lax.broadcasted_iota(jnp.int32, (tb, LANES), 1)
    gates_ref[...] = jnp.where(lane < M_HEADS, gates, _log_sigmoid(gates))


def _mlstm_proj(l, proj, hist, W, blocks_per_seq, tb):
    rows = proj.shape[0]
    has_hist = hist is not None
    in_specs = [pl.BlockSpec((tb, M_WIDTH), lambda i: (i, COL_XM // M_WIDTH))]
    args = [proj]
    if has_hist:
        in_specs.append(pl.BlockSpec((None, tb, M_WIDTH), lambda i: (l, i, 0)))
        args.append(hist)
    in_specs += [
        pl.BlockSpec((None, CONV_K, M_WIDTH), lambda i: (l, 0, 0)),
        _vec_spec(l, M_WIDTH),
        pl.BlockSpec((None, M_HEADS, M_DH, 2 * M_DH), lambda i: (l, 0, 0, 0)),
        pl.BlockSpec((None, M_HEADS, M_DH, 2 * M_DH), lambda i: (l, 0, 0, 0)),
        pl.BlockSpec((None, M_HEADS, 3, M_DH, LANES), lambda i: (l, 0, 0, 0, 0)),
        _vec_spec(l, LANES),
    ]
    args += [W["m_cw"], W["m_cb"], W["wqk"], W["wvo"], W["wg"], W["bg"]]
    row_spec = pl.BlockSpec((tb, M_WIDTH), lambda i: (i, 0))
    return pl.pallas_call(
        functools.partial(_mlstm_proj_kernel, has_hist=has_hist, blocks_per_seq=blocks_per_seq),
        grid=(rows // tb,),
        in_specs=in_specs,
        out_specs=[row_spec, row_spec, row_spec, row_spec, pl.BlockSpec((tb, LANES), lambda i: (i, 0))],
        out_shape=[
            jax.ShapeDtypeStruct((rows, M_WIDTH), BF16),
            jax.ShapeDtypeStruct((rows, M_WIDTH), BF16),
            jax.ShapeDtypeStruct((rows, M_WIDTH), BF16),
            jax.ShapeDtypeStruct((rows, M_WIDTH), F32),
            jax.ShapeDtypeStruct((rows, LANES), F32),
        ],
        scratch_shapes=[pltpu.VMEM((SUBLANES, M_WIDTH), F32)],
        compiler_params=_params(("arbitrary",)),
        name="mlstm_proj",
    )(*args)


def _mlstm_finish(hh, o, bo, ng, zm):
    hm = _sigmoid(o + bo) * hh
    return _rms(hm, ng) * _silu(zm)


def _mlstm_prep(gates_ref):
    L = gates_ref.shape[0]
    r2 = lax.broadcasted_iota(jnp.int32, (L, L), 0)
    c2 = lax.broadcasted_iota(jnp.int32, (L, L), 1)
    causal = c2 <= r2
    g = gates_ref[...]
    b_all = _dot_mask(causal, g)
    return causal, g, b_all, g.T, b_all.T


def _mlstm_head(h, prep, q_ref, k_ref, v_ref, o_ref, zm_ref, bo_ref, ng_ref, c1_ref, n1_ref, m1_ref, emit):
    causal, g, b_all, g_t, b_t = prep
    L = q_ref.shape[0]
    neg_inf = -jnp.inf
    cols = slice(h * M_DH, (h + 1) * M_DH)
    f = M_HEADS + h
    g_row = g_t[h:h + 1, :] - b_t[f:f + 1, :]
    b_col = b_all[:, f:f + 1]
    g_col = g[:, h:h + 1] - b_col
    mprev = m1_ref[0, h:h + 1, 0:1]
    gm = jnp.where(causal, g_row, neg_inf)
    m_col = jnp.maximum(mprev, jnp.max(gm, axis=1, keepdims=True))
    w_intra = jnp.exp(gm - m_col)
    w_inter = jnp.exp(mprev - m_col)
    m_new = b_col + m_col
    m_last = m_col[L - 1:L, :]
    wl_col = jnp.exp(g_col - m_last)
    wli = jnp.exp(mprev - m_last)
    q = q_ref[:, cols]
    k = k_ref[:, cols]
    v = v_ref[:, cols]
    cst = c1_ref[0, h]
    nst = n1_ref[0, h]
    s = _dot_nt(q, k) * w_intra
    num = _dot(s.astype(BF16), v) + w_inter * _dot_nt(q, cst.astype(BF16))
    den = (jnp.sum(s, axis=1, keepdims=True)
           + w_inter * jnp.sum(q.astype(F32) * nst, axis=1, keepdims=True))
    hh = num / jnp.maximum(jnp.abs(den), jnp.exp(-m_new))
    emit(cols, _mlstm_finish(hh, o_ref[:, cols], bo_ref[:, cols], ng_ref[:, cols], zm_ref[:, cols]))

    def update_state():
        c1_ref[0, h] = wli * cst + _dot((v.astype(F32) * wl_col).T.astype(BF16), k)
        n1_ref[0, h] = wli * nst + jnp.sum(k.astype(F32) * wl_col, axis=0, keepdims=True)
        m1_ref[0, h:h + 1, :] = jnp.broadcast_to(m_new[L - 1:L, :], (1, LANES))

    return update_state


def _block_masks(L, rs, t_lo):
    shift = rs.bit_length() - 1
    r2 = lax.broadcasted_iota(jnp.int32, (L, L), 0)
    c2 = lax.broadcasted_iota(jnp.int32, (L, L), 1)
    same = (r2 >> shift) == (c2 >> shift)
    valid_c = (c2 & (rs - 1)) >= t_lo
    ridx = lax.broadcasted_iota(jnp.int32, (L, 1), 0)
    valid_r = (ridx & (rs - 1)) >= t_lo
    return r2, c2, same, valid_c, ridx, valid_r


def _mlstm_sample_kernel(q_ref, k_ref, v_ref, o_ref, zm_ref, gates_ref, bo_ref, ng_ref,
                         c0_ref, n0_ref, m0_ref, alias_ref,
                         out_ref, c1_ref, n1_ref, mrow_ref,
                         gt_sc, bt_sc, col_sc, q_sc, numi_sc, nrow_sc, wk_sc, cbuf, csem, *, nseq, t_lo, layer):
    del alias_ref
    h = pl.program_id(1)
    L = q_ref.shape[0]
    rs = L // nseq

    step = pl.program_id(0) * M_HEADS + h
    nsteps = pl.num_programs(0) * M_HEADS

    def state_copy(s):
        return pltpu.make_async_copy(
            c0_ref.at[layer, pl.ds((s // M_HEADS) * nseq, nseq), s % M_HEADS],
            cbuf.at[s % STATE_SLOTS], csem.at[s % STATE_SLOTS])

    @pl.when(step == 0)
    def _():
        state_copy(step).start()
        state_copy(step + 1).start()

    @pl.when(step + 2 < nsteps)
    def _():
        state_copy(step + 2).start()

    state_copy(step).wait()
    slot = step % STATE_SLOTS
    r2, c2, same, valid_c, ridx, valid_r = _block_masks(L, rs, t_lo)
    eye = r2 == c2
    neg_inf = -jnp.inf
    g = gates_ref[...]
    b_all = _dot_mask(same & (c2 <= r2), jnp.where(valid_r, g, 0.0))
    gt_sc[...] = g.T
    bt_sc[...] = b_all.T
    lane_g = lax.broadcasted_iota(jnp.int32, (1, LANES), 1)
    i_col = jnp.sum(jnp.where(lane_g == h, g, 0.0), axis=1, keepdims=True)
    b_col = jnp.sum(jnp.where(lane_g == h + M_HEADS, b_all, 0.0), axis=1, keepdims=True)
    g_row = gt_sc[pl.ds(h, 1), :] - bt_sc[pl.ds(h + M_HEADS, 1), :]
    g_col = i_col - b_col
    mask = same & (c2 <= r2) & valid_c
    mprev = m0_ref[0]
    gm = jnp.where(mask, g_row, neg_inf)
    m_col = jnp.maximum(mprev, jnp.max(gm, axis=1, keepdims=True))
    m_row = jnp.sum(jnp.where(eye, m_col, 0.0), axis=0, keepdims=True)
    mlast_col = jnp.max(jnp.where(same, m_row, neg_inf), axis=1, keepdims=True)
    w_intra = jnp.exp(gm - m_col)
    w_inter = jnp.exp(mprev - m_col)
    m_new = b_col + m_col
    wl_col = jnp.where(valid_r, jnp.exp(g_col - mlast_col), 0.0)
    col_sc[:, 0:1] = jnp.exp(mprev - mlast_col)

    q = q_ref[...]
    k = k_ref[...]
    v = v_ref[...]
    s = _dot_nt(q, k) * w_intra
    num = _dot(s.astype(BF16), v)
    den = jnp.sum(s, axis=1, keepdims=True)
    wvt = (v.astype(F32) * wl_col).T.astype(BF16)
    wk_sc[...] = k.astype(F32) * wl_col
    qf = q.astype(F32)
    q_sc[0:L, :] = qf
    q_sc[L:L + SUBLANES, :] = jnp.zeros((SUBLANES, M_DH), F32)

    def seq_step(j, carry):
        rows = pl.ds(pl.multiple_of(j * rs, rs), rs)
        in_seq_r = (ridx >= j * rs) & (ridx < (j + 1) * rs)
        cj = cbuf[slot, j]
        nj = n0_ref[j, 0]
        q2 = q_sc[pl.ds(pl.multiple_of(j * rs, rs), 2 * rs), :].astype(BF16)
        numi_sc[rows, :] = _dot_nt(q2, cj.astype(BF16))[:rs]
        nrow_sc[rows, :] = jnp.broadcast_to(nj, (rs, M_DH))
        wli = col_sc[pl.ds(j * rs + rs - 1, 1), 0:1]
        c1_ref[j, 0] = wli * cj + _dot(wvt, jnp.where(in_seq_r, k, jnp.zeros_like(k)))
        n1_ref[j, 0] = wli * nj + jnp.sum(wk_sc[rows, :], axis=0, keepdims=True)
        return carry

    lax.fori_loop(0, nseq, seq_step, 0, unroll=SEQ_UNROLL)
    num = num + w_inter * numi_sc[...]
    den = den + w_inter * jnp.sum(qf * nrow_sc[...], axis=1, keepdims=True)
    hh = num / jnp.maximum(jnp.abs(den), jnp.exp(-m_new))
    out_ref[...] = _mlstm_finish(hh, o_ref[...], bo_ref[...], ng_ref[...], zm_ref[...]).astype(BF16)
    mrow_ref[0] = m_new


def _mlstm_sample(l, depth, q, k, v, o, gates, proj, W, c0, n0, m0rows, c_prev, nseq, t_lo):
    L = CHUNK
    rows = q.shape[0]
    nb = rows // L
    nbatch = nb * nseq
    rowblk = lambda b, h: (b, h)
    c_spec = pl.BlockSpec((None, nseq, 1, M_DH, M_DH), lambda b, h: (l, b, h, 0, 0))
    n_spec_in = pl.BlockSpec((None, nseq, 1, 1, M_DH), lambda b, h: (l, b, h, 0, 0))
    n_spec_out = pl.BlockSpec((nseq, 1, 1, M_DH), lambda b, h: (b, h, 0, 0))
    has_prev = c_prev is not None
    in_specs = [
        pl.BlockSpec((L, M_DH), rowblk),
        pl.BlockSpec((L, M_DH), rowblk),
        pl.BlockSpec((L, M_DH), rowblk),
        pl.BlockSpec((L, M_DH), rowblk),
        pl.BlockSpec((L, M_DH), lambda b, h: (b, COL_ZM // M_DH + h)),
        pl.BlockSpec((L, LANES), lambda b, h: (b, 0)),
        _vec_spec(l, M_DH, lambda b, h: h),
        _vec_spec(l, M_DH, lambda b, h: h),
        pl.BlockSpec(memory_space=pl.ANY),
        n_spec_in,
        pl.BlockSpec((None, 1, L, 1), lambda b, h: (l, h, b, 0)),
        pl.BlockSpec(memory_space=pl.ANY),
    ]
    args = [q, k, v, o, proj, gates, W["bo"], W["m_ng"], c0, n0, m0rows,
            c_prev if has_prev else jnp.zeros((SUBLANES, LANES), F32)]
    return pl.pallas_call(
        functools.partial(_mlstm_sample_kernel, nseq=nseq, t_lo=t_lo, layer=l),
        grid=(nb, M_HEADS),
        in_specs=in_specs,
        out_specs=[pl.BlockSpec((L, M_DH), rowblk), c_spec, n_spec_out,
                   pl.BlockSpec((1, L, 1), lambda b, h: (h, b, 0))],
        out_shape=[
            jax.ShapeDtypeStruct((rows, M_WIDTH), BF16),
            jax.ShapeDtypeStruct((depth, nbatch, M_HEADS, M_DH, M_DH), F32),
            jax.ShapeDtypeStruct((nbatch, M_HEADS, 1, M_DH), F32),
            jax.ShapeDtypeStruct((M_HEADS, rows, 1), F32),
        ],
        scratch_shapes=[
            pltpu.VMEM((LANES, L), F32),
            pltpu.VMEM((LANES, L), F32),
            pltpu.VMEM((L, LANES), F32),
            pltpu.VMEM((L + SUBLANES, M_DH), F32),
            pltpu.VMEM((L, M_DH), F32),
            pltpu.VMEM((L, M_DH), F32),
            pltpu.VMEM((L, M_DH), F32),
            pltpu.VMEM((STATE_SLOTS, nseq, M_DH, M_DH), F32),
            pltpu.SemaphoreType.DMA((STATE_SLOTS,)),
        ],
        input_output_aliases={11: 1} if has_prev else {},
        compiler_params=_params(("arbitrary", "arbitrary")),
        name="mlstm_sample",
    )(*args)


def _ssd_prompt_kernel(xs_ref, b_ref, c_ref, zs_ref, h_ref, wdt_ref,
                       cwx_ref, cwb_ref, cwc_ref, cbx_ref, cbb_ref, cbc_ref,
                       dtb_ref, alog_ref, dskip_ref, ng_ref, expand_ref,
                       out_ref, h1_ref, xex_sc, xeb_sc, xec_sc, ht_sc):
    c = pl.program_id(1)
    nc = pl.num_programs(1)
    L = CHUNK

    @pl.when(c == 0)
    def _():
        ht_sc[...] = jnp.zeros(ht_sc.shape, F32)

    for sub in range(xs_ref.shape[0] // L):
        _ssd_prompt_chunk(slice(sub * L, (sub + 1) * L), (c == 0) if sub == 0 else False,
                          xs_ref, b_ref, c_ref, zs_ref, h_ref, wdt_ref,
                          cwx_ref, cwb_ref, cwc_ref, cbx_ref, cbb_ref, cbc_ref,
                          dtb_ref, alog_ref, dskip_ref, ng_ref, expand_ref,
                          out_ref, xex_sc, xeb_sc, xec_sc, ht_sc)

    @pl.when(c == nc - 1)
    def _():
        for pr in range(S_HEADS // 2):
            blk = ht_sc[:, pr * LANES:(pr + 1) * LANES].T
            h1_ref[0, 2 * pr] = blk[:S_DH]
            h1_ref[0, 2 * pr + 1] = blk[S_DH:]


def _ssd_prompt_chunk(rows, first, xs_ref, b_ref, c_ref, zs_ref, h_ref, wdt_ref,
                      cwx_ref, cwb_ref, cwc_ref, cbx_ref, cbb_ref, cbc_ref,
                      dtb_ref, alog_ref, dskip_ref, ng_ref, expand_ref,
                      out_ref, xex_sc, xeb_sc, xec_sc, ht_sc):
    L = CHUNK
    xs = _conv_rows(xs_ref[rows, :], first, xex_sc, cwx_ref, cbx_ref)
    bm = _conv_rows(b_ref[rows, :], first, xeb_sc, cwb_ref, cbb_ref)
    cm = _conv_rows(c_ref[rows, :], first, xec_sc, cwc_ref, cbc_ref).astype(BF16)

    r2 = lax.broadcasted_iota(jnp.int32, (L, L), 0)
    c2 = lax.broadcasted_iota(jnp.int32, (L, L), 1)
    causal = c2 <= r2
    neg_inf = -jnp.inf
    dt = _softplus(_dot(h_ref[rows, :], wdt_ref[...]) + dtb_ref[...])
    da = dt * (-jnp.exp(alog_ref[...]))
    cs = _dot_mask(causal, da)
    cs_t = cs.T
    ecs = jnp.exp(cs)
    wend = jnp.exp(cs[L - 1:L, :] - cs) * dt

    expand = expand_ref[...]
    dte = _widen(dt, expand)
    wende = _widen(wend, expand)
    ecse = _widen(ecs, expand)
    xdt = (xs * dte).astype(BF16)
    wx = (xs * wende).astype(BF16)
    lane = lax.broadcasted_iota(jnp.int32, (1, LANES), 1)
    low_half = lane < S_DH
    zero_slab = jnp.zeros((L, LANES), BF16)
    for g in range(S_GROUPS):
        gcols = slice(g * S_GW, (g + 1) * S_GW)
        scols = slice(g * S_STATE, (g + 1) * S_STATE)
        bg = bm[:, scols]
        cg = cm[:, scols]
        cb = _dot_nt(cg, bg.astype(BF16))
        pairs = []
        for pr in range(S_HPG // 2):
            h0 = g * S_HPG + 2 * pr
            mixes = []
            for hh in (h0, h0 + 1):
                dec = jnp.exp(jnp.where(causal, cs[:, hh:hh + 1] - cs_t[hh:hh + 1, :], neg_inf))
                mixes.append((cb * dec).astype(BF16))
            slab = xdt[:, h0 * S_DH:(h0 + 2) * S_DH]
            rhs = jnp.concatenate([jnp.where(low_half, slab, zero_slab),
                                   jnp.where(low_half, zero_slab, slab)], axis=0)
            pairs.append(_dot(jnp.concatenate(mixes, axis=1), rhs))
        y_intra = jnp.concatenate(pairs, axis=1)
        ht = ht_sc[:, gcols]
        y = y_intra + ecse[:, gcols] * _dot(cg, ht.astype(BF16)) + dskip_ref[:, gcols] * xs[:, gcols]
        y = y * _silu(zs_ref[rows, gcols])
        out_ref[rows, gcols] = _rms(y, ng_ref[:, gcols]).astype(BF16)
        ht_sc[:, gcols] = ecse[L - 1:L, gcols] * ht + _dot(bg.T.astype(BF16), wx[:, gcols])


def _ssd_prompt(l, proj, h, W, nb, nc):
    L = CHUNK * SSD_SUB
    rows = nb * nc * L
    row = lambda b, c: b * nc + c
    conv_w = lambda width, blk: pl.BlockSpec((None, CONV_K, width), lambda b, c: (l, 0, blk))
    return pl.pallas_call(
        _ssd_prompt_kernel,
        grid=(nb, nc),
        in_specs=[
            pl.BlockSpec((L, S_WIDTH), lambda b, c: (row(b, c), COL_XBC // S_WIDTH)),
            pl.BlockSpec((L, S_BC), lambda b, c: (row(b, c), COL_B // S_BC)),
            pl.BlockSpec((L, S_BC), lambda b, c: (row(b, c), COL_C // S_BC)),
            pl.BlockSpec((L, S_WIDTH), lambda b, c: (row(b, c), COL_ZS // S_WIDTH)),
            pl.BlockSpec((L, D_MODEL), lambda b, c: (row(b, c), 0)),
            pl.BlockSpec((None, D_MODEL, LANES), lambda b, c: (l, 0, 0)),
            conv_w(S_WIDTH, 0), conv_w(S_BC, S_WIDTH // S_BC), conv_w(S_BC, S_WIDTH // S_BC + 1),
            _vec_spec(l, S_WIDTH, lambda b, c: 0),
            _vec_spec(l, S_BC, lambda b, c: S_WIDTH // S_BC),
            _vec_spec(l, S_BC, lambda b, c: S_WIDTH // S_BC + 1),
            _vec_spec(l, LANES),
            _vec_spec(l, LANES),
            _vec_spec(l, S_WIDTH),
            _vec_spec(l, S_WIDTH),
            pl.BlockSpec((2 * LANES, S_WIDTH), lambda b, c: (0, 0)),
        ],
        out_specs=[
            pl.BlockSpec((L, S_WIDTH), lambda b, c: (row(b, c), 0)),
            pl.BlockSpec((1, S_HEADS, S_DH, S_STATE), lambda b, c: (b, 0, 0, 0)),
        ],
        out_shape=[
            jax.ShapeDtypeStruct((rows, S_WIDTH), BF16),
            jax.ShapeDtypeStruct((nb, S_HEADS, S_DH, S_STATE), F32),
        ],
        scratch_shapes=[
            pltpu.VMEM((SUBLANES, S_WIDTH), F32),
            pltpu.VMEM((SUBLANES, S_BC), F32),
            pltpu.VMEM((SUBLANES, S_BC), F32),
            pltpu.VMEM((S_STATE, S_WIDTH), F32),
        ],
        compiler_params=_params(("arbitrary", "arbitrary")),
        name="ssd_prompt",
    )(proj, proj, proj, proj, h, W["w_dt_c"], W["s_cw"], W["s_cw"], W["s_cw"], W["s_cb"], W["s_cb"], W["s_cb"],
      W["dtb_c"], W["alog_c"], W["dskip_wide"], W["s_ng"], W["expand"])


def _ssd_sample_kernel(xs_ref, b_ref, c_ref, zs_ref, h_ref, wdt_ref, hx_ref, hb_ref, hc_ref,
                       cwx_ref, cwb_ref, cwc_ref, cbx_ref, cbb_ref, cbc_ref,
                       dtb_ref, alog_ref, dskip_ref, ng_ref, expand_ref, h0_ref, alias_ref,
                       out_ref, h1_ref, xex_sc, xeb_sc, xec_sc, tot_sc, cm_sc, yi_sc, hbuf, hsem,
                       *, nseq, t_lo, layer):
    del alias_ref
    L = xs_ref.shape[0]
    rs = L // nseq

    step = pl.program_id(0) * S_GROUPS + pl.program_id(1)
    nsteps = pl.num_programs(0) * S_GROUPS

    def state_copy(s):
        return pltpu.make_async_copy(
            h0_ref.at[layer, pl.ds((s // S_GROUPS) * nseq, nseq), pl.ds((s % S_GROUPS) * S_HPG, S_HPG)],
            hbuf.at[s % STATE_SLOTS], hsem.at[s % STATE_SLOTS])

    @pl.when(step == 0)
    def _():
        state_copy(step).start()
        state_copy(step + 1).start()

    @pl.when(step + 2 < nsteps)
    def _():
        state_copy(step + 2).start()

    state_copy(step).wait()
    slot = step % STATE_SLOTS
    first = True
    xs = _conv_rows(xs_ref[...] + hx_ref[...], first, xex_sc, cwx_ref, cbx_ref)
    bm = _conv_rows(b_ref[...] + hb_ref[...], first, xeb_sc, cwb_ref, cbb_ref)
    cm = _conv_rows(c_ref[...] + hc_ref[...], first, xec_sc, cwc_ref, cbc_ref)
    bmb = bm.astype(BF16)
    cmb = cm.astype(BF16)
    cm_sc[0:L, :] = cm
    cm_sc[L:L + SUBLANES, :] = jnp.zeros((SUBLANES, S_STATE), F32)

    r2, c2, same, valid_c, ridx, valid_r = _block_masks(L, rs, t_lo)
    mask = same & (c2 <= r2) & valid_c
    neg_inf = -jnp.inf
    dt = jnp.where(valid_r, _softplus(_dot(h_ref[...], wdt_ref[...]) + dtb_ref[...]), 0.0)
    da = dt * (-jnp.exp(alog_ref[...]))
    cs = _dot_mask(mask, da)
    sfx = _dot_mask(same & (c2 > r2), da)
    tot_sc[...] = cs + sfx
    cs_t = cs.T
    expand = expand_ref[...]
    dte = _widen(dt, expand)
    wende = _widen(jnp.exp(sfx) * dt, expand)
    ecse = _widen(jnp.exp(cs), expand)
    xdt = (xs * dte).astype(BF16)
    wxt = (xs * wende).T.astype(BF16)
    cb = _dot_nt(cmb, bmb)
    lane = lax.broadcasted_iota(jnp.int32, (1, LANES), 1)
    low_half = lane < S_DH
    zero_slab = jnp.zeros((L, LANES), BF16)
    pairs = []
    for pr in range(S_HPG // 2):
        mixes = []
        for hh in (2 * pr, 2 * pr + 1):
            dec = jnp.exp(jnp.where(mask, cs[:, hh:hh + 1] - cs_t[hh:hh + 1, :], neg_inf))
            mixes.append((cb * dec).astype(BF16))
        slab = xdt[:, pr * LANES:(pr + 1) * LANES]
        rhs = jnp.concatenate([jnp.where(low_half, slab, zero_slab),
                               jnp.where(low_half, zero_slab, slab)], axis=0)
        pairs.append(_dot(jnp.concatenate(mixes, axis=1), rhs))
    y_intra = jnp.concatenate(pairs, axis=1)

    def seq_step(j, carry):
        rows = pl.ds(pl.multiple_of(j * rs, rs), rs)
        in_seq_r = (ridx >= j * rs) & (ridx < (j + 1) * rs)
        hj = hbuf[slot, j]
        c2rows = cm_sc[pl.ds(pl.multiple_of(j * rs, rs), 2 * rs), :].astype(BF16)
        yi_sc[rows, :] = _dot_nt(c2rows, hj.reshape(S_GW, S_STATE).astype(BF16))[:rs]
        upd = _dot(wxt, jnp.where(in_seq_r, bmb, jnp.zeros_like(bmb)))
        dec_j = jnp.exp(tot_sc[pl.ds(j * rs + rs - 1, 1), :])
        for hh in range(S_HPG):
            h1_ref[j, hh] = dec_j[:, hh:hh + 1] * hj[hh] + upd[hh * S_DH:(hh + 1) * S_DH, :]
        return carry

    lax.fori_loop(0, nseq, seq_step, 0, unroll=SEQ_UNROLL)
    y = y_intra + ecse * yi_sc[...] + dskip_ref[...] * xs
    out_ref[...] = _rms(y * _silu(zs_ref[...]), ng_ref[...]).astype(BF16)


def _ssd_sample(l, depth, proj, h, hist, W, h0, h_prev, nseq, t_lo):
    L = CHUNK
    rows = proj.shape[0]
    nb = rows // L
    xblk = COL_XBC // S_GW
    bblk = COL_B // S_STATE
    cblk = COL_C // S_STATE
    hb_blk = S_WIDTH // S_STATE
    hc_blk = hb_blk + S_GROUPS
    has_prev = h_prev is not None
    conv_w = lambda width, blk: pl.BlockSpec((None, CONV_K, width), lambda b, g: (l, 0, blk(g)))
    h_spec = pl.BlockSpec((None, nseq, S_HPG, S_DH, S_STATE), lambda b, g: (l, b, g, 0, 0))
    in_specs = [
        pl.BlockSpec((L, S_GW), lambda b, g: (b, xblk + g)),
        pl.BlockSpec((L, S_STATE), lambda b, g: (b, bblk + g)),
        pl.BlockSpec((L, S_STATE), lambda b, g: (b, cblk + g)),
        pl.BlockSpec((L, S_GW), lambda b, g: (b, COL_ZS // S_GW + g)),
        pl.BlockSpec((L, D_MODEL), lambda b, g: (b, 0)),
        pl.BlockSpec((None, D_MODEL, LANES), lambda b, g: (l, 0, g)),
        pl.BlockSpec((None, L, S_GW), lambda b, g: (l, b, g)),
        pl.BlockSpec((None, L, S_STATE), lambda b, g: (l, b, hb_blk + g)),
        pl.BlockSpec((None, L, S_STATE), lambda b, g: (l, b, hc_blk + g)),
        conv_w(S_GW, lambda g: g), conv_w(S_STATE, lambda g: hb_blk + g), conv_w(S_STATE, lambda g: hc_blk + g),
        _vec_spec(l, S_GW, lambda b, g: g),
        _vec_spec(l, S_STATE, lambda b, g: hb_blk + g),
        _vec_spec(l, S_STATE, lambda b, g: hc_blk + g),
        _vec_spec(l, LANES, lambda b, g: g),
        _vec_spec(l, LANES, lambda b, g: g),
        _vec_spec(l, S_GW, lambda b, g: g),
        _vec_spec(l, S_GW, lambda b, g: g),
        pl.BlockSpec((2 * LANES, S_GW), lambda b, g: (0, 0)),
        pl.BlockSpec(memory_space=pl.ANY),
        pl.BlockSpec(memory_space=pl.ANY),
    ]
    args = [proj, proj, proj, proj, h, W["w_dt_g"], hist, hist, hist,
            W["s_cw"], W["s_cw"], W["s_cw"], W["s_cb"], W["s_cb"], W["s_cb"],
            W["dtb_g"], W["alog_g"], W["dskip_wide"], W["s_ng"], W["expand"], h0,
            h_prev if has_prev else jnp.zeros((SUBLANES, LANES), F32)]
    return pl.pallas_call(
        functools.partial(_ssd_sample_kernel, nseq=nseq, t_lo=t_lo, layer=l),
        grid=(nb, S_GROUPS),
        in_specs=in_specs,
        out_specs=[pl.BlockSpec((L, S_GW), lambda b, g: (b, g)), h_spec],
        out_shape=[
            jax.ShapeDtypeStruct((rows, S_WIDTH), BF16),
            jax.ShapeDtypeStruct((depth, nb * nseq, S_HEADS, S_DH, S_STATE), F32),
        ],
        scratch_shapes=[
            pltpu.VMEM((SUBLANES, S_GW), F32),
            pltpu.VMEM((SUBLANES, S_STATE), F32),
            pltpu.VMEM((SUBLANES, S_STATE), F32),
            pltpu.VMEM((L, LANES), F32),
            pltpu.VMEM((L + SUBLANES, S_STATE), F32),
            pltpu.VMEM((L, S_GW), F32),
            pltpu.VMEM((STATE_SLOTS, nseq, S_HPG, S_DH, S_STATE), F32),
            pltpu.SemaphoreType.DMA((STATE_SLOTS,)),
        ],
        input_output_aliases={21: 1} if has_prev else {},
        compiler_params=_params(("arbitrary", "arbitrary")),
        name="ssd_sample",
    )(*args)


def _outproj_body(x_ref, om, os_ref, u_ref, v_ref, zc_ref, wmix_ref, bias_ref, gv_ref,
                  w0_ref, w1_ref, w2_ref, w3_ref, g_ref, y_ref, h_ref, vn_ref, oc_sc, *, final, rs, t_lo,
                  between=None):
    emit_vn = vn_ref is not None
    if between is None:
        between = lambda k: None
    L = C_CHUNK
    r2, c2, same, valid_c, _, _ = _block_masks(L, min(rs, L), t_lo)
    mask = same & (c2 <= r2) & valid_c
    for g in range(C_GROUPS):
        cols = slice(g * C_DG, (g + 1) * C_DG)
        w = jnp.where(mask, wmix_ref[g], 0.0).astype(BF16)
        for ch in range(x_ref.shape[0] // L):
            rows = slice(ch * L, (ch + 1) * L)
            vn = _rms(v_ref[rows, cols], gv_ref[:, cols])
            if emit_vn:
                vn_ref[rows, cols] = vn
            mixed = _dot(w, vn.astype(BF16)) + bias_ref[g]
            oc_sc[rows, cols] = (u_ref[rows, cols] * mixed * _silu(zc_ref[rows, cols])).astype(BF16)
    kb = w0_ref.shape[0]
    hk = kb // 2
    lhs = (lambda c: om[:, c],
           lambda c: os_ref[:, c].astype(BF16),
           lambda c: os_ref[:, slice(kb + c.start, kb + c.stop)].astype(BF16),
           lambda c: oc_sc[:, c])
    acc = None
    for blk, w_ref in enumerate((w0_ref, w1_ref, w2_ref, w3_ref)):
        for part in range(2):
            c = slice(part * hk, (part + 1) * hk)
            d = _dot(lhs[blk](c), w_ref[c, :])
            acc = d if acc is None else acc + d
            between(2 * blk + part)
    y = x_ref[...] + acc
    if t_lo:
        ridx = lax.broadcasted_iota(jnp.int32, (y.shape[0], 1), 0)
        y = jnp.where((ridx & (rs - 1)) >= t_lo, y, 0.0)
    if final:
        y_ref[...] = _rms(y, g_ref[...])
    else:
        y_ref[...] = y
        h_ref[...] = _rms(y, g_ref[...]).astype(BF16)


def _outproj_kernel(*refs, final, emit_vn, rs, t_lo):
    (x_ref, om_ref, os_ref, u_ref, v_ref, zc_ref, wmix_ref, bias_ref, gv_ref,
     w0_ref, w1_ref, w2_ref, w3_ref, g_ref, y_ref) = refs[:15]
    rest = list(refs[15:])
    h_ref = None if final else rest.pop(0)
    vn_ref = rest.pop(0) if emit_vn else None
    (oc_sc,) = rest
    _outproj_body(x_ref, om_ref[...].astype(BF16), os_ref, u_ref, v_ref, zc_ref, wmix_ref, bias_ref, gv_ref,
                  w0_ref, w1_ref, w2_ref, w3_ref, g_ref, y_ref, h_ref, vn_ref, oc_sc,
                  final=final, rs=rs, t_lo=t_lo)


def _mlstm_out_kernel(*refs, final, nc):
    (q_ref, k_ref, v_ref, o_ref, zm_ref, gates_ref, bo_ref, ng_ref,
     x_ref, os_ref, u_ref, v2_ref, zc_ref, wmix_ref, bias_ref, gv_ref,
     w0_ref, w1_ref, w2_ref, w3_ref, g_ref, c1_ref, n1_ref, m1_ref, y_ref) = refs[:25]
    rest = list(refs[25:])
    h_ref = None if final else rest.pop(0)
    om_sc, oc_sc = rest
    i = pl.program_id(0)

    @pl.when(i == 0)
    def _():
        om_sc[...] = jnp.zeros(om_sc.shape, BF16)

    @pl.when(i % nc == 0)
    def _():
        c1_ref[...] = jnp.zeros(c1_ref.shape, F32)
        n1_ref[...] = jnp.zeros(n1_ref.shape, F32)
        m1_ref[...] = jnp.zeros(m1_ref.shape, F32)

    def emit(cols, val):
        om_sc[i % 2, :, cols] = val.astype(BF16)

    prep = _mlstm_prep(gates_ref)

    pending = {}

    def head(slot):
        h, second = divmod(slot, 2)
        if second:
            pending.pop(h)()
        else:
            pending[h] = _mlstm_head(h, prep, q_ref, k_ref, v_ref, o_ref, zm_ref, bo_ref, ng_ref,
                                     c1_ref, n1_ref, m1_ref, emit)

    _outproj_body(x_ref, om_sc[(i + 1) % 2], os_ref, u_ref, v2_ref, zc_ref, wmix_ref, bias_ref, gv_ref,
                  w0_ref, w1_ref, w2_ref, w3_ref, g_ref, y_ref, h_ref, None, oc_sc,
                  final=final, rs=CHUNK, t_lo=0, between=head)


def _mlstm_out(l, q, k, v, o, gates, proj, x, os_, wmix, bias, W, g, final, nb, nc):
    L = M_CHUNK
    nt = nb * nc
    rows = nt * L
    cur = lambda i: jnp.minimum(i, nt - 1)
    prev = lambda i: jnp.maximum(i - 1, 0)
    cur_spec = pl.BlockSpec((L, M_WIDTH), lambda i: (cur(i), 0))
    prev_spec = pl.BlockSpec((L, D_MODEL), lambda i: (prev(i), 0))
    wspec = lambda r: pl.BlockSpec((None, M_WIDTH, D_MODEL), lambda i: (l, r, 0), pipeline_mode=pl.Buffered(1))
    out_shape = [
        jax.ShapeDtypeStruct((nb + 1, M_HEADS, M_DH, M_DH), F32),
        jax.ShapeDtypeStruct((nb + 1, M_HEADS, 1, M_DH), F32),
        jax.ShapeDtypeStruct((nb + 1, SUBLANES, LANES), F32),
        jax.ShapeDtypeStruct((rows, D_MODEL), F32),
    ]
    out_specs = [
        pl.BlockSpec((1, M_HEADS, M_DH, M_DH), lambda i: (i // nc, 0, 0, 0)),
        pl.BlockSpec((1, M_HEADS, 1, M_DH), lambda i: (i // nc, 0, 0, 0)),
        pl.BlockSpec((1, SUBLANES, LANES), lambda i: (i // nc, 0, 0)),
        prev_spec,
    ]
    if not final:
        out_shape.append(jax.ShapeDtypeStruct((rows, D_MODEL), BF16))
        out_specs.append(prev_spec)
    return pl.pallas_call(
        functools.partial(_mlstm_out_kernel, final=final, nc=nc),
        grid=(nt + 1,),
        in_specs=[
            cur_spec, cur_spec, cur_spec, cur_spec,
            pl.BlockSpec((L, M_WIDTH), lambda i: (cur(i), COL_ZM // M_WIDTH)),
            pl.BlockSpec((L, LANES), lambda i: (cur(i), 0)),
            _vec_spec(l, M_WIDTH),
            _vec_spec(l, M_WIDTH),
            prev_spec,
            prev_spec,
            pl.BlockSpec((L, C_WIDTH), lambda i: (prev(i), COL_U // C_WIDTH)),
            pl.BlockSpec((L, C_WIDTH), lambda i: (prev(i), COL_V // C_WIDTH)),
            pl.BlockSpec((L, C_WIDTH), lambda i: (prev(i), COL_ZC // C_WIDTH)),
            pl.BlockSpec((None, C_GROUPS, C_CHUNK, C_CHUNK), lambda i: (l, 0, 0, 0)),
            pl.BlockSpec((None, C_GROUPS, C_CHUNK, 1), lambda i: (l, 0, 0, 0)),
            _vec_spec(l, C_WIDTH),
            wspec(0), wspec(1), wspec(2), wspec(3),
            pl.BlockSpec((1, D_MODEL), lambda i: (0, 0)) if final else _vec_spec(l + 1, D_MODEL),
        ],
        out_specs=out_specs,
        out_shape=out_shape,
        scratch_shapes=[pltpu.VMEM((2, L, M_WIDTH), BF16), pltpu.VMEM((L, C_WIDTH), BF16)],
        compiler_params=_params(("arbitrary",)),
        name="mlstm_out",
    )(q, k, v, o, proj, gates, W["bo"], W["m_ng"], x, os_, proj, proj, proj, wmix, bias, W["gv"],
      W["wo"], W["wo"], W["wo"], W["wo"], g)


def _outproj(l, x, om, os_, proj, wmix, bias, gv, wo, g, final, emit_vn, rs, t_lo):
    rows = x.shape[0]
    tm = min(M_CHUNK, rows)
    kb = M_WIDTH
    wspec = lambda r: pl.BlockSpec((None, kb, D_MODEL), lambda i: (l, r, 0), pipeline_mode=pl.Buffered(1))
    row_spec = pl.BlockSpec((tm, D_MODEL), lambda i: (i, 0))
    out_shape = [jax.ShapeDtypeStruct((rows, D_MODEL), F32)]
    out_specs = [row_spec]
    if not final:
        out_shape.append(jax.ShapeDtypeStruct((rows, D_MODEL), BF16))
        out_specs.append(row_spec)
    if emit_vn:
        out_shape.append(jax.ShapeDtypeStruct((rows, C_WIDTH), F32))
        out_specs.append(pl.BlockSpec((tm, C_WIDTH), lambda i: (i, 0)))
    return pl.pallas_call(
        functools.partial(_outproj_kernel, final=final, emit_vn=emit_vn, rs=rs, t_lo=t_lo),
        grid=(rows // tm,),
        in_specs=[
            row_spec,
            pl.BlockSpec((tm, M_WIDTH), lambda i: (i, 0)),
            pl.BlockSpec((tm, S_WIDTH), lambda i: (i, 0)),
            pl.BlockSpec((tm, C_WIDTH), lambda i: (i, COL_U // C_WIDTH)),
            pl.BlockSpec((tm, C_WIDTH), lambda i: (i, COL_V // C_WIDTH)),
            pl.BlockSpec((tm, C_WIDTH), lambda i: (i, COL_ZC // C_WIDTH)),
            pl.BlockSpec((None, C_GROUPS, C_CHUNK, C_CHUNK), lambda i: (l, 0, 0, 0)),
            pl.BlockSpec((None, C_GROUPS, C_CHUNK, 1), lambda i: (l, 0, 0, 0)),
            _vec_spec(l, C_WIDTH),
            wspec(0), wspec(1), wspec(2), wspec(3),
            pl.BlockSpec((1, D_MODEL), lambda i: (0, 0)) if final else _vec_spec(l + 1, D_MODEL),
        ],
        out_specs=out_specs,
        out_shape=out_shape,
        scratch_shapes=[pltpu.VMEM((tm, C_WIDTH), BF16)],
        compiler_params=_params(("arbitrary",)),
        name="outproj",
    )(x, om, os_, proj, proj, proj, wmix, bias, gv, wo, wo, wo, wo, g)


def _heads_compact(a):
    return jnp.pad(a, ((0, 0), (0, LANES - S_HEADS)))[:, None, :]


def _heads_grouped(a):
    a = a.reshape(a.shape[0], S_GROUPS, S_HPG)
    return jnp.pad(a, ((0, 0), (0, 0), (0, LANES - S_HPG))).reshape(a.shape[0], 1, S_GROUPS * LANES)


def _prepare_weights(norm_g, w_in, m_conv_w, m_conv_b, m_w_qk, m_w_vo, m_b_o, m_w_gate, m_b_gate,
                     m_norm_g, s_conv_w, s_conv_b, s_dt_bias, s_A_log, s_D, s_norm_g, c_v_norm_g,
                     c_w_s, c_b_s, w_out):
    depth = w_in.shape[0]
    vec = lambda a: a.reshape(depth, 1, -1)
    dt0 = PROJ_HEAD
    w_all = w_in.astype(BF16)
    w_dt = w_in[:, :, dt0:dt0 + S_HEADS]
    w_dt_g = jnp.pad(w_dt.reshape(depth, D_MODEL, S_GROUPS, S_HPG),
                     ((0, 0), (0, 0), (0, 0), (0, LANES - S_HPG))).reshape(depth, D_MODEL, -1).astype(BF16)
    w_dt_c = jnp.pad(w_dt, ((0, 0), (0, 0), (0, LANES - S_HEADS))).astype(BF16)
    wg = m_w_gate.reshape(depth, M_HEADS, 3, M_DH, 2 * M_HEADS)
    wg = jnp.pad(wg, ((0, 0),) * 4 + ((0, LANES - 2 * M_HEADS),)).astype(BF16)
    head_of_lane = jnp.arange(S_WIDTH) // S_DH
    expand = (jnp.arange(LANES)[:, None] == head_of_lane[None, :]).astype(BF16)
    expand = jnp.concatenate([expand, expand], axis=0)
    return dict(
        norm_g=vec(norm_g), w_all=w_all, w_tail=w_all[:, :, dt0 + S_HEADS:], w_dt_c=w_dt_c, w_dt_g=w_dt_g,
        m_cw=m_conv_w, m_cb=vec(m_conv_b),
        wqk=m_w_qk.astype(BF16), wvo=m_w_vo.astype(BF16), wg=wg,
        bg=vec(jnp.pad(m_b_gate, ((0, 0), (0, LANES - 2 * M_HEADS)))),
        bo=vec(m_b_o), m_ng=vec(m_norm_g),
        s_cw=s_conv_w, s_cb=vec(s_conv_b),
        dtb_c=_heads_compact(s_dt_bias), alog_c=_heads_compact(s_A_log),
        dtb_g=_heads_grouped(s_dt_bias), alog_g=_heads_grouped(s_A_log),
        dskip_wide=vec(jnp.repeat(s_D, S_DH, axis=1)), s_ng=vec(s_norm_g), expand=expand,
        gv=vec(c_v_norm_g), wo=w_out.astype(BF16),
    )


def kernel(x_prompt, x_sample, state_mlstm_C, state_mlstm_n, state_mlstm_m, state_mlstm_conv, state_ssm, state_ssm_conv, norm_g, w_in, m_conv_w, m_conv_b, m_w_qk, m_w_vo, m_b_o, m_w_gate, m_b_gate, m_norm_g, s_conv_w, s_conv_b, s_dt_bias, s_A_log, s_D, s_norm_g, c_v_norm_g, c_w_s, c_b_s, w_out, final_norm_g):
    bp, seq, _ = x_prompt.shape
    bs, dec_seq, _ = x_sample.shape
    depth = w_in.shape[0]
    t_lo = SAMPLE_ROWS - dec_seq
    hist_lo = t_lo - (CONV_K - 1)
    nseq = CHUNK // SAMPLE_ROWS
    W = _prepare_weights(norm_g, w_in, m_conv_w, m_conv_b, m_w_qk, m_w_vo, m_b_o, m_w_gate, m_b_gate,
                         m_norm_g, s_conv_w, s_conv_b, s_dt_bias, s_A_log, s_D, s_norm_g, c_v_norm_g,
                         c_w_s, c_b_s, w_out)
    fg = final_norm_g[None, :]
    yp = x_prompt.reshape(bp * seq, D_MODEL)
    ys = jnp.pad(x_sample, ((0, 0), (t_lo, 0), (0, 0))).reshape(bs * SAMPLE_ROWS, D_MODEL)

    pad_hist = ((0, 0), (0, 0), (hist_lo, dec_seq), (0, 0))
    hist_m = jnp.pad(state_mlstm_conv, pad_hist).reshape(depth, bs * SAMPLE_ROWS, M_WIDTH)
    hist_s = jnp.pad(state_ssm_conv, pad_hist).reshape(depth, bs * SAMPLE_ROWS, -1)
    m0rows = jnp.repeat(jnp.swapaxes(state_mlstm_m, 1, 2)[..., None], SAMPLE_ROWS, axis=2)
    n0 = state_mlstm_n[:, :, :, None, :]
    reps = C_CHUNK // SAMPLE_ROWS
    w4 = jnp.pad(c_w_s[:, :, :dec_seq, :dec_seq], ((0, 0), (0, 0), (t_lo, 0), (t_lo, 0)))
    b4 = jnp.pad(c_b_s[:, :, :dec_seq], ((0, 0), (0, 0), (t_lo, 0)))
    wmix_s = jnp.tile(w4, (1, 1, reps, reps))
    bias_s = jnp.tile(b4, (1, 1, reps))[..., None]
    bias_p = c_b_s[..., None]

    hp = _norm(0, yp, W["norm_g"])
    hs = _norm(0, ys, W["norm_g"])
    outs_p, outs_s = [], []
    c_all = h_all = None
    mc = min(M_CHUNK, seq)
    sc = min(CHUNK * SSD_SUB, seq)
    tbp = min(PROJ_ROWS, bp * seq)
    tbs = min(PROJ_ROWS, bs * SAMPLE_ROWS)
    for l in range(depth):
        final = l == depth - 1
        proj = _inproj(l, hp, W["w_all"], W["w_tail"])
        q, k, v, o, gates = _mlstm_proj(l, proj, None, W, max(1, seq // tbp), tbp)
        out_s, h1 = _ssd_prompt(l, proj, hp, W, bp, seq // sc)
        res = _mlstm_out(l, q, k, v, o, gates, proj, yp, out_s, c_w_s, bias_p, W,
                         fg if final else W["norm_g"], final, bp, seq // mc)
        c1, n1, m1 = res[0][:bp], res[1][:bp], res[2][:bp]
        yp, hp = res[3], (None if final else res[4])
        pj = proj.reshape(bp, seq, PROJ_MAIN)
        outs_p.append((
            c1, n1.reshape(bp, M_HEADS, M_DH), m1[:, :M_HEADS, 0],
            pj[:, seq - (CONV_K - 1):, COL_XM:COL_XM + M_WIDTH], h1,
            pj[:, seq - (CONV_K - 1):, COL_XBC:COL_XBC + S_WIDTH + 2 * S_BC]))
        proj = _inproj(l, hs, W["w_all"], W["w_tail"])
        q, k, v, o, gates = _mlstm_proj(l, proj, hist_m, W, 1, tbs)
        out_m, c_all, n1, mrow = _mlstm_sample(l, depth, q, k, v, o, gates, proj, W, state_mlstm_C, n0,
                                               m0rows, c_all, nseq, t_lo)
        out_s, h_all = _ssd_sample(l, depth, proj, hs, hist_s, W, state_ssm, h_all, nseq, t_lo)
        res = _outproj(l, ys, out_m, out_s, proj, wmix_s, bias_s, W["gv"], W["wo"],
                       fg if final else W["norm_g"], final, True, SAMPLE_ROWS, t_lo)
        ys, hs, vn = res[0], (None if final else res[1]), res[-1]
        pj = proj.reshape(bs, SAMPLE_ROWS, PROJ_MAIN)
        outs_s.append((
            n1.reshape(bs, M_HEADS, M_DH), mrow[:, SAMPLE_ROWS - 1::SAMPLE_ROWS, 0].T,
            pj[:, SAMPLE_ROWS - (CONV_K - 1):, COL_XM:COL_XM + M_WIDTH],
            pj[:, SAMPLE_ROWS - (CONV_K - 1):, COL_XBC:COL_XBC + S_WIDTH + 2 * S_BC],
            vn.reshape(bs, SAMPLE_ROWS, C_WIDTH)[:, t_lo:]))
    p_out = [jnp.stack([s[i] for s in outs_p]) for i in range(6)]
    s_n, s_m, s_mconv, s_sconv, s_cv = [jnp.stack([s[i] for s in outs_s]) for i in range(5)]
    y_prompt = yp.reshape(bp, seq, D_MODEL)
    y_sample = ys.reshape(bs, SAMPLE_ROWS, D_MODEL)[:, t_lo:]
    return (y_prompt, y_sample, *p_out, c_all, s_n, s_m, s_mconv, h_all, s_sconv, s_cv)
```
